```python
import jax, jax.numpy as jnp
from jax import lax
import numpy as np

D_MODEL = 2048
BATCH = 2
SEQ = 4096
DEPTH = 1

GRID_W = 64
CTX_LEN = 256
D_GLA = 1024
D_CONV = 1024
GLA_HEADS = 4
GLA_DK = 128
GLA_DV = 256
GLA_KEY = GLA_HEADS * GLA_DK
GATE_RANK = 16
GATE_NORMALIZER = 16.0
CHUNK = 64
CONV_WIDTH = 31
D_FF = 5632
N_MOD = 9
RMS_EPS = 1e-6
HEAD_NORM_EPS = 1e-5
LN_EPS = 1e-5

P_K = GLA_KEY
P_V = D_GLA
P_Q = GLA_KEY
P_G = D_GLA
P_GLU = 2 * D_CONV
OFF_V = P_K
OFF_GKF = OFF_V + P_V
OFF_GKB = OFF_GKF + GATE_RANK
CTX_COLS = OFF_GKB + GATE_RANK
OFF_Q = CTX_COLS
OFF_G = OFF_Q + P_Q
OFF_GLU = OFF_G + P_G
D_IN = OFF_GLU + P_GLU

kernel_name = "hybrid_gla_conformer_dit_block"


def rmsnorm(h, gain, eps=RMS_EPS):
    hf = h.astype(jnp.float32)
    hf = hf * lax.rsqrt(jnp.mean(hf * hf, axis=-1, keepdims=True) + eps)
    return (hf * gain.astype(jnp.float32)).astype(h.dtype)


def modulate(h, gain, shift, scale):
    return rmsnorm(h, gain) * (1 + scale) + shift


def swiglu(h, w_in, w_out):
    gate, up = jnp.split(h @ w_in, 2, axis=-1)
    return (jax.nn.silu(gate) * up) @ w_out


def heads(t, d):
    b, t_len, _ = t.shape
    return t.reshape(b, t_len, -1, d).transpose(0, 2, 1, 3)


def flip(t):
    return jnp.flip(t, axis=2)


def log_decay(p_gk, w2, b2):
    z = (p_gk.astype(jnp.float32) @ w2.astype(jnp.float32)) + b2.astype(jnp.float32)
    return jax.nn.log_sigmoid(z) / GATE_NORMALIZER


def gla_final_state(k, v, logd):
    b = jnp.cumsum(logd, axis=2)
    k_dec = k * jnp.exp(b[:, :, -1:] - b)
    return jnp.einsum('bhld,bhle->bhde', k_dec, v)


def gla_chunked(q, k, v, logd, s0):
    bsz, nh, t_len, dk = q.shape
    dv = v.shape[-1]
    n = t_len // CHUNK
    q = q.reshape(bsz, nh, n, CHUNK, dk)
    k = k.reshape(bsz, nh, n, CHUNK, dk)
    v = v.reshape(bsz, nh, n, CHUNK, dv)
    b = jnp.cumsum(logd.reshape(bsz, nh, n, CHUNK, dk), axis=3)
    b_mid = b[:, :, :, CHUNK // 2 - 1:CHUNK // 2]
    b_last = b[:, :, :, -1:]
    qs = q * jnp.exp(b - b_mid)
    ks = k * jnp.exp(b_mid - b)
    mask = jnp.tril(jnp.ones((CHUNK, CHUNK), dtype=bool))
    att = jnp.where(mask, jnp.einsum('bhncd,bhnsd->bhncs', qs, ks), 0.0)
    o = jnp.einsum('bhncs,bhnse->bhnce', att, v)
    chunk_kv = jnp.einsum('bhncd,bhnce->bhnde', k * jnp.exp(b_last - b), v)
    chunk_decay = jnp.exp(b_last[:, :, :, 0])

    def step(s, inp):
        dec, kv = inp
        return dec[..., None] * s + kv, s

    _, s_in = lax.scan(step, s0, (jnp.moveaxis(chunk_decay, 2, 0), jnp.moveaxis(chunk_kv, 2, 0)))
    s_in = jnp.moveaxis(s_in, 0, 2)
    o = o + jnp.einsum('bhncd,bhnde->bhnce', q * jnp.exp(b), s_in)
    return o.reshape(bsz, nh, t_len, dv)


def gla_inputs_kv(p, w_gk2, b_gk2):
    f32 = jnp.float32
    k = heads(p[..., :OFF_V].astype(f32), GLA_DK)
    v = heads(p[..., OFF_V:OFF_GKF].astype(f32), GLA_DV)
    ld_f = heads(log_decay(p[..., OFF_GKF:OFF_GKB], w_gk2[0], b_gk2[0]), GLA_DK)
    ld_b = heads(log_decay(p[..., OFF_GKB:CTX_COLS], w_gk2[1], b_gk2[1]), GLA_DK)
    return k, v, ld_f, ld_b


def context_states(pc, w_gk2, b_gk2):
    k, v, ld_f, ld_b = gla_inputs_kv(pc, w_gk2, b_gk2)
    return gla_final_state(k, v, ld_f), gla_final_state(flip(k), flip(v), flip(ld_b))


def gla_group(p, w_gk2, b_gk2, norm_g, s_f, s_b):
    k, v, ld_f, ld_b = gla_inputs_kv(p, w_gk2, b_gk2)
    q = heads(p[..., OFF_Q:OFF_G].astype(jnp.float32), GLA_DK) * (GLA_DK ** -0.5)
    g = p[..., OFF_G:OFF_GLU]
    o = gla_chunked(q, k, v, ld_f, s_f) + flip(gla_chunked(flip(q), flip(k), flip(v), flip(ld_b), s_b))
    o = o * lax.rsqrt(jnp.mean(o * o, axis=-1, keepdims=True) + HEAD_NORM_EPS) * norm_g.astype(jnp.float32)
    bsz, _, t_len, _ = o.shape
    o = o.transpose(0, 2, 1, 3).reshape(bsz, t_len, D_GLA).astype(p.dtype)
    return o * jax.nn.silu(g)


def conv_latent(u, w, b):
    bsz, t_len, ch = u.shape
    rows = t_len // GRID_W
    half = ch // 2
    grid = u.reshape(bsz, rows, GRID_W, ch)
    dn = ('NHWC', 'HWIO', 'NHWC')
    row_part = lax.conv_general_dilated(grid[..., :half], w[:, :half].reshape(1, CONV_WIDTH, 1, half).astype(u.dtype),
                                        (1, 1), 'SAME', dimension_numbers=dn, feature_group_count=half)
    col_part = lax.conv_general_dilated(grid[..., half:], w[:, half:].reshape(CONV_WIDTH, 1, 1, half).astype(u.dtype),
                                        (1, 1), 'SAME', dimension_numbers=dn, feature_group_count=half)
    return jnp.concatenate([row_part, col_part], axis=-1).reshape(bsz, t_len, ch) + b


def conv_seq(u, w, b):
    ch = u.shape[-1]
    y = lax.conv_general_dilated(u, w.reshape(CONV_WIDTH, 1, ch).astype(u.dtype), (1,), 'SAME',
                                 dimension_numbers=('NWC', 'WIO', 'NWC'), feature_group_count=ch)
    return y + b


def conformer_group(p_glu, conv_fn, conv_w, conv_b, ln_g, ln_b):
    a, gate = jnp.split(p_glu, 2, axis=-1)
    y = conv_fn(a * jax.nn.sigmoid(gate), conv_w, conv_b)
    yf = y.astype(jnp.float32)
    mu = jnp.mean(yf, axis=-1, keepdims=True)
    var = jnp.mean(jnp.square(yf - mu), axis=-1, keepdims=True)
    yf = (yf - mu) * lax.rsqrt(var + LN_EPS) * ln_g.astype(jnp.float32) + ln_b.astype(jnp.float32)
    return jax.nn.silu(yf).astype(p_glu.dtype)


def setup_inputs(seed: int = 0) -> dict:
    key = jax.random.key(seed)
    ks = jax.random.split(key, 26)
    f32 = jnp.float32

    def nrm(k, shape, scale):
        return jax.random.normal(k, shape, f32) * scale

    def gain(k, shape):
        return 1.0 + 0.02 * jax.random.normal(k, shape, f32)

    d = D_MODEL
    return {
        "x": nrm(ks[0], (BATCH, SEQ, d), 1.0),
        "c": nrm(ks[1], (BATCH, d), 1.0),
        "ctx": nrm(ks[2], (BATCH, CTX_LEN, d), 1.0),
        "c_ctx": nrm(ks[3], (d,), 1.0),
        "w_mod": nrm(ks[4], (DEPTH, d, N_MOD * d), 0.5 * d ** -0.5),
        "b_mod": nrm(ks[5], (DEPTH, N_MOD * d), 0.01),
        "norm_ffn1": gain(ks[6], (DEPTH, d)),
        "w_ffn1_in": nrm(ks[7], (DEPTH, d, 2 * D_FF), d ** -0.5),
        "w_ffn1_out": nrm(ks[8], (DEPTH, D_FF, d), D_FF ** -0.5),
        "norm_mix": gain(ks[9], (DEPTH, d)),
        "w_in": nrm(ks[10], (DEPTH, d, D_IN), d ** -0.5),
        "w_gk2": nrm(ks[11], (DEPTH, 2, GATE_RANK, GLA_KEY), GATE_RANK ** -0.5),
        "b_gk2": nrm(ks[12], (DEPTH, 2, GLA_KEY), 0.5),
        "gla_norm": gain(ks[13], (DEPTH, GLA_DV)),
        "conv_w": nrm(ks[14], (DEPTH, CONV_WIDTH, D_CONV), CONV_WIDTH ** -0.5),
        "conv_b": nrm(ks[15], (DEPTH, D_CONV), 0.01),
        "conv_ln_g": gain(ks[16], (DEPTH, D_CONV)),
        "conv_ln_b": nrm(ks[17], (DEPTH, D_CONV), 0.01),
        "w_out": nrm(ks[18], (DEPTH, D_GLA + D_CONV, d), (D_GLA + D_CONV) ** -0.5),
        "norm_ffn2": gain(ks[19], (DEPTH, d)),
        "w_ffn2_in": nrm(ks[20], (DEPTH, d, 2 * D_FF), d ** -0.5),
        "w_ffn2_out": nrm(ks[21], (DEPTH, D_FF, d), D_FF ** -0.5),
        "norm_final": gain(ks[22], (d,)),
    }


def reference(x, c, ctx, c_ctx, w_mod, b_mod, norm_ffn1, w_ffn1_in, w_ffn1_out, norm_mix, w_in,
              w_gk2, b_gk2, gla_norm, conv_w, conv_b, conv_ln_g, conv_ln_b, w_out, norm_ffn2,
              w_ffn2_in, w_ffn2_out, norm_final):
    bsz = x.shape[0]
    h = x
    hc = ctx
    for l in range(DEPTH):
        mod = (jax.nn.silu(c) @ w_mod[l] + b_mod[l]).reshape(bsz, N_MOD, D_MODEL)[:, :, None, :]
        mc = (jax.nn.silu(c_ctx) @ w_mod[l] + b_mod[l]).reshape(N_MOD, D_MODEL)

        h = h + 0.5 * mod[:, 2] * swiglu(modulate(h, norm_ffn1[l], mod[:, 0], mod[:, 1]), w_ffn1_in[l], w_ffn1_out[l])
        hc = hc + 0.5 * mc[2] * swiglu(modulate(hc, norm_ffn1[l], mc[0], mc[1]), w_ffn1_in[l], w_ffn1_out[l])

        hx = modulate(h, norm_mix[l], mod[:, 3], mod[:, 4])
        hcm = modulate(hc, norm_mix[l], mc[3], mc[4])
        px = hx @ w_in[l]
        pc_kv = hcm @ w_in[l][:, :CTX_COLS]
        s_f, s_b = context_states(pc_kv, w_gk2[l], b_gk2[l])
        o_gla = gla_group(px, w_gk2[l], b_gk2[l], gla_norm[l], s_f, s_b)
        o_conv = conformer_group(px[..., OFF_GLU:], conv_latent, conv_w[l], conv_b[l], conv_ln_g[l], conv_ln_b[l])
        h = h + mod[:, 5] * (jnp.concatenate([o_gla, o_conv], axis=-1) @ w_out[l])

        h = h + 0.5 * mod[:, 8] * swiglu(modulate(h, norm_ffn2[l], mod[:, 6], mod[:, 7]), w_ffn2_in[l], w_ffn2_out[l])

        if l + 1 < DEPTH:
            pc = hcm @ w_in[l]
            zeros = jnp.zeros((bsz, GLA_HEADS, GLA_DK, GLA_DV), jnp.float32)
            oc_gla = gla_group(pc, w_gk2[l], b_gk2[l], gla_norm[l], zeros, zeros)
            oc_conv = conformer_group(pc[..., OFF_GLU:], conv_seq, conv_w[l], conv_b[l], conv_ln_g[l], conv_ln_b[l])
            hc = hc + mc[5] * (jnp.concatenate([oc_gla, oc_conv], axis=-1) @ w_out[l])
            hc = hc + 0.5 * mc[8] * swiglu(modulate(hc, norm_ffn2[l], mc[6], mc[7]), w_ffn2_in[l], w_ffn2_out[l])
    return rmsnorm(h, norm_final)
```

```python
import functools

import jax
import jax.numpy as jnp
from jax import lax
from jax.experimental import pallas as pl
from jax.experimental.pallas import tpu as pltpu

F32 = jnp.float32
BF16 = jnp.bfloat16

D_MODEL = 2048
GRID_W = 64
GLA_HEADS = 4
GLA_DK = 128
GLA_DV = 256
GLA_KEY = GLA_HEADS * GLA_DK
D_GLA = GLA_HEADS * GLA_DV
D_CONV = 1024
GATE_RANK = 16
GATE_NORMALIZER = 16.0
CHUNK = 64
CONV_WIDTH = 31
CONV_HALF = CONV_WIDTH // 2
D_FF = 5632
N_MOD = 9
RMS_EPS = 1e-6
HEAD_NORM_EPS = 1e-5
LN_EPS = 1e-5

OFF_V = GLA_KEY
OFF_GKF = OFF_V + D_GLA
CTX_COLS = OFF_GKF + 2 * GATE_RANK
OFF_Q = CTX_COLS
OFF_G = OFF_Q + GLA_KEY
OFF_GLU = OFF_G + D_GLA

R_K = 0
R_V = R_K + GLA_KEY
R_Q = R_V + D_GLA
R_G = R_Q + GLA_KEY
R_A = R_G + D_GLA
R_B = R_A + D_CONV
R_GK = R_B + D_CONV
V7X_LANES = 128
R_END = R_GK + V7X_LANES

V7X_VMEM_SCOPED_LIMIT_BYTES = 60000 * 1024

FFN_TM = 512
FFN_TF = 512
PROJ_TM = 512
OUT_TM = 512
MOD_TN = 1024
GLA_TB = 512
GLA_SUB = 256
CONV_CB = 128


def _vmem_limit(*nbytes):
    need = int(sum(nbytes) * 1.25)
    return min(max(need, 16 * 1024 * 1024), V7X_VMEM_SCOPED_LIMIT_BYTES)


def _nbytes(shape, dtype):
    n = 1
    for s in shape:
        n *= s
    return n * jnp.dtype(dtype).itemsize


def _silu(x):
    return x * jax.nn.sigmoid(x)


def _rmsnorm_rows(x, gain, eps):
    ms = jnp.mean(x * x, axis=-1, keepdims=True)
    return x * lax.rsqrt(ms + eps) * gain


def _modulate(x, gain, shift, scale):
    return _rmsnorm_rows(x, gain, RMS_EPS) * (1.0 + scale) + shift


def _mod_kernel(s_ref, w_ref, b_ref, o_ref):
    s = _silu(s_ref[...]).astype(BF16)
    o_ref[...] = jnp.dot(s, w_ref[...].astype(BF16), preferred_element_type=F32) + b_ref[...]


def _mod_call(s_in, w_mod, b_mod):
    rows, d = s_in.shape
    n = w_mod.shape[1]
    return pl.pallas_call(
        _mod_kernel,
        grid=(n // MOD_TN,),
        in_specs=[
            pl.BlockSpec((rows, d), lambda j: (0, 0)),
            pl.BlockSpec((d, MOD_TN), lambda j: (0, j)),
            pl.BlockSpec((1, MOD_TN), lambda j: (0, j)),
        ],
        out_specs=pl.BlockSpec((rows, MOD_TN), lambda j: (0, j)),
        out_shape=jax.ShapeDtypeStruct((rows, n), F32),
        compiler_params=pltpu.CompilerParams(
            dimension_semantics=("arbitrary",),
            vmem_limit_bytes=_vmem_limit(3 * _nbytes((d, MOD_TN), F32)),
        ),
        name="mod",
    )(s_in, w_mod, b_mod)


def _ffn_kernel(epilogue, mod_row, x_ref, mod_ref, g_in_ref, g_next_ref, wg_ref, wu_ref, wo_ref,
                *rest):
    if epilogue == "mix":
        h_ref, hx_ref, hm_ref, acc_ref = rest
    else:
        h_ref, hm_ref, acc_ref = rest
    j = pl.program_id(1)

    @pl.when(j == 0)
    def _():
        shift = mod_ref[mod_row:mod_row + 1, :]
        scale = mod_ref[mod_row + 1:mod_row + 2, :]
        hm_ref[...] = _modulate(x_ref[...], g_in_ref[...], shift, scale).astype(BF16)
        acc_ref[...] = jnp.zeros_like(acc_ref)

    hm = hm_ref[...]
    gate = jnp.dot(hm, wg_ref[...], preferred_element_type=F32)
    up = jnp.dot(hm, wu_ref[...], preferred_element_type=F32)
    act = (_silu(gate) * up).astype(BF16)
    acc_ref[...] += jnp.dot(act, wo_ref[...], preferred_element_type=F32)

    @pl.when(j == pl.num_programs(1) - 1)
    def _():
        g = mod_ref[mod_row + 2:mod_row + 3, :]
        h = x_ref[...] + (0.5 * g) * acc_ref[...]
        if epilogue == "mix":
            h_ref[...] = h
            shift = mod_ref[mod_row + 3:mod_row + 4, :]
            scale = mod_ref[mod_row + 4:mod_row + 5, :]
            hx_ref[...] = _modulate(h, g_next_ref[...], shift, scale).astype(BF16)
        else:
            h_ref[...] = _rmsnorm_rows(h, g_next_ref[...], RMS_EPS)


def _ffn_call(x, mod, group_of_tile, g_in, g_next, w_in_bf, w_out_bf, *, epilogue, mod_row, name):
    rows, d = x.shape
    n_ff = w_out_bf.shape[0]
    n_j = n_ff // FFN_TF
    grid = (rows // FFN_TM, n_j)
    row_spec = pl.BlockSpec((FFN_TM, d), lambda i, j: (i, 0))
    vec_spec = pl.BlockSpec((1, d), lambda i, j: (0, 0))
    in_specs = [
        row_spec,
        pl.BlockSpec((None, N_MOD, d), lambda i, j: (group_of_tile(i), 0, 0)),
        vec_spec,
        vec_spec,
        pl.BlockSpec((d, FFN_TF), lambda i, j: (0, j)),
        pl.BlockSpec((d, FFN_TF), lambda i, j: (0, n_j + j)),
        pl.BlockSpec((FFN_TF, d), lambda i, j: (j, 0)),
    ]
    if epilogue == "mix":
        out_specs = [row_spec, row_spec]
        out_shape = [jax.ShapeDtypeStruct((rows, d), F32), jax.ShapeDtypeStruct((rows, d), BF16)]
    else:
        out_specs = row_spec
        out_shape = jax.ShapeDtypeStruct((rows, d), F32)
    tile_f32 = _nbytes((FFN_TM, d), F32)
    w_blk = _nbytes((d, FFN_TF), BF16)
    return pl.pallas_call(
        functools.partial(_ffn_kernel, epilogue, mod_row),
        grid=grid,
        in_specs=in_specs,
        out_specs=out_specs,
        out_shape=out_shape,
        scratch_shapes=[pltpu.VMEM((FFN_TM, d), BF16), pltpu.VMEM((FFN_TM, d), F32)],
        compiler_params=pltpu.CompilerParams(
            dimension_semantics=("parallel", "arbitrary"),
            vmem_limit_bytes=_vmem_limit(6 * tile_f32, 6 * w_blk, 4 * _nbytes((FFN_TM, FFN_TF), F32)),
        ),
        name=name,
    )(x, mod, g_in, g_next, w_in_bf, w_in_bf, w_out_bf)


def _log_sigmoid(z):
    return jnp.minimum(z, 0.0) - jnp.log1p(jnp.exp(-jnp.abs(z)))


def _proj_kernel(hx_ref, w_ref, w2_ref, b2_ref, k_ref, v_ref, q_ref, sg_ref, u_ref, ldf_ref, ldb_ref):
    hx = hx_ref[...]

    def proj(lo, hi):
        return jnp.dot(hx, w_ref[:, lo:hi], preferred_element_type=F32)

    k_ref[...] = proj(R_K, R_V).astype(BF16)
    v_ref[...] = proj(R_V, R_Q).astype(BF16)
    q_ref[...] = (proj(R_Q, R_G) * (GLA_DK ** -0.5)).astype(BF16)
    sg_ref[...] = _silu(proj(R_G, R_A)).astype(BF16)
    u_ref[...] = proj(R_A, R_B) * jax.nn.sigmoid(proj(R_B, R_GK))
    p_gk = proj(R_GK, R_END)
    z = jnp.dot(p_gk, w2_ref[...], preferred_element_type=F32,
                precision=lax.Precision.HIGHEST) + b2_ref[...]
    ld = _log_sigmoid(z) * (1.0 / GATE_NORMALIZER)
    ldf_ref[...] = ld[:, :GLA_KEY]
    ldb_ref[...] = ld[:, GLA_KEY:]


def _proj_call(hx, w_r, w2pad, b2, name):
    rows, d = hx.shape
    tm = PROJ_TM
    row = lambda n: pl.BlockSpec((tm, n), lambda i: (i, 0))
    whole = lambda a: pl.BlockSpec(a.shape, lambda i: (0, 0), pipeline_mode=pl.Buffered(1))
    outs = [(GLA_KEY, BF16), (D_GLA, BF16), (GLA_KEY, BF16), (D_GLA, BF16), (D_CONV, F32),
            (GLA_KEY, F32), (GLA_KEY, F32)]
    out_bytes = sum(_nbytes((tm, n), dt) for n, dt in outs)
    return pl.pallas_call(
        _proj_kernel,
        grid=(rows // tm,),
        in_specs=[row(d), whole(w_r), whole(w2pad), whole(b2)],
        out_specs=[row(n) for n, _ in outs],
        out_shape=[jax.ShapeDtypeStruct((rows, n), dt) for n, dt in outs],
        compiler_params=pltpu.CompilerParams(
            dimension_semantics=("parallel",),
            vmem_limit_bytes=_vmem_limit(_nbytes(w_r.shape, BF16), 2 * _nbytes((tm, d), BF16),
                                         2 * out_bytes, 3 * _nbytes((tm, D_CONV), F32)),
        ),
        name=name,
    )(hx, w_r, w2pad, b2)


def _ctxstate_kernel(k_ref, v_ref, ldf_ref, ldb_ref, sf_ref, sb_ref):
    t = k_ref.shape[0]
    r = lax.broadcasted_iota(jnp.int32, (t, t), 0)
    c = lax.broadcasted_iota(jnp.int32, (t, t), 1)
    after = (c > r).astype(F32)
    before = (c < r).astype(F32)
    hi = lax.Precision.HIGHEST
    e_f = jnp.dot(after, ldf_ref[...], preferred_element_type=F32, precision=hi)
    e_b = jnp.dot(before, ldb_ref[...], preferred_element_type=F32, precision=hi)
    k = k_ref[...].astype(F32)
    v = v_ref[...]
    tn = (((0,), (0,)), ((), ()))
    sf_ref[...] = lax.dot_general(v, (k * jnp.exp(e_f)).astype(BF16), tn, preferred_element_type=F32)
    sb_ref[...] = lax.dot_general(v, (k * jnp.exp(e_b)).astype(BF16), tn, preferred_element_type=F32)


def _ctxstate_call(k, v, ldf, ldb, bsz, t):
    blk = lambda n: pl.BlockSpec((t, n), lambda b, h: (b, h))
    st = pl.BlockSpec((None, None, GLA_DV, GLA_DK), lambda b, h: (b, h, 0, 0))
    shape = jax.ShapeDtypeStruct((bsz, GLA_HEADS, GLA_DV, GLA_DK), F32)
    return pl.pallas_call(
        _ctxstate_kernel,
        grid=(bsz, GLA_HEADS),
        in_specs=[blk(GLA_DK), blk(GLA_DV), blk(GLA_DK), blk(GLA_DK)],
        out_specs=[st, st],
        out_shape=[shape, shape],
        compiler_params=pltpu.CompilerParams(dimension_semantics=("parallel", "parallel")),
        name="ctxstate",
    )(k, v, ldf, ldb)


def _chunk_cumsum(x, reverse):
    n = x.shape[0]
    pos = lax.broadcasted_iota(jnp.int32, x.shape, 0) % CHUNK
    d = 1
    while d < CHUNK:
        if reverse:
            shifted = pltpu.roll(x, n - d, 0)
            x = x + jnp.where(pos < CHUNK - d, shifted, 0.0)
        else:
            shifted = pltpu.roll(x, d, 0)
            x = x + jnp.where(pos >= d, shifted, 0.0)
        d *= 2
    return x


def _rows_bcast(x, row_in_chunk):
    parts = []
    for c0 in range(0, x.shape[0], CHUNK):
        r = c0 + row_in_chunk
        parts.append(jnp.broadcast_to(x[r:r + 1, :], (CHUNK, x.shape[1])))
    return jnp.concatenate(parts, axis=0)


def _gla_direction(q_ref, k_ref, v_ref, ld_ref, s_ref, reverse):
    nt = (((1,), (1,)), ((), ()))
    tn = (((0,), (0,)), ((), ()))
    n_sub = GLA_TB // GLA_SUB
    n_chunk = GLA_SUB // CHUNK
    rr = lax.broadcasted_iota(jnp.int32, (GLA_SUB, GLA_SUB), 0)
    cc = lax.broadcasted_iota(jnp.int32, (GLA_SUB, GLA_SUB), 1)
    same_chunk = (rr // CHUNK) == (cc // CHUNK)
    mask = same_chunk & ((cc >= rr) if reverse else (cc <= rr))
    mid_row = CHUNK // 2 if reverse else CHUNK // 2 - 1
    last_row = 0 if reverse else CHUNK - 1

    outs = [None] * n_sub
    subs = range(n_sub - 1, -1, -1) if reverse else range(n_sub)
    for s in subs:
        rows = slice(s * GLA_SUB, (s + 1) * GLA_SUB)
        b = _chunk_cumsum(ld_ref[rows, :], reverse)
        b_mid = _rows_bcast(b, mid_row)
        b_last = _rows_bcast(b, last_row)
        q = q_ref[rows, :].astype(F32)
        k = k_ref[rows, :].astype(F32)
        v = v_ref[rows, :]
        qs = (q * jnp.exp(b - b_mid)).astype(BF16)
        ks = (k * jnp.exp(b_mid - b)).astype(BF16)
        qi = (q * jnp.exp(b)).astype(BF16)
        kd = (k * jnp.exp(b_last - b)).astype(BF16)
        att = lax.dot_general(qs, ks, nt, preferred_element_type=F32)
        att = jnp.where(mask, att, 0.0).astype(BF16)
        o_intra = jnp.dot(att, v, preferred_element_type=F32)
        decay = jnp.exp(b_last)
        o_parts = [None] * n_chunk
        chunks = range(n_chunk - 1, -1, -1) if reverse else range(n_chunk)
        for c in chunks:
            cr = slice(c * CHUNK, (c + 1) * CHUNK)
            state = s_ref[...]
            o_parts[c] = o_intra[cr, :] + lax.dot_general(
                qi[cr, :], state.astype(BF16), nt, preferred_element_type=F32)
            kv = lax.dot_general(v[cr, :], kd[cr, :], tn, preferred_element_type=F32)
            s_ref[...] = state * decay[c * CHUNK:c * CHUNK + 1, :] + kv
        outs[s] = jnp.concatenate(o_parts, axis=0)
    return jnp.concatenate(outs, axis=0)


def _gla_kernel(qf_ref, kf_ref, vf_ref, ldf_ref, qb_ref, kb_ref, vb_ref, ldb_ref, sg_ref,
                s0f_ref, s0b_ref, gain_ref, o_ref, sf_ref, sb_ref, acc_ref):
    nb = pl.program_id(2)
    n_blocks = pl.num_programs(2)

    @pl.when(nb == 0)
    def _():
        sf_ref[...] = s0f_ref[...]
        sb_ref[...] = s0b_ref[...]

    o_f = _gla_direction(qf_ref, kf_ref, vf_ref, ldf_ref, sf_ref, reverse=False)
    o_b = _gla_direction(qb_ref, kb_ref, vb_ref, ldb_ref, sb_ref, reverse=True)
    row_f = pl.multiple_of(nb * GLA_TB, GLA_TB)
    row_b = pl.multiple_of((n_blocks - 1 - nb) * GLA_TB, GLA_TB)

    @pl.when(nb < n_blocks // 2)
    def _():
        acc_ref[pl.ds(row_f, GLA_TB), :] = o_f
        acc_ref[pl.ds(row_b, GLA_TB), :] = o_b

    @pl.when(nb >= n_blocks // 2)
    def _():
        for row, part in ((row_f, o_f), (row_b, o_b)):
            o = acc_ref[pl.ds(row, GLA_TB), :] + part
            ms = jnp.mean(o * o, axis=-1, keepdims=True)
            o = o * lax.rsqrt(ms + HEAD_NORM_EPS) * gain_ref[...]
            o_ref[pl.ds(row, GLA_TB), :] = (o * sg_ref[pl.ds(row, GLA_TB), :].astype(F32)).astype(BF16)


def _gla_call(q, k, v, ldf, ldb, sg, s0f, s0b, gain, bsz, t):
    nb = t // GLA_TB
    fwd = lambda n: pl.BlockSpec((GLA_TB, n), lambda b, h, i: (b * nb + i, h))
    bwd = lambda n: pl.BlockSpec((GLA_TB, n), lambda b, h, i: (b * nb + nb - 1 - i, h))
    seq = pl.BlockSpec((t, GLA_DV), lambda b, h, i: (b, h))
    st = pl.BlockSpec((None, None, GLA_DV, GLA_DK), lambda b, h, i: (b, h, 0, 0))
    return pl.pallas_call(
        _gla_kernel,
        grid=(bsz, GLA_HEADS, nb),
        in_specs=[fwd(GLA_DK), fwd(GLA_DK), fwd(GLA_DV), fwd(GLA_DK),
                  bwd(GLA_DK), bwd(GLA_DK), bwd(GLA_DV), bwd(GLA_DK),
                  seq, st, st, pl.BlockSpec((1, GLA_DV), lambda b, h, i: (0, 0))],
        out_specs=seq,
        out_shape=jax.ShapeDtypeStruct((bsz * t, D_GLA), BF16),
        scratch_shapes=[pltpu.VMEM((GLA_DV, GLA_DK), F32), pltpu.VMEM((GLA_DV, GLA_DK), F32),
                        pltpu.VMEM((t, GLA_DV), F32)],
        compiler_params=pltpu.CompilerParams(
            dimension_semantics=("parallel", "parallel", "arbitrary"),
            vmem_limit_bytes=_vmem_limit(4 * _nbytes((t, GLA_DV), BF16), _nbytes((t, GLA_DV), F32),
                                         16 * _nbytes((GLA_TB, GLA_DV), F32)),
        ),
        name="gla",
    )(q, k, v, ldf, q, k, v, ldb, sg, s0f, s0b, gain)


ROW_PAD = 16
ROW_PITCH = GRID_W + 2 * ROW_PAD


def _conv_kernel(n_row_blocks, u_ref, w_ref, b_ref, y_ref, pad_ref):
    cb = pl.program_id(1)
    rows = u_ref.shape[0] // GRID_W
    bias = jnp.broadcast_to(b_ref[...], (GRID_W, CONV_CB))

    @pl.when(cb < n_row_blocks)
    def _():
        pad_ref[...] = jnp.zeros_like(pad_ref)

        def fill(r, carry):
            src = pl.multiple_of(r * GRID_W, GRID_W)
            dst = pl.multiple_of(r * ROW_PITCH + ROW_PAD, 8)
            pad_ref[pl.ds(dst, GRID_W), :] = u_ref[pl.ds(src, GRID_W), :]
            return carry

        lax.fori_loop(0, rows, fill, 0)

        def body(r, carry):
            base = r * ROW_PITCH + (ROW_PAD - CONV_HALF)
            acc = bias
            for j in range(CONV_WIDTH):
                acc = acc + w_ref[j:j + 1, :] * pad_ref[pl.ds(base + j, GRID_W), :]
            y_ref[pl.ds(pl.multiple_of(r * GRID_W, GRID_W), GRID_W), :] = acc
            return carry

        lax.fori_loop(0, rows, body, 0)

    @pl.when(cb >= n_row_blocks)
    def _():
        edge = CONV_HALF * GRID_W
        pad_ref[0:edge, :] = jnp.zeros((edge, CONV_CB), F32)
        pad_ref[edge + rows * GRID_W:2 * edge + rows * GRID_W, :] = jnp.zeros((edge, CONV_CB), F32)
        pad_ref[edge:edge + rows * GRID_W, :] = u_ref[...]

        def body(r, carry):
            acc = bias
            for j in range(CONV_WIDTH):
                src = pl.multiple_of((r + j) * GRID_W, GRID_W)
                acc = acc + w_ref[j:j + 1, :] * pad_ref[pl.ds(src, GRID_W), :]
            y_ref[pl.ds(pl.multiple_of(r * GRID_W, GRID_W), GRID_W), :] = acc
            return carry

        lax.fori_loop(0, rows, body, 0)


def _conv_call(u, w, b, bsz, t):
    ch = u.shape[1]
    rows = t // GRID_W
    n_cb = ch // CONV_CB
    pad_rows = max(rows * ROW_PITCH, (rows + 2 * CONV_HALF) * GRID_W)
    blk = pl.BlockSpec((t, CONV_CB), lambda bi, c: (bi, c))
    return pl.pallas_call(
        functools.partial(_conv_kernel, n_cb // 2),
        grid=(bsz, n_cb),
        in_specs=[blk, pl.BlockSpec((CONV_WIDTH, CONV_CB), lambda bi, c: (0, c)),
                  pl.BlockSpec((1, CONV_CB), lambda bi, c: (0, c))],
        out_specs=blk,
        out_shape=jax.ShapeDtypeStruct(u.shape, F32),
        scratch_shapes=[pltpu.VMEM((pad_rows, CONV_CB), F32)],
        compiler_params=pltpu.CompilerParams(
            dimension_semantics=("parallel", "parallel"),
            vmem_limit_bytes=_vmem_limit(4 * _nbytes((t, CONV_CB), F32), _nbytes((pad_rows, CONV_CB), F32)),
        ),
        name="conv",
    )(u, w, b)


def _outproj_kernel(og_ref, y_ref, lng_ref, lnb_ref, w_ref, h_ref, mod_ref, o_ref):
    y = y_ref[...]
    mu = jnp.mean(y, axis=-1, keepdims=True)
    yc = y - mu
    var = jnp.mean(yc * yc, axis=-1, keepdims=True)
    yn = yc * lax.rsqrt(var + LN_EPS) * lng_ref[...] + lnb_ref[...]
    oc = _silu(yn).astype(BF16)
    res = jnp.dot(og_ref[...], w_ref[0:D_GLA, :], preferred_element_type=F32)
    res = res + jnp.dot(oc, w_ref[D_GLA:D_GLA + D_CONV, :], preferred_element_type=F32)
    o_ref[...] = h_ref[...] + mod_ref[5:6, :] * res


def _outproj_call(og, y, ln_g, ln_b, w_out_bf, h, mod, group_of_tile):
    rows, d = h.shape
    tm = OUT_TM
    row = lambda n: pl.BlockSpec((tm, n), lambda i: (i, 0))
    vec = lambda n: pl.BlockSpec((1, n), lambda i: (0, 0))
    return pl.pallas_call(
        _outproj_kernel,
        grid=(rows // tm,),
        in_specs=[row(D_GLA), row(D_CONV), vec(D_CONV), vec(D_CONV),
                  pl.BlockSpec(w_out_bf.shape, lambda i: (0, 0), pipeline_mode=pl.Buffered(1)),
                  row(d), pl.BlockSpec((None, N_MOD, d), lambda i: (group_of_tile(i), 0, 0))],
        out_specs=row(d),
        out_shape=jax.ShapeDtypeStruct((rows, d), F32),
        compiler_params=pltpu.CompilerParams(
            dimension_semantics=("parallel",),
            vmem_limit_bytes=_vmem_limit(_nbytes(w_out_bf.shape, BF16), 5 * _nbytes((tm, d), F32),
                                         4 * _nbytes((tm, D_CONV), F32)),
        ),
        name="outproj",
    )(og, y, ln_g, ln_b, w_out_bf, h, mod)


def kernel(x, c, ctx, c_ctx, w_mod, b_mod, norm_ffn1, w_ffn1_in, w_ffn1_out, norm_mix, w_in, w_gk2, b_gk2, gla_norm, conv_w, conv_b, conv_ln_g, conv_ln_b, w_out, norm_ffn2, w_ffn2_in, w_ffn2_out, norm_final):
    bsz, t, d = x.shape
    t_ctx = ctx.shape[1]
    assert w_mod.shape[0] == 1, "single layer only"
    assert t % GLA_TB == 0 and (t // GLA_TB) % 2 == 0 and t % FFN_TM == 0

    wf1_in = w_ffn1_in[0].astype(BF16)
    wf1_out = w_ffn1_out[0].astype(BF16)
    wf2_in = w_ffn2_in[0].astype(BF16)
    wf2_out = w_ffn2_out[0].astype(BF16)
    w_out_bf = w_out[0].astype(BF16)
    wi = w_in[0]
    w_r = jnp.concatenate(
        [wi[:, :OFF_GKF], wi[:, OFF_Q:], wi[:, OFF_GKF:CTX_COLS],
         jnp.zeros((d, V7X_LANES - 2 * GATE_RANK), F32)], axis=1).astype(BF16)
    w2pad = jnp.zeros((V7X_LANES, 2 * GLA_KEY), F32)
    w2pad = w2pad.at[:GATE_RANK, :GLA_KEY].set(w_gk2[0, 0])
    w2pad = w2pad.at[GATE_RANK:2 * GATE_RANK, GLA_KEY:].set(w_gk2[0, 1])
    b2 = b_gk2[0].reshape(1, 2 * GLA_KEY)
    vec = lambda a: a.reshape(1, -1)

    n_rows = 8
    s_in = jnp.concatenate([c, c_ctx[None, :], jnp.zeros((n_rows - bsz - 1, d), F32)], axis=0)
    mod = _mod_call(s_in, w_mod[0], vec(b_mod[0])).reshape(n_rows, N_MOD, d)

    tiles_per_batch = t // FFN_TM
    lat_group = lambda i: i // tiles_per_batch
    ctx_group = lambda i: bsz

    xl = x.reshape(bsz * t, d)
    xc = ctx.reshape(bsz * t_ctx, d)
    h1, hx = _ffn_call(xl, mod, lat_group, vec(norm_ffn1[0]), vec(norm_mix[0]), wf1_in, wf1_out,
                       epilogue="mix", mod_row=0, name="ffn1")
    _, hxc = _ffn_call(xc, mod, ctx_group, vec(norm_ffn1[0]), vec(norm_mix[0]), wf1_in, wf1_out,
                       epilogue="mix", mod_row=0, name="ffn1_ctx")

    k, v, q, sg, u, ldf, ldb = _proj_call(hx, w_r, w2pad, b2, "proj")
    kc, vc, _, _, _, ldfc, ldbc = _proj_call(hxc, w_r, w2pad, b2, "proj_ctx")
    s0f, s0b = _ctxstate_call(kc, vc, ldfc, ldbc, bsz, t_ctx)

    og = _gla_call(q, k, v, ldf, ldb, sg, s0f, s0b, vec(gla_norm[0]), bsz, t)
    y = _conv_call(u, conv_w[0], vec(conv_b[0]), bsz, t)
    h2 = _outproj_call(og, y, vec(conv_ln_g[0]), vec(conv_ln_b[0]), w_out_bf, h1, mod,
                       lambda i: i // (t // OUT_TM))

    out = _ffn_call(h2, mod, lat_group, vec(norm_ffn2[0]), vec(norm_final), wf2_in, wf2_out,
                    epilogue="final", mod_row=6, name="ffn2")
    return out.reshape(bsz, t, d)
```

```python
import functools

import jax
import jax.numpy as jnp
from jax import lax
from jax.experimental import pallas as pl
from jax.experimental.pallas import tpu as pltpu

F32 = jnp.float32
BF16 = jnp.bfloat16

D_MODEL = 2048
GRID_W = 64
GLA_HEADS = 4
GLA_DK = 128
GLA_DV = 256
GLA_KEY = GLA_HEADS * GLA_DK
D_GLA = GLA_HEADS * GLA_DV
D_CONV = 1024
GATE_RANK = 16
GATE_NORMALIZER = 16.0
CHUNK = 64
CONV_WIDTH = 31
CONV_HALF = CONV_WIDTH // 2
D_FF = 5632
N_MOD = 9
RMS_EPS = 1e-6
HEAD_NORM_EPS = 1e-5
LN_EPS = 1e-5

OFF_V = GLA_KEY
OFF_GKF = OFF_V + D_GLA
CTX_COLS = OFF_GKF + 2 * GATE_RANK
OFF_Q = CTX_COLS
OFF_G = OFF_Q + GLA_KEY
OFF_GLU = OFF_G + D_GLA

R_K = 0
R_V = R_K + GLA_KEY
R_Q = R_V + D_GLA
R_G = R_Q + GLA_KEY
R_A = R_G + D_GLA
R_B = R_A + D_CONV
R_GK = R_B + D_CONV
V7X_LANES = 128
R_END = R_GK + V7X_LANES

V7X_VMEM_SCOPED_LIMIT_BYTES = 60000 * 1024

FFN_TM = 512
FFN_TF = 512
PROJ_TM = 512
OUT_TM = 512
MOD_TN = 1024
GLA_TB = 512
GLA_SUB = 256
CONV_CB = 128
WPREP_TR = 256


def _vmem_limit(*nbytes):
    need = int(sum(nbytes) * 1.25)
    return min(max(need, 16 * 1024 * 1024), V7X_VMEM_SCOPED_LIMIT_BYTES)


def _nbytes(shape, dtype):
    n = 1
    for s in shape:
        n *= s
    return n * jnp.dtype(dtype).itemsize


def _silu(x):
    return x * jax.nn.sigmoid(x)


def _rmsnorm_rows(x, gain, eps):
    ms = jnp.mean(x * x, axis=-1, keepdims=True)
    return x * lax.rsqrt(ms + eps) * gain


def _modulate(x, gain, shift, scale):
    return _rmsnorm_rows(x, gain, RMS_EPS) * (1.0 + scale) + shift


def _mod_kernel(s_ref, w_ref, b_ref, o_ref):
    s = _silu(s_ref[...]).astype(BF16)
    o_ref[...] = jnp.dot(s, w_ref[...].astype(BF16), preferred_element_type=F32) + b_ref[...]


def _mod_call(s_in, w_mod, b_mod):
    rows, d = s_in.shape
    n = w_mod.shape[1]
    return pl.pallas_call(
        _mod_kernel,
        grid=(n // MOD_TN,),
        in_specs=[
            pl.BlockSpec((rows, d), lambda j: (0, 0)),
            pl.BlockSpec((d, MOD_TN), lambda j: (0, j)),
            pl.BlockSpec((1, MOD_TN), lambda j: (0, j)),
        ],
        out_specs=pl.BlockSpec((rows, MOD_TN), lambda j: (0, j)),
        out_shape=jax.ShapeDtypeStruct((rows, n), F32),
        compiler_params=pltpu.CompilerParams(
            dimension_semantics=("arbitrary",),
            vmem_limit_bytes=_vmem_limit(3 * _nbytes((d, MOD_TN), F32)),
        ),
        name="mod",
    )(s_in, w_mod, b_mod)


def _ffn_kernel(epilogue, mod_row, x_ref, mod_ref, g_in_ref, g_next_ref, wg_ref, wu_ref, wo_ref,
                *rest):
    if epilogue == "mix":
        h_ref, hx_ref, hm_ref, acc_ref = rest
    else:
        h_ref, hm_ref, acc_ref = rest
    j = pl.program_id(1)
    last = pl.num_programs(1) - 1

    def partial_out(hm):
        gate = jnp.dot(hm, wg_ref[...], preferred_element_type=F32)
        up = jnp.dot(hm, wu_ref[...], preferred_element_type=F32)
        act = (_silu(gate) * up).astype(BF16)
        return jnp.dot(act, wo_ref[...], preferred_element_type=F32)

    @pl.when(j == 0)
    def _():
        shift = mod_ref[mod_row:mod_row + 1, :]
        scale = mod_ref[mod_row + 1:mod_row + 2, :]
        hm = _modulate(x_ref[...], g_in_ref[...], shift, scale).astype(BF16)
        hm_ref[...] = hm
        acc_ref[...] = partial_out(hm)

    @pl.when((j > 0) & (j < last))
    def _():
        acc_ref[...] += partial_out(hm_ref[...])

    @pl.when(j == last)
    def _():
        g = mod_ref[mod_row + 2:mod_row + 3, :]
        h = x_ref[...] + (0.5 * g) * (acc_ref[...] + partial_out(hm_ref[...]))
        if epilogue == "mix":
            h_ref[...] = h
            shift = mod_ref[mod_row + 3:mod_row + 4, :]
            scale = mod_ref[mod_row + 4:mod_row + 5, :]
            hx_ref[...] = _modulate(h, g_next_ref[...], shift, scale).astype(BF16)
        else:
            h_ref[...] = _rmsnorm_rows(h, g_next_ref[...], RMS_EPS)


def _ffn_call(x, mod, group_of_tile, g_in, g_next, w_in_bf, w_out_bf, *, epilogue, mod_row, name):
    rows, d = x.shape
    n_ff = w_out_bf.shape[0]
    n_j = n_ff // FFN_TF
    assert n_j >= 2, "first and last hidden block must be distinct grid steps"
    grid = (rows // FFN_TM, n_j)
    row_spec = pl.BlockSpec((FFN_TM, d), lambda i, j: (i, 0))
    vec_spec = pl.BlockSpec((1, d), lambda i, j: (0, 0))
    in_specs = [
        row_spec,
        pl.BlockSpec((None, N_MOD, d), lambda i, j: (group_of_tile(i), 0, 0)),
        vec_spec,
        vec_spec,
        pl.BlockSpec((d, FFN_TF), lambda i, j: (0, j)),
        pl.BlockSpec((d, FFN_TF), lambda i, j: (0, n_j + j)),
        pl.BlockSpec((FFN_TF, d), lambda i, j: (j, 0)),
    ]
    if epilogue == "mix":
        out_specs = [row_spec, row_spec]
        out_shape = [jax.ShapeDtypeStruct((rows, d), F32), jax.ShapeDtypeStruct((rows, d), BF16)]
    else:
        out_specs = row_spec
        out_shape = jax.ShapeDtypeStruct((rows, d), F32)
    tile_f32 = _nbytes((FFN_TM, d), F32)
    w_blk = _nbytes((d, FFN_TF), BF16)
    return pl.pallas_call(
        functools.partial(_ffn_kernel, epilogue, mod_row),
        grid=grid,
        in_specs=in_specs,
        out_specs=out_specs,
        out_shape=out_shape,
        scratch_shapes=[pltpu.VMEM((FFN_TM, d), BF16), pltpu.VMEM((FFN_TM, d), F32)],
        compiler_params=pltpu.CompilerParams(
            dimension_semantics=("parallel", "arbitrary"),
            vmem_limit_bytes=_vmem_limit(6 * tile_f32, 6 * w_blk, 4 * _nbytes((FFN_TM, FFN_TF), F32)),
        ),
        name=name,
    )(x, mod, g_in, g_next, w_in_bf, w_in_bf, w_out_bf)


def _wprep_kernel(w_ref, o_ref):
    o_ref[:, R_K:R_Q] = w_ref[:, 0:OFF_GKF].astype(BF16)
    o_ref[:, R_Q:R_GK] = w_ref[:, OFF_Q:OFF_GLU + 2 * D_CONV].astype(BF16)
    tail = jnp.concatenate(
        [w_ref[:, OFF_GKF:CTX_COLS], jnp.zeros((w_ref.shape[0], V7X_LANES - 2 * GATE_RANK), F32)], axis=1)
    o_ref[:, R_GK:R_END] = tail.astype(BF16)


def _wprep_call(w):
    d, n = w.shape
    tr = WPREP_TR
    return pl.pallas_call(
        _wprep_kernel,
        grid=(d // tr,),
        in_specs=[pl.BlockSpec((tr, n), lambda i: (i, 0))],
        out_specs=pl.BlockSpec((tr, R_END), lambda i: (i, 0)),
        out_shape=jax.ShapeDtypeStruct((d, R_END), BF16),
        compiler_params=pltpu.CompilerParams(
            dimension_semantics=("parallel",),
            vmem_limit_bytes=_vmem_limit(4 * _nbytes((tr, n), F32)),
        ),
        name="wprep",
    )(w)


def _log_sigmoid(z):
    return jnp.minimum(z, 0.0) - jnp.log1p(jnp.exp(-jnp.abs(z)))


def _proj_kernel(hx_ref, w_ref, w2_ref, b2_ref, k_ref, v_ref, q_ref, sg_ref, u_ref, ldf_ref, ldb_ref):
    hx = hx_ref[...]

    def proj(lo, hi):
        return jnp.dot(hx, w_ref[:, lo:hi], preferred_element_type=F32)

    k_ref[...] = proj(R_K, R_V).astype(BF16)
    v_ref[...] = proj(R_V, R_Q).astype(BF16)
    q_ref[...] = (proj(R_Q, R_G) * (GLA_DK ** -0.5)).astype(BF16)
    sg_ref[...] = _silu(proj(R_G, R_A)).astype(BF16)
    u_ref[...] = proj(R_A, R_B) * jax.nn.sigmoid(proj(R_B, R_GK))
    p_gk = proj(R_GK, R_END).astype(BF16)
    z = jnp.dot(p_gk, w2_ref[...], preferred_element_type=F32) + b2_ref[...]
    ld = _log_sigmoid(z) * (1.0 / GATE_NORMALIZER)
    ldf_ref[...] = ld[:, :GLA_KEY]
    ldb_ref[...] = ld[:, GLA_KEY:]


def _proj_call(hx, w_r, w2pad, b2, name):
    rows, d = hx.shape
    tm = PROJ_TM
    row = lambda n: pl.BlockSpec((tm, n), lambda i: (i, 0))
    whole = lambda a: pl.BlockSpec(a.shape, lambda i: (0, 0), pipeline_mode=pl.Buffered(1))
    outs = [(GLA_KEY, BF16), (D_GLA, BF16), (GLA_KEY, BF16), (D_GLA, BF16), (D_CONV, F32),
            (GLA_KEY, F32), (GLA_KEY, F32)]
    out_bytes = sum(_nbytes((tm, n), dt) for n, dt in outs)
    return pl.pallas_call(
        _proj_kernel,
        grid=(rows // tm,),
        in_specs=[row(d), whole(w_r), whole(w2pad), whole(b2)],
        out_specs=[row(n) for n, _ in outs],
        out_shape=[jax.ShapeDtypeStruct((rows, n), dt) for n, dt in outs],
        compiler_params=pltpu.CompilerParams(
            dimension_semantics=("parallel",),
            vmem_limit_bytes=_vmem_limit(_nbytes(w_r.shape, BF16), 2 * _nbytes((tm, d), BF16),
                                         2 * out_bytes, 3 * _nbytes((tm, D_CONV), F32)),
        ),
        name=name,
    )(hx, w_r, w2pad, b2)


def _ctxstate_kernel(k_ref, v_ref, ldf_ref, ldb_ref, sf_ref, sb_ref):
    t = k_ref.shape[0]
    r = lax.broadcasted_iota(jnp.int32, (t, t), 0)
    c = lax.broadcasted_iota(jnp.int32, (t, t), 1)
    after = (c > r).astype(F32)
    before = (c < r).astype(F32)
    hi = lax.Precision.HIGHEST
    e_f = jnp.dot(after, ldf_ref[...], preferred_element_type=F32, precision=hi)
    e_b = jnp.dot(before, ldb_ref[...], preferred_element_type=F32, precision=hi)
    k = k_ref[...].astype(F32)
    v = v_ref[...]
    tn = (((0,), (0,)), ((), ()))
    sf_ref[...] = lax.dot_general(v, (k * jnp.exp(e_f)).astype(BF16), tn, preferred_element_type=F32)
    sb_ref[...] = lax.dot_general(v, (k * jnp.exp(e_b)).astype(BF16), tn, preferred_element_type=F32)


def _ctxstate_call(k, v, ldf, ldb, bsz, t):
    blk = lambda n: pl.BlockSpec((t, n), lambda b, h: (b, h))
    st = pl.BlockSpec((None, None, GLA_DV, GLA_DK), lambda b, h: (b, h, 0, 0))
    shape = jax.ShapeDtypeStruct((bsz, GLA_HEADS, GLA_DV, GLA_DK), F32)
    return pl.pallas_call(
        _ctxstate_kernel,
        grid=(bsz, GLA_HEADS),
        in_specs=[blk(GLA_DK), blk(GLA_DV), blk(GLA_DK), blk(GLA_DK)],
        out_specs=[st, st],
        out_shape=[shape, shape],
        compiler_params=pltpu.CompilerParams(dimension_semantics=("parallel", "parallel")),
        name="ctxstate",
    )(k, v, ldf, ldb)


def _chunk_cumsum(x, reverse):
    n = x.shape[0]
    pos = lax.broadcasted_iota(jnp.int32, x.shape, 0) % CHUNK
    d = 1
    while d < CHUNK:
        if reverse:
            shifted = pltpu.roll(x, n - d, 0)
            x = x + jnp.where(pos < CHUNK - d, shifted, 0.0)
        else:
            shifted = pltpu.roll(x, d, 0)
            x = x + jnp.where(pos >= d, shifted, 0.0)
        d *= 2
    return x


def _rows_bcast(x, row_in_chunk):
    parts = []
    for c0 in range(0, x.shape[0], CHUNK):
        r = c0 + row_in_chunk
        parts.append(jnp.broadcast_to(x[r:r + 1, :], (CHUNK, x.shape[1])))
    return jnp.concatenate(parts, axis=0)


def _gla_direction(q_ref, k_ref, v_ref, ld_ref, s_ref, reverse):
    nt = (((1,), (1,)), ((), ()))
    tn = (((0,), (0,)), ((), ()))
    n_sub = GLA_TB // GLA_SUB
    n_chunk = GLA_SUB // CHUNK
    rr = lax.broadcasted_iota(jnp.int32, (GLA_SUB, GLA_SUB), 0)
    cc = lax.broadcasted_iota(jnp.int32, (GLA_SUB, GLA_SUB), 1)
    same_chunk = (rr // CHUNK) == (cc // CHUNK)
    mask = same_chunk & ((cc >= rr) if reverse else (cc <= rr))
    mid_row = CHUNK // 2 if reverse else CHUNK // 2 - 1
    last_row = 0 if reverse else CHUNK - 1

    outs = [None] * n_sub
    subs = range(n_sub - 1, -1, -1) if reverse else range(n_sub)
    for s in subs:
        rows = slice(s * GLA_SUB, (s + 1) * GLA_SUB)
        b = _chunk_cumsum(ld_ref[rows, :], reverse)
        b_mid = _rows_bcast(b, mid_row)
        b_last = _rows_bcast(b, last_row)
        q = q_ref[rows, :].astype(F32)
        k = k_ref[rows, :].astype(F32)
        v = v_ref[rows, :]
        qs = (q * jnp.exp(b - b_mid)).astype(BF16)
        ks = (k * jnp.exp(b_mid - b)).astype(BF16)
        qi = (q * jnp.exp(b)).astype(BF16)
        kd = (k * jnp.exp(b_last - b)).astype(BF16)
        att = lax.dot_general(qs, ks, nt, preferred_element_type=F32)
        att = jnp.where(mask, att, 0.0).astype(BF16)
        o_intra = jnp.dot(att, v, preferred_element_type=F32)
        decay = jnp.exp(b_last)
        o_parts = [None] * n_chunk
        chunks = range(n_chunk - 1, -1, -1) if reverse else range(n_chunk)
        for c in chunks:
            cr = slice(c * CHUNK, (c + 1) * CHUNK)
            state = s_ref[...]
            o_parts[c] = o_intra[cr, :] + lax.dot_general(
                qi[cr, :], state.astype(BF16), nt, preferred_element_type=F32)
            kv = lax.dot_general(v[cr, :], kd[cr, :], tn, preferred_element_type=F32)
            s_ref[...] = state * decay[c * CHUNK:c * CHUNK + 1, :] + kv
        outs[s] = jnp.concatenate(o_parts, axis=0)
    return jnp.concatenate(outs, axis=0)


def _gla_kernel(qf_ref, kf_ref, vf_ref, ldf_ref, qb_ref, kb_ref, vb_ref, ldb_ref, sg_ref,
                s0f_ref, s0b_ref, gain_ref, o_ref, sf_ref, sb_ref, acc_ref):
    nb = pl.program_id(2)
    n_blocks = pl.num_programs(2)

    @pl.when(nb == 0)
    def _():
        sf_ref[...] = s0f_ref[...]
        sb_ref[...] = s0b_ref[...]

    o_f = _gla_direction(qf_ref, kf_ref, vf_ref, ldf_ref, sf_ref, reverse=False)
    o_b = _gla_direction(qb_ref, kb_ref, vb_ref, ldb_ref, sb_ref, reverse=True)
    row_f = pl.multiple_of(nb * GLA_TB, GLA_TB)
    row_b = pl.multiple_of((n_blocks - 1 - nb) * GLA_TB, GLA_TB)

    @pl.when(nb < n_blocks // 2)
    def _():
        acc_ref[pl.ds(row_f, GLA_TB), :] = o_f
        acc_ref[pl.ds(row_b, GLA_TB), :] = o_b

    @pl.when(nb >= n_blocks // 2)
    def _():
        for row, part in ((row_f, o_f), (row_b, o_b)):
            o = acc_ref[pl.ds(row, GLA_TB), :] + part
            ms = jnp.mean(o * o, axis=-1, keepdims=True)
            o = o * lax.rsqrt(ms + HEAD_NORM_EPS) * gain_ref[...]
            o_ref[pl.ds(row, GLA_TB), :] = (o * sg_ref[pl.ds(row, GLA_TB), :].astype(F32)).astype(BF16)


def _gla_call(q, k, v, ldf, ldb, sg, s0f, s0b, gain, bsz, t):
    nb = t // GLA_TB
    fwd = lambda n: pl.BlockSpec((GLA_TB, n), lambda b, h, i: (b * nb + i, h))
    bwd = lambda n: pl.BlockSpec((GLA_TB, n), lambda b, h, i: (b * nb + nb - 1 - i, h))
    seq = pl.BlockSpec((t, GLA_DV), lambda b, h, i: (b, h))
    st = pl.BlockSpec((None, None, GLA_DV, GLA_DK), lambda b, h, i: (b, h, 0, 0))
    return pl.pallas_call(
        _gla_kernel,
        grid=(bsz, GLA_HEADS, nb),
        in_specs=[fwd(GLA_DK), fwd(GLA_DK), fwd(GLA_DV), fwd(GLA_DK),
                  bwd(GLA_DK), bwd(GLA_DK), bwd(GLA_DV), bwd(GLA_DK),
                  seq, st, st, pl.BlockSpec((1, GLA_DV), lambda b, h, i: (0, 0))],
        out_specs=seq,
        out_shape=jax.ShapeDtypeStruct((bsz * t, D_GLA), BF16),
        scratch_shapes=[pltpu.VMEM((GLA_DV, GLA_DK), F32), pltpu.VMEM((GLA_DV, GLA_DK), F32),
                        pltpu.VMEM((t, GLA_DV), F32)],
        compiler_params=pltpu.CompilerParams(
            dimension_semantics=("parallel", "parallel", "arbitrary"),
            vmem_limit_bytes=_vmem_limit(4 * _nbytes((t, GLA_DV), BF16), _nbytes((t, GLA_DV), F32),
                                         16 * _nbytes((GLA_TB, GLA_DV), F32)),
        ),
        name="gla",
    )(q, k, v, ldf, q, k, v, ldb, sg, s0f, s0b, gain)


ROW_PAD = 16
ROW_PITCH = GRID_W + 2 * ROW_PAD


def _conv_kernel(n_row_blocks, u_ref, w_ref, b_ref, y_ref, pad_ref):
    cb = pl.program_id(1)
    rows = u_ref.shape[0] // GRID_W
    bias = jnp.broadcast_to(b_ref[...], (GRID_W, CONV_CB))

    @pl.when(cb < n_row_blocks)
    def _():
        pad_ref[...] = jnp.zeros_like(pad_ref)

        def fill(r, carry):
            src = pl.multiple_of(r * GRID_W, GRID_W)
            dst = pl.multiple_of(r * ROW_PITCH + ROW_PAD, 8)
            pad_ref[pl.ds(dst, GRID_W), :] = u_ref[pl.ds(src, GRID_W), :]
            return carry

        lax.fori_loop(0, rows, fill, 0)

        def body(r, carry):
            base = r * ROW_PITCH + (ROW_PAD - CONV_HALF)
            acc = bias
            for j in range(CONV_WIDTH):
                acc = acc + w_ref[j:j + 1, :] * pad_ref[pl.ds(base + j, GRID_W), :]
            y_ref[pl.ds(pl.multiple_of(r * GRID_W, GRID_W), GRID_W), :] = acc
            return carry

        lax.fori_loop(0, rows, body, 0)

    @pl.when(cb >= n_row_blocks)
    def _():
        edge = CONV_HALF * GRID_W
        pad_ref[0:edge, :] = jnp.zeros((edge, CONV_CB), F32)
        pad_ref[edge + rows * GRID_W:2 * edge + rows * GRID_W, :] = jnp.zeros((edge, CONV_CB), F32)
        pad_ref[edge:edge + rows * GRID_W, :] = u_ref[...]

        def body(r, carry):
            acc = bias
            for j in range(CONV_WIDTH):
                src = pl.multiple_of((r + j) * GRID_W, GRID_W)
                acc = acc + w_ref[j:j + 1, :] * pad_ref[pl.ds(src, GRID_W), :]
            y_ref[pl.ds(pl.multiple_of(r * GRID_W, GRID_W), GRID_W), :] = acc
            return carry

        lax.fori_loop(0, rows, body, 0)


def _conv_call(u, w, b, bsz, t):
    ch = u.shape[1]
    rows = t // GRID_W
    n_cb = ch // CONV_CB
    pad_rows = max(rows * ROW_PITCH, (rows + 2 * CONV_HALF) * GRID_W)
    blk = pl.BlockSpec((t, CONV_CB), lambda bi, c: (bi, c))
    return pl.pallas_call(
        functools.partial(_conv_kernel, n_cb // 2),
        grid=(bsz, n_cb),
        in_specs=[blk, pl.BlockSpec((CONV_WIDTH, CONV_CB), lambda bi, c: (0, c)),
                  pl.BlockSpec((1, CONV_CB), lambda bi, c: (0, c))],
        out_specs=blk,
        out_shape=jax.ShapeDtypeStruct(u.shape, F32),
        scratch_shapes=[pltpu.VMEM((pad_rows, CONV_CB), F32)],
        compiler_params=pltpu.CompilerParams(
            dimension_semantics=("parallel", "parallel"),
            vmem_limit_bytes=_vmem_limit(4 * _nbytes((t, CONV_CB), F32), _nbytes((pad_rows, CONV_CB), F32)),
        ),
        name="conv",
    )(u, w, b)


def _outproj_kernel(og_ref, y_ref, lng_ref, lnb_ref, w_ref, h_ref, mod_ref, o_ref):
    y = y_ref[...]
    mu = jnp.mean(y, axis=-1, keepdims=True)
    yc = y - mu
    var = jnp.mean(yc * yc, axis=-1, keepdims=True)
    yn = yc * lax.rsqrt(var + LN_EPS) * lng_ref[...] + lnb_ref[...]
    oc = _silu(yn).astype(BF16)
    res = jnp.dot(og_ref[...], w_ref[0:D_GLA, :], preferred_element_type=F32)
    res = res + jnp.dot(oc, w_ref[D_GLA:D_GLA + D_CONV, :], preferred_element_type=F32)
    o_ref[...] = h_ref[...] + mod_ref[5:6, :] * res


def _outproj_call(og, y, ln_g, ln_b, w_out_bf, h, mod, group_of_tile):
    rows, d = h.shape
    tm = OUT_TM
    row = lambda n: pl.BlockSpec((tm, n), lambda i: (i, 0))
    vec = lambda n: pl.BlockSpec((1, n), lambda i: (0, 0))
    return pl.pallas_call(
        _outproj_kernel,
        grid=(rows // tm,),
        in_specs=[row(D_GLA), row(D_CONV), vec(D_CONV), vec(D_CONV),
                  pl.BlockSpec(w_out_bf.shape, lambda i: (0, 0), pipeline_mode=pl.Buffered(1)),
                  row(d), pl.BlockSpec((None, N_MOD, d), lambda i: (group_of_tile(i), 0, 0))],
        out_specs=row(d),
        out_shape=jax.ShapeDtypeStruct((rows, d), F32),
        compiler_params=pltpu.CompilerParams(
            dimension_semantics=("parallel",),
            vmem_limit_bytes=_vmem_limit(_nbytes(w_out_bf.shape, BF16), 5 * _nbytes((tm, d), F32),
                                         4 * _nbytes((tm, D_CONV), F32)),
        ),
        name="outproj",
    )(og, y, ln_g, ln_b, w_out_bf, h, mod)


def kernel(x, c, ctx, c_ctx, w_mod, b_mod, norm_ffn1, w_ffn1_in, w_ffn1_out, norm_mix, w_in, w_gk2, b_gk2, gla_norm, conv_w, conv_b, conv_ln_g, conv_ln_b, w_out, norm_ffn2, w_ffn2_in, w_ffn2_out, norm_final):
    bsz, t, d = x.shape
    t_ctx = ctx.shape[1]
    assert w_mod.shape[0] == 1, "single layer only"
    assert t % GLA_TB == 0 and (t // GLA_TB) % 2 == 0 and t % FFN_TM == 0

    wf1_in = w_ffn1_in[0].astype(BF16)
    wf1_out = w_ffn1_out[0].astype(BF16)
    wf2_in = w_ffn2_in[0].astype(BF16)
    wf2_out = w_ffn2_out[0].astype(BF16)
    w_out_bf = w_out[0].astype(BF16)
    w_r = _wprep_call(w_in[0])
    w2pad = jnp.zeros((V7X_LANES, 2 * GLA_KEY), BF16)
    w2pad = w2pad.at[:GATE_RANK, :GLA_KEY].set(w_gk2[0, 0].astype(BF16))
    w2pad = w2pad.at[GATE_RANK:2 * GATE_RANK, GLA_KEY:].set(w_gk2[0, 1].astype(BF16))
    b2 = b_gk2[0].reshape(1, 2 * GLA_KEY)
    vec = lambda a: a.reshape(1, -1)

    n_rows = 8
    s_in = jnp.concatenate([c, c_ctx[None, :], jnp.zeros((n_rows - bsz - 1, d), F32)], axis=0)
    mod = _mod_call(s_in, w_mod[0], vec(b_mod[0])).reshape(n_rows, N_MOD, d)

    tiles_per_batch = t // FFN_TM
    lat_group = lambda i: i // tiles_per_batch
    ctx_group = lambda i: bsz

    xl = x.reshape(bsz * t, d)
    xc = ctx.reshape(bsz * t_ctx, d)
    h1, hx = _ffn_call(xl, mod, lat_group, vec(norm_ffn1[0]), vec(norm_mix[0]), wf1_in, wf1_out,
                       epilogue="mix", mod_row=0, name="ffn1")
    _, hxc = _ffn_call(xc, mod, ctx_group, vec(norm_ffn1[0]), vec(norm_mix[0]), wf1_in, wf1_out,
                       epilogue="mix", mod_row=0, name="ffn1_ctx")

    k, v, q, sg, u, ldf, ldb = _proj_call(hx, w_r, w2pad, b2, "proj")
    kc, vc, _, _, _, ldfc, ldbc = _proj_call(hxc, w_r, w2pad, b2, "proj_ctx")
    s0f, s0b = _ctxstate_call(kc, vc, ldfc, ldbc, bsz, t_ctx)

    og = _gla_call(q, k, v, ldf, ldb, sg, s0f, s0b, vec(gla_norm[0]), bsz, t)
    y = _conv_call(u, conv_w[0], vec(conv_b[0]), bsz, t)
    h2 = _outproj_call(og, y, vec(conv_ln_g[0]), vec(conv_ln_b[0]), w_out_bf, h1, mod,
                       lambda i: i // (t // OUT_TM))

    out = _ffn_call(h2, mod, lat_group, vec(norm_ffn2[0]), vec(norm_final), wf2_in, wf2_out,
                    epilogue="final", mod_row=6, name="ffn2")
    return out.reshape(bsz, t, d)
```

```python
import functools

import jax
import jax.numpy as jnp
from jax import lax
from jax.experimental import pallas as pl
from jax.experimental.pallas import tpu as pltpu

F32 = jnp.float32
BF16 = jnp.bfloat16

D_MODEL = 2048
GRID_W = 64
GLA_HEADS = 4
GLA_DK = 128
GLA_DV = 256
GLA_KEY = GLA_HEADS * GLA_DK
D_GLA = GLA_HEADS * GLA_DV
D_CONV = 1024
GATE_RANK = 16
GATE_NORMALIZER = 16.0
CHUNK = 64
CONV_WIDTH = 31
CONV_HALF = CONV_WIDTH // 2
D_FF = 5632
N_MOD = 9
RMS_EPS = 1e-6
HEAD_NORM_EPS = 1e-5
LN_EPS = 1e-5

OFF_V = GLA_KEY
OFF_GKF = OFF_V + D_GLA
CTX_COLS = OFF_GKF + 2 * GATE_RANK
OFF_Q = CTX_COLS
OFF_G = OFF_Q + GLA_KEY
OFF_GLU = OFF_G + D_GLA

R_K = 0
R_V = R_K + GLA_KEY
R_Q = R_V + D_GLA
R_G = R_Q + GLA_KEY
R_A = R_G + D_GLA
R_B = R_A + D_CONV
R_GK = R_B + D_CONV
V7X_LANES = 128
R_END = R_GK + V7X_LANES

V7X_VMEM_SCOPED_LIMIT_BYTES = 60000 * 1024

FFN_TM = 512
FFN_TF = 512
PROJ_TM = 512
OUT_TM = 512
MOD_TN = 1024
GLA_TB = 512
GLA_SUB = 256
CONV_CB = 128
WPREP_TC = 256


def _vmem_limit(*nbytes):
    need = int(sum(nbytes) * 1.25)
    return min(max(need, 16 * 1024 * 1024), V7X_VMEM_SCOPED_LIMIT_BYTES)


def _nbytes(shape, dtype):
    n = 1
    for s in shape:
        n *= s
    return n * jnp.dtype(dtype).itemsize


def _silu(x):
    return x * jax.nn.sigmoid(x)


def _rmsnorm_rows(x, gain, eps):
    ms = jnp.mean(x * x, axis=-1, keepdims=True)
    return x * lax.rsqrt(ms + eps) * gain


def _modulate(x, gain, shift, scale):
    return _rmsnorm_rows(x, gain, RMS_EPS) * (1.0 + scale) + shift


def _mod_kernel(s_ref, w_ref, b_ref, o_ref):
    s = _silu(s_ref[...]).astype(BF16)
    o_ref[...] = jnp.dot(s, w_ref[...].astype(BF16), preferred_element_type=F32) + b_ref[...]


def _mod_call(s_in, w_mod, b_mod):
    rows, d = s_in.shape
    n = w_mod.shape[1]
    return pl.pallas_call(
        _mod_kernel,
        grid=(n // MOD_TN,),
        in_specs=[
            pl.BlockSpec((rows, d), lambda j: (0, 0)),
            pl.BlockSpec((d, MOD_TN), lambda j: (0, j)),
            pl.BlockSpec((1, MOD_TN), lambda j: (0, j)),
        ],
        out_specs=pl.BlockSpec((rows, MOD_TN), lambda j: (0, j)),
        out_shape=jax.ShapeDtypeStruct((rows, n), F32),
        compiler_params=pltpu.CompilerParams(
            dimension_semantics=("arbitrary",),
            vmem_limit_bytes=_vmem_limit(3 * _nbytes((d, MOD_TN), F32)),
        ),
        name="mod",
    )(s_in, w_mod, b_mod)


def _ffn_kernel(epilogue, mod_row, n_cast, x_ref, mod_ref, g_in_ref, g_next_ref, wg_ref, wu_ref, wo_ref,
                *rest):
    cast_in, rest = rest[:n_cast], rest[n_cast:]
    if epilogue == "mix":
        h_ref, hx_ref = rest[:2]
        rest = rest[2:]
    else:
        h_ref = rest[0]
        rest = rest[1:]
    cast_out = rest[:n_cast]
    hm_ref, acc_ref = rest[n_cast:]
    j = pl.program_id(1)
    last = pl.num_programs(1) - 1

    for src_ref, dst_ref in zip(cast_in, cast_out):
        dst_ref[...] = src_ref[...].astype(dst_ref.dtype)

    def partial_out(hm):
        gate = jnp.dot(hm, wg_ref[...], preferred_element_type=F32)
        up = jnp.dot(hm, wu_ref[...], preferred_element_type=F32)
        act = (_silu(gate) * up).astype(BF16)
        return jnp.dot(act, wo_ref[...], preferred_element_type=F32)

    @pl.when(j == 0)
    def _():
        shift = mod_ref[mod_row:mod_row + 1, :]
        scale = mod_ref[mod_row + 1:mod_row + 2, :]
        hm = _modulate(x_ref[...], g_in_ref[...], shift, scale).astype(BF16)
        hm_ref[...] = hm
        acc_ref[...] = partial_out(hm)

    @pl.when((j > 0) & (j < last))
    def _():
        acc_ref[...] += partial_out(hm_ref[...])

    @pl.when(j == last)
    def _():
        g = mod_ref[mod_row + 2:mod_row + 3, :]
        h = x_ref[...] + (0.5 * g) * (acc_ref[...] + partial_out(hm_ref[...]))
        if epilogue == "mix":
            h_ref[...] = h
            shift = mod_ref[mod_row + 3:mod_row + 4, :]
            scale = mod_ref[mod_row + 4:mod_row + 5, :]
            hx_ref[...] = _modulate(h, g_next_ref[...], shift, scale).astype(BF16)
        else:
            h_ref[...] = _rmsnorm_rows(h, g_next_ref[...], RMS_EPS)


def _ffn_call(x, mod, group_of_tile, g_in, g_next, w_in_bf, w_out_bf, *, epilogue, mod_row, name,
              cast_jobs=()):
    rows, d = x.shape
    n_ff = w_out_bf.shape[0]
    n_j = n_ff // FFN_TF
    assert n_j >= 2, "first and last hidden block must be distinct grid steps"
    grid = (rows // FFN_TM, n_j)
    row_spec = pl.BlockSpec((FFN_TM, d), lambda i, j: (i, 0))
    vec_spec = pl.BlockSpec((1, d), lambda i, j: (0, 0))
    cast_specs = [pl.BlockSpec(blk, imap) for _, blk, imap in cast_jobs]
    in_specs = [
        row_spec,
        pl.BlockSpec((None, N_MOD, d), lambda i, j: (group_of_tile(i), 0, 0)),
        vec_spec,
        vec_spec,
        pl.BlockSpec((d, FFN_TF), lambda i, j: (0, j)),
        pl.BlockSpec((d, FFN_TF), lambda i, j: (0, n_j + j)),
        pl.BlockSpec((FFN_TF, d), lambda i, j: (j, 0)),
    ] + cast_specs
    out_specs = [row_spec]
    out_shape = [jax.ShapeDtypeStruct((rows, d), F32)]
    if epilogue == "mix":
        out_specs.append(row_spec)
        out_shape.append(jax.ShapeDtypeStruct((rows, d), BF16))
    out_specs += cast_specs
    out_shape += [jax.ShapeDtypeStruct(a.shape, BF16) for a, _, _ in cast_jobs]
    tile_f32 = _nbytes((FFN_TM, d), F32)
    w_blk = _nbytes((d, FFN_TF), BF16)
    cast_bytes = sum(3 * _nbytes(blk, F32) for _, blk, _ in cast_jobs)
    return pl.pallas_call(
        functools.partial(_ffn_kernel, epilogue, mod_row, len(cast_jobs)),
        grid=grid,
        in_specs=in_specs,
        out_specs=out_specs,
        out_shape=out_shape,
        scratch_shapes=[pltpu.VMEM((FFN_TM, d), BF16), pltpu.VMEM((FFN_TM, d), F32)],
        compiler_params=pltpu.CompilerParams(
            dimension_semantics=("parallel", "arbitrary"),
            vmem_limit_bytes=_vmem_limit(6 * tile_f32, 6 * w_blk, 4 * _nbytes((FFN_TM, FFN_TF), F32),
                                         cast_bytes),
        ),
        name=name,
    )(x, mod, g_in, g_next, w_in_bf, w_in_bf, w_out_bf, *[a for a, _, _ in cast_jobs])


def _wprep_kernel(w_ref, o_ref):
    o_ref[R_K:R_Q, :] = w_ref[0:OFF_GKF, :].astype(BF16)
    o_ref[R_Q:R_GK, :] = w_ref[OFF_Q:OFF_GLU + 2 * D_CONV, :].astype(BF16)
    o_ref[R_GK:R_GK + 2 * GATE_RANK, :] = w_ref[OFF_GKF:CTX_COLS, :].astype(BF16)
    o_ref[R_GK + 2 * GATE_RANK:R_END, :] = jnp.zeros((V7X_LANES - 2 * GATE_RANK, o_ref.shape[1]), BF16)


def _wprep_call(w_t):
    n, d = w_t.shape
    tc = WPREP_TC
    return pl.pallas_call(
        _wprep_kernel,
        grid=(d // tc,),
        in_specs=[pl.BlockSpec((n, tc), lambda i: (0, i))],
        out_specs=pl.BlockSpec((R_END, tc), lambda i: (0, i)),
        out_shape=jax.ShapeDtypeStruct((R_END, d), BF16),
        compiler_params=pltpu.CompilerParams(
            dimension_semantics=("parallel",),
            vmem_limit_bytes=_vmem_limit(3 * _nbytes((n, tc), F32)),
        ),
        name="wprep",
    )(w_t)


def _log_sigmoid(z):
    return jnp.minimum(z, 0.0) - jnp.log1p(jnp.exp(-jnp.abs(z)))


def _proj_kernel(hx_ref, w_ref, w2_ref, b2_ref, k_ref, v_ref, q_ref, sg_ref, u_ref, ldf_ref, ldb_ref):
    hx = hx_ref[...]

    def proj(lo, hi):
        return lax.dot_general(hx, w_ref[lo:hi, :], (((1,), (1,)), ((), ())), preferred_element_type=F32)

    k_ref[...] = proj(R_K, R_V).astype(BF16)
    v_ref[...] = proj(R_V, R_Q).astype(BF16)
    q_ref[...] = (proj(R_Q, R_G) * (GLA_DK ** -0.5)).astype(BF16)
    sg_ref[...] = _silu(proj(R_G, R_A)).astype(BF16)
    u_ref[...] = proj(R_A, R_B) * jax.nn.sigmoid(proj(R_B, R_GK))
    p_gk = proj(R_GK, R_END).astype(BF16)
    z = jnp.dot(p_gk, w2_ref[...], preferred_element_type=F32) + b2_ref[...]
    ld = _log_sigmoid(z) * (1.0 / GATE_NORMALIZER)
    ldf_ref[...] = ld[:, :GLA_KEY]
    ldb_ref[...] = ld[:, GLA_KEY:]


def _proj_call(hx, w_r, w2pad, b2, name):
    rows, d = hx.shape
    tm = PROJ_TM
    row = lambda n: pl.BlockSpec((tm, n), lambda i: (i, 0))
    whole = lambda a: pl.BlockSpec(a.shape, lambda i: (0, 0), pipeline_mode=pl.Buffered(1))
    outs = [(GLA_KEY, BF16), (D_GLA, BF16), (GLA_KEY, BF16), (D_GLA, BF16), (D_CONV, F32),
            (GLA_KEY, F32), (GLA_KEY, F32)]
    out_bytes = sum(_nbytes((tm, n), dt) for n, dt in outs)
    return pl.pallas_call(
        _proj_kernel,
        grid=(rows // tm,),
        in_specs=[row(d), whole(w_r), whole(w2pad), whole(b2)],
        out_specs=[row(n) for n, _ in outs],
        out_shape=[jax.ShapeDtypeStruct((rows, n), dt) for n, dt in outs],
        compiler_params=pltpu.CompilerParams(
            dimension_semantics=("parallel",),
            vmem_limit_bytes=_vmem_limit(_nbytes(w_r.shape, BF16), 2 * _nbytes((tm, d), BF16),
                                         2 * out_bytes, 3 * _nbytes((tm, D_CONV), F32)),
        ),
        name=name,
    )(hx, w_r, w2pad, b2)


def _ctxstate_kernel(k_ref, v_ref, ldf_ref, ldb_ref, sf_ref, sb_ref):
    t = k_ref.shape[0]
    r = lax.broadcasted_iota(jnp.int32, (t, t), 0)
    c = lax.broadcasted_iota(jnp.int32, (t, t), 1)
    after = (c > r).astype(F32)
    before = (c < r).astype(F32)
    hi = lax.Precision.HIGHEST
    e_f = jnp.dot(after, ldf_ref[...], preferred_element_type=F32, precision=hi)
    e_b = jnp.dot(before, ldb_ref[...], preferred_element_type=F32, precision=hi)
    k = k_ref[...].astype(F32)
    v = v_ref[...]
    tn = (((0,), (0,)), ((), ()))
    sf_ref[...] = lax.dot_general(v, (k * jnp.exp(e_f)).astype(BF16), tn, preferred_element_type=F32)
    sb_ref[...] = lax.dot_general(v, (k * jnp.exp(e_b)).astype(BF16), tn, preferred_element_type=F32)


def _ctxstate_call(k, v, ldf, ldb, bsz, t):
    blk = lambda n: pl.BlockSpec((t, n), lambda b, h: (b, h))
    st = pl.BlockSpec((None, None, GLA_DV, GLA_DK), lambda b, h: (b, h, 0, 0))
    shape = jax.ShapeDtypeStruct((bsz, GLA_HEADS, GLA_DV, GLA_DK), F32)
    return pl.pallas_call(
        _ctxstate_kernel,
        grid=(bsz, GLA_HEADS),
        in_specs=[blk(GLA_DK), blk(GLA_DV), blk(GLA_DK), blk(GLA_DK)],
        out_specs=[st, st],
        out_shape=[shape, shape],
        compiler_params=pltpu.CompilerParams(dimension_semantics=("parallel", "parallel")),
        name="ctxstate",
    )(k, v, ldf, ldb)


def _chunk_cumsum(x, reverse):
    n = x.shape[0]
    pos = lax.broadcasted_iota(jnp.int32, x.shape, 0) % CHUNK
    d = 1
    while d < CHUNK:
        if reverse:
            shifted = pltpu.roll(x, n - d, 0)
            x = x + jnp.where(pos < CHUNK - d, shifted, 0.0)
        else:
            shifted = pltpu.roll(x, d, 0)
            x = x + jnp.where(pos >= d, shifted, 0.0)
        d *= 2
    return x


def _rows_bcast(x, row_in_chunk):
    parts = []
    for c0 in range(0, x.shape[0], CHUNK):
        r = c0 + row_in_chunk
        parts.append(jnp.broadcast_to(x[r:r + 1, :], (CHUNK, x.shape[1])))
    return jnp.concatenate(parts, axis=0)


def _gla_direction(q_ref, k_ref, v_ref, ld_ref, s_ref, reverse):
    nt = (((1,), (1,)), ((), ()))
    tn = (((0,), (0,)), ((), ()))
    n_sub = GLA_TB // GLA_SUB
    n_chunk = GLA_SUB // CHUNK
    rr = lax.broadcasted_iota(jnp.int32, (GLA_SUB, GLA_SUB), 0)
    cc = lax.broadcasted_iota(jnp.int32, (GLA_SUB, GLA_SUB), 1)
    same_chunk = (rr // CHUNK) == (cc // CHUNK)
    mask = same_chunk & ((cc >= rr) if reverse else (cc <= rr))
    mid_row = CHUNK // 2 if reverse else CHUNK // 2 - 1
    last_row = 0 if reverse else CHUNK - 1

    outs = [None] * n_sub
    subs = range(n_sub - 1, -1, -1) if reverse else range(n_sub)
    for s in subs:
        rows = slice(s * GLA_SUB, (s + 1) * GLA_SUB)
        b = _chunk_cumsum(ld_ref[rows, :], reverse)
        b_mid = _rows_bcast(b, mid_row)
        b_last = _rows_bcast(b, last_row)
        q = q_ref[rows, :].astype(F32)
        k = k_ref[rows, :].astype(F32)
        v = v_ref[rows, :]
        qs = (q * jnp.exp(b - b_mid)).astype(BF16)
        ks = (k * jnp.exp(b_mid - b)).astype(BF16)
        qi = (q * jnp.exp(b)).astype(BF16)
        kd = (k * jnp.exp(b_last - b)).astype(BF16)
        att = lax.dot_general(qs, ks, nt, preferred_element_type=F32)
        att = jnp.where(mask, att, 0.0).astype(BF16)
        o_intra = jnp.dot(att, v, preferred_element_type=F32)
        decay = jnp.exp(b_last)
        o_parts = [None] * n_chunk
        chunks = range(n_chunk - 1, -1, -1) if reverse else range(n_chunk)
        for c in chunks:
            cr = slice(c * CHUNK, (c + 1) * CHUNK)
            state = s_ref[...]
            o_parts[c] = o_intra[cr, :] + lax.dot_general(
                qi[cr, :], state.astype(BF16), nt, preferred_element_type=F32)
            kv = lax.dot_general(v[cr, :], kd[cr, :], tn, preferred_element_type=F32)
            s_ref[...] = state * decay[c * CHUNK:c * CHUNK + 1, :] + kv
        outs[s] = jnp.concatenate(o_parts, axis=0)
    return jnp.concatenate(outs, axis=0)


def _gla_kernel(qf_ref, kf_ref, vf_ref, ldf_ref, qb_ref, kb_ref, vb_ref, ldb_ref, sg_ref,
                s0f_ref, s0b_ref, gain_ref, o_ref, sf_ref, sb_ref, acc_ref):
    nb = pl.program_id(2)
    n_blocks = pl.num_programs(2)

    @pl.when(nb == 0)
    def _():
        sf_ref[...] = s0f_ref[...]
        sb_ref[...] = s0b_ref[...]

    o_f = _gla_direction(qf_ref, kf_ref, vf_ref, ldf_ref, sf_ref, reverse=False)
    o_b = _gla_direction(qb_ref, kb_ref, vb_ref, ldb_ref, sb_ref, reverse=True)
    row_f = pl.multiple_of(nb * GLA_TB, GLA_TB)
    row_b = pl.multiple_of((n_blocks - 1 - nb) * GLA_TB, GLA_TB)

    @pl.when(nb < n_blocks // 2)
    def _():
        acc_ref[pl.ds(row_f, GLA_TB), :] = o_f
        acc_ref[pl.ds(row_b, GLA_TB), :] = o_b

    @pl.when(nb >= n_blocks // 2)
    def _():
        for row, part in ((row_f, o_f), (row_b, o_b)):
            o = acc_ref[pl.ds(row, GLA_TB), :] + part
            ms = jnp.mean(o * o, axis=-1, keepdims=True)
            o = o * lax.rsqrt(ms + HEAD_NORM_EPS) * gain_ref[...]
            o_ref[pl.ds(row, GLA_TB), :] = (o * sg_ref[pl.ds(row, GLA_TB), :].astype(F32)).astype(BF16)


def _gla_call(q, k, v, ldf, ldb, sg, s0f, s0b, gain, bsz, t):
    nb = t // GLA_TB
    fwd = lambda n: pl.BlockSpec((GLA_TB, n), lambda b, h, i: (b * nb + i, h))
    bwd = lambda n: pl.BlockSpec((GLA_TB, n), lambda b, h, i: (b * nb + nb - 1 - i, h))
    seq = pl.BlockSpec((t, GLA_DV), lambda b, h, i: (b, h))
    st = pl.BlockSpec((None, None, GLA_DV, GLA_DK), lambda b, h, i: (b, h, 0, 0))
    return pl.pallas_call(
        _gla_kernel,
        grid=(bsz, GLA_HEADS, nb),
        in_specs=[fwd(GLA_DK), fwd(GLA_DK), fwd(GLA_DV), fwd(GLA_DK),
                  bwd(GLA_DK), bwd(GLA_DK), bwd(GLA_DV), bwd(GLA_DK),
                  seq, st, st, pl.BlockSpec((1, GLA_DV), lambda b, h, i: (0, 0))],
        out_specs=seq,
        out_shape=jax.ShapeDtypeStruct((bsz * t, D_GLA), BF16),
        scratch_shapes=[pltpu.VMEM((GLA_DV, GLA_DK), F32), pltpu.VMEM((GLA_DV, GLA_DK), F32),
                        pltpu.VMEM((t, GLA_DV), F32)],
        compiler_params=pltpu.CompilerParams(
            dimension_semantics=("parallel", "parallel", "arbitrary"),
            vmem_limit_bytes=_vmem_limit(4 * _nbytes((t, GLA_DV), BF16), _nbytes((t, GLA_DV), F32),
                                         16 * _nbytes((GLA_TB, GLA_DV), F32)),
        ),
        name="gla",
    )(q, k, v, ldf, q, k, v, ldb, sg, s0f, s0b, gain)


ROW_PAD = 16
ROW_PITCH = GRID_W + 2 * ROW_PAD


def _conv_kernel(n_row_blocks, u_ref, w_ref, b_ref, y_ref, pad_ref):
    cb = pl.program_id(1)
    rows = u_ref.shape[0] // GRID_W
    bias = jnp.broadcast_to(b_ref[...], (GRID_W, CONV_CB))

    @pl.when(cb < n_row_blocks)
    def _():
        pad_ref[...] = jnp.zeros_like(pad_ref)

        def fill(r, carry):
            src = pl.multiple_of(r * GRID_W, GRID_W)
            dst = pl.multiple_of(r * ROW_PITCH + ROW_PAD, 8)
            pad_ref[pl.ds(dst, GRID_W), :] = u_ref[pl.ds(src, GRID_W), :]
            return carry

        lax.fori_loop(0, rows, fill, 0)

        def body(r, carry):
            base = r * ROW_PITCH + (ROW_PAD - CONV_HALF)
            acc = bias
            for j in range(CONV_WIDTH):
                acc = acc + w_ref[j:j + 1, :] * pad_ref[pl.ds(base + j, GRID_W), :]
            y_ref[pl.ds(pl.multiple_of(r * GRID_W, GRID_W), GRID_W), :] = acc
            return carry

        lax.fori_loop(0, rows, body, 0)

    @pl.when(cb >= n_row_blocks)
    def _():
        edge = CONV_HALF * GRID_W
        pad_ref[0:edge, :] = jnp.zeros((edge, CONV_CB), F32)
        pad_ref[edge + rows * GRID_W:2 * edge + rows * GRID_W, :] = jnp.zeros((edge, CONV_CB), F32)
        pad_ref[edge:edge + rows * GRID_W, :] = u_ref[...]

        def body(r, carry):
            acc = bias
            for j in range(CONV_WIDTH):
                src = pl.multiple_of((r + j) * GRID_W, GRID_W)
                acc = acc + w_ref[j:j + 1, :] * pad_ref[pl.ds(src, GRID_W), :]
            y_ref[pl.ds(pl.multiple_of(r * GRID_W, GRID_W), GRID_W), :] = acc
            return carry

        lax.fori_loop(0, rows, body, 0)


def _conv_call(u, w, b, bsz, t):
    ch = u.shape[1]
    rows = t // GRID_W
    n_cb = ch // CONV_CB
    pad_rows = max(rows * ROW_PITCH, (rows + 2 * CONV_HALF) * GRID_W)
    blk = pl.BlockSpec((t, CONV_CB), lambda bi, c: (bi, c))
    return pl.pallas_call(
        functools.partial(_conv_kernel, n_cb // 2),
        grid=(bsz, n_cb),
        in_specs=[blk, pl.BlockSpec((CONV_WIDTH, CONV_CB), lambda bi, c: (0, c)),
                  pl.BlockSpec((1, CONV_CB), lambda bi, c: (0, c))],
        out_specs=blk,
        out_shape=jax.ShapeDtypeStruct(u.shape, F32),
        scratch_shapes=[pltpu.VMEM((pad_rows, CONV_CB), F32)],
        compiler_params=pltpu.CompilerParams(
            dimension_semantics=("parallel", "parallel"),
            vmem_limit_bytes=_vmem_limit(4 * _nbytes((t, CONV_CB), F32), _nbytes((pad_rows, CONV_CB), F32)),
        ),
        name="conv",
    )(u, w, b)


def _outproj_kernel(og_ref, y_ref, lng_ref, lnb_ref, w_ref, h_ref, mod_ref, o_ref):
    y = y_ref[...]
    mu = jnp.mean(y, axis=-1, keepdims=True)
    yc = y - mu
    var = jnp.mean(yc * yc, axis=-1, keepdims=True)
    yn = yc * lax.rsqrt(var + LN_EPS) * lng_ref[...] + lnb_ref[...]
    oc = _silu(yn).astype(BF16)
    res = jnp.dot(og_ref[...], w_ref[0:D_GLA, :], preferred_element_type=F32)
    res = res + jnp.dot(oc, w_ref[D_GLA:D_GLA + D_CONV, :], preferred_element_type=F32)
    o_ref[...] = h_ref[...] + mod_ref[5:6, :] * res


def _outproj_call(og, y, ln_g, ln_b, w_out_bf, h, mod, group_of_tile):
    rows, d = h.shape
    tm = OUT_TM
    row = lambda n: pl.BlockSpec((tm, n), lambda i: (i, 0))
    vec = lambda n: pl.BlockSpec((1, n), lambda i: (0, 0))
    return pl.pallas_call(
        _outproj_kernel,
        grid=(rows // tm,),
        in_specs=[row(D_GLA), row(D_CONV), vec(D_CONV), vec(D_CONV),
                  pl.BlockSpec(w_out_bf.shape, lambda i: (0, 0), pipeline_mode=pl.Buffered(1)),
                  row(d), pl.BlockSpec((None, N_MOD, d), lambda i: (group_of_tile(i), 0, 0))],
        out_specs=row(d),
        out_shape=jax.ShapeDtypeStruct((rows, d), F32),
        compiler_params=pltpu.CompilerParams(
            dimension_semantics=("parallel",),
            vmem_limit_bytes=_vmem_limit(_nbytes(w_out_bf.shape, BF16), 5 * _nbytes((tm, d), F32),
                                         4 * _nbytes((tm, D_CONV), F32)),
        ),
        name="outproj",
    )(og, y, ln_g, ln_b, w_out_bf, h, mod)


def kernel(x, c, ctx, c_ctx, w_mod, b_mod, norm_ffn1, w_ffn1_in, w_ffn1_out, norm_mix, w_in, w_gk2, b_gk2, gla_norm, conv_w, conv_b, conv_ln_g, conv_ln_b, w_out, norm_ffn2, w_ffn2_in, w_ffn2_out, norm_final):
    bsz, t, d = x.shape
    t_ctx = ctx.shape[1]
    assert w_mod.shape[0] == 1, "single layer only"
    assert t % GLA_TB == 0 and (t // GLA_TB) % 2 == 0 and t % FFN_TM == 0

    wf1_in = w_ffn1_in[0].astype(BF16)
    wf1_out = w_ffn1_out[0].astype(BF16)
    w_r = _wprep_call(jnp.swapaxes(w_in[0], 0, 1))
    w2pad = jnp.zeros((V7X_LANES, 2 * GLA_KEY), BF16)
    w2pad = w2pad.at[:GATE_RANK, :GLA_KEY].set(w_gk2[0, 0].astype(BF16))
    w2pad = w2pad.at[GATE_RANK:2 * GATE_RANK, GLA_KEY:].set(w_gk2[0, 1].astype(BF16))
    b2 = b_gk2[0].reshape(1, 2 * GLA_KEY)
    vec = lambda a: a.reshape(1, -1)

    n_rows = 8
    s_in = jnp.concatenate([c, c_ctx[None, :], jnp.zeros((n_rows - bsz - 1, d), F32)], axis=0)
    mod = _mod_call(s_in, w_mod[0], vec(b_mod[0])).reshape(n_rows, N_MOD, d)

    tiles_per_batch = t // FFN_TM
    lat_group = lambda i: i // tiles_per_batch
    ctx_group = lambda i: bsz

    xl = x.reshape(bsz * t, d)
    xc = ctx.reshape(bsz * t_ctx, d)
    n_i, n_j = (bsz * t) // FFN_TM, D_FF // FFN_TF
    assert d % n_i == 0 and (2 * D_FF) % n_j == 0
    cast_jobs = (
        (w_ffn2_in[0], (d // n_i, 2 * D_FF // n_j), lambda i, j: (i, j)),
        (w_ffn2_out[0], (FFN_TF, d // n_i), lambda i, j: (j, i)),
        (w_out[0], (d // n_i, d), lambda i, j: (i, 0)),
    )
    h1, hx, wf2_in, wf2_out, w_out_bf = _ffn_call(
        xl, mod, lat_group, vec(norm_ffn1[0]), vec(norm_mix[0]), wf1_in, wf1_out,
        epilogue="mix", mod_row=0, name="ffn1", cast_jobs=cast_jobs)
    _, hxc = _ffn_call(xc, mod, ctx_group, vec(norm_ffn1[0]), vec(norm_mix[0]), wf1_in, wf1_out,
                       epilogue="mix", mod_row=0, name="ffn1_ctx")

    k, v, q, sg, u, ldf, ldb = _proj_call(hx, w_r, w2pad, b2, "proj")
    kc, vc, _, _, _, ldfc, ldbc = _proj_call(hxc, w_r, w2pad, b2, "proj_ctx")
    s0f, s0b = _ctxstate_call(kc, vc, ldfc, ldbc, bsz, t_ctx)

    og = _gla_call(q, k, v, ldf, ldb, sg, s0f, s0b, vec(gla_norm[0]), bsz, t)
    y = _conv_call(u, conv_w[0], vec(conv_b[0]), bsz, t)
    h2 = _outproj_call(og, y, vec(conv_ln_g[0]), vec(conv_ln_b[0]), w_out_bf, h1, mod,
                       lambda i: i // (t // OUT_TM))

    (out,) = _ffn_call(h2, mod, lat_group, vec(norm_ffn2[0]), vec(norm_final), wf2_in, wf2_out,
                       epilogue="final", mod_row=6, name="ffn2")
    return out.reshape(bsz, t, d)
```

```python
import functools
from typing import NamedTuple

import jax
import jax.numpy as jnp
from jax import lax
from jax.experimental import pallas as pl
from jax.experimental.pallas import tpu as pltpu

F32 = jnp.float32
BF16 = jnp.bfloat16

D_MODEL = 2048
GRID_W = 64
GLA_HEADS = 4
GLA_DK = 128
GLA_DV = 256
GLA_KEY = GLA_HEADS * GLA_DK
D_GLA = GLA_HEADS * GLA_DV
D_CONV = 1024
GATE_RANK = 16
GATE_NORMALIZER = 16.0
CHUNK = 64
CONV_WIDTH = 31
CONV_HALF = CONV_WIDTH // 2
D_FF = 5632
N_MOD = 9
RMS_EPS = 1e-6
HEAD_NORM_EPS = 1e-5
LN_EPS = 1e-5

OFF_V = GLA_KEY
OFF_GKF = OFF_V + D_GLA
CTX_COLS = OFF_GKF + 2 * GATE_RANK
OFF_Q = CTX_COLS
OFF_G = OFF_Q + GLA_KEY
OFF_GLU = OFF_G + D_GLA

V7X_LANES = 128
R_K = 0
R_V = R_K + GLA_KEY
R_Q = R_V + D_GLA
R_G = R_Q + GLA_KEY
R_A = R_G + D_GLA
R_B = R_A + D_CONV
R_GK = R_B + D_CONV
R_END = R_GK + V7X_LANES

V7X_VMEM_SCOPED_LIMIT_BYTES = 60000 * 1024

FFN_TM = 512
FFN_TF = 512
FFN_TF_CAST = 256
PROJ_TM = 512
OUT_TM = 512
MOD_TN = 1024
GLA_TB = 512
GLA_SUB = 256
CONV_CB = 128
WPREP_TC = 256

NT_DIMS = (((1,), (1,)), ((), ()))
TN_DIMS = (((0,), (0,)), ((), ()))


def _params(*semantics):
    return pltpu.CompilerParams(dimension_semantics=semantics,
                                vmem_limit_bytes=V7X_VMEM_SCOPED_LIMIT_BYTES)


def _silu(x):
    return x * jax.nn.sigmoid(x)


def _rmsnorm_rows(x, gain, eps):
    ms = jnp.mean(x * x, axis=-1, keepdims=True)
    return x * lax.rsqrt(ms + eps) * gain


def _modulate(x, gain, shift, scale):
    return _rmsnorm_rows(x, gain, RMS_EPS) * (1.0 + scale) + shift


def _log_sigmoid(z):
    return jnp.minimum(z, 0.0) - jnp.log1p(jnp.exp(-jnp.abs(z)))


def _mod_kernel(s_ref, w_ref, b_ref, o_ref):
    s = _silu(s_ref[...]).astype(BF16)
    o_ref[...] = jnp.dot(s, w_ref[...].astype(BF16), preferred_element_type=F32) + b_ref[...]


def _mod_call(s_in, w_mod, b_mod):
    rows, d = s_in.shape
    n = w_mod.shape[1]
    return pl.pallas_call(
        _mod_kernel,
        grid=(n // MOD_TN,),
        in_specs=[
            pl.BlockSpec((rows, d), lambda j: (0, 0)),
            pl.BlockSpec((d, MOD_TN), lambda j: (0, j)),
            pl.BlockSpec((1, MOD_TN), lambda j: (0, j)),
        ],
        out_specs=pl.BlockSpec((rows, MOD_TN), lambda j: (0, j)),
        out_shape=jax.ShapeDtypeStruct((rows, n), F32),
        compiler_params=_params("arbitrary"),
        name="mod",
    )(s_in, w_mod, b_mod)


class _FfnCfg(NamedTuple):
    mod_row: int
    emit_h: bool
    emit_hx: bool
    final_norm: bool
    cast_w: bool
    n_cast: int


def _ffn_kernel(cfg, x_ref, mod_ref, g_in_ref, g_next_ref, wg_ref, wu_ref, wo_ref, *rest):
    cast_in, rest = rest[:cfg.n_cast], rest[cfg.n_cast:]
    outs = []
    for flag in (cfg.emit_h, cfg.emit_hx, cfg.cast_w, cfg.cast_w, cfg.cast_w):
        outs.append(rest[0] if flag else None)
        rest = rest[1:] if flag else rest
    h_ref, hx_ref, wg_bf_ref, wu_bf_ref, wo_bf_ref = outs
    cast_out, (hm_ref, acc_ref) = rest[:cfg.n_cast], rest[cfg.n_cast:]
    j = pl.program_id(1)
    last = pl.num_programs(1) - 1
    r0 = cfg.mod_row

    for src_ref, dst_ref in zip(cast_in, cast_out):
        dst_ref[...] = src_ref[...].astype(dst_ref.dtype)

    def partial_out(hm):
        wg, wu, wo = wg_ref[...], wu_ref[...], wo_ref[...]
        if cfg.cast_w:
            wg, wu, wo = wg.astype(BF16), wu.astype(BF16), wo.astype(BF16)
            wg_bf_ref[...] = wg
            wu_bf_ref[...] = wu
            wo_bf_ref[...] = wo
        gate = jnp.dot(hm, wg, preferred_element_type=F32)
        up = jnp.dot(hm, wu, preferred_element_type=F32)
        act = (_silu(gate) * up).astype(BF16)
        return jnp.dot(act, wo, preferred_element_type=F32)

    @pl.when(j == 0)
    def _():
        hm = _modulate(x_ref[...], g_in_ref[...], mod_ref[r0:r0 + 1, :], mod_ref[r0 + 1:r0 + 2, :])
        hm = hm.astype(BF16)
        hm_ref[...] = hm
        acc_ref[...] = partial_out(hm)

    @pl.when((j > 0) & (j < last))
    def _():
        acc_ref[...] += partial_out(hm_ref[...])

    @pl.when(j == last)
    def _():
        g = mod_ref[r0 + 2:r0 + 3, :]
        h = x_ref[...] + (0.5 * g) * (acc_ref[...] + partial_out(hm_ref[...]))
        if cfg.emit_hx:
            hx = _modulate(h, g_next_ref[...], mod_ref[r0 + 3:r0 + 4, :], mod_ref[r0 + 4:r0 + 5, :])
            hx_ref[...] = hx.astype(BF16)
        if cfg.emit_h:
            h_ref[...] = _rmsnorm_rows(h, g_next_ref[...], RMS_EPS) if cfg.final_norm else h


def _ffn_call(x, mod, group_of_tile, g_in, g_next, wg, wu, wo, up_col0, cfg, name, cast_jobs=()):
    rows, d = x.shape
    n_ff = wo.shape[0]
    tf = FFN_TF_CAST if cfg.cast_w else FFN_TF
    n_j = n_ff // tf
    assert n_j >= 2, "first and last hidden block must be distinct grid steps"
    assert not cfg.cast_w or rows == FFN_TM, "bf16 weight copies are written by one row tile only"
    assert cfg.n_cast == len(cast_jobs)
    up_blk0 = up_col0 // tf
    row_spec = pl.BlockSpec((FFN_TM, d), lambda i, j: (i, 0))
    vec_spec = pl.BlockSpec((1, d), lambda i, j: (0, 0))
    wg_spec = pl.BlockSpec((d, tf), lambda i, j: (0, j))
    wu_spec = pl.BlockSpec((d, tf), lambda i, j: (0, up_blk0 + j))
    wo_spec = pl.BlockSpec((tf, d), lambda i, j: (j, 0))
    cast_specs = [pl.BlockSpec(blk, imap) for _, blk, imap in cast_jobs]
    in_specs = [row_spec, pl.BlockSpec((None, N_MOD, d), lambda i, j: (group_of_tile(i), 0, 0)),
                vec_spec, vec_spec, wg_spec, wu_spec, wo_spec] + cast_specs
    out_specs, out_shape = [], []
    if cfg.emit_h:
        out_specs.append(row_spec)
        out_shape.append(jax.ShapeDtypeStruct((rows, d), F32))
    if cfg.emit_hx:
        out_specs.append(row_spec)
        out_shape.append(jax.ShapeDtypeStruct((rows, d), BF16))
    if cfg.cast_w:
        out_specs += [wg_spec, pl.BlockSpec((d, tf), lambda i, j: (0, j)), wo_spec]
        out_shape += [jax.ShapeDtypeStruct((d, n_ff), BF16), jax.ShapeDtypeStruct((d, n_ff), BF16),
                      jax.ShapeDtypeStruct((n_ff, d), BF16)]
    out_specs += cast_specs
    out_shape += [jax.ShapeDtypeStruct(a.shape, BF16) for a, _, _ in cast_jobs]
    return pl.pallas_call(
        functools.partial(_ffn_kernel, cfg),
        grid=(rows // FFN_TM, n_j),
        in_specs=in_specs,
        out_specs=out_specs,
        out_shape=out_shape,
        scratch_shapes=[pltpu.VMEM((FFN_TM, d), BF16), pltpu.VMEM((FFN_TM, d), F32)],
        compiler_params=_params("parallel", "arbitrary"),
        name=name,
    )(x, mod, g_in, g_next, wg, wu, wo, *[a for a, _, _ in cast_jobs])


def _wprep_kernel(w_ref, o_ref):
    o_ref[R_K:R_Q, :] = w_ref[0:OFF_GKF, :].astype(BF16)
    o_ref[R_Q:R_GK, :] = w_ref[OFF_Q:OFF_GLU + 2 * D_CONV, :].astype(BF16)
    o_ref[R_GK:R_GK + 2 * GATE_RANK, :] = w_ref[OFF_GKF:CTX_COLS, :].astype(BF16)
    o_ref[R_GK + 2 * GATE_RANK:R_END, :] = jnp.zeros((V7X_LANES - 2 * GATE_RANK, o_ref.shape[1]), BF16)


def _wprep_call(w_t):
    n, d = w_t.shape
    tc = WPREP_TC
    return pl.pallas_call(
        _wprep_kernel,
        grid=(d // tc,),
        in_specs=[pl.BlockSpec((n, tc), lambda i: (0, i))],
        out_specs=pl.BlockSpec((R_END, tc), lambda i: (0, i)),
        out_shape=jax.ShapeDtypeStruct((R_END, d), BF16),
        compiler_params=_params("parallel"),
        name="wprep",
    )(w_t)


def _log_decays(p_gk, w2_ref, b2_ref):
    z = jnp.dot(p_gk.astype(BF16), w2_ref[...], preferred_element_type=F32) + b2_ref[...]
    return _log_sigmoid(z) * (1.0 / GATE_NORMALIZER)


def _proj_kernel(hx_ref, w_ref, w2_ref, b2_ref, k_ref, v_ref, q_ref, sg_ref, u_ref, ldf_ref, ldb_ref):
    hx = hx_ref[...]

    def proj(lo, hi):
        return lax.dot_general(hx, w_ref[lo:hi, :], NT_DIMS, preferred_element_type=F32)

    k_ref[...] = proj(R_K, R_V).astype(BF16)
    v_ref[...] = proj(R_V, R_Q).astype(BF16)
    q_ref[...] = (proj(R_Q, R_G) * (GLA_DK ** -0.5)).astype(BF16)
    sg_ref[...] = _silu(proj(R_G, R_A)).astype(BF16)
    u_ref[...] = proj(R_A, R_B) * jax.nn.sigmoid(proj(R_B, R_GK))
    ld = _log_decays(proj(R_GK, R_END), w2_ref, b2_ref)
    ldf_ref[...] = ld[:, :GLA_KEY]
    ldb_ref[...] = ld[:, GLA_KEY:]


def _proj_call(hx, w_r, w2pad, b2):
    rows, d = hx.shape
    tm = PROJ_TM
    row = lambda n: pl.BlockSpec((tm, n), lambda i: (i, 0))
    whole = lambda a: pl.BlockSpec(a.shape, lambda i: (0, 0), pipeline_mode=pl.Buffered(1))
    outs = [(GLA_KEY, BF16), (D_GLA, BF16), (GLA_KEY, BF16), (D_GLA, BF16), (D_CONV, F32),
            (GLA_KEY, F32), (GLA_KEY, F32)]
    return pl.pallas_call(
        _proj_kernel,
        grid=(rows // tm,),
        in_specs=[row(d), whole(w_r), whole(w2pad), whole(b2)],
        out_specs=[row(n) for n, _ in outs],
        out_shape=[jax.ShapeDtypeStruct((rows, n), dt) for n, dt in outs],
        compiler_params=_params("parallel"),
        name="proj",
    )(hx, w_r, w2pad, b2)


def _ctx_kernel(hx_ref, wkv_ref, wgk_ref, w2_ref, b2_ref, sf_ref, sb_ref):
    hx = hx_ref[...]
    t = hx.shape[0]
    kv = lax.dot_general(hx, wkv_ref[...], NT_DIMS, preferred_element_type=F32)
    ld = _log_decays(lax.dot_general(hx, wgk_ref[...], NT_DIMS, preferred_element_type=F32),
                     w2_ref, b2_ref)
    r = lax.broadcasted_iota(jnp.int32, (t, t), 0)
    c = lax.broadcasted_iota(jnp.int32, (t, t), 1)
    hi = lax.Precision.HIGHEST
    e_f = jnp.dot((c > r).astype(F32), ld[:, :GLA_KEY], preferred_element_type=F32, precision=hi)
    e_b = jnp.dot((c < r).astype(F32), ld[:, GLA_KEY:], preferred_element_type=F32, precision=hi)
    for h in range(GLA_HEADS):
        ks = slice(h * GLA_DK, (h + 1) * GLA_DK)
        k = kv[:, R_K + h * GLA_DK:R_K + (h + 1) * GLA_DK]
        v = kv[:, R_V + h * GLA_DV:R_V + (h + 1) * GLA_DV].astype(BF16)
        sf_ref[h] = lax.dot_general(v, (k * jnp.exp(e_f[:, ks])).astype(BF16), TN_DIMS,
                                    preferred_element_type=F32)
        sb_ref[h] = lax.dot_general(v, (k * jnp.exp(e_b[:, ks])).astype(BF16), TN_DIMS,
                                    preferred_element_type=F32)


def _ctx_call(hxc, w_r, w2pad, b2, bsz, t):
    d = hxc.shape[1]
    const = lambda a: pl.BlockSpec(a.shape, lambda b: (0, 0))
    st = pl.BlockSpec((None, GLA_HEADS, GLA_DV, GLA_DK), lambda b: (b, 0, 0, 0))
    shape = jax.ShapeDtypeStruct((bsz, GLA_HEADS, GLA_DV, GLA_DK), F32)
    return pl.pallas_call(
        _ctx_kernel,
        grid=(bsz,),
        in_specs=[pl.BlockSpec((t, d), lambda b: (b, 0)),
                  pl.BlockSpec((R_Q, d), lambda b: (0, 0)),
                  pl.BlockSpec((V7X_LANES, d), lambda b: (R_GK // V7X_LANES, 0)),
                  const(w2pad), const(b2)],
        out_specs=[st, st],
        out_shape=[shape, shape],
        compiler_params=_params("parallel"),
        name="ctx",
    )(hxc, w_r, w_r, w2pad, b2)


def _chunk_cumsum(x, reverse):
    n = x.shape[0]
    pos = lax.broadcasted_iota(jnp.int32, x.shape, 0) % CHUNK
    d = 1
    while d < CHUNK:
        if reverse:
            shifted = pltpu.roll(x, n - d, 0)
            x = x + jnp.where(pos < CHUNK - d, shifted, 0.0)
        else:
            shifted = pltpu.roll(x, d, 0)
            x = x + jnp.where(pos >= d, shifted, 0.0)
        d *= 2
    return x


def _rows_bcast(x, row_in_chunk):
    parts = []
    for c0 in range(0, x.shape[0], CHUNK):
        r = c0 + row_in_chunk
        parts.append(jnp.broadcast_to(x[r:r + 1, :], (CHUNK, x.shape[1])))
    return jnp.concatenate(parts, axis=0)


def _gla_direction(q_ref, k_ref, v_ref, ld_ref, s_ref, reverse):
    n_sub = GLA_TB // GLA_SUB
    n_chunk = GLA_SUB // CHUNK
    rr = lax.broadcasted_iota(jnp.int32, (GLA_SUB, GLA_SUB), 0)
    cc = lax.broadcasted_iota(jnp.int32, (GLA_SUB, GLA_SUB), 1)
    same_chunk = (rr // CHUNK) == (cc // CHUNK)
    mask = same_chunk & ((cc >= rr) if reverse else (cc <= rr))
    mid_row = CHUNK // 2 if reverse else CHUNK // 2 - 1
    last_row = 0 if reverse else CHUNK - 1

    outs = [None] * n_sub
    subs = range(n_sub - 1, -1, -1) if reverse else range(n_sub)
    for s in subs:
        rows = slice(s * GLA_SUB, (s + 1) * GLA_SUB)
        b = _chunk_cumsum(ld_ref[rows, :], reverse)
        b_mid = _rows_bcast(b, mid_row)
        b_last = _rows_bcast(b, last_row)
        q = q_ref[rows, :].astype(F32)
        k = k_ref[rows, :].astype(F32)
        v = v_ref[rows, :]
        qs = (q * jnp.exp(b - b_mid)).astype(BF16)
        ks = (k * jnp.exp(b_mid - b)).astype(BF16)
        qi = (q * jnp.exp(b)).astype(BF16)
        kd = (k * jnp.exp(b_last - b)).astype(BF16)
        att = lax.dot_general(qs, ks, NT_DIMS, preferred_element_type=F32)
        att = jnp.where(mask, att, 0.0).astype(BF16)
        o_intra = jnp.dot(att, v, preferred_element_type=F32)
        decay = jnp.exp(b_last)
        o_parts = [None] * n_chunk
        chunks = range(n_chunk - 1, -1, -1) if reverse else range(n_chunk)
        for c in chunks:
            cr = slice(c * CHUNK, (c + 1) * CHUNK)
            state = s_ref[...]
            o_parts[c] = o_intra[cr, :] + lax.dot_general(
                qi[cr, :], state.astype(BF16), NT_DIMS, preferred_element_type=F32)
            kv = lax.dot_general(v[cr, :], kd[cr, :], TN_DIMS, preferred_element_type=F32)
            s_ref[...] = state * decay[c * CHUNK:c * CHUNK + 1, :] + kv
        outs[s] = jnp.concatenate(o_parts, axis=0)
    return jnp.concatenate(outs, axis=0)


def _gla_kernel(qf_ref, kf_ref, vf_ref, ldf_ref, qb_ref, kb_ref, vb_ref, ldb_ref, sg_ref,
                s0f_ref, s0b_ref, gain_ref, o_ref, sf_ref, sb_ref, acc_ref):
    nb = pl.program_id(2)
    n_blocks = pl.num_programs(2)

    @pl.when(nb == 0)
    def _():
        sf_ref[...] = s0f_ref[...]
        sb_ref[...] = s0b_ref[...]

    o_f = _gla_direction(qf_ref, kf_ref, vf_ref, ldf_ref, sf_ref, reverse=False)
    o_b = _gla_direction(qb_ref, kb_ref, vb_ref, ldb_ref, sb_ref, reverse=True)
    row_f = pl.multiple_of(nb * GLA_TB, GLA_TB)
    row_b = pl.multiple_of((n_blocks - 1 - nb) * GLA_TB, GLA_TB)

    @pl.when(nb < n_blocks // 2)
    def _():
        acc_ref[pl.ds(row_f, GLA_TB), :] = o_f
        acc_ref[pl.ds(row_b, GLA_TB), :] = o_b

    @pl.when(nb >= n_blocks // 2)
    def _():
        for row, part in ((row_f, o_f), (row_b, o_b)):
            o = acc_ref[pl.ds(row, GLA_TB), :] + part
            ms = jnp.mean(o * o, axis=-1, keepdims=True)
            o = o * lax.rsqrt(ms + HEAD_NORM_EPS) * gain_ref[...]
            o_ref[pl.ds(row, GLA_TB), :] = (o * sg_ref[pl.ds(row, GLA_TB), :].astype(F32)).astype(BF16)


def _gla_call(q, k, v, ldf, ldb, sg, s0f, s0b, gain, bsz, t):
    nb = t // GLA_TB
    fwd = lambda n: pl.BlockSpec((GLA_TB, n), lambda b, h, i: (b * nb + i, h))
    bwd = lambda n: pl.BlockSpec((GLA_TB, n), lambda b, h, i: (b * nb + nb - 1 - i, h))
    seq = pl.BlockSpec((t, GLA_DV), lambda b, h, i: (b, h))
    st = pl.BlockSpec((None, None, GLA_DV, GLA_DK), lambda b, h, i: (b, h, 0, 0))
    return pl.pallas_call(
        _gla_kernel,
        grid=(bsz, GLA_HEADS, nb),
        in_specs=[fwd(GLA_DK), fwd(GLA_DK), fwd(GLA_DV), fwd(GLA_DK),
                  bwd(GLA_DK), bwd(GLA_DK), bwd(GLA_DV), bwd(GLA_DK),
                  seq, st, st, pl.BlockSpec((1, GLA_DV), lambda b, h, i: (0, 0))],
        out_specs=seq,
        out_shape=jax.ShapeDtypeStruct((bsz * t, D_GLA), BF16),
        scratch_shapes=[pltpu.VMEM((GLA_DV, GLA_DK), F32), pltpu.VMEM((GLA_DV, GLA_DK), F32),
                        pltpu.VMEM((t, GLA_DV), F32)],
        compiler_params=_params("parallel", "parallel", "arbitrary"),
        name="gla",
    )(q, k, v, ldf, q, k, v, ldb, sg, s0f, s0b, gain)


ROW_PAD = 16
ROW_PITCH = GRID_W + 2 * ROW_PAD


def _conv_kernel(n_row_blocks, u_ref, w_ref, b_ref, y_ref, pad_ref):
    cb = pl.program_id(1)
    rows = u_ref.shape[0] // GRID_W
    bias = jnp.broadcast_to(b_ref[...], (GRID_W, CONV_CB))

    @pl.when(cb < n_row_blocks)
    def _():
        pad_ref[...] = jnp.zeros_like(pad_ref)

        def fill(r, carry):
            src = pl.multiple_of(r * GRID_W, GRID_W)
            dst = pl.multiple_of(r * ROW_PITCH + ROW_PAD, 8)
            pad_ref[pl.ds(dst, GRID_W), :] = u_ref[pl.ds(src, GRID_W), :]
            return carry

        lax.fori_loop(0, rows, fill, 0)

        def body(r, carry):
            base = r * ROW_PITCH + (ROW_PAD - CONV_HALF)
            acc = bias
            for j in range(CONV_WIDTH):
                acc = acc + w_ref[j:j + 1, :] * pad_ref[pl.ds(base + j, GRID_W), :]
            y_ref[pl.ds(pl.multiple_of(r * GRID_W, GRID_W), GRID_W), :] = acc
            return carry

        lax.fori_loop(0, rows, body, 0)

    @pl.when(cb >= n_row_blocks)
    def _():
        edge = CONV_HALF * GRID_W
        pad_ref[0:edge, :] = jnp.zeros((edge, CONV_CB), F32)
        pad_ref[edge + rows * GRID_W:2 * edge + rows * GRID_W, :] = jnp.zeros((edge, CONV_CB), F32)
        pad_ref[edge:edge + rows * GRID_W, :] = u_ref[...]

        def body(r, carry):
            acc = bias
            for j in range(CONV_WIDTH):
                src = pl.multiple_of((r + j) * GRID_W, GRID_W)
                acc = acc + w_ref[j:j + 1, :] * pad_ref[pl.ds(src, GRID_W), :]
            y_ref[pl.ds(pl.multiple_of(r * GRID_W, GRID_W), GRID_W), :] = acc
            return carry

        lax.fori_loop(0, rows, body, 0)


def _conv_call(u, w, b, bsz, t):
    ch = u.shape[1]
    rows = t // GRID_W
    n_cb = ch // CONV_CB
    pad_rows = max(rows * ROW_PITCH, (rows + 2 * CONV_HALF) * GRID_W)
    blk = pl.BlockSpec((t, CONV_CB), lambda bi, c: (bi, c))
    return pl.pallas_call(
        functools.partial(_conv_kernel, n_cb // 2),
        grid=(bsz, n_cb),
        in_specs=[blk, pl.BlockSpec((CONV_WIDTH, CONV_CB), lambda bi, c: (0, c)),
                  pl.BlockSpec((1, CONV_CB), lambda bi, c: (0, c))],
        out_specs=blk,
        out_shape=jax.ShapeDtypeStruct(u.shape, F32),
        scratch_shapes=[pltpu.VMEM((pad_rows, CONV_CB), F32)],
        compiler_params=_params("parallel", "parallel"),
        name="conv",
    )(u, w, b)


def _outproj_kernel(og_ref, y_ref, lng_ref, lnb_ref, w_ref, h_ref, mod_ref, o_ref):
    y = y_ref[...]
    mu = jnp.mean(y, axis=-1, keepdims=True)
    yc = y - mu
    var = jnp.mean(yc * yc, axis=-1, keepdims=True)
    yn = yc * lax.rsqrt(var + LN_EPS) * lng_ref[...] + lnb_ref[...]
    oc = _silu(yn).astype(BF16)
    res = jnp.dot(og_ref[...], w_ref[0:D_GLA, :], preferred_element_type=F32)
    res = res + jnp.dot(oc, w_ref[D_GLA:D_GLA + D_CONV, :], preferred_element_type=F32)
    o_ref[...] = h_ref[...] + mod_ref[5:6, :] * res


def _outproj_call(og, y, ln_g, ln_b, w_out_bf, h, mod, group_of_tile):
    rows, d = h.shape
    tm = OUT_TM
    row = lambda n: pl.BlockSpec((tm, n), lambda i: (i, 0))
    vec = lambda n: pl.BlockSpec((1, n), lambda i: (0, 0))
    return pl.pallas_call(
        _outproj_kernel,
        grid=(rows // tm,),
        in_specs=[row(D_GLA), row(D_CONV), vec(D_CONV), vec(D_CONV),
                  pl.BlockSpec(w_out_bf.shape, lambda i: (0, 0), pipeline_mode=pl.Buffered(1)),
                  row(d), pl.BlockSpec((None, N_MOD, d), lambda i: (group_of_tile(i), 0, 0))],
        out_specs=row(d),
        out_shape=jax.ShapeDtypeStruct((rows, d), F32),
        compiler_params=_params("parallel"),
        name="outproj",
    )(og, y, ln_g, ln_b, w_out_bf, h, mod)


def kernel(x, c, ctx, c_ctx, w_mod, b_mod, norm_ffn1, w_ffn1_in, w_ffn1_out, norm_mix, w_in, w_gk2, b_gk2, gla_norm, conv_w, conv_b, conv_ln_g, conv_ln_b, w_out, norm_ffn2, w_ffn2_in, w_ffn2_out, norm_final):
    bsz, t, d = x.shape
    t_ctx = ctx.shape[1]
    assert w_mod.shape[0] == 1, "single layer only"
    assert t % GLA_TB == 0 and (t // GLA_TB) % 2 == 0 and t % FFN_TM == 0
    vec = lambda a: a.reshape(1, -1)

    w2pad = jnp.zeros((V7X_LANES, 2 * GLA_KEY), BF16)
    w2pad = w2pad.at[:GATE_RANK, :GLA_KEY].set(w_gk2[0, 0].astype(BF16))
    w2pad = w2pad.at[GATE_RANK:2 * GATE_RANK, GLA_KEY:].set(w_gk2[0, 1].astype(BF16))
    b2 = b_gk2[0].reshape(1, 2 * GLA_KEY)
    w_r = _wprep_call(jnp.swapaxes(w_in[0], 0, 1))

    n_rows = 8
    s_in = jnp.concatenate([c, c_ctx[None, :], jnp.zeros((n_rows - bsz - 1, d), F32)], axis=0)
    mod = _mod_call(s_in, w_mod[0], vec(b_mod[0])).reshape(n_rows, N_MOD, d)

    tiles_per_batch = t // FFN_TM
    lat_group = lambda i: i // tiles_per_batch
    ctx_group = lambda i: bsz
    g1, gm = vec(norm_ffn1[0]), vec(norm_mix[0])

    xc = ctx.reshape(bsz * t_ctx, d)
    hxc, wg1, wu1, wo1 = _ffn_call(
        xc, mod, ctx_group, g1, gm, w_ffn1_in[0], w_ffn1_in[0], w_ffn1_out[0], D_FF,
        _FfnCfg(mod_row=0, emit_h=False, emit_hx=True, final_norm=False, cast_w=True, n_cast=0),
        "ffn1_ctx")
    s0f, s0b = _ctx_call(hxc, w_r, w2pad, b2, bsz, t_ctx)

    xl = x.reshape(bsz * t, d)
    n_i, n_j = (bsz * t) // FFN_TM, D_FF // FFN_TF
    assert d % n_i == 0 and (2 * D_FF) % n_j == 0
    cast_jobs = (
        (w_ffn2_in[0], (d // n_i, 2 * D_FF // n_j), lambda i, j: (i, j)),
        (w_ffn2_out[0], (FFN_TF, d // n_i), lambda i, j: (j, i)),
        (w_out[0], (d // n_i, d), lambda i, j: (i, 0)),
    )
    h1, hx, wf2_in, wf2_out, w_out_bf = _ffn_call(
        xl, mod, lat_group, g1, gm, wg1, wu1, wo1, 0,
        _FfnCfg(mod_row=0, emit_h=True, emit_hx=True, final_norm=False, cast_w=False, n_cast=3),
        "ffn1", cast_jobs)

    k, v, q, sg, u, ldf, ldb = _proj_call(hx, w_r, w2pad, b2)
    og = _gla_call(q, k, v, ldf, ldb, sg, s0f, s0b, vec(gla_norm[0]), bsz, t)
    y = _conv_call(u, conv_w[0], vec(conv_b[0]), bsz, t)
    h2 = _outproj_call(og, y, vec(conv_ln_g[0]), vec(conv_ln_b[0]), w_out_bf, h1, mod,
                       lambda i: i // (t // OUT_TM))

    (out,) = _ffn_call(
        h2, mod, lat_group, vec(norm_ffn2[0]), vec(norm_final), wf2_in, wf2_in, wf2_out, D_FF,
        _FfnCfg(mod_row=6, emit_h=True, emit_hx=False, final_norm=True, cast_w=False, n_cast=0),
        "ffn2")
    return out.reshape(bsz, t, d)
```

```python
import functools
from typing import NamedTuple

import jax
import jax.numpy as jnp
from jax import lax
from jax.experimental import pallas as pl
from jax.experimental.pallas import tpu as pltpu

F32 = jnp.float32
BF16 = jnp.bfloat16

D_MODEL = 2048
GRID_W = 64
GLA_HEADS = 4
GLA_DK = 128
GLA_DV = 256
GLA_KEY = GLA_HEADS * GLA_DK
D_GLA = GLA_HEADS * GLA_DV
D_CONV = 1024
GATE_RANK = 16
GATE_NORMALIZER = 16.0
CHUNK = 64
CONV_WIDTH = 31
CONV_HALF = CONV_WIDTH // 2
D_FF = 5632
N_MOD = 9
RMS_EPS = 1e-6
HEAD_NORM_EPS = 1e-5
LN_EPS = 1e-5

OFF_V = GLA_KEY
OFF_GKF = OFF_V + D_GLA
CTX_COLS = OFF_GKF + 2 * GATE_RANK
OFF_Q = CTX_COLS
OFF_G = OFF_Q + GLA_KEY
OFF_GLU = OFF_G + D_GLA

V7X_LANES = 128
R_K = 0
R_V = R_K + GLA_KEY
R_Q = R_V + D_GLA
R_G = R_Q + GLA_KEY
R_A = R_G + D_GLA
R_B = R_A + D_CONV
R_GK = R_B + D_CONV
R_END = R_GK + V7X_LANES

V7X_VMEM_SCOPED_LIMIT_BYTES = 60000 * 1024
V7X_BF16_SUBLANES = 16

FFN_TM = 512
FFN_TF = 512
FFN_TF_CAST = 256
PROJ_TM = 512
OUT_TM = 512
MOD_TN = 1024
GLA_TB = 512
GLA_SUB = 256
CONV_CB = 128
WPREP_TC = 256

NT_DIMS = (((1,), (1,)), ((), ()))
TN_DIMS = (((0,), (0,)), ((), ()))


def _params(*semantics):
    return pltpu.CompilerParams(dimension_semantics=semantics,
                                vmem_limit_bytes=V7X_VMEM_SCOPED_LIMIT_BYTES)


def _silu(x):
    return x * jax.nn.sigmoid(x)


def _rmsnorm_rows(x, gain, eps):
    ms = jnp.mean(x * x, axis=-1, keepdims=True)
    return x * lax.rsqrt(ms + eps) * gain


def _modulate(x, gain, shift, scale):
    return _rmsnorm_rows(x, gain, RMS_EPS) * (1.0 + scale) + shift


def _log_sigmoid(z):
    return jnp.minimum(z, 0.0) - jnp.log1p(jnp.exp(-jnp.abs(z)))


class _CastJob(NamedTuple):
    array: jax.Array
    block: tuple
    index_map: object


def _cast_specs(jobs):
    for job in jobs:
        assert all(n % b == 0 for n, b in zip(job.array.shape, job.block))
        assert job.block[0] % V7X_BF16_SUBLANES == 0 and job.block[1] % V7X_LANES == 0
    specs = [pl.BlockSpec(job.block, job.index_map) for job in jobs]
    shapes = [jax.ShapeDtypeStruct(job.array.shape, BF16) for job in jobs]
    return specs, shapes


def _run_casts(src_refs, dst_refs):
    for src_ref, dst_ref in zip(src_refs, dst_refs):
        dst_ref[...] = src_ref[...].astype(dst_ref.dtype)


def _mod_kernel(s_ref, w_ref, b_ref, o_ref):
    s = _silu(s_ref[...]).astype(BF16)
    o_ref[...] = jnp.dot(s, w_ref[...].astype(BF16), preferred_element_type=F32) + b_ref[...]


def _mod_call(s_in, w_mod, b_mod):
    rows, d = s_in.shape
    n = w_mod.shape[1]
    return pl.pallas_call(
        _mod_kernel,
        grid=(n // MOD_TN,),
        in_specs=[
            pl.BlockSpec((rows, d), lambda j: (0, 0)),
            pl.BlockSpec((d, MOD_TN), lambda j: (0, j)),
            pl.BlockSpec((1, MOD_TN), lambda j: (0, j)),
        ],
        out_specs=pl.BlockSpec((rows, MOD_TN), lambda j: (0, j)),
        out_shape=jax.ShapeDtypeStruct((rows, n), F32),
        compiler_params=_params("arbitrary"),
        name="mod",
    )(s_in, w_mod, b_mod)


class _FfnCfg(NamedTuple):
    mod_row: int
    emit_h: bool
    emit_hx: bool
    final_norm: bool
    cast_w: bool


def _ffn_kernel(cfg, x_ref, mod_ref, g_in_ref, g_next_ref, wg_ref, wu_ref, wo_ref, *rest):
    outs = []
    for flag in (cfg.emit_h, cfg.emit_hx, cfg.cast_w, cfg.cast_w, cfg.cast_w):
        outs.append(rest[0] if flag else None)
        rest = rest[1:] if flag else rest
    h_ref, hx_ref, wg_bf_ref, wu_bf_ref, wo_bf_ref = outs
    hm_ref, acc_ref = rest
    j = pl.program_id(1)
    last = pl.num_programs(1) - 1
    r0 = cfg.mod_row

    def partial_out(hm):
        wg, wu, wo = wg_ref[...], wu_ref[...], wo_ref[...]
        if cfg.cast_w:
            wg, wu, wo = wg.astype(BF16), wu.astype(BF16), wo.astype(BF16)
            wg_bf_ref[...] = wg
            wu_bf_ref[...] = wu
            wo_bf_ref[...] = wo
        gate = jnp.dot(hm, wg, preferred_element_type=F32)
        up = jnp.dot(hm, wu, preferred_element_type=F32)
        act = (_silu(gate) * up).astype(BF16)
        return jnp.dot(act, wo, preferred_element_type=F32)

    @pl.when(j == 0)
    def _():
        hm = _modulate(x_ref[...], g_in_ref[...], mod_ref[r0:r0 + 1, :], mod_ref[r0 + 1:r0 + 2, :])
        hm = hm.astype(BF16)
        hm_ref[...] = hm
        acc_ref[...] = partial_out(hm)

    @pl.when((j > 0) & (j < last))
    def _():
        acc_ref[...] += partial_out(hm_ref[...])

    @pl.when(j == last)
    def _():
        g = mod_ref[r0 + 2:r0 + 3, :]
        h = x_ref[...] + (0.5 * g) * (acc_ref[...] + partial_out(hm_ref[...]))
        if cfg.emit_hx:
            hx = _modulate(h, g_next_ref[...], mod_ref[r0 + 3:r0 + 4, :], mod_ref[r0 + 4:r0 + 5, :])
            hx_ref[...] = hx.astype(BF16)
        if cfg.emit_h:
            h_ref[...] = _rmsnorm_rows(h, g_next_ref[...], RMS_EPS) if cfg.final_norm else h


def _ffn_call(x, mod, group_of_tile, g_in, g_next, wg, wu, wo, up_col0, cfg, name):
    rows, d = x.shape
    n_ff = wo.shape[0]
    tf = FFN_TF_CAST if cfg.cast_w else FFN_TF
    n_j = n_ff // tf
    assert n_j >= 2, "first and last hidden block must be distinct grid steps"
    assert not cfg.cast_w or rows == FFN_TM, "bf16 weight copies are written by one row tile only"
    up_blk0 = up_col0 // tf
    row_spec = pl.BlockSpec((FFN_TM, d), lambda i, j: (i, 0))
    vec_spec = pl.BlockSpec((1, d), lambda i, j: (0, 0))
    wg_spec = pl.BlockSpec((d, tf), lambda i, j: (0, j))
    wu_spec = pl.BlockSpec((d, tf), lambda i, j: (0, up_blk0 + j))
    wo_spec = pl.BlockSpec((tf, d), lambda i, j: (j, 0))
    in_specs = [row_spec, pl.BlockSpec((None, N_MOD, d), lambda i, j: (group_of_tile(i), 0, 0)),
                vec_spec, vec_spec, wg_spec, wu_spec, wo_spec]
    out_specs, out_shape = [], []
    if cfg.emit_h:
        out_specs.append(row_spec)
        out_shape.append(jax.ShapeDtypeStruct((rows, d), F32))
    if cfg.emit_hx:
        out_specs.append(row_spec)
        out_shape.append(jax.ShapeDtypeStruct((rows, d), BF16))
    if cfg.cast_w:
        out_specs += [wg_spec, pl.BlockSpec((d, tf), lambda i, j: (0, j)), wo_spec]
        out_shape += [jax.ShapeDtypeStruct((d, n_ff), BF16), jax.ShapeDtypeStruct((d, n_ff), BF16),
                      jax.ShapeDtypeStruct((n_ff, d), BF16)]
    return pl.pallas_call(
        functools.partial(_ffn_kernel, cfg),
        grid=(rows // FFN_TM, n_j),
        in_specs=in_specs,
        out_specs=out_specs,
        out_shape=out_shape,
        scratch_shapes=[pltpu.VMEM((FFN_TM, d), BF16), pltpu.VMEM((FFN_TM, d), F32)],
        compiler_params=_params("parallel", "arbitrary"),
        name=name,
    )(x, mod, g_in, g_next, wg, wu, wo)


def _wprep_kernel(w_ref, o_ref):
    o_ref[R_K:R_Q, :] = w_ref[0:OFF_GKF, :].astype(BF16)
    o_ref[R_Q:R_GK, :] = w_ref[OFF_Q:OFF_GLU + 2 * D_CONV, :].astype(BF16)
    o_ref[R_GK:R_GK + 2 * GATE_RANK, :] = w_ref[OFF_GKF:CTX_COLS, :].astype(BF16)
    o_ref[R_GK + 2 * GATE_RANK:R_END, :] = jnp.zeros((V7X_LANES - 2 * GATE_RANK, o_ref.shape[1]), BF16)


def _wprep_call(w_t):
    n, d = w_t.shape
    tc = WPREP_TC
    return pl.pallas_call(
        _wprep_kernel,
        grid=(d // tc,),
        in_specs=[pl.BlockSpec((n, tc), lambda i: (0, i))],
        out_specs=pl.BlockSpec((R_END, tc), lambda i: (0, i)),
        out_shape=jax.ShapeDtypeStruct((R_END, d), BF16),
        compiler_params=_params("parallel"),
        name="wprep",
    )(w_t)


def _log_decays(p_gk, w2_ref, b2_ref):
    z = jnp.dot(p_gk.astype(BF16), w2_ref[...], preferred_element_type=F32) + b2_ref[...]
    return _log_sigmoid(z) * (1.0 / GATE_NORMALIZER)


def _proj_kernel(hx_ref, w_ref, w2_ref, b2_ref, k_ref, v_ref, q_ref, sg_ref, u_ref, ldf_ref, ldb_ref):
    hx = hx_ref[...]

    def proj(lo, hi):
        return lax.dot_general(hx, w_ref[lo:hi, :], NT_DIMS, preferred_element_type=F32)

    k_ref[...] = proj(R_K, R_V).astype(BF16)
    v_ref[...] = proj(R_V, R_Q).astype(BF16)
    q_ref[...] = (proj(R_Q, R_G) * (GLA_DK ** -0.5)).astype(BF16)
    sg_ref[...] = _silu(proj(R_G, R_A)).astype(BF16)
    u_ref[...] = proj(R_A, R_B) * jax.nn.sigmoid(proj(R_B, R_GK))
    ld = _log_decays(proj(R_GK, R_END), w2_ref, b2_ref)
    ldf_ref[...] = ld[:, :GLA_KEY]
    ldb_ref[...] = ld[:, GLA_KEY:]


def _proj_call(hx, w_r, w2pad, b2):
    rows, d = hx.shape
    tm = PROJ_TM
    row = lambda n: pl.BlockSpec((tm, n), lambda i: (i, 0))
    whole = lambda a: pl.BlockSpec(a.shape, lambda i: (0, 0), pipeline_mode=pl.Buffered(1))
    outs = [(GLA_KEY, BF16), (D_GLA, BF16), (GLA_KEY, BF16), (D_GLA, BF16), (D_CONV, F32),
            (GLA_KEY, F32), (GLA_KEY, F32)]
    return pl.pallas_call(
        _proj_kernel,
        grid=(rows // tm,),
        in_specs=[row(d), whole(w_r), whole(w2pad), whole(b2)],
        out_specs=[row(n) for n, _ in outs],
        out_shape=[jax.ShapeDtypeStruct((rows, n), dt) for n, dt in outs],
        compiler_params=_params("parallel"),
        name="proj",
    )(hx, w_r, w2pad, b2)


def _ctx_kernel(hx_ref, wkv_ref, wgk_ref, w2_ref, b2_ref, sf_ref, sb_ref):
    hx = hx_ref[...]
    t = hx.shape[0]
    kv = lax.dot_general(hx, wkv_ref[...], NT_DIMS, preferred_element_type=F32)
    ld = _log_decays(lax.dot_general(hx, wgk_ref[...], NT_DIMS, preferred_element_type=F32),
                     w2_ref, b2_ref)
    r = lax.broadcasted_iota(jnp.int32, (t, t), 0)
    c = lax.broadcasted_iota(jnp.int32, (t, t), 1)
    hi = lax.Precision.HIGHEST
    e_f = jnp.dot((c > r).astype(F32), ld[:, :GLA_KEY], preferred_element_type=F32, precision=hi)
    e_b = jnp.dot((c < r).astype(F32), ld[:, GLA_KEY:], preferred_element_type=F32, precision=hi)
    for h in range(GLA_HEADS):
        ks = slice(h * GLA_DK, (h + 1) * GLA_DK)
        k = kv[:, R_K + h * GLA_DK:R_K + (h + 1) * GLA_DK]
        v = kv[:, R_V + h * GLA_DV:R_V + (h + 1) * GLA_DV].astype(BF16)
        sf_ref[h] = lax.dot_general(v, (k * jnp.exp(e_f[:, ks])).astype(BF16), TN_DIMS,
                                    preferred_element_type=F32)
        sb_ref[h] = lax.dot_general(v, (k * jnp.exp(e_b[:, ks])).astype(BF16), TN_DIMS,
                                    preferred_element_type=F32)


def _ctx_call(hxc, w_r, w2pad, b2, bsz, t):
    d = hxc.shape[1]
    const = lambda a: pl.BlockSpec(a.shape, lambda b: (0, 0))
    st = pl.BlockSpec((None, GLA_HEADS, GLA_DV, GLA_DK), lambda b: (b, 0, 0, 0))
    shape = jax.ShapeDtypeStruct((bsz, GLA_HEADS, GLA_DV, GLA_DK), F32)
    return pl.pallas_call(
        _ctx_kernel,
        grid=(bsz,),
        in_specs=[pl.BlockSpec((t, d), lambda b: (b, 0)),
                  pl.BlockSpec((R_Q, d), lambda b: (0, 0)),
                  pl.BlockSpec((V7X_LANES, d), lambda b: (R_GK // V7X_LANES, 0)),
                  const(w2pad), const(b2)],
        out_specs=[st, st],
        out_shape=[shape, shape],
        compiler_params=_params("parallel"),
        name="ctx",
    )(hxc, w_r, w_r, w2pad, b2)


def _chunk_cumsum(x, reverse):
    n = x.shape[0]
    pos = lax.broadcasted_iota(jnp.int32, x.shape, 0) % CHUNK
    d = 1
    while d < CHUNK:
        if reverse:
            shifted = pltpu.roll(x, n - d, 0)
            x = x + jnp.where(pos < CHUNK - d, shifted, 0.0)
        else:
            shifted = pltpu.roll(x, d, 0)
            x = x + jnp.where(pos >= d, shifted, 0.0)
        d *= 2
    return x


def _rows_bcast(x, row_in_chunk):
    parts = []
    for c0 in range(0, x.shape[0], CHUNK):
        r = c0 + row_in_chunk
        parts.append(jnp.broadcast_to(x[r:r + 1, :], (CHUNK, x.shape[1])))
    return jnp.concatenate(parts, axis=0)


def _gla_direction(q_ref, k_ref, v_ref, ld_ref, s_ref, reverse):
    n_sub = GLA_TB // GLA_SUB
    n_chunk = GLA_SUB // CHUNK
    rr = lax.broadcasted_iota(jnp.int32, (GLA_SUB, GLA_SUB), 0)
    cc = lax.broadcasted_iota(jnp.int32, (GLA_SUB, GLA_SUB), 1)
    same_chunk = (rr // CHUNK) == (cc // CHUNK)
    mask = same_chunk & ((cc >= rr) if reverse else (cc <= rr))
    mid_row = CHUNK // 2 if reverse else CHUNK // 2 - 1
    last_row = 0 if reverse else CHUNK - 1

    outs = [None] * n_sub
    subs = range(n_sub - 1, -1, -1) if reverse else range(n_sub)
    for s in subs:
        rows = slice(s * GLA_SUB, (s + 1) * GLA_SUB)
        b = _chunk_cumsum(ld_ref[rows, :], reverse)
        b_mid = _rows_bcast(b, mid_row)
        b_last = _rows_bcast(b, last_row)
        q = q_ref[rows, :].astype(F32)
        k = k_ref[rows, :].astype(F32)
        v = v_ref[rows, :]
        qs = (q * jnp.exp(b - b_mid)).astype(BF16)
        ks = (k * jnp.exp(b_mid - b)).astype(BF16)
        qi = (q * jnp.exp(b)).astype(BF16)
        kd = (k * jnp.exp(b_last - b)).astype(BF16)
        att = lax.dot_general(qs, ks, NT_DIMS, preferred_element_type=F32)
        att = jnp.where(mask, att, 0.0).astype(BF16)
        o_intra = jnp.dot(att, v, preferred_element_type=F32)
        decay = jnp.exp(b_last)
        o_parts = [None] * n_chunk
        chunks = range(n_chunk - 1, -1, -1) if reverse else range(n_chunk)
        for c in chunks:
            cr = slice(c * CHUNK, (c + 1) * CHUNK)
            state = s_ref[...]
            o_parts[c] = o_intra[cr, :] + lax.dot_general(
                qi[cr, :], state.astype(BF16), NT_DIMS, preferred_element_type=F32)
            kv = lax.dot_general(v[cr, :], kd[cr, :], TN_DIMS, preferred_element_type=F32)
            s_ref[...] = state * decay[c * CHUNK:c * CHUNK + 1, :] + kv
        outs[s] = jnp.concatenate(o_parts, axis=0)
    return jnp.concatenate(outs, axis=0)


def _gla_kernel(n_cast, qf_ref, kf_ref, vf_ref, ldf_ref, qb_ref, kb_ref, vb_ref, ldb_ref, sg_ref,
                s0f_ref, s0b_ref, gain_ref, *rest):
    cast_in, o_ref, rest = rest[:n_cast], rest[n_cast], rest[n_cast + 1:]
    cast_out, (sf_ref, sb_ref, acc_ref) = rest[:n_cast], rest[n_cast:]
    nb = pl.program_id(2)
    n_blocks = pl.num_programs(2)
    _run_casts(cast_in, cast_out)

    @pl.when(nb == 0)
    def _():
        sf_ref[...] = s0f_ref[...]
        sb_ref[...] = s0b_ref[...]

    o_f = _gla_direction(qf_ref, kf_ref, vf_ref, ldf_ref, sf_ref, reverse=False)
    o_b = _gla_direction(qb_ref, kb_ref, vb_ref, ldb_ref, sb_ref, reverse=True)
    row_f = pl.multiple_of(nb * GLA_TB, GLA_TB)
    row_b = pl.multiple_of((n_blocks - 1 - nb) * GLA_TB, GLA_TB)

    @pl.when(nb < n_blocks // 2)
    def _():
        acc_ref[pl.ds(row_f, GLA_TB), :] = o_f
        acc_ref[pl.ds(row_b, GLA_TB), :] = o_b

    @pl.when(nb >= n_blocks // 2)
    def _():
        for row, part in ((row_f, o_f), (row_b, o_b)):
            o = acc_ref[pl.ds(row, GLA_TB), :] + part
            ms = jnp.mean(o * o, axis=-1, keepdims=True)
            o = o * lax.rsqrt(ms + HEAD_NORM_EPS) * gain_ref[...]
            o_ref[pl.ds(row, GLA_TB), :] = (o * sg_ref[pl.ds(row, GLA_TB), :].astype(F32)).astype(BF16)


def _gla_call(q, k, v, ldf, ldb, sg, s0f, s0b, gain, bsz, t, cast_jobs=()):
    nb = t // GLA_TB
    fwd = lambda n: pl.BlockSpec((GLA_TB, n), lambda b, h, i: (b * nb + i, h))
    bwd = lambda n: pl.BlockSpec((GLA_TB, n), lambda b, h, i: (b * nb + nb - 1 - i, h))
    seq = pl.BlockSpec((t, GLA_DV), lambda b, h, i: (b, h))
    st = pl.BlockSpec((None, None, GLA_DV, GLA_DK), lambda b, h, i: (b, h, 0, 0))
    cast_specs, cast_shapes = _cast_specs(cast_jobs)
    return pl.pallas_call(
        functools.partial(_gla_kernel, len(cast_jobs)),
        grid=(bsz, GLA_HEADS, nb),
        in_specs=[fwd(GLA_DK), fwd(GLA_DK), fwd(GLA_DV), fwd(GLA_DK),
                  bwd(GLA_DK), bwd(GLA_DK), bwd(GLA_DV), bwd(GLA_DK),
                  seq, st, st, pl.BlockSpec((1, GLA_DV), lambda b, h, i: (0, 0))] + cast_specs,
        out_specs=[seq] + cast_specs,
        out_shape=[jax.ShapeDtypeStruct((bsz * t, D_GLA), BF16)] + cast_shapes,
        scratch_shapes=[pltpu.VMEM((GLA_DV, GLA_DK), F32), pltpu.VMEM((GLA_DV, GLA_DK), F32),
                        pltpu.VMEM((t, GLA_DV), F32)],
        compiler_params=_params("parallel", "parallel", "arbitrary"),
        name="gla",
    )(q, k, v, ldf, q, k, v, ldb, sg, s0f, s0b, gain, *[job.array for job in cast_jobs])


ROW_PAD = 16
ROW_PITCH = GRID_W + 2 * ROW_PAD


def _conv_kernel(n_row_blocks, n_cast, u_ref, w_ref, b_ref, *rest):
    cast_in, y_ref, rest = rest[:n_cast], rest[n_cast], rest[n_cast + 1:]
    cast_out, (pad_ref,) = rest[:n_cast], rest[n_cast:]
    cb = pl.program_id(1)
    rows = u_ref.shape[0] // GRID_W
    bias = jnp.broadcast_to(b_ref[...], (GRID_W, CONV_CB))
    _run_casts(cast_in, cast_out)

    @pl.when(cb < n_row_blocks)
    def _():
        pad_ref[...] = jnp.zeros_like(pad_ref)

        def fill(r, carry):
            src = pl.multiple_of(r * GRID_W, GRID_W)
            dst = pl.multiple_of(r * ROW_PITCH + ROW_PAD, 8)
            pad_ref[pl.ds(dst, GRID_W), :] = u_ref[pl.ds(src, GRID_W), :]
            return carry

        lax.fori_loop(0, rows, fill, 0)

        def body(r, carry):
            base = r * ROW_PITCH + (ROW_PAD - CONV_HALF)
            acc = bias
            for j in range(CONV_WIDTH):
                acc = acc + w_ref[j:j + 1, :] * pad_ref[pl.ds(base + j, GRID_W), :]
            y_ref[pl.ds(pl.multiple_of(r * GRID_W, GRID_W), GRID_W), :] = acc
            return carry

        lax.fori_loop(0, rows, body, 0)

    @pl.when(cb >= n_row_blocks)
    def _():
        edge = CONV_HALF * GRID_W
        pad_ref[0:edge, :] = jnp.zeros((edge, CONV_CB), F32)
        pad_ref[edge + rows * GRID_W:2 * edge + rows * GRID_W, :] = jnp.zeros((edge, CONV_CB), F32)
        pad_ref[edge:edge + rows * GRID_W, :] = u_ref[...]

        def body(r, carry):
            acc = bias
            for j in range(CONV_WIDTH):
                src = pl.multiple_of((r + j) * GRID_W, GRID_W)
                acc = acc + w_ref[j:j + 1, :] * pad_ref[pl.ds(src, GRID_W), :]
            y_ref[pl.ds(pl.multiple_of(r * GRID_W, GRID_W), GRID_W), :] = acc
            return carry

        lax.fori_loop(0, rows, body, 0)


def _conv_call(u, w, b, bsz, t, cast_jobs=()):
    ch = u.shape[1]
    rows = t // GRID_W
    n_cb = ch // CONV_CB
    pad_rows = max(rows * ROW_PITCH, (rows + 2 * CONV_HALF) * GRID_W)
    blk = pl.BlockSpec((t, CONV_CB), lambda bi, c: (bi, c))
    cast_specs, cast_shapes = _cast_specs(cast_jobs)
    return pl.pallas_call(
        functools.partial(_conv_kernel, n_cb // 2, len(cast_jobs)),
        grid=(bsz, n_cb),
        in_specs=[blk, pl.BlockSpec((CONV_WIDTH, CONV_CB), lambda bi, c: (0, c)),
                  pl.BlockSpec((1, CONV_CB), lambda bi, c: (0, c))] + cast_specs,
        out_specs=[blk] + cast_specs,
        out_shape=[jax.ShapeDtypeStruct(u.shape, F32)] + cast_shapes,
        scratch_shapes=[pltpu.VMEM((pad_rows, CONV_CB), F32)],
        compiler_params=_params("parallel", "parallel"),
        name="conv",
    )(u, w, b, *[job.array for job in cast_jobs])


def _outproj_kernel(og_ref, y_ref, lng_ref, lnb_ref, w_ref, h_ref, mod_ref, o_ref):
    y = y_ref[...]
    mu = jnp.mean(y, axis=-1, keepdims=True)
    yc = y - mu
    var = jnp.mean(yc * yc, axis=-1, keepdims=True)
    yn = yc * lax.rsqrt(var + LN_EPS) * lng_ref[...] + lnb_ref[...]
    oc = _silu(yn).astype(BF16)
    res = jnp.dot(og_ref[...], w_ref[0:D_GLA, :], preferred_element_type=F32)
    res = res + jnp.dot(oc, w_ref[D_GLA:D_GLA + D_CONV, :], preferred_element_type=F32)
    o_ref[...] = h_ref[...] + mod_ref[5:6, :] * res


def _outproj_call(og, y, ln_g, ln_b, w_out_bf, h, mod, group_of_tile):
    rows, d = h.shape
    tm = OUT_TM
    row = lambda n: pl.BlockSpec((tm, n), lambda i: (i, 0))
    vec = lambda n: pl.BlockSpec((1, n), lambda i: (0, 0))
    return pl.pallas_call(
        _outproj_kernel,
        grid=(rows // tm,),
        in_specs=[row(D_GLA), row(D_CONV), vec(D_CONV), vec(D_CONV),
                  pl.BlockSpec(w_out_bf.shape, lambda i: (0, 0), pipeline_mode=pl.Buffered(1)),
                  row(d), pl.BlockSpec((None, N_MOD, d), lambda i: (group_of_tile(i), 0, 0))],
        out_specs=row(d),
        out_shape=jax.ShapeDtypeStruct((rows, d), F32),
        compiler_params=_params("parallel"),
        name="outproj",
    )(og, y, ln_g, ln_b, w_out_bf, h, mod)


def kernel(x, c, ctx, c_ctx, w_mod, b_mod, norm_ffn1, w_ffn1_in, w_ffn1_out, norm_mix, w_in, w_gk2, b_gk2, gla_norm, conv_w, conv_b, conv_ln_g, conv_ln_b, w_out, norm_ffn2, w_ffn2_in, w_ffn2_out, norm_final):
    bsz, t, d = x.shape
    t_ctx = ctx.shape[1]
    assert w_mod.shape[0] == 1, "single layer only"
    assert t % GLA_TB == 0 and (t // GLA_TB) % 2 == 0 and t % FFN_TM == 0
    vec = lambda a: a.reshape(1, -1)

    w2pad = jnp.zeros((V7X_LANES, 2 * GLA_KEY), BF16)
    w2pad = w2pad.at[:GATE_RANK, :GLA_KEY].set(w_gk2[0, 0].astype(BF16))
    w2pad = w2pad.at[GATE_RANK:2 * GATE_RANK, GLA_KEY:].set(w_gk2[0, 1].astype(BF16))
    b2 = b_gk2[0].reshape(1, 2 * GLA_KEY)
    w_r = _wprep_call(jnp.swapaxes(w_in[0], 0, 1))

    n_rows = 8
    s_in = jnp.concatenate([c, c_ctx[None, :], jnp.zeros((n_rows - bsz - 1, d), F32)], axis=0)
    mod = _mod_call(s_in, w_mod[0], vec(b_mod[0])).reshape(n_rows, N_MOD, d)

    tiles_per_batch = t // FFN_TM
    lat_group = lambda i: i // tiles_per_batch
    ctx_group = lambda i: bsz
    g1, gm = vec(norm_ffn1[0]), vec(norm_mix[0])

    xc = ctx.reshape(bsz * t_ctx, d)
    hxc, wg1, wu1, wo1 = _ffn_call(
        xc, mod, ctx_group, g1, gm, w_ffn1_in[0], w_ffn1_in[0], w_ffn1_out[0], D_FF,
        _FfnCfg(mod_row=0, emit_h=False, emit_hx=True, final_norm=False, cast_w=True), "ffn1_ctx")
    s0f, s0b = _ctx_call(hxc, w_r, w2pad, b2, bsz, t_ctx)

    xl = x.reshape(bsz * t, d)
    h1, hx = _ffn_call(
        xl, mod, lat_group, g1, gm, wg1, wu1, wo1, 0,
        _FfnCfg(mod_row=0, emit_h=True, emit_hx=True, final_norm=False, cast_w=False), "ffn1")

    k, v, q, sg, u, ldf, ldb = _proj_call(hx, w_r, w2pad, b2)

    n_gla_blocks = t // GLA_TB
    n_bh = bsz * GLA_HEADS
    og, wf2_out, w_out_bf = _gla_call(
        q, k, v, ldf, ldb, sg, s0f, s0b, vec(gla_norm[0]), bsz, t,
        (_CastJob(w_ffn2_out[0], (D_FF // n_gla_blocks, d // n_bh),
                  lambda b, h, i: (i, b * GLA_HEADS + h)),
         _CastJob(w_out[0], (d // n_gla_blocks, d // n_bh), lambda b, h, i: (i, b * GLA_HEADS + h))))
    n_cb = D_CONV // CONV_CB
    y, wf2_in = _conv_call(
        u, conv_w[0], vec(conv_b[0]), bsz, t,
        (_CastJob(w_ffn2_in[0], (d // (bsz * n_cb), 2 * D_FF), lambda bi, cb: (bi * n_cb + cb, 0)),))
    h2 = _outproj_call(og, y, vec(conv_ln_g[0]), vec(conv_ln_b[0]), w_out_bf, h1, mod,
                       lambda i: i // (t // OUT_TM))

    (out,) = _ffn_call(
        h2, mod, lat_group, vec(norm_ffn2[0]), vec(norm_final), wf2_in, wf2_in, wf2_out, D_FF,
        _FfnCfg(mod_row=6, emit_h=True, emit_hx=False, final_norm=True, cast_w=False), "ffn2")
    return out.reshape(bsz, t, d)
```

```python
import functools
from typing import NamedTuple

import jax
import jax.numpy as jnp
from jax import lax
from jax.experimental import pallas as pl
from jax.experimental.pallas import tpu as pltpu

F32 = jnp.float32
BF16 = jnp.bfloat16

D_MODEL = 2048
GRID_W = 64
GLA_HEADS = 4
GLA_DK = 128
GLA_DV = 256
GLA_KEY = GLA_HEADS * GLA_DK
D_GLA = GLA_HEADS * GLA_DV
D_CONV = 1024
GATE_RANK = 16
GATE_NORMALIZER = 16.0
CHUNK = 64
CONV_WIDTH = 31
CONV_HALF = CONV_WIDTH // 2
D_FF = 5632
N_MOD = 9
RMS_EPS = 1e-6
HEAD_NORM_EPS = 1e-5
LN_EPS = 1e-5

OFF_V = GLA_KEY
OFF_GKF = OFF_V + D_GLA
CTX_COLS = OFF_GKF + 2 * GATE_RANK
OFF_Q = CTX_COLS
OFF_G = OFF_Q + GLA_KEY
OFF_GLU = OFF_G + D_GLA

V7X_LANES = 128
R_K = 0
R_V = R_K + GLA_KEY
R_Q = R_V + D_GLA
R_G = R_Q + GLA_KEY
R_A = R_G + D_GLA
R_B = R_A + D_CONV
R_GK = R_B + D_CONV
R_END = R_GK + V7X_LANES

V7X_VMEM_SCOPED_LIMIT_BYTES = 60000 * 1024
V7X_BF16_SUBLANES = 16

FFN_TM = 512
FFN_TF = 512
FFN_TF_CAST = 256
PROJ_TM = 512
OUT_TM = 512
MOD_TN = 1024
GLA_TB = 512
GLA_SUB = 256
CONV_CB = 128
CONV_UNROLL = 8
WPREP_TC = 256

NT_DIMS = (((1,), (1,)), ((), ()))
TN_DIMS = (((0,), (0,)), ((), ()))


def _params(*semantics):
    return pltpu.CompilerParams(dimension_semantics=semantics,
                                vmem_limit_bytes=V7X_VMEM_SCOPED_LIMIT_BYTES)


def _silu(x):
    return x * jax.nn.sigmoid(x)


def _rmsnorm_rows(x, gain, eps):
    ms = jnp.mean(x * x, axis=-1, keepdims=True)
    return x * lax.rsqrt(ms + eps) * gain


def _modulate(x, gain, shift, scale):
    return _rmsnorm_rows(x, gain, RMS_EPS) * (1.0 + scale) + shift


def _log_sigmoid(z):
    return jnp.minimum(z, 0.0) - jnp.log1p(jnp.exp(-jnp.abs(z)))


class _CastJob(NamedTuple):
    array: jax.Array
    block: tuple
    index_map: object


def _cast_specs(jobs):
    for job in jobs:
        assert all(n % b == 0 for n, b in zip(job.array.shape, job.block))
        assert job.block[0] % V7X_BF16_SUBLANES == 0 and job.block[1] % V7X_LANES == 0
    specs = [pl.BlockSpec(job.block, job.index_map) for job in jobs]
    shapes = [jax.ShapeDtypeStruct(job.array.shape, BF16) for job in jobs]
    return specs, shapes


def _run_casts(src_refs, dst_refs):
    for src_ref, dst_ref in zip(src_refs, dst_refs):
        dst_ref[...] = src_ref[...].astype(dst_ref.dtype)


def _mod_kernel(s_ref, w_ref, b_ref, o_ref):
    s = _silu(s_ref[...]).astype(BF16)
    o_ref[...] = jnp.dot(s, w_ref[...].astype(BF16), preferred_element_type=F32) + b_ref[...]


def _mod_call(s_in, w_mod, b_mod):
    rows, d = s_in.shape
    n = w_mod.shape[1]
    return pl.pallas_call(
        _mod_kernel,
        grid=(n // MOD_TN,),
        in_specs=[
            pl.BlockSpec((rows, d), lambda j: (0, 0)),
            pl.BlockSpec((d, MOD_TN), lambda j: (0, j)),
            pl.BlockSpec((1, MOD_TN), lambda j: (0, j)),
        ],
        out_specs=pl.BlockSpec((rows, MOD_TN), lambda j: (0, j)),
        out_shape=jax.ShapeDtypeStruct((rows, n), F32),
        compiler_params=_params("arbitrary"),
        name="mod",
    )(s_in, w_mod, b_mod)


class _FfnCfg(NamedTuple):
    mod_row: int
    emit_h: bool
    emit_hx: bool
    final_norm: bool
    cast_w: bool


def _ffn_kernel(cfg, x_ref, mod_ref, g_in_ref, g_next_ref, wg_ref, wu_ref, wo_ref, *rest):
    outs = []
    for flag in (cfg.emit_h, cfg.emit_hx, cfg.cast_w, cfg.cast_w, cfg.cast_w):
        outs.append(rest[0] if flag else None)
        rest = rest[1:] if flag else rest
    h_ref, hx_ref, wg_bf_ref, wu_bf_ref, wo_bf_ref = outs
    hm_ref, acc_ref = rest
    j = pl.program_id(1)
    last = pl.num_programs(1) - 1
    r0 = cfg.mod_row

    def partial_out(hm):
        wg, wu, wo = wg_ref[...], wu_ref[...], wo_ref[...]
        if cfg.cast_w:
            wg, wu, wo = wg.astype(BF16), wu.astype(BF16), wo.astype(BF16)
            wg_bf_ref[...] = wg
            wu_bf_ref[...] = wu
            wo_bf_ref[...] = wo
        gate = jnp.dot(hm, wg, preferred_element_type=F32)
        up = jnp.dot(hm, wu, preferred_element_type=F32)
        act = (_silu(gate) * up).astype(BF16)
        return jnp.dot(act, wo, preferred_element_type=F32)

    @pl.when(j == 0)
    def _():
        hm = _modulate(x_ref[...], g_in_ref[...], mod_ref[r0:r0 + 1, :], mod_ref[r0 + 1:r0 + 2, :])
        hm = hm.astype(BF16)
        hm_ref[...] = hm
        acc_ref[...] = partial_out(hm)

    @pl.when((j > 0) & (j < last))
    def _():
        acc_ref[...] += partial_out(hm_ref[...])

    @pl.when(j == last)
    def _():
        g = mod_ref[r0 + 2:r0 + 3, :]
        h = x_ref[...] + (0.5 * g) * (acc_ref[...] + partial_out(hm_ref[...]))
        if cfg.emit_hx:
            hx = _modulate(h, g_next_ref[...], mod_ref[r0 + 3:r0 + 4, :], mod_ref[r0 + 4:r0 + 5, :])
            hx_ref[...] = hx.astype(BF16)
        if cfg.emit_h:
            h_ref[...] = _rmsnorm_rows(h, g_next_ref[...], RMS_EPS) if cfg.final_norm else h


def _ffn_call(x, mod, group_of_tile, g_in, g_next, wg, wu, wo, up_col0, cfg, name):
    rows, d = x.shape
    n_ff = wo.shape[0]
    tf = FFN_TF_CAST if cfg.cast_w else FFN_TF
    n_j = n_ff // tf
    assert n_j >= 2, "first and last hidden block must be distinct grid steps"
    assert not cfg.cast_w or rows == FFN_TM, "bf16 weight copies are written by one row tile only"
    up_blk0 = up_col0 // tf
    row_spec = pl.BlockSpec((FFN_TM, d), lambda i, j: (i, 0))
    vec_spec = pl.BlockSpec((1, d), lambda i, j: (0, 0))
    wg_spec = pl.BlockSpec((d, tf), lambda i, j: (0, j))
    wu_spec = pl.BlockSpec((d, tf), lambda i, j: (0, up_blk0 + j))
    wo_spec = pl.BlockSpec((tf, d), lambda i, j: (j, 0))
    in_specs = [row_spec, pl.BlockSpec((None, N_MOD, d), lambda i, j: (group_of_tile(i), 0, 0)),
                vec_spec, vec_spec, wg_spec, wu_spec, wo_spec]
    out_specs, out_shape = [], []
    if cfg.emit_h:
        out_specs.append(row_spec)
        out_shape.append(jax.ShapeDtypeStruct((rows, d), F32))
    if cfg.emit_hx:
        out_specs.append(row_spec)
        out_shape.append(jax.ShapeDtypeStruct((rows, d), BF16))
    if cfg.cast_w:
        out_specs += [wg_spec, pl.BlockSpec((d, tf), lambda i, j: (0, j)), wo_spec]
        out_shape += [jax.ShapeDtypeStruct((d, n_ff), BF16), jax.ShapeDtypeStruct((d, n_ff), BF16),
                      jax.ShapeDtypeStruct((n_ff, d), BF16)]
    return pl.pallas_call(
        functools.partial(_ffn_kernel, cfg),
        grid=(rows // FFN_TM, n_j),
        in_specs=in_specs,
        out_specs=out_specs,
        out_shape=out_shape,
        scratch_shapes=[pltpu.VMEM((FFN_TM, d), BF16), pltpu.VMEM((FFN_TM, d), F32)],
        compiler_params=_params("parallel", "arbitrary"),
        name=name,
    )(x, mod, g_in, g_next, wg, wu, wo)


def _wprep_kernel(w_ref, o_ref):
    o_ref[R_K:R_Q, :] = w_ref[0:OFF_GKF, :].astype(BF16)
    o_ref[R_Q:R_GK, :] = w_ref[OFF_Q:OFF_GLU + 2 * D_CONV, :].astype(BF16)
    o_ref[R_GK:R_GK + 2 * GATE_RANK, :] = w_ref[OFF_GKF:CTX_COLS, :].astype(BF16)
    o_ref[R_GK + 2 * GATE_RANK:R_END, :] = jnp.zeros((V7X_LANES - 2 * GATE_RANK, o_ref.shape[1]), BF16)


def _wprep_call(w_t):
    n, d = w_t.shape
    tc = WPREP_TC
    return pl.pallas_call(
        _wprep_kernel,
        grid=(d // tc,),
        in_specs=[pl.BlockSpec((n, tc), lambda i: (0, i))],
        out_specs=pl.BlockSpec((R_END, tc), lambda i: (0, i)),
        out_shape=jax.ShapeDtypeStruct((R_END, d), BF16),
        compiler_params=_params("parallel"),
        name="wprep",
    )(w_t)


def _log_decays(p_gk, w2_ref, b2_ref):
    z = jnp.dot(p_gk.astype(BF16), w2_ref[...], preferred_element_type=F32) + b2_ref[...]
    return _log_sigmoid(z) * (1.0 / GATE_NORMALIZER)


def _chunk_cumsum(x, reverse):
    n = x.shape[0]
    pos = lax.broadcasted_iota(jnp.int32, x.shape, 0) % CHUNK
    d = 1
    while d < CHUNK:
        if reverse:
            shifted = pltpu.roll(x, n - d, 0)
            x = x + jnp.where(pos < CHUNK - d, shifted, 0.0)
        else:
            shifted = pltpu.roll(x, d, 0)
            x = x + jnp.where(pos >= d, shifted, 0.0)
        d *= 2
    return x


def _chunk_rows(x, row_in_chunk):
    return jnp.concatenate([x[c0 + row_in_chunk:c0 + row_in_chunk + 1, :]
                            for c0 in range(0, x.shape[0], CHUNK)], axis=0)


def _chunk_bcast(rows_per_chunk):
    return jnp.concatenate([jnp.broadcast_to(rows_per_chunk[c:c + 1, :], (CHUNK, rows_per_chunk.shape[1]))
                            for c in range(rows_per_chunk.shape[0])], axis=0)


GLA_PARTS = 4


def _proj_kernel(hx_ref, w_ref, w2_ref, b2_ref, v_ref, sg_ref, u_ref, qkf_ref, qkb_ref, decf_ref,
                 decb_ref):
    hx = hx_ref[...]

    def proj(lo, hi):
        return lax.dot_general(hx, w_ref[lo:hi, :], NT_DIMS, preferred_element_type=F32)

    ld = _log_decays(proj(R_GK, R_END), w2_ref, b2_ref)
    k = proj(R_K, R_V)
    q = proj(R_Q, R_G) * (GLA_DK ** -0.5)

    for reverse, ld_d, qk_ref, dec_ref in ((False, ld[:, :GLA_KEY], qkf_ref, decf_ref),
                                           (True, ld[:, GLA_KEY:], qkb_ref, decb_ref)):
        b = _chunk_cumsum(ld_d, reverse)
        b_mid_rows = _chunk_rows(b, CHUNK // 2 if reverse else CHUNK // 2 - 1)
        b_last_rows = _chunk_rows(b, 0 if reverse else CHUNK - 1)
        b_mid, b_last = _chunk_bcast(b_mid_rows), _chunk_bcast(b_last_rows)
        parts = (q * jnp.exp(b - b_mid), k * jnp.exp(b_mid - b), q * jnp.exp(b), k * jnp.exp(b_last - b))
        for h in range(GLA_HEADS):
            for p, part in enumerate(parts):
                col = (h * GLA_PARTS + p) * GLA_DK
                qk_ref[:, col:col + GLA_DK] = part[:, h * GLA_DK:(h + 1) * GLA_DK].astype(BF16)
        dec_ref[...] = jnp.exp(b_last_rows)

    v_ref[...] = proj(R_V, R_Q).astype(BF16)
    sg_ref[...] = _silu(proj(R_G, R_A)).astype(BF16)
    u_ref[...] = proj(R_A, R_B) * jax.nn.sigmoid(proj(R_B, R_GK))


def _proj_call(hx, w_r, w2pad, b2):
    rows, d = hx.shape
    tm = PROJ_TM
    row = lambda n: pl.BlockSpec((tm, n), lambda i: (i, 0))
    whole = lambda a: pl.BlockSpec(a.shape, lambda i: (0, 0), pipeline_mode=pl.Buffered(1))
    n_qk = GLA_PARTS * GLA_KEY
    outs = [(rows, tm, D_GLA, BF16), (rows, tm, D_GLA, BF16), (rows, tm, D_CONV, F32),
            (rows, tm, n_qk, BF16), (rows, tm, n_qk, BF16),
            (rows // CHUNK, tm // CHUNK, GLA_KEY, F32), (rows // CHUNK, tm // CHUNK, GLA_KEY, F32)]
    return pl.pallas_call(
        _proj_kernel,
        grid=(rows // tm,),
        in_specs=[row(d), whole(w_r), whole(w2pad), whole(b2)],
        out_specs=[pl.BlockSpec((blk, n), lambda i: (i, 0)) for _, blk, n, _ in outs],
        out_shape=[jax.ShapeDtypeStruct((r, n), dt) for r, _, n, dt in outs],
        compiler_params=_params("parallel"),
        name="proj",
    )(hx, w_r, w2pad, b2)


def _ctx_kernel(hx_ref, wkv_ref, wgk_ref, w2_ref, b2_ref, sf_ref, sb_ref):
    hx = hx_ref[...]
    t = hx.shape[0]
    kv = lax.dot_general(hx, wkv_ref[...], NT_DIMS, preferred_element_type=F32)
    ld = _log_decays(lax.dot_general(hx, wgk_ref[...], NT_DIMS, preferred_element_type=F32),
                     w2_ref, b2_ref)
    r = lax.broadcasted_iota(jnp.int32, (t, t), 0)
    c = lax.broadcasted_iota(jnp.int32, (t, t), 1)
    hi = lax.Precision.HIGHEST
    e_f = jnp.dot((c > r).astype(F32), ld[:, :GLA_KEY], preferred_element_type=F32, precision=hi)
    e_b = jnp.dot((c < r).astype(F32), ld[:, GLA_KEY:], preferred_element_type=F32, precision=hi)
    for h in range(GLA_HEADS):
        ks = slice(h * GLA_DK, (h + 1) * GLA_DK)
        k = kv[:, R_K + h * GLA_DK:R_K + (h + 1) * GLA_DK]
        v = kv[:, R_V + h * GLA_DV:R_V + (h + 1) * GLA_DV].astype(BF16)
        sf_ref[h] = lax.dot_general((k * jnp.exp(e_f[:, ks])).astype(BF16), v, TN_DIMS,
                                    preferred_element_type=F32)
        sb_ref[h] = lax.dot_general((k * jnp.exp(e_b[:, ks])).astype(BF16), v, TN_DIMS,
                                    preferred_element_type=F32)


def _ctx_call(hxc, w_r, w2pad, b2, bsz, t):
    d = hxc.shape[1]
    const = lambda a: pl.BlockSpec(a.shape, lambda b: (0, 0))
    st = pl.BlockSpec((None, GLA_HEADS, GLA_DK, GLA_DV), lambda b: (b, 0, 0, 0))
    shape = jax.ShapeDtypeStruct((bsz, GLA_HEADS, GLA_DK, GLA_DV), F32)
    return pl.pallas_call(
        _ctx_kernel,
        grid=(bsz,),
        in_specs=[pl.BlockSpec((t, d), lambda b: (b, 0)),
                  pl.BlockSpec((R_Q, d), lambda b: (0, 0)),
                  pl.BlockSpec((V7X_LANES, d), lambda b: (R_GK // V7X_LANES, 0)),
                  const(w2pad), const(b2)],
        out_specs=[st, st],
        out_shape=[shape, shape],
        compiler_params=_params("parallel"),
        name="ctx",
    )(hxc, w_r, w_r, w2pad, b2)


def _gla_direction(qk_ref, v_ref, dec_ref, s_ref, reverse):
    n_sub = GLA_TB // GLA_SUB
    n_chunk = GLA_SUB // CHUNK
    rr = lax.broadcasted_iota(jnp.int32, (GLA_SUB, GLA_SUB), 0)
    cc = lax.broadcasted_iota(jnp.int32, (GLA_SUB, GLA_SUB), 1)
    same_chunk = (rr // CHUNK) == (cc // CHUNK)
    mask = same_chunk & ((cc >= rr) if reverse else (cc <= rr))
    part = lambda rows, p: qk_ref[rows, p * GLA_DK:(p + 1) * GLA_DK]
    dec_t = dec_ref[...].T

    outs = [None] * n_sub
    subs = range(n_sub - 1, -1, -1) if reverse else range(n_sub)
    state = s_ref[...]
    for s in subs:
        rows = slice(s * GLA_SUB, (s + 1) * GLA_SUB)
        v = v_ref[rows, :]
        att = lax.dot_general(part(rows, 0), part(rows, 1), NT_DIMS, preferred_element_type=F32)
        att = jnp.where(mask, att, 0.0).astype(BF16)
        o_intra = jnp.dot(att, v, preferred_element_type=F32)
        o_parts = [None] * n_chunk
        chunks = range(n_chunk - 1, -1, -1) if reverse else range(n_chunk)
        for c in chunks:
            cr = slice(c * CHUNK, (c + 1) * CHUNK)
            tok = slice(s * GLA_SUB + c * CHUNK, s * GLA_SUB + (c + 1) * CHUNK)
            ci = s * n_chunk + c
            o_parts[c] = o_intra[cr, :] + jnp.dot(part(tok, 2), state.astype(BF16),
                                                  preferred_element_type=F32)
            kv = lax.dot_general(part(tok, 3), v[cr, :], TN_DIMS, preferred_element_type=F32)
            state = state * dec_t[:, ci:ci + 1] + kv
        outs[s] = jnp.concatenate(o_parts, axis=0)
    s_ref[...] = state
    return jnp.concatenate(outs, axis=0)


def _gla_kernel(n_cast, qkf_ref, vf_ref, decf_ref, qkb_ref, vb_ref, decb_ref, sg_ref,
                s0f_ref, s0b_ref, gain_ref, *rest):
    cast_in, o_ref, rest = rest[:n_cast], rest[n_cast], rest[n_cast + 1:]
    cast_out, (sf_ref, sb_ref, acc_ref) = rest[:n_cast], rest[n_cast:]
    nb = pl.program_id(2)
    n_blocks = pl.num_programs(2)

    @pl.when(nb == 0)
    def _():
        sf_ref[...] = s0f_ref[...]
        sb_ref[...] = s0b_ref[...]
        acc_ref[...] = jnp.zeros_like(acc_ref)

    _run_casts(cast_in, cast_out)
    o_f = _gla_direction(qkf_ref, vf_ref, decf_ref, sf_ref, reverse=False)
    o_b = _gla_direction(qkb_ref, vb_ref, decb_ref, sb_ref, reverse=True)
    row_f = pl.multiple_of(nb * GLA_TB, GLA_TB)
    row_b = pl.multiple_of((n_blocks - 1 - nb) * GLA_TB, GLA_TB)

    for row, part in ((row_f, o_f), (row_b, o_b)):
        o = acc_ref[pl.ds(row, GLA_TB), :] + part
        acc_ref[pl.ds(row, GLA_TB), :] = o
        ms = jnp.mean(o * o, axis=-1, keepdims=True)
        o = o * lax.rsqrt(ms + HEAD_NORM_EPS) * gain_ref[...]
        o_ref[pl.ds(row, GLA_TB), :] = (o * sg_ref[pl.ds(row, GLA_TB), :].astype(F32)).astype(BF16)


def _gla_call(qkf, qkb, v, decf, decb, sg, s0f, s0b, gain, bsz, t, cast_jobs=()):
    nb = t // GLA_TB
    fwd = lambda r, n: pl.BlockSpec((r, n), lambda b, h, i: (b * nb + i, h))
    bwd = lambda r, n: pl.BlockSpec((r, n), lambda b, h, i: (b * nb + nb - 1 - i, h))
    n_qk, n_dec = GLA_PARTS * GLA_DK, GLA_TB // CHUNK
    seq = pl.BlockSpec((t, GLA_DV), lambda b, h, i: (b, h))
    st = pl.BlockSpec((None, None, GLA_DK, GLA_DV), lambda b, h, i: (b, h, 0, 0))
    cast_specs, cast_shapes = _cast_specs(cast_jobs)
    return pl.pallas_call(
        functools.partial(_gla_kernel, len(cast_jobs)),
        grid=(bsz, GLA_HEADS, nb),
        in_specs=[fwd(GLA_TB, n_qk), fwd(GLA_TB, GLA_DV), fwd(n_dec, GLA_DK),
                  bwd(GLA_TB, n_qk), bwd(GLA_TB, GLA_DV), bwd(n_dec, GLA_DK),
                  seq, st, st, pl.BlockSpec((1, GLA_DV), lambda b, h, i: (0, 0))] + cast_specs,
        out_specs=[seq] + cast_specs,
        out_shape=[jax.ShapeDtypeStruct((bsz * t, D_GLA), BF16)] + cast_shapes,
        scratch_shapes=[pltpu.VMEM((GLA_DK, GLA_DV), F32), pltpu.VMEM((GLA_DK, GLA_DV), F32),
                        pltpu.VMEM((t, GLA_DV), F32)],
        compiler_params=_params("parallel", "parallel", "arbitrary"),
        name="gla",
    )(qkf, v, decf, qkb, v, decb, sg, s0f, s0b, gain, *[job.array for job in cast_jobs])


ROW_PAD = 16
ROW_PITCH = GRID_W + 2 * ROW_PAD


def _conv_kernel(n_row_blocks, n_cast, u_ref, w_ref, b_ref, *rest):
    cast_in, y_ref, rest = rest[:n_cast], rest[n_cast], rest[n_cast + 1:]
    cast_out, (pad_ref,) = rest[:n_cast], rest[n_cast:]
    cb = pl.program_id(1)
    rows = u_ref.shape[0] // GRID_W
    bias = jnp.broadcast_to(b_ref[...], (GRID_W, CONV_CB))
    _run_casts(cast_in, cast_out)

    @pl.when(cb < n_row_blocks)
    def _():
        @pl.when(cb == 0)
        def _():
            pad_ref[...] = jnp.zeros_like(pad_ref)

        def fill(r, carry):
            src = pl.multiple_of(r * GRID_W, GRID_W)
            dst = pl.multiple_of(r * ROW_PITCH + ROW_PAD, 8)
            pad_ref[pl.ds(dst, GRID_W), :] = u_ref[pl.ds(src, GRID_W), :]
            return carry

        lax.fori_loop(0, rows, fill, 0)

        def body(r, carry):
            base = r * ROW_PITCH + (ROW_PAD - CONV_HALF)
            acc = bias
            for j in range(CONV_WIDTH):
                acc = acc + w_ref[j:j + 1, :] * pad_ref[pl.ds(base + j, GRID_W), :]
            y_ref[pl.ds(pl.multiple_of(r * GRID_W, GRID_W), GRID_W), :] = acc
            return carry

        lax.fori_loop(0, rows, body, 0, unroll=CONV_UNROLL)

    @pl.when(cb >= n_row_blocks)
    def _():
        edge = CONV_HALF * GRID_W

        @pl.when(cb == n_row_blocks)
        def _():
            pad_ref[0:edge, :] = jnp.zeros((edge, CONV_CB), F32)
            pad_ref[edge + rows * GRID_W:2 * edge + rows * GRID_W, :] = jnp.zeros((edge, CONV_CB), F32)

        pad_ref[edge:edge + rows * GRID_W, :] = u_ref[...]

        def body(r, carry):
            acc = bias
            for j in range(CONV_WIDTH):
                src = pl.multiple_of((r + j) * GRID_W, GRID_W)
                acc = acc + w_ref[j:j + 1, :] * pad_ref[pl.ds(src, GRID_W), :]
            y_ref[pl.ds(pl.multiple_of(r * GRID_W, GRID_W), GRID_W), :] = acc
            return carry

        lax.fori_loop(0, rows, body, 0, unroll=CONV_UNROLL)


def _conv_call(u, w, b, bsz, t, cast_jobs=()):
    ch = u.shape[1]
    rows = t // GRID_W
    n_cb = ch // CONV_CB
    pad_rows = max(rows * ROW_PITCH, (rows + 2 * CONV_HALF) * GRID_W)
    blk = pl.BlockSpec((t, CONV_CB), lambda bi, c: (bi, c))
    cast_specs, cast_shapes = _cast_specs(cast_jobs)
    return pl.pallas_call(
        functools.partial(_conv_kernel, n_cb // 2, len(cast_jobs)),
        grid=(bsz, n_cb),
        in_specs=[blk, pl.BlockSpec((CONV_WIDTH, CONV_CB), lambda bi, c: (0, c)),
                  pl.BlockSpec((1, CONV_CB), lambda bi, c: (0, c))] + cast_specs,
        out_specs=[blk] + cast_specs,
        out_shape=[jax.ShapeDtypeStruct(u.shape, F32)] + cast_shapes,
        scratch_shapes=[pltpu.VMEM((pad_rows, CONV_CB), F32)],
        compiler_params=_params("arbitrary", "arbitrary"),
        name="conv",
    )(u, w, b, *[job.array for job in cast_jobs])


def _outproj_kernel(og_ref, y_ref, lng_ref, lnb_ref, w_ref, h_ref, mod_ref, o_ref):
    y = y_ref[...]
    mu = jnp.mean(y, axis=-1, keepdims=True)
    yc = y - mu
    var = jnp.mean(yc * yc, axis=-1, keepdims=True)
    yn = yc * lax.rsqrt(var + LN_EPS) * lng_ref[...] + lnb_ref[...]
    oc = _silu(yn).astype(BF16)
    res = jnp.dot(og_ref[...], w_ref[0:D_GLA, :], preferred_element_type=F32)
    res = res + jnp.dot(oc, w_ref[D_GLA:D_GLA + D_CONV, :], preferred_element_type=F32)
    o_ref[...] = h_ref[...] + mod_ref[5:6, :] * res


def _outproj_call(og, y, ln_g, ln_b, w_out_bf, h, mod, group_of_tile):
    rows, d = h.shape
    tm = OUT_TM
    row = lambda n: pl.BlockSpec((tm, n), lambda i: (i, 0))
    vec = lambda n: pl.BlockSpec((1, n), lambda i: (0, 0))
    return pl.pallas_call(
        _outproj_kernel,
        grid=(rows // tm,),
        in_specs=[row(D_GLA), row(D_CONV), vec(D_CONV), vec(D_CONV),
                  pl.BlockSpec(w_out_bf.shape, lambda i: (0, 0), pipeline_mode=pl.Buffered(1)),
                  row(d), pl.BlockSpec((None, N_MOD, d), lambda i: (group_of_tile(i), 0, 0))],
        out_specs=row(d),
        out_shape=jax.ShapeDtypeStruct((rows, d), F32),
        compiler_params=_params("parallel"),
        name="outproj",
    )(og, y, ln_g, ln_b, w_out_bf, h, mod)


def kernel(x, c, ctx, c_ctx, w_mod, b_mod, norm_ffn1, w_ffn1_in, w_ffn1_out, norm_mix, w_in, w_gk2, b_gk2, gla_norm, conv_w, conv_b, conv_ln_g, conv_ln_b, w_out, norm_ffn2, w_ffn2_in, w_ffn2_out, norm_final):
    bsz, t, d = x.shape
    t_ctx = ctx.shape[1]
    assert w_mod.shape[0] == 1, "single layer only"
    assert t % GLA_TB == 0 and (t // GLA_TB) % 2 == 0 and t % FFN_TM == 0
    vec = lambda a: a.reshape(1, -1)

    w2pad = jnp.zeros((V7X_LANES, 2 * GLA_KEY), BF16)
    w2pad = w2pad.at[:GATE_RANK, :GLA_KEY].set(w_gk2[0, 0].astype(BF16))
    w2pad = w2pad.at[GATE_RANK:2 * GATE_RANK, GLA_KEY:].set(w_gk2[0, 1].astype(BF16))
    b2 = b_gk2[0].reshape(1, 2 * GLA_KEY)
    w_r = _wprep_call(jnp.swapaxes(w_in[0], 0, 1))

    n_rows = 8
    s_in = jnp.concatenate([c, c_ctx[None, :], jnp.zeros((n_rows - bsz - 1, d), F32)], axis=0)
    mod = _mod_call(s_in, w_mod[0], vec(b_mod[0])).reshape(n_rows, N_MOD, d)

    tiles_per_batch = t // FFN_TM
    lat_group = lambda i: i // tiles_per_batch
    ctx_group = lambda i: bsz
    g1, gm = vec(norm_ffn1[0]), vec(norm_mix[0])

    xc = ctx.reshape(bsz * t_ctx, d)
    hxc, wg1, wu1, wo1 = _ffn_call(
        xc, mod, ctx_group, g1, gm, w_ffn1_in[0], w_ffn1_in[0], w_ffn1_out[0], D_FF,
        _FfnCfg(mod_row=0, emit_h=False, emit_hx=True, final_norm=False, cast_w=True), "ffn1_ctx")
    s0f, s0b = _ctx_call(hxc, w_r, w2pad, b2, bsz, t_ctx)

    xl = x.reshape(bsz * t, d)
    h1, hx = _ffn_call(
        xl, mod, lat_group, g1, gm, wg1, wu1, wo1, 0,
        _FfnCfg(mod_row=0, emit_h=True, emit_hx=True, final_norm=False, cast_w=False), "ffn1")

    v, sg, u, qkf, qkb, decf, decb = _proj_call(hx, w_r, w2pad, b2)

    n_gla_blocks = t // GLA_TB
    n_bh = bsz * GLA_HEADS
    og, wf2_out, w_out_bf = _gla_call(
        qkf, qkb, v, decf, decb, sg, s0f, s0b, vec(gla_norm[0]), bsz, t,
        (_CastJob(w_ffn2_out[0], (D_FF // n_gla_blocks, d // n_bh),
                  lambda b, h, i: (i, b * GLA_HEADS + h)),
         _CastJob(w_out[0], (d // n_gla_blocks, d // n_bh), lambda b, h, i: (i, b * GLA_HEADS + h))))
    n_cb = D_CONV // CONV_CB
    y, wf2_in = _conv_call(
        u, conv_w[0], vec(conv_b[0]), bsz, t,
        (_CastJob(w_ffn2_in[0], (d // (bsz * n_cb), 2 * D_FF), lambda bi, cb: (bi * n_cb + cb, 0)),))
    h2 = _outproj_call(og, y, vec(conv_ln_g[0]), vec(conv_ln_b[0]), w_out_bf, h1, mod,
                       lambda i: i // (t // OUT_TM))

    (out,) = _ffn_call(
        h2, mod, lat_group, vec(norm_ffn2[0]), vec(norm_final), wf2_in, wf2_in, wf2_out, D_FF,
        _FfnCfg(mod_row=6, emit_h=True, emit_hx=False, final_norm=True, cast_w=False), "ffn2")
    return out.reshape(bsz, t, d)
```

```python
import functools
from typing import NamedTuple

import jax
import jax.numpy as jnp
from jax import lax
from jax.experimental import pallas as pl
from jax.experimental.pallas import tpu as pltpu

F32 = jnp.float32
BF16 = jnp.bfloat16

D_MODEL = 2048
GRID_W = 64
GLA_HEADS = 4
GLA_DK = 128
GLA_DV = 256
GLA_KEY = GLA_HEADS * GLA_DK
D_GLA = GLA_HEADS * GLA_DV
D_CONV = 1024
GATE_RANK = 16
GATE_NORMALIZER = 16.0
CHUNK = 64
CONV_WIDTH = 31
CONV_HALF = CONV_WIDTH // 2
D_FF = 5632
N_MOD = 9
RMS_EPS = 1e-6
HEAD_NORM_EPS = 1e-5
LN_EPS = 1e-5

OFF_V = GLA_KEY
OFF_GKF = OFF_V + D_GLA
CTX_COLS = OFF_GKF + 2 * GATE_RANK
OFF_Q = CTX_COLS
OFF_G = OFF_Q + GLA_KEY
OFF_GLU = OFF_G + D_GLA

V7X_LANES = 128
R_K = 0
R_V = R_K + GLA_KEY
R_Q = R_V + D_GLA
R_G = R_Q + GLA_KEY
R_A = R_G + D_GLA
R_B = R_A + D_CONV
R_GK = R_B + D_CONV
R_END = R_GK + V7X_LANES

V7X_VMEM_SCOPED_LIMIT_BYTES = 60000 * 1024
V7X_BF16_SUBLANES = 16

FFN_TM = 512
FFN_TF = 512
FFN_TF_CAST = 256
PROJ_TM = 512
OUT_TM = 512
MOD_TN = 1024
GLA_TB = 1024
GLA_SUB = 256
CONV_CB = 128
CONV_UNROLL = 8
WPREP_TC = 256

NT_DIMS = (((1,), (1,)), ((), ()))
TN_DIMS = (((0,), (0,)), ((), ()))


def _params(*semantics):
    return pltpu.CompilerParams(dimension_semantics=semantics,
                                vmem_limit_bytes=V7X_VMEM_SCOPED_LIMIT_BYTES)


def _silu(x):
    return x * jax.nn.sigmoid(x)


def _rmsnorm_rows(x, gain, eps):
    ms = jnp.mean(x * x, axis=-1, keepdims=True)
    return x * lax.rsqrt(ms + eps) * gain


def _modulate(x, gain, shift, scale):
    return _rmsnorm_rows(x, gain, RMS_EPS) * (1.0 + scale) + shift


def _log_sigmoid(z):
    return jnp.minimum(z, 0.0) - jnp.log1p(jnp.exp(-jnp.abs(z)))


class _CastJob(NamedTuple):
    array: jax.Array
    block: tuple
    index_map: object


def _cast_specs(jobs):
    for job in jobs:
        assert all(n % b == 0 for n, b in zip(job.array.shape, job.block))
        assert job.block[0] % V7X_BF16_SUBLANES == 0 and job.block[1] % V7X_LANES == 0
    specs = [pl.BlockSpec(job.block, job.index_map) for job in jobs]
    shapes = [jax.ShapeDtypeStruct(job.array.shape, BF16) for job in jobs]
    return specs, shapes


def _run_casts(src_refs, dst_refs):
    for src_ref, dst_ref in zip(src_refs, dst_refs):
        dst_ref[...] = src_ref[...].astype(dst_ref.dtype)


def _mod_kernel(s_ref, w_ref, b_ref, o_ref):
    s = _silu(s_ref[...]).astype(BF16)
    o_ref[...] = jnp.dot(s, w_ref[...].astype(BF16), preferred_element_type=F32) + b_ref[...]


def _mod_call(s_in, w_mod, b_mod):
    rows, d = s_in.shape
    n = w_mod.shape[1]
    return pl.pallas_call(
        _mod_kernel,
        grid=(n // MOD_TN,),
        in_specs=[
            pl.BlockSpec((rows, d), lambda j: (0, 0)),
            pl.BlockSpec((d, MOD_TN), lambda j: (0, j)),
            pl.BlockSpec((1, MOD_TN), lambda j: (0, j)),
        ],
        out_specs=pl.BlockSpec((rows, MOD_TN), lambda j: (0, j)),
        out_shape=jax.ShapeDtypeStruct((rows, n), F32),
        compiler_params=_params("arbitrary"),
        name="mod",
    )(s_in, w_mod, b_mod)


class _FfnCfg(NamedTuple):
    mod_row: int
    emit_h: bool
    emit_hx: bool
    final_norm: bool
    cast_w: bool


def _ffn_kernel(cfg, x_ref, mod_ref, g_in_ref, g_next_ref, wg_ref, wu_ref, wo_ref, *rest):
    outs = []
    for flag in (cfg.emit_h, cfg.emit_hx, cfg.cast_w, cfg.cast_w, cfg.cast_w):
        outs.append(rest[0] if flag else None)
        rest = rest[1:] if flag else rest
    h_ref, hx_ref, wg_bf_ref, wu_bf_ref, wo_bf_ref = outs
    hm_ref, acc_ref = rest
    j = pl.program_id(1)
    last = pl.num_programs(1) - 1
    r0 = cfg.mod_row

    def partial_out(hm):
        wg, wu, wo = wg_ref[...], wu_ref[...], wo_ref[...]
        if cfg.cast_w:
            wg, wu, wo = wg.astype(BF16), wu.astype(BF16), wo.astype(BF16)
            wg_bf_ref[...] = wg
            wu_bf_ref[...] = wu
            wo_bf_ref[...] = wo
        gate = jnp.dot(hm, wg, preferred_element_type=F32)
        up = jnp.dot(hm, wu, preferred_element_type=F32)
        act = (_silu(gate) * up).astype(BF16)
        return jnp.dot(act, wo, preferred_element_type=F32)

    @pl.when(j == 0)
    def _():
        hm = _modulate(x_ref[...], g_in_ref[...], mod_ref[r0:r0 + 1, :], mod_ref[r0 + 1:r0 + 2, :])
        hm = hm.astype(BF16)
        hm_ref[...] = hm
        acc_ref[...] = partial_out(hm)

    @pl.when((j > 0) & (j < last))
    def _():
        acc_ref[...] += partial_out(hm_ref[...])

    @pl.when(j == last)
    def _():
        g = mod_ref[r0 + 2:r0 + 3, :]
        h = x_ref[...] + (0.5 * g) * (acc_ref[...] + partial_out(hm_ref[...]))
        if cfg.emit_hx:
            hx = _modulate(h, g_next_ref[...], mod_ref[r0 + 3:r0 + 4, :], mod_ref[r0 + 4:r0 + 5, :])
            hx_ref[...] = hx.astype(BF16)
        if cfg.emit_h:
            h_ref[...] = _rmsnorm_rows(h, g_next_ref[...], RMS_EPS) if cfg.final_norm else h


def _ffn_call(x, mod, group_of_tile, g_in, g_next, wg, wu, wo, up_col0, cfg, name):
    rows, d = x.shape
    n_ff = wo.shape[0]
    tf = FFN_TF_CAST if cfg.cast_w else FFN_TF
    n_j = n_ff // tf
    assert n_j >= 2, "first and last hidden block must be distinct grid steps"
    assert not cfg.cast_w or rows == FFN_TM, "bf16 weight copies are written by one row tile only"
    up_blk0 = up_col0 // tf
    row_spec = pl.BlockSpec((FFN_TM, d), lambda i, j: (i, 0))
    vec_spec = pl.BlockSpec((1, d), lambda i, j: (0, 0))
    wg_spec = pl.BlockSpec((d, tf), lambda i, j: (0, j))
    wu_spec = pl.BlockSpec((d, tf), lambda i, j: (0, up_blk0 + j))
    wo_spec = pl.BlockSpec((tf, d), lambda i, j: (j, 0))
    in_specs = [row_spec, pl.BlockSpec((None, N_MOD, d), lambda i, j: (group_of_tile(i), 0, 0)),
                vec_spec, vec_spec, wg_spec, wu_spec, wo_spec]
    out_specs, out_shape = [], []
    if cfg.emit_h:
        out_specs.append(row_spec)
        out_shape.append(jax.ShapeDtypeStruct((rows, d), F32))
    if cfg.emit_hx:
        out_specs.append(row_spec)
        out_shape.append(jax.ShapeDtypeStruct((rows, d), BF16))
    if cfg.cast_w:
        out_specs += [wg_spec, pl.BlockSpec((d, tf), lambda i, j: (0, j)), wo_spec]
        out_shape += [jax.ShapeDtypeStruct((d, n_ff), BF16), jax.ShapeDtypeStruct((d, n_ff), BF16),
                      jax.ShapeDtypeStruct((n_ff, d), BF16)]
    return pl.pallas_call(
        functools.partial(_ffn_kernel, cfg),
        grid=(rows // FFN_TM, n_j),
        in_specs=in_specs,
        out_specs=out_specs,
        out_shape=out_shape,
        scratch_shapes=[pltpu.VMEM((FFN_TM, d), BF16), pltpu.VMEM((FFN_TM, d), F32)],
        compiler_params=_params("parallel", "arbitrary"),
        name=name,
    )(x, mod, g_in, g_next, wg, wu, wo)


def _wprep_kernel(w_ref, o_ref):
    o_ref[R_K:R_Q, :] = w_ref[0:OFF_GKF, :].astype(BF16)
    o_ref[R_Q:R_GK, :] = w_ref[OFF_Q:OFF_GLU + 2 * D_CONV, :].astype(BF16)
    o_ref[R_GK:R_GK + 2 * GATE_RANK, :] = w_ref[OFF_GKF:CTX_COLS, :].astype(BF16)
    o_ref[R_GK + 2 * GATE_RANK:R_END, :] = jnp.zeros((V7X_LANES - 2 * GATE_RANK, o_ref.shape[1]), BF16)


def _wprep_call(w_t):
    n, d = w_t.shape
    tc = WPREP_TC
    return pl.pallas_call(
        _wprep_kernel,
        grid=(d // tc,),
        in_specs=[pl.BlockSpec((n, tc), lambda i: (0, i))],
        out_specs=pl.BlockSpec((R_END, tc), lambda i: (0, i)),
        out_shape=jax.ShapeDtypeStruct((R_END, d), BF16),
        compiler_params=_params("parallel"),
        name="wprep",
    )(w_t)


def _log_decays(p_gk, w2_ref, b2_ref):
    z = jnp.dot(p_gk.astype(BF16), w2_ref[...], preferred_element_type=F32) + b2_ref[...]
    return _log_sigmoid(z) * (1.0 / GATE_NORMALIZER)


def _chunk_cumsum(x, reverse):
    n = x.shape[0]
    pos = lax.broadcasted_iota(jnp.int32, x.shape, 0) % CHUNK
    d = 1
    while d < CHUNK:
        if reverse:
            shifted = pltpu.roll(x, n - d, 0)
            x = x + jnp.where(pos < CHUNK - d, shifted, 0.0)
        else:
            shifted = pltpu.roll(x, d, 0)
            x = x + jnp.where(pos >= d, shifted, 0.0)
        d *= 2
    return x


def _chunk_rows(x, row_in_chunk):
    return jnp.concatenate([x[c0 + row_in_chunk:c0 + row_in_chunk + 1, :]
                            for c0 in range(0, x.shape[0], CHUNK)], axis=0)


def _chunk_bcast(rows_per_chunk):
    return jnp.concatenate([jnp.broadcast_to(rows_per_chunk[c:c + 1, :], (CHUNK, rows_per_chunk.shape[1]))
                            for c in range(rows_per_chunk.shape[0])], axis=0)


QKV_HEAD = 2 * GLA_DK + GLA_DV
LD_HEAD = 2 * GLA_DK


def _proj_kernel(hx_ref, w_ref, w2_ref, b2_ref, qkv_ref, ld_ref, sg_ref, u_ref):
    hx = hx_ref[...]

    def proj(lo, hi):
        return lax.dot_general(hx, w_ref[lo:hi, :], NT_DIMS, preferred_element_type=F32)

    k = proj(R_K, R_V).astype(BF16)
    v = proj(R_V, R_Q).astype(BF16)
    q = (proj(R_Q, R_G) * (GLA_DK ** -0.5)).astype(BF16)
    ld = _log_decays(proj(R_GK, R_END), w2_ref, b2_ref)
    for h in range(GLA_HEADS):
        dk = slice(h * GLA_DK, (h + 1) * GLA_DK)
        c0 = h * QKV_HEAD
        qkv_ref[:, c0:c0 + GLA_DK] = q[:, dk]
        qkv_ref[:, c0 + GLA_DK:c0 + 2 * GLA_DK] = k[:, dk]
        qkv_ref[:, c0 + 2 * GLA_DK:c0 + QKV_HEAD] = v[:, h * GLA_DV:(h + 1) * GLA_DV]
        ld_ref[:, h * LD_HEAD:h * LD_HEAD + GLA_DK] = ld[:, dk]
        ld_ref[:, h * LD_HEAD + GLA_DK:(h + 1) * LD_HEAD] = ld[:, GLA_KEY + h * GLA_DK:GLA_KEY + (h + 1) * GLA_DK]
    sg_ref[...] = _silu(proj(R_G, R_A)).astype(BF16)
    u_ref[...] = proj(R_A, R_B) * jax.nn.sigmoid(proj(R_B, R_GK))


def _proj_call(hx, w_r, w2pad, b2):
    rows, d = hx.shape
    tm = PROJ_TM
    row = lambda n: pl.BlockSpec((tm, n), lambda i: (i, 0))
    whole = lambda a: pl.BlockSpec(a.shape, lambda i: (0, 0), pipeline_mode=pl.Buffered(1))
    outs = [(GLA_HEADS * QKV_HEAD, BF16), (GLA_HEADS * LD_HEAD, F32), (D_GLA, BF16), (D_CONV, F32)]
    return pl.pallas_call(
        _proj_kernel,
        grid=(rows // tm,),
        in_specs=[row(d), whole(w_r), whole(w2pad), whole(b2)],
        out_specs=[row(n) for n, _ in outs],
        out_shape=[jax.ShapeDtypeStruct((rows, n), dt) for n, dt in outs],
        compiler_params=_params("parallel"),
        name="proj",
    )(hx, w_r, w2pad, b2)


def _ctx_kernel(hx_ref, wkv_ref, wgk_ref, w2_ref, b2_ref, sf_ref, sb_ref):
    hx = hx_ref[...]
    t = hx.shape[0]
    kv = lax.dot_general(hx, wkv_ref[...], NT_DIMS, preferred_element_type=F32)
    ld = _log_decays(lax.dot_general(hx, wgk_ref[...], NT_DIMS, preferred_element_type=F32),
                     w2_ref, b2_ref)
    r = lax.broadcasted_iota(jnp.int32, (t, t), 0)
    c = lax.broadcasted_iota(jnp.int32, (t, t), 1)
    hi = lax.Precision.HIGHEST
    e_f = jnp.dot((c > r).astype(F32), ld[:, :GLA_KEY], preferred_element_type=F32, precision=hi)
    e_b = jnp.dot((c < r).astype(F32), ld[:, GLA_KEY:], preferred_element_type=F32, precision=hi)
    for h in range(GLA_HEADS):
        ks = slice(h * GLA_DK, (h + 1) * GLA_DK)
        k = kv[:, R_K + h * GLA_DK:R_K + (h + 1) * GLA_DK]
        v = kv[:, R_V + h * GLA_DV:R_V + (h + 1) * GLA_DV].astype(BF16)
        sf_ref[h] = lax.dot_general(v, (k * jnp.exp(e_f[:, ks])).astype(BF16), TN_DIMS,
                                    preferred_element_type=F32)
        sb_ref[h] = lax.dot_general(v, (k * jnp.exp(e_b[:, ks])).astype(BF16), TN_DIMS,
                                    preferred_element_type=F32)


def _ctx_call(hxc, w_r, w2pad, b2, bsz, t):
    d = hxc.shape[1]
    const = lambda a: pl.BlockSpec(a.shape, lambda b: (0, 0))
    st = pl.BlockSpec((None, GLA_HEADS, GLA_DV, GLA_DK), lambda b: (b, 0, 0, 0))
    shape = jax.ShapeDtypeStruct((bsz, GLA_HEADS, GLA_DV, GLA_DK), F32)
    return pl.pallas_call(
        _ctx_kernel,
        grid=(bsz,),
        in_specs=[pl.BlockSpec((t, d), lambda b: (b, 0)),
                  pl.BlockSpec((R_Q, d), lambda b: (0, 0)),
                  pl.BlockSpec((V7X_LANES, d), lambda b: (R_GK // V7X_LANES, 0)),
                  const(w2pad), const(b2)],
        out_specs=[st, st],
        out_shape=[shape, shape],
        compiler_params=_params("parallel"),
        name="ctx",
    )(hxc, w_r, w_r, w2pad, b2)


def _gla_direction(qkv_ref, ld_ref, s_ref, reverse):
    n_sub = GLA_TB // GLA_SUB
    n_chunk = GLA_SUB // CHUNK
    rr = lax.broadcasted_iota(jnp.int32, (GLA_SUB, GLA_SUB), 0)
    cc = lax.broadcasted_iota(jnp.int32, (GLA_SUB, GLA_SUB), 1)
    same_chunk = (rr // CHUNK) == (cc // CHUNK)
    mask = same_chunk & ((cc >= rr) if reverse else (cc <= rr))
    mid_row = CHUNK // 2 if reverse else CHUNK // 2 - 1
    last_row = 0 if reverse else CHUNK - 1
    ld0 = GLA_DK if reverse else 0

    outs = [None] * n_sub
    subs = range(n_sub - 1, -1, -1) if reverse else range(n_sub)
    state = s_ref[...]
    for s in subs:
        rows = slice(s * GLA_SUB, (s + 1) * GLA_SUB)
        b = _chunk_cumsum(ld_ref[rows, ld0:ld0 + GLA_DK], reverse)
        b_mid = _chunk_bcast(_chunk_rows(b, mid_row))
        b_last_rows = _chunk_rows(b, last_row)
        b_last = _chunk_bcast(b_last_rows)
        q = qkv_ref[rows, 0:GLA_DK].astype(F32)
        k = qkv_ref[rows, GLA_DK:2 * GLA_DK].astype(F32)
        v = qkv_ref[rows, 2 * GLA_DK:QKV_HEAD]
        qs = (q * jnp.exp(b - b_mid)).astype(BF16)
        ks = (k * jnp.exp(b_mid - b)).astype(BF16)
        qi = (q * jnp.exp(b)).astype(BF16)
        kd = (k * jnp.exp(b_last - b)).astype(BF16)
        att = lax.dot_general(qs, ks, NT_DIMS, preferred_element_type=F32)
        att = jnp.where(mask, att, 0.0).astype(BF16)
        o_intra = jnp.dot(att, v, preferred_element_type=F32)
        decay = jnp.exp(b_last_rows)
        o_parts = [None] * n_chunk
        chunks = range(n_chunk - 1, -1, -1) if reverse else range(n_chunk)
        for c in chunks:
            cr = slice(c * CHUNK, (c + 1) * CHUNK)
            o_parts[c] = o_intra[cr, :] + lax.dot_general(
                qi[cr, :], state.astype(BF16), NT_DIMS, preferred_element_type=F32)
            kv = lax.dot_general(v[cr, :], kd[cr, :], TN_DIMS, preferred_element_type=F32)
            state = state * decay[c:c + 1, :] + kv
        outs[s] = jnp.concatenate(o_parts, axis=0)
    s_ref[...] = state
    return jnp.concatenate(outs, axis=0)


def _gla_kernel(qkvf_ref, ldf_ref, qkvb_ref, ldb_ref, sg_ref, s0f_ref, s0b_ref, gain_ref, o_ref,
                sf_ref, sb_ref, acc_ref):
    nb = pl.program_id(2)
    n_blocks = pl.num_programs(2)

    @pl.when(nb == 0)
    def _():
        sf_ref[...] = s0f_ref[...]
        sb_ref[...] = s0b_ref[...]
        acc_ref[...] = jnp.zeros_like(acc_ref)

    o_f = _gla_direction(qkvf_ref, ldf_ref, sf_ref, reverse=False)
    o_b = _gla_direction(qkvb_ref, ldb_ref, sb_ref, reverse=True)
    row_f = pl.multiple_of(nb * GLA_TB, GLA_TB)
    row_b = pl.multiple_of((n_blocks - 1 - nb) * GLA_TB, GLA_TB)

    for row, part in ((row_f, o_f), (row_b, o_b)):
        o = acc_ref[pl.ds(row, GLA_TB), :] + part
        acc_ref[pl.ds(row, GLA_TB), :] = o
        ms = jnp.mean(o * o, axis=-1, keepdims=True)
        o = o * lax.rsqrt(ms + HEAD_NORM_EPS) * gain_ref[...]
        o_ref[pl.ds(row, GLA_TB), :] = (o * sg_ref[pl.ds(row, GLA_TB), :].astype(F32)).astype(BF16)


def _gla_call(qkv, ld, sg, s0f, s0b, gain, bsz, t):
    nb = t // GLA_TB
    fwd = lambda n: pl.BlockSpec((GLA_TB, n), lambda b, h, i: (b * nb + i, h))
    bwd = lambda n: pl.BlockSpec((GLA_TB, n), lambda b, h, i: (b * nb + nb - 1 - i, h))
    seq = pl.BlockSpec((t, GLA_DV), lambda b, h, i: (b, h))
    st = pl.BlockSpec((None, None, GLA_DV, GLA_DK), lambda b, h, i: (b, h, 0, 0))
    return pl.pallas_call(
        _gla_kernel,
        grid=(bsz, GLA_HEADS, nb),
        in_specs=[fwd(QKV_HEAD), fwd(LD_HEAD), bwd(QKV_HEAD), bwd(LD_HEAD),
                  seq, st, st, pl.BlockSpec((1, GLA_DV), lambda b, h, i: (0, 0))],
        out_specs=seq,
        out_shape=jax.ShapeDtypeStruct((bsz * t, D_GLA), BF16),
        scratch_shapes=[pltpu.VMEM((GLA_DV, GLA_DK), F32), pltpu.VMEM((GLA_DV, GLA_DK), F32),
                        pltpu.VMEM((t, GLA_DV), F32)],
        compiler_params=_params("parallel", "parallel", "arbitrary"),
        name="gla",
    )(qkv, ld, qkv, ld, sg, s0f, s0b, gain)


ROW_PAD = 16
ROW_PITCH = GRID_W + 2 * ROW_PAD


def _conv_kernel(n_row_blocks, n_cast, u_ref, w_ref, b_ref, *rest):
    cast_in, y_ref, rest = rest[:n_cast], rest[n_cast], rest[n_cast + 1:]
    cast_out, (pad_ref,) = rest[:n_cast], rest[n_cast:]
    cb = pl.program_id(1)
    rows = u_ref.shape[0] // GRID_W
    bias = jnp.broadcast_to(b_ref[...], (GRID_W, CONV_CB))
    _run_casts(cast_in, cast_out)

    @pl.when(cb < n_row_blocks)
    def _():
        @pl.when(cb == 0)
        def _():
            pad_ref[...] = jnp.zeros_like(pad_ref)

        def fill(r, carry):
            src = pl.multiple_of(r * GRID_W, GRID_W)
            dst = pl.multiple_of(r * ROW_PITCH + ROW_PAD, 8)
            pad_ref[pl.ds(dst, GRID_W), :] = u_ref[pl.ds(src, GRID_W), :]
            return carry

        lax.fori_loop(0, rows, fill, 0)

        def body(r, carry):
            base = r * ROW_PITCH + (ROW_PAD - CONV_HALF)
            acc = bias
            for j in range(CONV_WIDTH):
                acc = acc + w_ref[j:j + 1, :] * pad_ref[pl.ds(base + j, GRID_W), :]
            y_ref[pl.ds(pl.multiple_of(r * GRID_W, GRID_W), GRID_W), :] = acc
            return carry

        lax.fori_loop(0, rows, body, 0, unroll=CONV_UNROLL)

    @pl.when(cb >= n_row_blocks)
    def _():
        edge = CONV_HALF * GRID_W

        @pl.when(cb == n_row_blocks)
        def _():
            pad_ref[0:edge, :] = jnp.zeros((edge, CONV_CB), F32)
            pad_ref[edge + rows * GRID_W:2 * edge + rows * GRID_W, :] = jnp.zeros((edge, CONV_CB), F32)

        pad_ref[edge:edge + rows * GRID_W, :] = u_ref[...]

        def body(r, carry):
            acc = bias
            for j in range(CONV_WIDTH):
                src = pl.multiple_of((r + j) * GRID_W, GRID_W)
                acc = acc + w_ref[j:j + 1, :] * pad_ref[pl.ds(src, GRID_W), :]
            y_ref[pl.ds(pl.multiple_of(r * GRID_W, GRID_W), GRID_W), :] = acc
            return carry

        lax.fori_loop(0, rows, body, 0, unroll=CONV_UNROLL)


def _conv_call(u, w, b, bsz, t, cast_jobs=()):
    ch = u.shape[1]
    rows = t // GRID_W
    n_cb = ch // CONV_CB
    pad_rows = max(rows * ROW_PITCH, (rows + 2 * CONV_HALF) * GRID_W)
    blk = pl.BlockSpec((t, CONV_CB), lambda bi, c: (bi, c))
    cast_specs, cast_shapes = _cast_specs(cast_jobs)
    return pl.pallas_call(
        functools.partial(_conv_kernel, n_cb // 2, len(cast_jobs)),
        grid=(bsz, n_cb),
        in_specs=[blk, pl.BlockSpec((CONV_WIDTH, CONV_CB), lambda bi, c: (0, c)),
                  pl.BlockSpec((1, CONV_CB), lambda bi, c: (0, c))] + cast_specs,
        out_specs=[blk] + cast_specs,
        out_shape=[jax.ShapeDtypeStruct(u.shape, F32)] + cast_shapes,
        scratch_shapes=[pltpu.VMEM((pad_rows, CONV_CB), F32)],
        compiler_params=_params("arbitrary", "arbitrary"),
        name="conv",
    )(u, w, b, *[job.array for job in cast_jobs])


def _outproj_kernel(og_ref, y_ref, lng_ref, lnb_ref, w_ref, h_ref, mod_ref, o_ref):
    y = y_ref[...]
    mu = jnp.mean(y, axis=-1, keepdims=True)
    yc = y - mu
    var = jnp.mean(yc * yc, axis=-1, keepdims=True)
    yn = yc * lax.rsqrt(var + LN_EPS) * lng_ref[...] + lnb_ref[...]
    oc = _silu(yn).astype(BF16)
    res = jnp.dot(og_ref[...], w_ref[0:D_GLA, :], preferred_element_type=F32)
    res = res + jnp.dot(oc, w_ref[D_GLA:D_GLA + D_CONV, :], preferred_element_type=F32)
    o_ref[...] = h_ref[...] + mod_ref[5:6, :] * res


def _outproj_call(og, y, ln_g, ln_b, w_out_bf, h, mod, group_of_tile):
    rows, d = h.shape
    tm = OUT_TM
    row = lambda n: pl.BlockSpec((tm, n), lambda i: (i, 0))
    vec = lambda n: pl.BlockSpec((1, n), lambda i: (0, 0))
    return pl.pallas_call(
        _outproj_kernel,
        grid=(rows // tm,),
        in_specs=[row(D_GLA), row(D_CONV), vec(D_CONV), vec(D_CONV),
                  pl.BlockSpec(w_out_bf.shape, lambda i: (0, 0), pipeline_mode=pl.Buffered(1)),
                  row(d), pl.BlockSpec((None, N_MOD, d), lambda i: (group_of_tile(i), 0, 0))],
        out_specs=row(d),
        out_shape=jax.ShapeDtypeStruct((rows, d), F32),
        compiler_params=_params("parallel"),
        name="outproj",
    )(og, y, ln_g, ln_b, w_out_bf, h, mod)


def kernel(x, c, ctx, c_ctx, w_mod, b_mod, norm_ffn1, w_ffn1_in, w_ffn1_out, norm_mix, w_in, w_gk2, b_gk2, gla_norm, conv_w, conv_b, conv_ln_g, conv_ln_b, w_out, norm_ffn2, w_ffn2_in, w_ffn2_out, norm_final):
    bsz, t, d = x.shape
    t_ctx = ctx.shape[1]
    assert w_mod.shape[0] == 1, "single layer only"
    assert t % GLA_TB == 0 and (t // GLA_TB) % 2 == 0 and t % FFN_TM == 0
    vec = lambda a: a.reshape(1, -1)

    w2pad = jnp.zeros((V7X_LANES, 2 * GLA_KEY), BF16)
    w2pad = w2pad.at[:GATE_RANK, :GLA_KEY].set(w_gk2[0, 0].astype(BF16))
    w2pad = w2pad.at[GATE_RANK:2 * GATE_RANK, GLA_KEY:].set(w_gk2[0, 1].astype(BF16))
    b2 = b_gk2[0].reshape(1, 2 * GLA_KEY)
    w_r = _wprep_call(jnp.swapaxes(w_in[0], 0, 1))

    n_rows = 8
    s_in = jnp.concatenate([c, c_ctx[None, :], jnp.zeros((n_rows - bsz - 1, d), F32)], axis=0)
    mod = _mod_call(s_in, w_mod[0], vec(b_mod[0])).reshape(n_rows, N_MOD, d)

    tiles_per_batch = t // FFN_TM
    lat_group = lambda i: i // tiles_per_batch
    ctx_group = lambda i: bsz
    g1, gm = vec(norm_ffn1[0]), vec(norm_mix[0])

    xc = ctx.reshape(bsz * t_ctx, d)
    hxc, wg1, wu1, wo1 = _ffn_call(
        xc, mod, ctx_group, g1, gm, w_ffn1_in[0], w_ffn1_in[0], w_ffn1_out[0], D_FF,
        _FfnCfg(mod_row=0, emit_h=False, emit_hx=True, final_norm=False, cast_w=True), "ffn1_ctx")
    s0f, s0b = _ctx_call(hxc, w_r, w2pad, b2, bsz, t_ctx)

    xl = x.reshape(bsz * t, d)
    h1, hx = _ffn_call(
        xl, mod, lat_group, g1, gm, wg1, wu1, wo1, 0,
        _FfnCfg(mod_row=0, emit_h=True, emit_hx=True, final_norm=False, cast_w=False), "ffn1")

    qkv, ld, sg, u = _proj_call(hx, w_r, w2pad, b2)
    og = _gla_call(qkv, ld, sg, s0f, s0b, vec(gla_norm[0]), bsz, t)

    n_conv_steps = bsz * (D_CONV // CONV_CB)
    step = lambda bi, cb: (bi * (D_CONV // CONV_CB) + cb, 0)
    y, wf2_in, wf2_out, w_out_bf = _conv_call(
        u, conv_w[0], vec(conv_b[0]), bsz, t,
        (_CastJob(w_ffn2_in[0], (d // n_conv_steps, 2 * D_FF), step),
         _CastJob(w_ffn2_out[0], (D_FF // n_conv_steps, d), step),
         _CastJob(w_out[0], (d // n_conv_steps, d), step)))
    h2 = _outproj_call(og, y, vec(conv_ln_g[0]), vec(conv_ln_b[0]), w_out_bf, h1, mod,
                       lambda i: i // (t // OUT_TM))

    (out,) = _ffn_call(
        h2, mod, lat_group, vec(norm_ffn2[0]), vec(norm_final), wf2_in, wf2_in, wf2_out, D_FF,
        _FfnCfg(mod_row=6, emit_h=True, emit_hx=False, final_norm=True, cast_w=False), "ffn2")
    return out.reshape(bsz, t, d)
```

```python
import functools
from typing import NamedTuple

import jax
import jax.numpy as jnp
from jax import lax
from jax.experimental import pallas as pl
from jax.experimental.pallas import tpu as pltpu

F32 = jnp.float32
BF16 = jnp.bfloat16

D_MODEL = 2048
GRID_W = 64
GLA_HEADS = 4
GLA_DK = 128
GLA_DV = 256
GLA_KEY = GLA_HEADS * GLA_DK
D_GLA = GLA_HEADS * GLA_DV
D_CONV = 1024
GATE_RANK = 16
GATE_NORMALIZER = 16.0
CHUNK = 64
CONV_WIDTH = 31
CONV_HALF = CONV_WIDTH // 2
D_FF = 5632
N_MOD = 9
RMS_EPS = 1e-6
HEAD_NORM_EPS = 1e-5
LN_EPS = 1e-5

OFF_V = GLA_KEY
OFF_GKF = OFF_V + D_GLA
CTX_COLS = OFF_GKF + 2 * GATE_RANK
OFF_Q = CTX_COLS
OFF_G = OFF_Q + GLA_KEY
OFF_GLU = OFF_G + D_GLA

V7X_LANES = 128
R_K = 0
R_V = R_K + GLA_KEY
R_Q = R_V + D_GLA
R_G = R_Q + GLA_KEY
R_A = R_G + D_GLA
R_B = R_A + D_CONV
R_GK = R_B + D_CONV
R_END = R_GK + V7X_LANES

V7X_VMEM_SCOPED_LIMIT_BYTES = 60000 * 1024
V7X_BF16_SUBLANES = 16

FFN_TM = 512
FFN_TF = 512
FFN_TF_CAST = 256
FFN_ROW_SPLITS = (slice(0, FFN_TM // 2), slice(FFN_TM // 2, FFN_TM))
PROJ_TM = 512
OUT_TM = 512
MOD_TN = 1024
GLA_TB = 1024
GLA_SUB = 256
CONV_CB = 128
CONV_UNROLL = 8
WPREP_TC = 256

NT_DIMS = (((1,), (1,)), ((), ()))
TN_DIMS = (((0,), (0,)), ((), ()))


def _params(*semantics):
    return pltpu.CompilerParams(dimension_semantics=semantics,
                                vmem_limit_bytes=V7X_VMEM_SCOPED_LIMIT_BYTES)


def _silu(x):
    return x * jax.nn.sigmoid(x)


def _rmsnorm_rows(x, gain, eps):
    ms = jnp.mean(x * x, axis=-1, keepdims=True)
    return x * lax.rsqrt(ms + eps) * gain


def _modulate(x, gain, shift, scale):
    return _rmsnorm_rows(x, gain, RMS_EPS) * (1.0 + scale) + shift


def _log_sigmoid(z):
    return jnp.minimum(z, 0.0) - jnp.log1p(jnp.exp(-jnp.abs(z)))


class _CastJob(NamedTuple):
    array: jax.Array
    block: tuple
    index_map: object


def _cast_specs(jobs):
    for job in jobs:
        assert all(n % b == 0 for n, b in zip(job.array.shape, job.block))
        assert job.block[0] % V7X_BF16_SUBLANES == 0 and job.block[1] % V7X_LANES == 0
    specs = [pl.BlockSpec(job.block, job.index_map) for job in jobs]
    shapes = [jax.ShapeDtypeStruct(job.array.shape, BF16) for job in jobs]
    return specs, shapes


def _run_casts(src_refs, dst_refs):
    for src_ref, dst_ref in zip(src_refs, dst_refs):
        dst_ref[...] = src_ref[...].astype(dst_ref.dtype)


def _mod_kernel(s_ref, w_ref, b_ref, o_ref):
    s = _silu(s_ref[...]).astype(BF16)
    o_ref[...] = jnp.dot(s, w_ref[...].astype(BF16), preferred_element_type=F32) + b_ref[...]


def _mod_call(s_in, w_mod, b_mod):
    rows, d = s_in.shape
    n = w_mod.shape[1]
    return pl.pallas_call(
        _mod_kernel,
        grid=(n // MOD_TN,),
        in_specs=[
            pl.BlockSpec((rows, d), lambda j: (0, 0)),
            pl.BlockSpec((d, MOD_TN), lambda j: (0, j)),
            pl.BlockSpec((1, MOD_TN), lambda j: (0, j)),
        ],
        out_specs=pl.BlockSpec((rows, MOD_TN), lambda j: (0, j)),
        out_shape=jax.ShapeDtypeStruct((rows, n), F32),
        compiler_params=_params("arbitrary"),
        name="mod",
    )(s_in, w_mod, b_mod)


class _FfnCfg(NamedTuple):
    mod_row: int
    emit_h: bool
    emit_hx: bool
    final_norm: bool
    cast_w: bool


def _ffn_kernel(cfg, x_ref, mod_ref, g_in_ref, g_next_ref, wg_ref, wu_ref, wo_ref, *rest):
    outs = []
    for flag in (cfg.emit_h, cfg.emit_hx, cfg.cast_w, cfg.cast_w, cfg.cast_w):
        outs.append(rest[0] if flag else None)
        rest = rest[1:] if flag else rest
    h_ref, hx_ref, wg_bf_ref, wu_bf_ref, wo_bf_ref = outs
    hm_ref, acc_ref = rest
    j = pl.program_id(1)
    last = pl.num_programs(1) - 1
    r0 = cfg.mod_row

    def partial_out(hm):
        wg, wu, wo = wg_ref[...], wu_ref[...], wo_ref[...]
        if cfg.cast_w:
            wg, wu, wo = wg.astype(BF16), wu.astype(BF16), wo.astype(BF16)
            wg_bf_ref[...] = wg
            wu_bf_ref[...] = wu
            wo_bf_ref[...] = wo
        gate_up = [(jnp.dot(hm[rows, :], wg, preferred_element_type=F32),
                    jnp.dot(hm[rows, :], wu, preferred_element_type=F32)) for rows in FFN_ROW_SPLITS]
        parts = [jnp.dot((_silu(gate) * up).astype(BF16), wo, preferred_element_type=F32)
                 for gate, up in gate_up]
        return jnp.concatenate(parts, axis=0)

    @pl.when(j == 0)
    def _():
        hm = _modulate(x_ref[...], g_in_ref[...], mod_ref[r0:r0 + 1, :], mod_ref[r0 + 1:r0 + 2, :])
        hm = hm.astype(BF16)
        hm_ref[...] = hm
        acc_ref[...] = partial_out(hm)

    @pl.when((j > 0) & (j < last))
    def _():
        acc_ref[...] += partial_out(hm_ref[...])

    @pl.when(j == last)
    def _():
        g = mod_ref[r0 + 2:r0 + 3, :]
        h = x_ref[...] + (0.5 * g) * (acc_ref[...] + partial_out(hm_ref[...]))
        if cfg.emit_hx:
            hx = _modulate(h, g_next_ref[...], mod_ref[r0 + 3:r0 + 4, :], mod_ref[r0 + 4:r0 + 5, :])
            hx_ref[...] = hx.astype(BF16)
        if cfg.emit_h:
            h_ref[...] = _rmsnorm_rows(h, g_next_ref[...], RMS_EPS) if cfg.final_norm else h


def _ffn_call(x, mod, group_of_tile, g_in, g_next, wg, wu, wo, up_col0, cfg, name):
    rows, d = x.shape
    n_ff = wo.shape[0]
    tf = FFN_TF_CAST if cfg.cast_w else FFN_TF
    n_j = n_ff // tf
    assert n_j >= 2, "first and last hidden block must be distinct grid steps"
    assert not cfg.cast_w or rows == FFN_TM, "bf16 weight copies are written by one row tile only"
    up_blk0 = up_col0 // tf
    row_spec = pl.BlockSpec((FFN_TM, d), lambda i, j: (i, 0))
    vec_spec = pl.BlockSpec((1, d), lambda i, j: (0, 0))
    wg_spec = pl.BlockSpec((d, tf), lambda i, j: (0, j))
    wu_spec = pl.BlockSpec((d, tf), lambda i, j: (0, up_blk0 + j))
    wo_spec = pl.BlockSpec((tf, d), lambda i, j: (j, 0))
    in_specs = [row_spec, pl.BlockSpec((None, N_MOD, d), lambda i, j: (group_of_tile(i), 0, 0)),
                vec_spec, vec_spec, wg_spec, wu_spec, wo_spec]
    out_specs, out_shape = [], []
    if cfg.emit_h:
        out_specs.append(row_spec)
        out_shape.append(jax.ShapeDtypeStruct((rows, d), F32))
    if cfg.emit_hx:
        out_specs.append(row_spec)
        out_shape.append(jax.ShapeDtypeStruct((rows, d), BF16))
    if cfg.cast_w:
        out_specs += [wg_spec, pl.BlockSpec((d, tf), lambda i, j: (0, j)), wo_spec]
        out_shape += [jax.ShapeDtypeStruct((d, n_ff), BF16), jax.ShapeDtypeStruct((d, n_ff), BF16),
                      jax.ShapeDtypeStruct((n_ff, d), BF16)]
    return pl.pallas_call(
        functools.partial(_ffn_kernel, cfg),
        grid=(rows // FFN_TM, n_j),
        in_specs=in_specs,
        out_specs=out_specs,
        out_shape=out_shape,
        scratch_shapes=[pltpu.VMEM((FFN_TM, d), BF16), pltpu.VMEM((FFN_TM, d), F32)],
        compiler_params=_params("parallel", "arbitrary"),
        name=name,
    )(x, mod, g_in, g_next, wg, wu, wo)


def _wprep_kernel(w_ref, o_ref):
    o_ref[R_K:R_Q, :] = w_ref[0:OFF_GKF, :].astype(BF16)
    o_ref[R_Q:R_GK, :] = w_ref[OFF_Q:OFF_GLU + 2 * D_CONV, :].astype(BF16)
    o_ref[R_GK:R_GK + 2 * GATE_RANK, :] = w_ref[OFF_GKF:CTX_COLS, :].astype(BF16)
    o_ref[R_GK + 2 * GATE_RANK:R_END, :] = jnp.zeros((V7X_LANES - 2 * GATE_RANK, o_ref.shape[1]), BF16)


def _wprep_call(w_t):
    n, d = w_t.shape
    tc = WPREP_TC
    return pl.pallas_call(
        _wprep_kernel,
        grid=(d // tc,),
        in_specs=[pl.BlockSpec((n, tc), lambda i: (0, i))],
        out_specs=pl.BlockSpec((R_END, tc), lambda i: (0, i)),
        out_shape=jax.ShapeDtypeStruct((R_END, d), BF16),
        compiler_params=_params("parallel"),
        name="wprep",
    )(w_t)


def _log_decays(p_gk, w2_ref, b2_ref):
    z = jnp.dot(p_gk.astype(BF16), w2_ref[...], preferred_element_type=F32) + b2_ref[...]
    return _log_sigmoid(z) * (1.0 / GATE_NORMALIZER)


def _chunk_cumsum(x, reverse):
    n = x.shape[0]
    pos = lax.broadcasted_iota(jnp.int32, x.shape, 0) % CHUNK
    d = 1
    while d < CHUNK:
        if reverse:
            shifted = pltpu.roll(x, n - d, 0)
            x = x + jnp.where(pos < CHUNK - d, shifted, 0.0)
        else:
            shifted = pltpu.roll(x, d, 0)
            x = x + jnp.where(pos >= d, shifted, 0.0)
        d *= 2
    return x


def _chunk_rows(x, row_in_chunk):
    return jnp.concatenate([x[c0 + row_in_chunk:c0 + row_in_chunk + 1, :]
                            for c0 in range(0, x.shape[0], CHUNK)], axis=0)


def _chunk_bcast(rows_per_chunk):
    return jnp.concatenate([jnp.broadcast_to(rows_per_chunk[c:c + 1, :], (CHUNK, rows_per_chunk.shape[1]))
                            for c in range(rows_per_chunk.shape[0])], axis=0)


QKV_HEAD = 2 * GLA_DK + GLA_DV
LD_HEAD = 2 * GLA_DK


def _proj_kernel(hx_ref, w_ref, w2_ref, b2_ref, qkv_ref, ld_ref, sg_ref, u_ref):
    hx = hx_ref[...]

    def proj(lo, hi):
        return lax.dot_general(hx, w_ref[lo:hi, :], NT_DIMS, preferred_element_type=F32)

    k = proj(R_K, R_V).astype(BF16)
    v = proj(R_V, R_Q).astype(BF16)
    q = (proj(R_Q, R_G) * (GLA_DK ** -0.5)).astype(BF16)
    ld = _log_decays(proj(R_GK, R_END), w2_ref, b2_ref)
    for h in range(GLA_HEADS):
        dk = slice(h * GLA_DK, (h + 1) * GLA_DK)
        c0 = h * QKV_HEAD
        qkv_ref[:, c0:c0 + GLA_DK] = q[:, dk]
        qkv_ref[:, c0 + GLA_DK:c0 + 2 * GLA_DK] = k[:, dk]
        qkv_ref[:, c0 + 2 * GLA_DK:c0 + QKV_HEAD] = v[:, h * GLA_DV:(h + 1) * GLA_DV]
        ld_ref[:, h * LD_HEAD:h * LD_HEAD + GLA_DK] = ld[:, dk]
        ld_ref[:, h * LD_HEAD + GLA_DK:(h + 1) * LD_HEAD] = ld[:, GLA_KEY + h * GLA_DK:GLA_KEY + (h + 1) * GLA_DK]
    sg_ref[...] = _silu(proj(R_G, R_A)).astype(BF16)
    u_ref[...] = proj(R_A, R_B) * jax.nn.sigmoid(proj(R_B, R_GK))


def _proj_call(hx, w_r, w2pad, b2):
    rows, d = hx.shape
    tm = PROJ_TM
    row = lambda n: pl.BlockSpec((tm, n), lambda i: (i, 0))
    whole = lambda a: pl.BlockSpec(a.shape, lambda i: (0, 0), pipeline_mode=pl.Buffered(1))
    outs = [(GLA_HEADS * QKV_HEAD, BF16), (GLA_HEADS * LD_HEAD, F32), (D_GLA, BF16), (D_CONV, F32)]
    return pl.pallas_call(
        _proj_kernel,
        grid=(rows // tm,),
        in_specs=[row(d), whole(w_r), whole(w2pad), whole(b2)],
        out_specs=[row(n) for n, _ in outs],
        out_shape=[jax.ShapeDtypeStruct((rows, n), dt) for n, dt in outs],
        compiler_params=_params("parallel"),
        name="proj",
    )(hx, w_r, w2pad, b2)


def _ctx_kernel(hx_ref, wkv_ref, wgk_ref, w2_ref, b2_ref, sf_ref, sb_ref):
    hx = hx_ref[...]
    t = hx.shape[0]
    kv = lax.dot_general(hx, wkv_ref[...], NT_DIMS, preferred_element_type=F32)
    ld = _log_decays(lax.dot_general(hx, wgk_ref[...], NT_DIMS, preferred_element_type=F32),
                     w2_ref, b2_ref)
    r = lax.broadcasted_iota(jnp.int32, (t, t), 0)
    c = lax.broadcasted_iota(jnp.int32, (t, t), 1)
    hi = lax.Precision.HIGHEST
    e_f = jnp.dot((c > r).astype(F32), ld[:, :GLA_KEY], preferred_element_type=F32, precision=hi)
    e_b = jnp.dot((c < r).astype(F32), ld[:, GLA_KEY:], preferred_element_type=F32, precision=hi)
    for h in range(GLA_HEADS):
        ks = slice(h * GLA_DK, (h + 1) * GLA_DK)
        k = kv[:, R_K + h * GLA_DK:R_K + (h + 1) * GLA_DK]
        v = kv[:, R_V + h * GLA_DV:R_V + (h + 1) * GLA_DV].astype(BF16)
        sf_ref[h] = lax.dot_general(v, (k * jnp.exp(e_f[:, ks])).astype(BF16), TN_DIMS,
                                    preferred_element_type=F32)
        sb_ref[h] = lax.dot_general(v, (k * jnp.exp(e_b[:, ks])).astype(BF16), TN_DIMS,
                                    preferred_element_type=F32)


def _ctx_call(hxc, w_r, w2pad, b2, bsz, t):
    d = hxc.shape[1]
    const = lambda a: pl.BlockSpec(a.shape, lambda b: (0, 0))
    st = pl.BlockSpec((None, GLA_HEADS, GLA_DV, GLA_DK), lambda b: (b, 0, 0, 0))
    shape = jax.ShapeDtypeStruct((bsz, GLA_HEADS, GLA_DV, GLA_DK), F32)
    return pl.pallas_call(
        _ctx_kernel,
        grid=(bsz,),
        in_specs=[pl.BlockSpec((t, d), lambda b: (b, 0)),
                  pl.BlockSpec((R_Q, d), lambda b: (0, 0)),
                  pl.BlockSpec((V7X_LANES, d), lambda b: (R_GK // V7X_LANES, 0)),
                  const(w2pad), const(b2)],
        out_specs=[st, st],
        out_shape=[shape, shape],
        compiler_params=_params("parallel"),
        name="ctx",
    )(hxc, w_r, w_r, w2pad, b2)


def _gla_direction(qkv_ref, ld_ref, s_ref, reverse):
    n_sub = GLA_TB // GLA_SUB
    n_chunk = GLA_SUB // CHUNK
    rr = lax.broadcasted_iota(jnp.int32, (GLA_SUB, GLA_SUB), 0)
    cc = lax.broadcasted_iota(jnp.int32, (GLA_SUB, GLA_SUB), 1)
    same_chunk = (rr // CHUNK) == (cc // CHUNK)
    mask = same_chunk & ((cc >= rr) if reverse else (cc <= rr))
    mid_row = CHUNK // 2 if reverse else CHUNK // 2 - 1
    last_row = 0 if reverse else CHUNK - 1
    ld0 = GLA_DK if reverse else 0

    outs = [None] * n_sub
    subs = range(n_sub - 1, -1, -1) if reverse else range(n_sub)
    state = s_ref[...]
    for s in subs:
        rows = slice(s * GLA_SUB, (s + 1) * GLA_SUB)
        b = _chunk_cumsum(ld_ref[rows, ld0:ld0 + GLA_DK], reverse)
        b_mid = _chunk_bcast(_chunk_rows(b, mid_row))
        b_last_rows = _chunk_rows(b, last_row)
        b_last = _chunk_bcast(b_last_rows)
        q = qkv_ref[rows, 0:GLA_DK].astype(F32)
        k = qkv_ref[rows, GLA_DK:2 * GLA_DK].astype(F32)
        v = qkv_ref[rows, 2 * GLA_DK:QKV_HEAD]
        qs = (q * jnp.exp(b - b_mid)).astype(BF16)
        ks = (k * jnp.exp(b_mid - b)).astype(BF16)
        qi = (q * jnp.exp(b)).astype(BF16)
        kd = (k * jnp.exp(b_last - b)).astype(BF16)
        att = lax.dot_general(qs, ks, NT_DIMS, preferred_element_type=F32)
        att = jnp.where(mask, att, 0.0).astype(BF16)
        o_intra = jnp.dot(att, v, preferred_element_type=F32)
        decay = jnp.exp(b_last_rows)
        o_parts = [None] * n_chunk
        chunks = range(n_chunk - 1, -1, -1) if reverse else range(n_chunk)
        for c in chunks:
            cr = slice(c * CHUNK, (c + 1) * CHUNK)
            o_parts[c] = o_intra[cr, :] + lax.dot_general(
                qi[cr, :], state.astype(BF16), NT_DIMS, preferred_element_type=F32)
            kv = lax.dot_general(v[cr, :], kd[cr, :], TN_DIMS, preferred_element_type=F32)
            state = state * decay[c:c + 1, :] + kv
        outs[s] = jnp.concatenate(o_parts, axis=0)
    s_ref[...] = state
    return jnp.concatenate(outs, axis=0)


def _gla_kernel(n_cast, qkvf_ref, ldf_ref, qkvb_ref, ldb_ref, sg_ref, s0f_ref, s0b_ref, gain_ref,
                *rest):
    cast_in, o_ref, rest = rest[:n_cast], rest[n_cast], rest[n_cast + 1:]
    cast_out, (sf_ref, sb_ref, acc_ref) = rest[:n_cast], rest[n_cast:]
    nb = pl.program_id(2)
    n_blocks = pl.num_programs(2)

    @pl.when(nb == 0)
    def _():
        sf_ref[...] = s0f_ref[...]
        sb_ref[...] = s0b_ref[...]
        acc_ref[...] = jnp.zeros_like(acc_ref)

    _run_casts(cast_in, cast_out)
    o_f = _gla_direction(qkvf_ref, ldf_ref, sf_ref, reverse=False)
    o_b = _gla_direction(qkvb_ref, ldb_ref, sb_ref, reverse=True)
    row_f = pl.multiple_of(nb * GLA_TB, GLA_TB)
    row_b = pl.multiple_of((n_blocks - 1 - nb) * GLA_TB, GLA_TB)

    for row, part in ((row_f, o_f), (row_b, o_b)):
        o = acc_ref[pl.ds(row, GLA_TB), :] + part
        acc_ref[pl.ds(row, GLA_TB), :] = o
        ms = jnp.mean(o * o, axis=-1, keepdims=True)
        o = o * lax.rsqrt(ms + HEAD_NORM_EPS) * gain_ref[...]
        o_ref[pl.ds(row, GLA_TB), :] = (o * sg_ref[pl.ds(row, GLA_TB), :].astype(F32)).astype(BF16)


def _gla_call(qkv, ld, sg, s0f, s0b, gain, bsz, t, cast_jobs=()):
    nb = t // GLA_TB
    fwd = lambda n: pl.BlockSpec((GLA_TB, n), lambda b, h, i: (b * nb + i, h))
    bwd = lambda n: pl.BlockSpec((GLA_TB, n), lambda b, h, i: (b * nb + nb - 1 - i, h))
    seq = pl.BlockSpec((t, GLA_DV), lambda b, h, i: (b, h))
    st = pl.BlockSpec((None, None, GLA_DV, GLA_DK), lambda b, h, i: (b, h, 0, 0))
    cast_specs, cast_shapes = _cast_specs(cast_jobs)
    return pl.pallas_call(
        functools.partial(_gla_kernel, len(cast_jobs)),
        grid=(bsz, GLA_HEADS, nb),
        in_specs=[fwd(QKV_HEAD), fwd(LD_HEAD), bwd(QKV_HEAD), bwd(LD_HEAD),
                  seq, st, st, pl.BlockSpec((1, GLA_DV), lambda b, h, i: (0, 0))] + cast_specs,
        out_specs=[seq] + cast_specs,
        out_shape=[jax.ShapeDtypeStruct((bsz * t, D_GLA), BF16)] + cast_shapes,
        scratch_shapes=[pltpu.VMEM((GLA_DV, GLA_DK), F32), pltpu.VMEM((GLA_DV, GLA_DK), F32),
                        pltpu.VMEM((t, GLA_DV), F32)],
        compiler_params=_params("parallel", "parallel", "arbitrary"),
        name="gla",
    )(qkv, ld, qkv, ld, sg, s0f, s0b, gain, *[job.array for job in cast_jobs])


ROW_PAD = 16
ROW_PITCH = GRID_W + 2 * ROW_PAD


def _conv_kernel(n_row_blocks, n_cast, u_ref, w_ref, b_ref, *rest):
    cast_in, y_ref, rest = rest[:n_cast], rest[n_cast], rest[n_cast + 1:]
    cast_out, (pad_ref,) = rest[:n_cast], rest[n_cast:]
    cb = pl.program_id(1)
    rows = u_ref.shape[0] // GRID_W
    bias = jnp.broadcast_to(b_ref[...], (GRID_W, CONV_CB))
    _run_casts(cast_in, cast_out)

    @pl.when(cb < n_row_blocks)
    def _():
        @pl.when(cb == 0)
        def _():
            pad_ref[...] = jnp.zeros_like(pad_ref)

        def fill(r, carry):
            src = pl.multiple_of(r * GRID_W, GRID_W)
            dst = pl.multiple_of(r * ROW_PITCH + ROW_PAD, 8)
            pad_ref[pl.ds(dst, GRID_W), :] = u_ref[pl.ds(src, GRID_W), :]
            return carry

        lax.fori_loop(0, rows, fill, 0)

        def body(r, carry):
            base = r * ROW_PITCH + (ROW_PAD - CONV_HALF)
            acc = bias
            for j in range(CONV_WIDTH):
                acc = acc + w_ref[j:j + 1, :] * pad_ref[pl.ds(base + j, GRID_W), :]
            y_ref[pl.ds(pl.multiple_of(r * GRID_W, GRID_W), GRID_W), :] = acc
            return carry

        lax.fori_loop(0, rows, body, 0, unroll=CONV_UNROLL)

    @pl.when(cb >= n_row_blocks)
    def _():
        edge = CONV_HALF * GRID_W

        @pl.when(cb == n_row_blocks)
        def _():
            pad_ref[0:edge, :] = jnp.zeros((edge, CONV_CB), F32)
            pad_ref[edge + rows * GRID_W:2 * edge + rows * GRID_W, :] = jnp.zeros((edge, CONV_CB), F32)

        pad_ref[edge:edge + rows * GRID_W, :] = u_ref[...]

        def body(r, carry):
            acc = bias
            for j in range(CONV_WIDTH):
                src = pl.multiple_of((r + j) * GRID_W, GRID_W)
                acc = acc + w_ref[j:j + 1, :] * pad_ref[pl.ds(src, GRID_W), :]
            y_ref[pl.ds(pl.multiple_of(r * GRID_W, GRID_W), GRID_W), :] = acc
            return carry

        lax.fori_loop(0, rows, body, 0, unroll=CONV_UNROLL)


def _conv_call(u, w, b, bsz, t, cast_jobs=()):
    ch = u.shape[1]
    rows = t // GRID_W
    n_cb = ch // CONV_CB
    pad_rows = max(rows * ROW_PITCH, (rows + 2 * CONV_HALF) * GRID_W)
    blk = pl.BlockSpec((t, CONV_CB), lambda bi, c: (bi, c))
    cast_specs, cast_shapes = _cast_specs(cast_jobs)
    return pl.pallas_call(
        functools.partial(_conv_kernel, n_cb // 2, len(cast_jobs)),
        grid=(bsz, n_cb),
        in_specs=[blk, pl.BlockSpec((CONV_WIDTH, CONV_CB), lambda bi, c: (0, c)),
                  pl.BlockSpec((1, CONV_CB), lambda bi, c: (0, c))] + cast_specs,
        out_specs=[blk] + cast_specs,
        out_shape=[jax.ShapeDtypeStruct(u.shape, F32)] + cast_shapes,
        scratch_shapes=[pltpu.VMEM((pad_rows, CONV_CB), F32)],
        compiler_params=_params("arbitrary", "arbitrary"),
        name="conv",
    )(u, w, b, *[job.array for job in cast_jobs])


def _outproj_kernel(og_ref, y_ref, lng_ref, lnb_ref, w_ref, h_ref, mod_ref, o_ref):
    y = y_ref[...]
    mu = jnp.mean(y, axis=-1, keepdims=True)
    yc = y - mu
    var = jnp.mean(yc * yc, axis=-1, keepdims=True)
    yn = yc * lax.rsqrt(var + LN_EPS) * lng_ref[...] + lnb_ref[...]
    oc = _silu(yn).astype(BF16)
    res = jnp.dot(og_ref[...], w_ref[0:D_GLA, :], preferred_element_type=F32)
    res = res + jnp.dot(oc, w_ref[D_GLA:D_GLA + D_CONV, :], preferred_element_type=F32)
    o_ref[...] = h_ref[...] + mod_ref[5:6, :] * res


def _outproj_call(og, y, ln_g, ln_b, w_out_bf, h, mod, group_of_tile):
    rows, d = h.shape
    tm = OUT_TM
    row = lambda n: pl.BlockSpec((tm, n), lambda i: (i, 0))
    vec = lambda n: pl.BlockSpec((1, n), lambda i: (0, 0))
    return pl.pallas_call(
        _outproj_kernel,
        grid=(rows // tm,),
        in_specs=[row(D_GLA), row(D_CONV), vec(D_CONV), vec(D_CONV),
                  pl.BlockSpec(w_out_bf.shape, lambda i: (0, 0), pipeline_mode=pl.Buffered(1)),
                  row(d), pl.BlockSpec((None, N_MOD, d), lambda i: (group_of_tile(i), 0, 0))],
        out_specs=row(d),
        out_shape=jax.ShapeDtypeStruct((rows, d), F32),
        compiler_params=_params("parallel"),
        name="outproj",
    )(og, y, ln_g, ln_b, w_out_bf, h, mod)


def kernel(x, c, ctx, c_ctx, w_mod, b_mod, norm_ffn1, w_ffn1_in, w_ffn1_out, norm_mix, w_in, w_gk2, b_gk2, gla_norm, conv_w, conv_b, conv_ln_g, conv_ln_b, w_out, norm_ffn2, w_ffn2_in, w_ffn2_out, norm_final):
    bsz, t, d = x.shape
    t_ctx = ctx.shape[1]
    assert w_mod.shape[0] == 1, "single layer only"
    assert t % GLA_TB == 0 and (t // GLA_TB) % 2 == 0 and t % FFN_TM == 0
    vec = lambda a: a.reshape(1, -1)

    w2pad = jnp.zeros((V7X_LANES, 2 * GLA_KEY), BF16)
    w2pad = w2pad.at[:GATE_RANK, :GLA_KEY].set(w_gk2[0, 0].astype(BF16))
    w2pad = w2pad.at[GATE_RANK:2 * GATE_RANK, GLA_KEY:].set(w_gk2[0, 1].astype(BF16))
    b2 = b_gk2[0].reshape(1, 2 * GLA_KEY)
    w_r = _wprep_call(jnp.swapaxes(w_in[0], 0, 1))

    n_rows = 8
    s_in = jnp.concatenate([c, c_ctx[None, :], jnp.zeros((n_rows - bsz - 1, d), F32)], axis=0)
    mod = _mod_call(s_in, w_mod[0], vec(b_mod[0])).reshape(n_rows, N_MOD, d)

    tiles_per_batch = t // FFN_TM
    lat_group = lambda i: i // tiles_per_batch
    ctx_group = lambda i: bsz
    g1, gm = vec(norm_ffn1[0]), vec(norm_mix[0])

    xc = ctx.reshape(bsz * t_ctx, d)
    hxc, wg1, wu1, wo1 = _ffn_call(
        xc, mod, ctx_group, g1, gm, w_ffn1_in[0], w_ffn1_in[0], w_ffn1_out[0], D_FF,
        _FfnCfg(mod_row=0, emit_h=False, emit_hx=True, final_norm=False, cast_w=True), "ffn1_ctx")
    s0f, s0b = _ctx_call(hxc, w_r, w2pad, b2, bsz, t_ctx)

    xl = x.reshape(bsz * t, d)
    h1, hx = _ffn_call(
        xl, mod, lat_group, g1, gm, wg1, wu1, wo1, 0,
        _FfnCfg(mod_row=0, emit_h=True, emit_hx=True, final_norm=False, cast_w=False), "ffn1")

    qkv, ld, sg, u = _proj_call(hx, w_r, w2pad, b2)
    n_bh, n_gla_blocks = bsz * GLA_HEADS, t // GLA_TB
    og, wf2_in = _gla_call(
        qkv, ld, sg, s0f, s0b, vec(gla_norm[0]), bsz, t,
        (_CastJob(w_ffn2_in[0], (d // n_bh, 2 * D_FF // n_gla_blocks),
                  lambda b, h, i: (b * GLA_HEADS + h, i)),))
    n_conv_steps = bsz * (D_CONV // CONV_CB)
    step = lambda bi, cb: (bi * (D_CONV // CONV_CB) + cb, 0)
    y, wf2_out, w_out_bf = _conv_call(
        u, conv_w[0], vec(conv_b[0]), bsz, t,
        (_CastJob(w_ffn2_out[0], (D_FF // n_conv_steps, d), step),
         _CastJob(w_out[0], (d // n_conv_steps, d), step)))
    h2 = _outproj_call(og, y, vec(conv_ln_g[0]), vec(conv_ln_b[0]), w_out_bf, h1, mod,
                       lambda i: i // (t // OUT_TM))

    (out,) = _ffn_call(
        h2, mod, lat_group, vec(norm_ffn2[0]), vec(norm_final), wf2_in, wf2_in, wf2_out, D_FF,
        _FfnCfg(mod_row=6, emit_h=True, emit_hx=False, final_norm=True, cast_w=False), "ffn2")
    return out.reshape(bsz, t, d)
```

```python
import functools
from typing import NamedTuple

import jax
import jax.numpy as jnp
from jax import lax
from jax.experimental import pallas as pl
from jax.experimental.pallas import tpu as pltpu

F32 = jnp.float32
BF16 = jnp.bfloat16

D_MODEL = 2048
GRID_W = 64
GLA_HEADS = 4
GLA_DK = 128
GLA_DV = 256
GLA_KEY = GLA_HEADS * GLA_DK
D_GLA = GLA_HEADS * GLA_DV
D_CONV = 1024
GATE_RANK = 16
GATE_NORMALIZER = 16.0
CHUNK = 64
CONV_WIDTH = 31
CONV_HALF = CONV_WIDTH // 2
D_FF = 5632
N_MOD = 9
RMS_EPS = 1e-6
HEAD_NORM_EPS = 1e-5
LN_EPS = 1e-5

OFF_V = GLA_KEY
OFF_GKF = OFF_V + D_GLA
CTX_COLS = OFF_GKF + 2 * GATE_RANK
OFF_Q = CTX_COLS
OFF_G = OFF_Q + GLA_KEY
OFF_GLU = OFF_G + D_GLA

V7X_LANES = 128
R_K = 0
R_V = R_K + GLA_KEY
R_Q = R_V + D_GLA
R_G = R_Q + GLA_KEY
R_A = R_G + D_GLA
R_B = R_A + D_CONV
R_GK = R_B + D_CONV
R_END = R_GK + V7X_LANES

V7X_VMEM_SCOPED_LIMIT_BYTES = 60000 * 1024
V7X_BF16_SUBLANES = 16

FFN_TM = 512
FFN_TF = 512
FFN_TF_CAST = 256
PROJ_TM = 512
OUT_TM = 512
MOD_TN = 1024
GLA_TB = 1024
GLA_SUB = 256
CONV_CB = 128
CONV_UNROLL = 8
WPREP_TC = 256

NT_DIMS = (((1,), (1,)), ((), ()))
TN_DIMS = (((0,), (0,)), ((), ()))


def _params(*semantics):
    return pltpu.CompilerParams(dimension_semantics=semantics,
                                vmem_limit_bytes=V7X_VMEM_SCOPED_LIMIT_BYTES)


def _silu(x):
    return x * jax.nn.sigmoid(x)


def _rmsnorm_rows(x, gain, eps):
    ms = jnp.mean(x * x, axis=-1, keepdims=True)
    return x * lax.rsqrt(ms + eps) * gain


def _modulate(x, gain, shift, scale):
    return _rmsnorm_rows(x, gain, RMS_EPS) * (1.0 + scale) + shift


def _log_sigmoid(z):
    return jnp.minimum(z, 0.0) - jnp.log1p(jnp.exp(-jnp.abs(z)))


class _CastJob(NamedTuple):
    array: jax.Array
    block: tuple
    index_map: object


def _cast_specs(jobs):
    for job in jobs:
        assert all(n % b == 0 for n, b in zip(job.array.shape, job.block))
        assert job.block[0] % V7X_BF16_SUBLANES == 0 and job.block[1] % V7X_LANES == 0
    specs = [pl.BlockSpec(job.block, job.index_map) for job in jobs]
    shapes = [jax.ShapeDtypeStruct(job.array.shape, BF16) for job in jobs]
    return specs, shapes


def _run_casts(src_refs, dst_refs):
    for src_ref, dst_ref in zip(src_refs, dst_refs):
        dst_ref[...] = src_ref[...].astype(dst_ref.dtype)


def _mod_kernel(s_ref, w_ref, b_ref, o_ref):
    s = _silu(s_ref[...]).astype(BF16)
    o_ref[...] = jnp.dot(s, w_ref[...].astype(BF16), preferred_element_type=F32) + b_ref[...]


def _mod_call(s_in, w_mod, b_mod):
    rows, d = s_in.shape
    n = w_mod.shape[1]
    return pl.pallas_call(
        _mod_kernel,
        grid=(n // MOD_TN,),
        in_specs=[
            pl.BlockSpec((rows, d), lambda j: (0, 0)),
            pl.BlockSpec((d, MOD_TN), lambda j: (0, j)),
            pl.BlockSpec((1, MOD_TN), lambda j: (0, j)),
        ],
        out_specs=pl.BlockSpec((rows, MOD_TN), lambda j: (0, j)),
        out_shape=jax.ShapeDtypeStruct((rows, n), F32),
        compiler_params=_params("arbitrary"),
        name="mod",
    )(s_in, w_mod, b_mod)


class _FfnCfg(NamedTuple):
    mod_row: int
    emit_h: bool
    emit_hx: bool
    final_norm: bool
    cast_w: bool


def _ffn_kernel(cfg, n_cast, x_ref, mod_ref, g_in_ref, g_next_ref, wg_ref, wu_ref, wo_ref, *rest):
    cast_in, rest = rest[:n_cast], rest[n_cast:]
    outs = []
    for flag in (cfg.emit_h, cfg.emit_hx, cfg.cast_w, cfg.cast_w, cfg.cast_w):
        outs.append(rest[0] if flag else None)
        rest = rest[1:] if flag else rest
    h_ref, hx_ref, wg_bf_ref, wu_bf_ref, wo_bf_ref = outs
    cast_out, (hm_ref, acc_ref) = rest[:n_cast], rest[n_cast:]
    j = pl.program_id(1)
    last = pl.num_programs(1) - 1
    r0 = cfg.mod_row
    _run_casts(cast_in, cast_out)

    def partial_out(hm):
        wg, wu, wo = wg_ref[...], wu_ref[...], wo_ref[...]
        if cfg.cast_w:
            wg, wu, wo = wg.astype(BF16), wu.astype(BF16), wo.astype(BF16)
            wg_bf_ref[...] = wg
            wu_bf_ref[...] = wu
            wo_bf_ref[...] = wo
        gate = jnp.dot(hm, wg, preferred_element_type=F32)
        up = jnp.dot(hm, wu, preferred_element_type=F32)
        act = (_silu(gate) * up).astype(BF16)
        return jnp.dot(act, wo, preferred_element_type=F32)

    @pl.when(j == 0)
    def _():
        hm = _modulate(x_ref[...], g_in_ref[...], mod_ref[r0:r0 + 1, :], mod_ref[r0 + 1:r0 + 2, :])
        hm = hm.astype(BF16)
        hm_ref[...] = hm
        acc_ref[...] = partial_out(hm)

    @pl.when((j > 0) & (j < last))
    def _():
        acc_ref[...] += partial_out(hm_ref[...])

    @pl.when(j == last)
    def _():
        g = mod_ref[r0 + 2:r0 + 3, :]
        h = x_ref[...] + (0.5 * g) * (acc_ref[...] + partial_out(hm_ref[...]))
        if cfg.emit_hx:
            hx = _modulate(h, g_next_ref[...], mod_ref[r0 + 3:r0 + 4, :], mod_ref[r0 + 4:r0 + 5, :])
            hx_ref[...] = hx.astype(BF16)
        if cfg.emit_h:
            h_ref[...] = _rmsnorm_rows(h, g_next_ref[...], RMS_EPS) if cfg.final_norm else h


def _ffn_call(x, mod, group_of_tile, g_in, g_next, wg, wu, wo, up_col0, cfg, name, cast_jobs=()):
    rows, d = x.shape
    n_ff = wo.shape[0]
    tf = FFN_TF_CAST if cfg.cast_w else FFN_TF
    n_j = n_ff // tf
    assert n_j >= 2, "first and last hidden block must be distinct grid steps"
    assert not cfg.cast_w or rows == FFN_TM, "bf16 weight copies are written by one row tile only"
    up_blk0 = up_col0 // tf
    row_spec = pl.BlockSpec((FFN_TM, d), lambda i, j: (i, 0))
    vec_spec = pl.BlockSpec((1, d), lambda i, j: (0, 0))
    wg_spec = pl.BlockSpec((d, tf), lambda i, j: (0, j))
    wu_spec = pl.BlockSpec((d, tf), lambda i, j: (0, up_blk0 + j))
    wo_spec = pl.BlockSpec((tf, d), lambda i, j: (j, 0))
    cast_specs, cast_shapes = _cast_specs(cast_jobs)
    in_specs = [row_spec, pl.BlockSpec((None, N_MOD, d), lambda i, j: (group_of_tile(i), 0, 0)),
                vec_spec, vec_spec, wg_spec, wu_spec, wo_spec] + cast_specs
    out_specs, out_shape = [], []
    if cfg.emit_h:
        out_specs.append(row_spec)
        out_shape.append(jax.ShapeDtypeStruct((rows, d), F32))
    if cfg.emit_hx:
        out_specs.append(row_spec)
        out_shape.append(jax.ShapeDtypeStruct((rows, d), BF16))
    if cfg.cast_w:
        out_specs += [wg_spec, pl.BlockSpec((d, tf), lambda i, j: (0, j)), wo_spec]
        out_shape += [jax.ShapeDtypeStruct((d, n_ff), BF16), jax.ShapeDtypeStruct((d, n_ff), BF16),
                      jax.ShapeDtypeStruct((n_ff, d), BF16)]
    return pl.pallas_call(
        functools.partial(_ffn_kernel, cfg, len(cast_jobs)),
        grid=(rows // FFN_TM, n_j),
        in_specs=in_specs,
        out_specs=out_specs + cast_specs,
        out_shape=out_shape + cast_shapes,
        scratch_shapes=[pltpu.VMEM((FFN_TM, d), BF16), pltpu.VMEM((FFN_TM, d), F32)],
        compiler_params=_params("parallel", "arbitrary"),
        name=name,
    )(x, mod, g_in, g_next, wg, wu, wo, *[job.array for job in cast_jobs])


def _wprep_kernel(w_ref, o_ref):
    o_ref[R_K:R_Q, :] = w_ref[0:OFF_GKF, :].astype(BF16)
    o_ref[R_Q:R_GK, :] = w_ref[OFF_Q:OFF_GLU + 2 * D_CONV, :].astype(BF16)
    o_ref[R_GK:R_GK + 2 * GATE_RANK, :] = w_ref[OFF_GKF:CTX_COLS, :].astype(BF16)
    o_ref[R_GK + 2 * GATE_RANK:R_END, :] = jnp.zeros((V7X_LANES - 2 * GATE_RANK, o_ref.shape[1]), BF16)


def _wprep_call(w_t):
    n, d = w_t.shape
    tc = WPREP_TC
    return pl.pallas_call(
        _wprep_kernel,
        grid=(d // tc,),
        in_specs=[pl.BlockSpec((n, tc), lambda i: (0, i))],
        out_specs=pl.BlockSpec((R_END, tc), lambda i: (0, i)),
        out_shape=jax.ShapeDtypeStruct((R_END, d), BF16),
        compiler_params=_params("parallel"),
        name="wprep",
    )(w_t)


def _log_decays(p_gk, w2_ref, b2_ref):
    z = jnp.dot(p_gk.astype(BF16), w2_ref[...], preferred_element_type=F32) + b2_ref[...]
    return _log_sigmoid(z) * (1.0 / GATE_NORMALIZER)


def _chunk_cumsum(x, reverse):
    n = x.shape[0]
    pos = lax.broadcasted_iota(jnp.int32, x.shape, 0) % CHUNK
    d = 1
    while d < CHUNK:
        if reverse:
            shifted = pltpu.roll(x, n - d, 0)
            x = x + jnp.where(pos < CHUNK - d, shifted, 0.0)
        else:
            shifted = pltpu.roll(x, d, 0)
            x = x + jnp.where(pos >= d, shifted, 0.0)
        d *= 2
    return x


def _chunk_rows(x, row_in_chunk):
    return jnp.concatenate([x[c0 + row_in_chunk:c0 + row_in_chunk + 1, :]
                            for c0 in range(0, x.shape[0], CHUNK)], axis=0)


def _chunk_bcast(rows_per_chunk):
    return jnp.concatenate([jnp.broadcast_to(rows_per_chunk[c:c + 1, :], (CHUNK, rows_per_chunk.shape[1]))
                            for c in range(rows_per_chunk.shape[0])], axis=0)


QKV_HEAD = 2 * GLA_DK + GLA_DV
LD_HEAD = 2 * GLA_DK


def _proj_kernel(hx_ref, w_ref, w2_ref, b2_ref, qkv_ref, ld_ref, sg_ref, u_ref):
    hx = hx_ref[...]

    def proj(lo, hi):
        return lax.dot_general(hx, w_ref[lo:hi, :], NT_DIMS, preferred_element_type=F32)

    k = proj(R_K, R_V).astype(BF16)
    v = proj(R_V, R_Q).astype(BF16)
    q = (proj(R_Q, R_G) * (GLA_DK ** -0.5)).astype(BF16)
    ld = _log_decays(proj(R_GK, R_END), w2_ref, b2_ref)
    for h in range(GLA_HEADS):
        dk = slice(h * GLA_DK, (h + 1) * GLA_DK)
        c0 = h * QKV_HEAD
        qkv_ref[:, c0:c0 + GLA_DK] = q[:, dk]
        qkv_ref[:, c0 + GLA_DK:c0 + 2 * GLA_DK] = k[:, dk]
        qkv_ref[:, c0 + 2 * GLA_DK:c0 + QKV_HEAD] = v[:, h * GLA_DV:(h + 1) * GLA_DV]
        ld_ref[:, h * LD_HEAD:h * LD_HEAD + GLA_DK] = ld[:, dk]
        ld_ref[:, h * LD_HEAD + GLA_DK:(h + 1) * LD_HEAD] = ld[:, GLA_KEY + h * GLA_DK:GLA_KEY + (h + 1) * GLA_DK]
    sg_ref[...] = _silu(proj(R_G, R_A)).astype(BF16)
    u_ref[...] = proj(R_A, R_B) * jax.nn.sigmoid(proj(R_B, R_GK))


def _proj_call(hx, w_r, w2pad, b2):
    rows, d = hx.shape
    tm = PROJ_TM
    row = lambda n: pl.BlockSpec((tm, n), lambda i: (i, 0))
    whole = lambda a: pl.BlockSpec(a.shape, lambda i: (0, 0), pipeline_mode=pl.Buffered(1))
    outs = [(GLA_HEADS * QKV_HEAD, BF16), (GLA_HEADS * LD_HEAD, F32), (D_GLA, BF16), (D_CONV, F32)]
    return pl.pallas_call(
        _proj_kernel,
        grid=(rows // tm,),
        in_specs=[row(d), whole(w_r), whole(w2pad), whole(b2)],
        out_specs=[row(n) for n, _ in outs],
        out_shape=[jax.ShapeDtypeStruct((rows, n), dt) for n, dt in outs],
        compiler_params=_params("parallel"),
        name="proj",
    )(hx, w_r, w2pad, b2)


def _ctx_kernel(hx_ref, wkv_ref, wgk_ref, w2_ref, b2_ref, sf_ref, sb_ref):
    hx = hx_ref[...]
    t = hx.shape[0]
    kv = lax.dot_general(hx, wkv_ref[...], NT_DIMS, preferred_element_type=F32)
    ld = _log_decays(lax.dot_general(hx, wgk_ref[...], NT_DIMS, preferred_element_type=F32),
                     w2_ref, b2_ref)
    r = lax.broadcasted_iota(jnp.int32, (t, t), 0)
    c = lax.broadcasted_iota(jnp.int32, (t, t), 1)
    hi = lax.Precision.HIGHEST
    e_f = jnp.dot((c > r).astype(F32), ld[:, :GLA_KEY], preferred_element_type=F32, precision=hi)
    e_b = jnp.dot((c < r).astype(F32), ld[:, GLA_KEY:], preferred_element_type=F32, precision=hi)
    for h in range(GLA_HEADS):
        ks = slice(h * GLA_DK, (h + 1) * GLA_DK)
        k = kv[:, R_K + h * GLA_DK:R_K + (h + 1) * GLA_DK]
        v = kv[:, R_V + h * GLA_DV:R_V + (h + 1) * GLA_DV].astype(BF16)
        sf_ref[h] = lax.dot_general(v, (k * jnp.exp(e_f[:, ks])).astype(BF16), TN_DIMS,
                                    preferred_element_type=F32)
        sb_ref[h] = lax.dot_general(v, (k * jnp.exp(e_b[:, ks])).astype(BF16), TN_DIMS,
                                    preferred_element_type=F32)


def _ctx_call(hxc, w_r, w2pad, b2, bsz, t):
    d = hxc.shape[1]
    const = lambda a: pl.BlockSpec(a.shape, lambda b: (0, 0))
    st = pl.BlockSpec((None, GLA_HEADS, GLA_DV, GLA_DK), lambda b: (b, 0, 0, 0))
    shape = jax.ShapeDtypeStruct((bsz, GLA_HEADS, GLA_DV, GLA_DK), F32)
    return pl.pallas_call(
        _ctx_kernel,
        grid=(bsz,),
        in_specs=[pl.BlockSpec((t, d), lambda b: (b, 0)),
                  pl.BlockSpec((R_Q, d), lambda b: (0, 0)),
                  pl.BlockSpec((V7X_LANES, d), lambda b: (R_GK // V7X_LANES, 0)),
                  const(w2pad), const(b2)],
        out_specs=[st, st],
        out_shape=[shape, shape],
        compiler_params=_params("parallel"),
        name="ctx",
    )(hxc, w_r, w_r, w2pad, b2)


def _gla_direction(qkv_ref, ld_ref, s_ref, reverse):
    n_sub = GLA_TB // GLA_SUB
    n_chunk = GLA_SUB // CHUNK
    rr = lax.broadcasted_iota(jnp.int32, (GLA_SUB, GLA_SUB), 0)
    cc = lax.broadcasted_iota(jnp.int32, (GLA_SUB, GLA_SUB), 1)
    same_chunk = (rr // CHUNK) == (cc // CHUNK)
    mask = same_chunk & ((cc >= rr) if reverse else (cc <= rr))
    mid_row = CHUNK // 2 if reverse else CHUNK // 2 - 1
    last_row = 0 if reverse else CHUNK - 1
    ld0 = GLA_DK if reverse else 0

    outs = [None] * n_sub
    subs = range(n_sub - 1, -1, -1) if reverse else range(n_sub)
    state = s_ref[...]
    for s in subs:
        rows = slice(s * GLA_SUB, (s + 1) * GLA_SUB)
        b = _chunk_cumsum(ld_ref[rows, ld0:ld0 + GLA_DK], reverse)
        b_mid = _chunk_bcast(_chunk_rows(b, mid_row))
        b_last_rows = _chunk_rows(b, last_row)
        b_last = _chunk_bcast(b_last_rows)
        q = qkv_ref[rows, 0:GLA_DK].astype(F32)
        k = qkv_ref[rows, GLA_DK:2 * GLA_DK].astype(F32)
        v = qkv_ref[rows, 2 * GLA_DK:QKV_HEAD]
        qs = (q * jnp.exp(b - b_mid)).astype(BF16)
        ks = (k * jnp.exp(b_mid - b)).astype(BF16)
        qi = (q * jnp.exp(b)).astype(BF16)
        kd = (k * jnp.exp(b_last - b)).astype(BF16)
        att = lax.dot_general(qs, ks, NT_DIMS, preferred_element_type=F32)
        att = jnp.where(mask, att, 0.0).astype(BF16)
        o_intra = jnp.dot(att, v, preferred_element_type=F32)
        decay = jnp.exp(b_last_rows)
        o_parts = [None] * n_chunk
        chunks = range(n_chunk - 1, -1, -1) if reverse else range(n_chunk)
        for c in chunks:
            cr = slice(c * CHUNK, (c + 1) * CHUNK)
            o_parts[c] = o_intra[cr, :] + lax.dot_general(
                qi[cr, :], state.astype(BF16), NT_DIMS, preferred_element_type=F32)
            kv = lax.dot_general(v[cr, :], kd[cr, :], TN_DIMS, preferred_element_type=F32)
            state = state * decay[c:c + 1, :] + kv
        outs[s] = jnp.concatenate(o_parts, axis=0)
    s_ref[...] = state
    return jnp.concatenate(outs, axis=0)


def _gla_kernel(n_cast, qkvf_ref, ldf_ref, qkvb_ref, ldb_ref, sg_ref, s0f_ref, s0b_ref, gain_ref,
                *rest):
    cast_in, o_ref, rest = rest[:n_cast], rest[n_cast], rest[n_cast + 1:]
    cast_out, (sf_ref, sb_ref, acc_ref) = rest[:n_cast], rest[n_cast:]
    nb = pl.program_id(2)
    n_blocks = pl.num_programs(2)

    @pl.when(nb == 0)
    def _():
        sf_ref[...] = s0f_ref[...]
        sb_ref[...] = s0b_ref[...]
        acc_ref[...] = jnp.zeros_like(acc_ref)

    _run_casts(cast_in, cast_out)
    o_f = _gla_direction(qkvf_ref, ldf_ref, sf_ref, reverse=False)
    o_b = _gla_direction(qkvb_ref, ldb_ref, sb_ref, reverse=True)
    row_f = pl.multiple_of(nb * GLA_TB, GLA_TB)
    row_b = pl.multiple_of((n_blocks - 1 - nb) * GLA_TB, GLA_TB)

    for row, part in ((row_f, o_f), (row_b, o_b)):
        o = acc_ref[pl.ds(row, GLA_TB), :] + part
        acc_ref[pl.ds(row, GLA_TB), :] = o
        ms = jnp.mean(o * o, axis=-1, keepdims=True)
        o = o * lax.rsqrt(ms + HEAD_NORM_EPS) * gain_ref[...]
        o_ref[pl.ds(row, GLA_TB), :] = (o * sg_ref[pl.ds(row, GLA_TB), :].astype(F32)).astype(BF16)


def _gla_call(qkv, ld, sg, s0f, s0b, gain, bsz, t, cast_jobs=()):
    nb = t // GLA_TB
    fwd = lambda n: pl.BlockSpec((GLA_TB, n), lambda b, h, i: (b * nb + i, h))
    bwd = lambda n: pl.BlockSpec((GLA_TB, n), lambda b, h, i: (b * nb + nb - 1 - i, h))
    seq = pl.BlockSpec((t, GLA_DV), lambda b, h, i: (b, h))
    st = pl.BlockSpec((None, None, GLA_DV, GLA_DK), lambda b, h, i: (b, h, 0, 0))
    cast_specs, cast_shapes = _cast_specs(cast_jobs)
    return pl.pallas_call(
        functools.partial(_gla_kernel, len(cast_jobs)),
        grid=(bsz, GLA_HEADS, nb),
        in_specs=[fwd(QKV_HEAD), fwd(LD_HEAD), bwd(QKV_HEAD), bwd(LD_HEAD),
                  seq, st, st, pl.BlockSpec((1, GLA_DV), lambda b, h, i: (0, 0))] + cast_specs,
        out_specs=[seq] + cast_specs,
        out_shape=[jax.ShapeDtypeStruct((bsz * t, D_GLA), BF16)] + cast_shapes,
        scratch_shapes=[pltpu.VMEM((GLA_DV, GLA_DK), F32), pltpu.VMEM((GLA_DV, GLA_DK), F32),
                        pltpu.VMEM((t, GLA_DV), F32)],
        compiler_params=_params("parallel", "parallel", "arbitrary"),
        name="gla",
    )(qkv, ld, qkv, ld, sg, s0f, s0b, gain, *[job.array for job in cast_jobs])


ROW_PAD = 16
ROW_PITCH = GRID_W + 2 * ROW_PAD


def _conv_kernel(n_row_blocks, n_cast, u_ref, w_ref, b_ref, *rest):
    cast_in, y_ref, rest = rest[:n_cast], rest[n_cast], rest[n_cast + 1:]
    cast_out, (pad_ref,) = rest[:n_cast], rest[n_cast:]
    cb = pl.program_id(1)
    rows = u_ref.shape[0] // GRID_W
    bias = jnp.broadcast_to(b_ref[...], (GRID_W, CONV_CB))
    _run_casts(cast_in, cast_out)

    @pl.when(cb < n_row_blocks)
    def _():
        @pl.when(cb == 0)
        def _():
            pad_ref[...] = jnp.zeros_like(pad_ref)

        def fill(r, carry):
            src = pl.multiple_of(r * GRID_W, GRID_W)
            dst = pl.multiple_of(r * ROW_PITCH + ROW_PAD, 8)
            pad_ref[pl.ds(dst, GRID_W), :] = u_ref[pl.ds(src, GRID_W), :]
            return carry

        lax.fori_loop(0, rows, fill, 0)

        def body(r, carry):
            base = r * ROW_PITCH + (ROW_PAD - CONV_HALF)
            acc = bias
            for j in range(CONV_WIDTH):
                acc = acc + w_ref[j:j + 1, :] * pad_ref[pl.ds(base + j, GRID_W), :]
            y_ref[pl.ds(pl.multiple_of(r * GRID_W, GRID_W), GRID_W), :] = acc
            return carry

        lax.fori_loop(0, rows, body, 0, unroll=CONV_UNROLL)

    @pl.when(cb >= n_row_blocks)
    def _():
        edge = CONV_HALF * GRID_W

        @pl.when(cb == n_row_blocks)
        def _():
            pad_ref[0:edge, :] = jnp.zeros((edge, CONV_CB), F32)
            pad_ref[edge + rows * GRID_W:2 * edge + rows * GRID_W, :] = jnp.zeros((edge, CONV_CB), F32)

        pad_ref[edge:edge + rows * GRID_W, :] = u_ref[...]

        def body(r, carry):
            acc = bias
            for j in range(CONV_WIDTH):
                src = pl.multiple_of((r + j) * GRID_W, GRID_W)
                acc = acc + w_ref[j:j + 1, :] * pad_ref[pl.ds(src, GRID_W), :]
            y_ref[pl.ds(pl.multiple_of(r * GRID_W, GRID_W), GRID_W), :] = acc
            return carry

        lax.fori_loop(0, rows, body, 0, unroll=CONV_UNROLL)


def _conv_call(u, w, b, bsz, t, cast_jobs=()):
    ch = u.shape[1]
    rows = t // GRID_W
    n_cb = ch // CONV_CB
    pad_rows = max(rows * ROW_PITCH, (rows + 2 * CONV_HALF) * GRID_W)
    blk = pl.BlockSpec((t, CONV_CB), lambda bi, c: (bi, c))
    cast_specs, cast_shapes = _cast_specs(cast_jobs)
    return pl.pallas_call(
        functools.partial(_conv_kernel, n_cb // 2, len(cast_jobs)),
        grid=(bsz, n_cb),
        in_specs=[blk, pl.BlockSpec((CONV_WIDTH, CONV_CB), lambda bi, c: (0, c)),
                  pl.BlockSpec((1, CONV_CB), lambda bi, c: (0, c))] + cast_specs,
        out_specs=[blk] + cast_specs,
        out_shape=[jax.ShapeDtypeStruct(u.shape, F32)] + cast_shapes,
        scratch_shapes=[pltpu.VMEM((pad_rows, CONV_CB), F32)],
        compiler_params=_params("arbitrary", "arbitrary"),
        name="conv",
    )(u, w, b, *[job.array for job in cast_jobs])


def _outproj_kernel(og_ref, y_ref, lng_ref, lnb_ref, w_ref, h_ref, mod_ref, o_ref):
    y = y_ref[...]
    mu = jnp.mean(y, axis=-1, keepdims=True)
    yc = y - mu
    var = jnp.mean(yc * yc, axis=-1, keepdims=True)
    yn = yc * lax.rsqrt(var + LN_EPS) * lng_ref[...] + lnb_ref[...]
    oc = _silu(yn).astype(BF16)
    res = jnp.dot(og_ref[...], w_ref[0:D_GLA, :], preferred_element_type=F32)
    res = res + jnp.dot(oc, w_ref[D_GLA:D_GLA + D_CONV, :], preferred_element_type=F32)
    o_ref[...] = h_ref[...] + mod_ref[5:6, :] * res


def _outproj_call(og, y, ln_g, ln_b, w_out_bf, h, mod, group_of_tile):
    rows, d = h.shape
    tm = OUT_TM
    row = lambda n: pl.BlockSpec((tm, n), lambda i: (i, 0))
    vec = lambda n: pl.BlockSpec((1, n), lambda i: (0, 0))
    return pl.pallas_call(
        _outproj_kernel,
        grid=(rows // tm,),
        in_specs=[row(D_GLA), row(D_CONV), vec(D_CONV), vec(D_CONV),
                  pl.BlockSpec(w_out_bf.shape, lambda i: (0, 0), pipeline_mode=pl.Buffered(1)),
                  row(d), pl.BlockSpec((None, N_MOD, d), lambda i: (group_of_tile(i), 0, 0))],
        out_specs=row(d),
        out_shape=jax.ShapeDtypeStruct((rows, d), F32),
        compiler_params=_params("parallel"),
        name="outproj",
    )(og, y, ln_g, ln_b, w_out_bf, h, mod)


def kernel(x, c, ctx, c_ctx, w_mod, b_mod, norm_ffn1, w_ffn1_in, w_ffn1_out, norm_mix, w_in, w_gk2, b_gk2, gla_norm, conv_w, conv_b, conv_ln_g, conv_ln_b, w_out, norm_ffn2, w_ffn2_in, w_ffn2_out, norm_final):
    bsz, t, d = x.shape
    t_ctx = ctx.shape[1]
    assert w_mod.shape[0] == 1, "single layer only"
    assert t % GLA_TB == 0 and (t // GLA_TB) % 2 == 0 and t % FFN_TM == 0
    vec = lambda a: a.reshape(1, -1)

    w2pad = jnp.zeros((V7X_LANES, 2 * GLA_KEY), BF16)
    w2pad = w2pad.at[:GATE_RANK, :GLA_KEY].set(w_gk2[0, 0].astype(BF16))
    w2pad = w2pad.at[GATE_RANK:2 * GATE_RANK, GLA_KEY:].set(w_gk2[0, 1].astype(BF16))
    b2 = b_gk2[0].reshape(1, 2 * GLA_KEY)
    w_r = _wprep_call(jnp.swapaxes(w_in[0], 0, 1))

    n_rows = 8
    s_in = jnp.concatenate([c, c_ctx[None, :], jnp.zeros((n_rows - bsz - 1, d), F32)], axis=0)
    mod = _mod_call(s_in, w_mod[0], vec(b_mod[0])).reshape(n_rows, N_MOD, d)

    tiles_per_batch = t // FFN_TM
    lat_group = lambda i: i // tiles_per_batch
    ctx_group = lambda i: bsz
    g1, gm = vec(norm_ffn1[0]), vec(norm_mix[0])

    xc = ctx.reshape(bsz * t_ctx, d)
    hxc, wg1, wu1, wo1 = _ffn_call(
        xc, mod, ctx_group, g1, gm, w_ffn1_in[0], w_ffn1_in[0], w_ffn1_out[0], D_FF,
        _FfnCfg(mod_row=0, emit_h=False, emit_hx=True, final_norm=False, cast_w=True), "ffn1_ctx")
    s0f, s0b = _ctx_call(hxc, w_r, w2pad, b2, bsz, t_ctx)

    xl = x.reshape(bsz * t, d)
    n_i, n_j = (bsz * t) // FFN_TM, D_FF // FFN_TF
    h1, hx, wf2_in = _ffn_call(
        xl, mod, lat_group, g1, gm, wg1, wu1, wo1, 0,
        _FfnCfg(mod_row=0, emit_h=True, emit_hx=True, final_norm=False, cast_w=False), "ffn1",
        (_CastJob(w_ffn2_in[0], (d // n_i, 2 * D_FF // n_j), lambda i, j: (i, j)),))

    qkv, ld, sg, u = _proj_call(hx, w_r, w2pad, b2)
    (og,) = _gla_call(qkv, ld, sg, s0f, s0b, vec(gla_norm[0]), bsz, t)
    n_conv_steps = bsz * (D_CONV // CONV_CB)
    step = lambda bi, cb: (bi * (D_CONV // CONV_CB) + cb, 0)
    y, wf2_out, w_out_bf = _conv_call(
        u, conv_w[0], vec(conv_b[0]), bsz, t,
        (_CastJob(w_ffn2_out[0], (D_FF // n_conv_steps, d), step),
         _CastJob(w_out[0], (d // n_conv_steps, d), step)))
    h2 = _outproj_call(og, y, vec(conv_ln_g[0]), vec(conv_ln_b[0]), w_out_bf, h1, mod,
                       lambda i: i // (t // OUT_TM))

    (out,) = _ffn_call(
        h2, mod, lat_group, vec(norm_ffn2[0]), vec(norm_final), wf2_in, wf2_in, wf2_out, D_FF,
        _FfnCfg(mod_row=6, emit_h=True, emit_hx=False, final_norm=True, cast_w=False), "ffn2")
    return out.reshape(bsz, t, d)
```

```python
import functools
from typing import NamedTuple

import jax
import jax.numpy as jnp
from jax import lax
from jax.experimental import pallas as pl
from jax.experimental.pallas import tpu as pltpu

F32 = jnp.float32
BF16 = jnp.bfloat16

D_MODEL = 2048
GRID_W = 64
GLA_HEADS = 4
GLA_DK = 128
GLA_DV = 256
GLA_KEY = GLA_HEADS * GLA_DK
D_GLA = GLA_HEADS * GLA_DV
D_CONV = 1024
GATE_RANK = 16
GATE_NORMALIZER = 16.0
CHUNK = 64
CONV_WIDTH = 31
CONV_HALF = CONV_WIDTH // 2
D_FF = 5632
N_MOD = 9
RMS_EPS = 1e-6
HEAD_NORM_EPS = 1e-5
LN_EPS = 1e-5

OFF_V = GLA_KEY
OFF_GKF = OFF_V + D_GLA
CTX_COLS = OFF_GKF + 2 * GATE_RANK
OFF_Q = CTX_COLS
OFF_G = OFF_Q + GLA_KEY
OFF_GLU = OFF_G + D_GLA

V7X_LANES = 128
R_K = 0
R_V = R_K + GLA_KEY
R_Q = R_V + D_GLA
R_G = R_Q + GLA_KEY
R_A = R_G + D_GLA
R_B = R_A + D_CONV
R_GK = R_B + D_CONV
R_END = R_GK + V7X_LANES

V7X_VMEM_SCOPED_LIMIT_BYTES = 60000 * 1024
V7X_BF16_SUBLANES = 16

FFN_TM = 512
FFN_TF = 512
FFN_TF_CAST = 256
PROJ_TM = 512
OUT_TM = 512
MOD_TN = 2048
GLA_TB = 2048
GLA_SUB = 256
CONV_CB = 128
CONV_UNROLL = 8
WPREP_TC = 256

NT_DIMS = (((1,), (1,)), ((), ()))
TN_DIMS = (((0,), (0,)), ((), ()))


def _params(*semantics):
    return pltpu.CompilerParams(dimension_semantics=semantics,
                                vmem_limit_bytes=V7X_VMEM_SCOPED_LIMIT_BYTES)


def _silu(x):
    return x * jax.nn.sigmoid(x)


def _rmsnorm_rows(x, gain, eps):
    ms = jnp.mean(x * x, axis=-1, keepdims=True)
    return x * lax.rsqrt(ms + eps) * gain


def _modulate(x, gain, shift, scale):
    return _rmsnorm_rows(x, gain, RMS_EPS) * (1.0 + scale) + shift


def _log_sigmoid(z):
    return jnp.minimum(z, 0.0) - jnp.log1p(jnp.exp(-jnp.abs(z)))


class _CastJob(NamedTuple):
    array: jax.Array
    block: tuple
    index_map: object


def _cast_specs(jobs):
    for job in jobs:
        assert all(n % b == 0 for n, b in zip(job.array.shape, job.block))
        assert job.block[0] % V7X_BF16_SUBLANES == 0 and job.block[1] % V7X_LANES == 0
    specs = [pl.BlockSpec(job.block, job.index_map) for job in jobs]
    shapes = [jax.ShapeDtypeStruct(job.array.shape, BF16) for job in jobs]
    return specs, shapes


def _run_casts(src_refs, dst_refs):
    for src_ref, dst_ref in zip(src_refs, dst_refs):
        dst_ref[...] = src_ref[...].astype(dst_ref.dtype)


def _mod_kernel(s_ref, w_ref, b_ref, o_ref):
    s = _silu(s_ref[...]).astype(BF16)
    o_ref[...] = jnp.dot(s, w_ref[...].astype(BF16), preferred_element_type=F32) + b_ref[...]


def _mod_call(s_in, w_mod, b_mod):
    rows, d = s_in.shape
    n = w_mod.shape[1]
    return pl.pallas_call(
        _mod_kernel,
        grid=(n // MOD_TN,),
        in_specs=[
            pl.BlockSpec((rows, d), lambda j: (0, 0)),
            pl.BlockSpec((d, MOD_TN), lambda j: (0, j)),
            pl.BlockSpec((1, MOD_TN), lambda j: (0, j)),
        ],
        out_specs=pl.BlockSpec((rows, MOD_TN), lambda j: (0, j)),
        out_shape=jax.ShapeDtypeStruct((rows, n), F32),
        compiler_params=_params("arbitrary"),
        name="mod",
    )(s_in, w_mod, b_mod)


class _FfnCfg(NamedTuple):
    mod_row: int
    emit_h: bool
    emit_hx: bool
    final_norm: bool
    cast_w: bool


def _ffn_kernel(cfg, n_cast, x_ref, mod_ref, g_in_ref, g_next_ref, wg_ref, wu_ref, wo_ref, *rest):
    cast_in, rest = rest[:n_cast], rest[n_cast:]
    outs = []
    for flag in (cfg.emit_h, cfg.emit_hx, cfg.cast_w, cfg.cast_w, cfg.cast_w):
        outs.append(rest[0] if flag else None)
        rest = rest[1:] if flag else rest
    h_ref, hx_ref, wg_bf_ref, wu_bf_ref, wo_bf_ref = outs
    cast_out, (hm_ref, acc_ref) = rest[:n_cast], rest[n_cast:]
    j = pl.program_id(1)
    last = pl.num_programs(1) - 1
    r0 = cfg.mod_row
    _run_casts(cast_in, cast_out)

    def partial_out(hm):
        wg, wu, wo = wg_ref[...], wu_ref[...], wo_ref[...]
        if cfg.cast_w:
            wg, wu, wo = wg.astype(BF16), wu.astype(BF16), wo.astype(BF16)
            wg_bf_ref[...] = wg
            wu_bf_ref[...] = wu
            wo_bf_ref[...] = wo
        gate = jnp.dot(hm, wg, preferred_element_type=F32)
        up = jnp.dot(hm, wu, preferred_element_type=F32)
        act = (_silu(gate) * up).astype(BF16)
        return jnp.dot(act, wo, preferred_element_type=F32)

    @pl.when(j == 0)
    def _():
        hm = _modulate(x_ref[...], g_in_ref[...], mod_ref[r0:r0 + 1, :], mod_ref[r0 + 1:r0 + 2, :])
        hm = hm.astype(BF16)
        hm_ref[...] = hm
        acc_ref[...] = partial_out(hm)

    @pl.when((j > 0) & (j < last))
    def _():
        acc_ref[...] += partial_out(hm_ref[...])

    @pl.when(j == last)
    def _():
        g = mod_ref[r0 + 2:r0 + 3, :]
        h = x_ref[...] + (0.5 * g) * (acc_ref[...] + partial_out(hm_ref[...]))
        if cfg.emit_hx:
            hx = _modulate(h, g_next_ref[...], mod_ref[r0 + 3:r0 + 4, :], mod_ref[r0 + 4:r0 + 5, :])
            hx_ref[...] = hx.astype(BF16)
        if cfg.emit_h:
            h_ref[...] = _rmsnorm_rows(h, g_next_ref[...], RMS_EPS) if cfg.final_norm else h


def _ffn_call(x, mod, group_of_tile, g_in, g_next, wg, wu, wo, up_col0, cfg, name, cast_jobs=()):
    rows, d = x.shape
    n_ff = wo.shape[0]
    tf = FFN_TF_CAST if cfg.cast_w else FFN_TF
    n_j = n_ff // tf
    assert n_j >= 2, "first and last hidden block must be distinct grid steps"
    assert not cfg.cast_w or rows == FFN_TM, "bf16 weight copies are written by one row tile only"
    up_blk0 = up_col0 // tf
    row_spec = pl.BlockSpec((FFN_TM, d), lambda i, j: (i, 0))
    vec_spec = pl.BlockSpec((1, d), lambda i, j: (0, 0))
    wg_spec = pl.BlockSpec((d, tf), lambda i, j: (0, j))
    wu_spec = pl.BlockSpec((d, tf), lambda i, j: (0, up_blk0 + j))
    wo_spec = pl.BlockSpec((tf, d), lambda i, j: (j, 0))
    cast_specs, cast_shapes = _cast_specs(cast_jobs)
    in_specs = [row_spec, pl.BlockSpec((None, N_MOD, d), lambda i, j: (group_of_tile(i), 0, 0)),
                vec_spec, vec_spec, wg_spec, wu_spec, wo_spec] + cast_specs
    out_specs, out_shape = [], []
    if cfg.emit_h:
        out_specs.append(row_spec)
        out_shape.append(jax.ShapeDtypeStruct((rows, d), F32))
    if cfg.emit_hx:
        out_specs.append(row_spec)
        out_shape.append(jax.ShapeDtypeStruct((rows, d), BF16))
    if cfg.cast_w:
        out_specs += [wg_spec, pl.BlockSpec((d, tf), lambda i, j: (0, j)), wo_spec]
        out_shape += [jax.ShapeDtypeStruct((d, n_ff), BF16), jax.ShapeDtypeStruct((d, n_ff), BF16),
                      jax.ShapeDtypeStruct((n_ff, d), BF16)]
    return pl.pallas_call(
        functools.partial(_ffn_kernel, cfg, len(cast_jobs)),
        grid=(rows // FFN_TM, n_j),
        in_specs=in_specs,
        out_specs=out_specs + cast_specs,
        out_shape=out_shape + cast_shapes,
        scratch_shapes=[pltpu.VMEM((FFN_TM, d), BF16), pltpu.VMEM((FFN_TM, d), F32)],
        compiler_params=_params("parallel", "arbitrary"),
        name=name,
    )(x, mod, g_in, g_next, wg, wu, wo, *[job.array for job in cast_jobs])


def _wprep_kernel(w_ref, o_ref):
    o_ref[R_K:R_Q, :] = w_ref[0:OFF_GKF, :].astype(BF16)
    o_ref[R_Q:R_GK, :] = w_ref[OFF_Q:OFF_GLU + 2 * D_CONV, :].astype(BF16)
    o_ref[R_GK:R_GK + 2 * GATE_RANK, :] = w_ref[OFF_GKF:CTX_COLS, :].astype(BF16)
    o_ref[R_GK + 2 * GATE_RANK:R_END, :] = jnp.zeros((V7X_LANES - 2 * GATE_RANK, o_ref.shape[1]), BF16)


def _wprep_call(w_t):
    n, d = w_t.shape
    tc = WPREP_TC
    return pl.pallas_call(
        _wprep_kernel,
        grid=(d // tc,),
        in_specs=[pl.BlockSpec((n, tc), lambda i: (0, i))],
        out_specs=pl.BlockSpec((R_END, tc), lambda i: (0, i)),
        out_shape=jax.ShapeDtypeStruct((R_END, d), BF16),
        compiler_params=_params("parallel"),
        name="wprep",
    )(w_t)


def _log_decays(p_gk, w2_ref, b2_ref):
    z = jnp.dot(p_gk.astype(BF16), w2_ref[...], preferred_element_type=F32) + b2_ref[...]
    return _log_sigmoid(z) * (1.0 / GATE_NORMALIZER)


def _chunk_cumsum(x, reverse):
    n = x.shape[0]
    pos = lax.broadcasted_iota(jnp.int32, x.shape, 0) % CHUNK
    d = 1
    while d < CHUNK:
        if reverse:
            shifted = pltpu.roll(x, n - d, 0)
            x = x + jnp.where(pos < CHUNK - d, shifted, 0.0)
        else:
            shifted = pltpu.roll(x, d, 0)
            x = x + jnp.where(pos >= d, shifted, 0.0)
        d *= 2
    return x


def _chunk_rows(x, row_in_chunk):
    return jnp.concatenate([x[c0 + row_in_chunk:c0 + row_in_chunk + 1, :]
                            for c0 in range(0, x.shape[0], CHUNK)], axis=0)


def _chunk_bcast(rows_per_chunk):
    return jnp.concatenate([jnp.broadcast_to(rows_per_chunk[c:c + 1, :], (CHUNK, rows_per_chunk.shape[1]))
                            for c in range(rows_per_chunk.shape[0])], axis=0)


QKV_HEAD = 2 * GLA_DK + GLA_DV
LD_HEAD = 2 * GLA_DK


def _proj_kernel(hx_ref, w_ref, w2_ref, b2_ref, qkv_ref, ld_ref, sg_ref, u_ref):
    hx = hx_ref[...]

    def proj(lo, hi):
        return lax.dot_general(hx, w_ref[lo:hi, :], NT_DIMS, preferred_element_type=F32)

    ld = _log_decays(proj(R_GK, R_END), w2_ref, b2_ref)
    k = proj(R_K, R_V).astype(BF16)
    v = proj(R_V, R_Q).astype(BF16)
    q = (proj(R_Q, R_G) * (GLA_DK ** -0.5)).astype(BF16)
    for h in range(GLA_HEADS):
        dk = slice(h * GLA_DK, (h + 1) * GLA_DK)
        c0 = h * QKV_HEAD
        qkv_ref[:, c0:c0 + GLA_DK] = q[:, dk]
        qkv_ref[:, c0 + GLA_DK:c0 + 2 * GLA_DK] = k[:, dk]
        qkv_ref[:, c0 + 2 * GLA_DK:c0 + QKV_HEAD] = v[:, h * GLA_DV:(h + 1) * GLA_DV]
        ld_ref[:, h * LD_HEAD:h * LD_HEAD + GLA_DK] = ld[:, dk]
        ld_ref[:, h * LD_HEAD + GLA_DK:(h + 1) * LD_HEAD] = ld[:, GLA_KEY + h * GLA_DK:GLA_KEY + (h + 1) * GLA_DK]
    sg_ref[...] = _silu(proj(R_G, R_A)).astype(BF16)
    u_ref[...] = proj(R_A, R_B) * jax.nn.sigmoid(proj(R_B, R_GK))


def _proj_call(hx, w_r, w2pad, b2):
    rows, d = hx.shape
    tm = PROJ_TM
    row = lambda n: pl.BlockSpec((tm, n), lambda i: (i, 0))
    whole = lambda a: pl.BlockSpec(a.shape, lambda i: (0, 0), pipeline_mode=pl.Buffered(1))
    outs = [(GLA_HEADS * QKV_HEAD, BF16), (GLA_HEADS * LD_HEAD, F32), (D_GLA, BF16), (D_CONV, F32)]
    return pl.pallas_call(
        _proj_kernel,
        grid=(rows // tm,),
        in_specs=[row(d), whole(w_r), whole(w2pad), whole(b2)],
        out_specs=[row(n) for n, _ in outs],
        out_shape=[jax.ShapeDtypeStruct((rows, n), dt) for n, dt in outs],
        compiler_params=_params("parallel"),
        name="proj",
    )(hx, w_r, w2pad, b2)


def _ctx_kernel(hx_ref, wkv_ref, wgk_ref, w2_ref, b2_ref, sf_ref, sb_ref):
    hx = hx_ref[...]
    t = hx.shape[0]
    kv = lax.dot_general(hx, wkv_ref[...], NT_DIMS, preferred_element_type=F32)
    ld = _log_decays(lax.dot_general(hx, wgk_ref[...], NT_DIMS, preferred_element_type=F32),
                     w2_ref, b2_ref)
    r = lax.broadcasted_iota(jnp.int32, (t, t), 0)
    c = lax.broadcasted_iota(jnp.int32, (t, t), 1)
    hi = lax.Precision.HIGHEST
    e_f = jnp.dot((c > r).astype(F32), ld[:, :GLA_KEY], preferred_element_type=F32, precision=hi)
    e_b = jnp.dot((c < r).astype(F32), ld[:, GLA_KEY:], preferred_element_type=F32, precision=hi)
    for h in range(GLA_HEADS):
        ks = slice(h * GLA_DK, (h + 1) * GLA_DK)
        k = kv[:, R_K + h * GLA_DK:R_K + (h + 1) * GLA_DK]
        v = kv[:, R_V + h * GLA_DV:R_V + (h + 1) * GLA_DV].astype(BF16)
        sf_ref[h] = lax.dot_general(v, (k * jnp.exp(e_f[:, ks])).astype(BF16), TN_DIMS,
                                    preferred_element_type=F32)
        sb_ref[h] = lax.dot_general(v, (k * jnp.exp(e_b[:, ks])).astype(BF16), TN_DIMS,
                                    preferred_element_type=F32)


def _ctx_call(hxc, w_r, w2pad, b2, bsz, t):
    d = hxc.shape[1]
    const = lambda a: pl.BlockSpec(a.shape, lambda b: (0, 0))
    st = pl.BlockSpec((None, GLA_HEADS, GLA_DV, GLA_DK), lambda b: (b, 0, 0, 0))
    shape = jax.ShapeDtypeStruct((bsz, GLA_HEADS, GLA_DV, GLA_DK), F32)
    return pl.pallas_call(
        _ctx_kernel,
        grid=(bsz,),
        in_specs=[pl.BlockSpec((t, d), lambda b: (b, 0)),
                  pl.BlockSpec((R_Q, d), lambda b: (0, 0)),
                  pl.BlockSpec((V7X_LANES, d), lambda b: (R_GK // V7X_LANES, 0)),
                  const(w2pad), const(b2)],
        out_specs=[st, st],
        out_shape=[shape, shape],
        compiler_params=_params("parallel"),
        name="ctx",
    )(hxc, w_r, w_r, w2pad, b2)


def _gla_direction(qkv_ref, ld_ref, s_ref, reverse):
    n_sub = GLA_TB // GLA_SUB
    n_chunk = GLA_SUB // CHUNK
    rr = lax.broadcasted_iota(jnp.int32, (GLA_SUB, GLA_SUB), 0)
    cc = lax.broadcasted_iota(jnp.int32, (GLA_SUB, GLA_SUB), 1)
    same_chunk = (rr // CHUNK) == (cc // CHUNK)
    mask = same_chunk & ((cc >= rr) if reverse else (cc <= rr))
    mid_row = CHUNK // 2 if reverse else CHUNK // 2 - 1
    last_row = 0 if reverse else CHUNK - 1
    ld0 = GLA_DK if reverse else 0

    outs = [None] * n_sub
    subs = range(n_sub - 1, -1, -1) if reverse else range(n_sub)
    state = s_ref[...]
    for s in subs:
        rows = slice(s * GLA_SUB, (s + 1) * GLA_SUB)
        b = _chunk_cumsum(ld_ref[rows, ld0:ld0 + GLA_DK], reverse)
        b_mid = _chunk_bcast(_chunk_rows(b, mid_row))
        b_last_rows = _chunk_rows(b, last_row)
        b_last = _chunk_bcast(b_last_rows)
        q = qkv_ref[rows, 0:GLA_DK].astype(F32)
        k = qkv_ref[rows, GLA_DK:2 * GLA_DK].astype(F32)
        v = qkv_ref[rows, 2 * GLA_DK:QKV_HEAD]
        qs = (q * jnp.exp(b - b_mid)).astype(BF16)
        ks = (k * jnp.exp(b_mid - b)).astype(BF16)
        qi = (q * jnp.exp(b)).astype(BF16)
        kd = (k * jnp.exp(b_last - b)).astype(BF16)
        att = lax.dot_general(qs, ks, NT_DIMS, preferred_element_type=F32)
        att = jnp.where(mask, att, 0.0).astype(BF16)
        o_intra = jnp.dot(att, v, preferred_element_type=F32)
        decay = jnp.exp(b_last_rows)
        o_parts = [None] * n_chunk
        chunks = range(n_chunk - 1, -1, -1) if reverse else range(n_chunk)
        for c in chunks:
            cr = slice(c * CHUNK, (c + 1) * CHUNK)
            o_parts[c] = o_intra[cr, :] + lax.dot_general(
                qi[cr, :], state.astype(BF16), NT_DIMS, preferred_element_type=F32)
            kv = lax.dot_general(v[cr, :], kd[cr, :], TN_DIMS, preferred_element_type=F32)
            state = state * decay[c:c + 1, :] + kv
        outs[s] = jnp.concatenate(o_parts, axis=0)
    s_ref[...] = state
    return jnp.concatenate(outs, axis=0)


def _gla_kernel(n_cast, qkvf_ref, ldf_ref, qkvb_ref, ldb_ref, sg_ref, s0f_ref, s0b_ref, gain_ref,
                *rest):
    cast_in, o_ref, rest = rest[:n_cast], rest[n_cast], rest[n_cast + 1:]
    cast_out, (sf_ref, sb_ref, acc_ref) = rest[:n_cast], rest[n_cast:]
    nb = pl.program_id(2)
    n_blocks = pl.num_programs(2)

    @pl.when(nb == 0)
    def _():
        sf_ref[...] = s0f_ref[...]
        sb_ref[...] = s0b_ref[...]
        acc_ref[...] = jnp.zeros_like(acc_ref)

    _run_casts(cast_in, cast_out)
    o_f = _gla_direction(qkvf_ref, ldf_ref, sf_ref, reverse=False)
    o_b = _gla_direction(qkvb_ref, ldb_ref, sb_ref, reverse=True)
    row_f = pl.multiple_of(nb * GLA_TB, GLA_TB)
    row_b = pl.multiple_of((n_blocks - 1 - nb) * GLA_TB, GLA_TB)

    for row, part in ((row_f, o_f), (row_b, o_b)):
        o = acc_ref[pl.ds(row, GLA_TB), :] + part
        acc_ref[pl.ds(row, GLA_TB), :] = o
        ms = jnp.mean(o * o, axis=-1, keepdims=True)
        o = o * lax.rsqrt(ms + HEAD_NORM_EPS) * gain_ref[...]
        o_ref[pl.ds(row, GLA_TB), :] = (o * sg_ref[pl.ds(row, GLA_TB), :].astype(F32)).astype(BF16)


def _gla_call(qkv, ld, sg, s0f, s0b, gain, bsz, t, cast_jobs=()):
    nb = t // GLA_TB
    fwd = lambda n: pl.BlockSpec((GLA_TB, n), lambda b, h, i: (b * nb + i, h))
    bwd = lambda n: pl.BlockSpec((GLA_TB, n), lambda b, h, i: (b * nb + nb - 1 - i, h))
    seq = pl.BlockSpec((t, GLA_DV), lambda b, h, i: (b, h))
    st = pl.BlockSpec((None, None, GLA_DV, GLA_DK), lambda b, h, i: (b, h, 0, 0))
    cast_specs, cast_shapes = _cast_specs(cast_jobs)
    return pl.pallas_call(
        functools.partial(_gla_kernel, len(cast_jobs)),
        grid=(bsz, GLA_HEADS, nb),
        in_specs=[fwd(QKV_HEAD), fwd(LD_HEAD), bwd(QKV_HEAD), bwd(LD_HEAD),
                  seq, st, st, pl.BlockSpec((1, GLA_DV), lambda b, h, i: (0, 0))] + cast_specs,
        out_specs=[seq] + cast_specs,
        out_shape=[jax.ShapeDtypeStruct((bsz * t, D_GLA), BF16)] + cast_shapes,
        scratch_shapes=[pltpu.VMEM((GLA_DV, GLA_DK), F32), pltpu.VMEM((GLA_DV, GLA_DK), F32),
                        pltpu.VMEM((t, GLA_DV), F32)],
        compiler_params=_params("parallel", "parallel", "arbitrary"),
        name="gla",
    )(qkv, ld, qkv, ld, sg, s0f, s0b, gain, *[job.array for job in cast_jobs])


ROW_PAD = 16
ROW_PITCH = GRID_W + 2 * ROW_PAD


def _conv_kernel(n_row_blocks, n_cast, u_ref, w_ref, b_ref, *rest):
    cast_in, y_ref, rest = rest[:n_cast], rest[n_cast], rest[n_cast + 1:]
    cast_out, (pad_ref,) = rest[:n_cast], rest[n_cast:]
    cb = pl.program_id(1)
    rows = u_ref.shape[0] // GRID_W
    bias = jnp.broadcast_to(b_ref[...], (GRID_W, CONV_CB))
    _run_casts(cast_in, cast_out)

    @pl.when(cb < n_row_blocks)
    def _():
        @pl.when(cb == 0)
        def _():
            pad_ref[...] = jnp.zeros_like(pad_ref)

        def fill(r, carry):
            src = pl.multiple_of(r * GRID_W, GRID_W)
            dst = pl.multiple_of(r * ROW_PITCH + ROW_PAD, 8)
            pad_ref[pl.ds(dst, GRID_W), :] = u_ref[pl.ds(src, GRID_W), :]
            return carry

        lax.fori_loop(0, rows, fill, 0)

        def body(r, carry):
            base = r * ROW_PITCH + (ROW_PAD - CONV_HALF)
            acc = bias
            for j in range(CONV_WIDTH):
                acc = acc + w_ref[j:j + 1, :] * pad_ref[pl.ds(base + j, GRID_W), :]
            y_ref[pl.ds(pl.multiple_of(r * GRID_W, GRID_W), GRID_W), :] = acc
            return carry

        lax.fori_loop(0, rows, body, 0, unroll=CONV_UNROLL)

    @pl.when(cb >= n_row_blocks)
    def _():
        edge = CONV_HALF * GRID_W

        @pl.when(cb == n_row_blocks)
        def _():
            pad_ref[0:edge, :] = jnp.zeros((edge, CONV_CB), F32)
            pad_ref[edge + rows * GRID_W:2 * edge + rows * GRID_W, :] = jnp.zeros((edge, CONV_CB), F32)

        pad_ref[edge:edge + rows * GRID_W, :] = u_ref[...]

        def body(r, carry):
            acc = bias
            for j in range(CONV_WIDTH):
                src = pl.multiple_of((r + j) * GRID_W, GRID_W)
                acc = acc + w_ref[j:j + 1, :] * pad_ref[pl.ds(src, GRID_W), :]
            y_ref[pl.ds(pl.multiple_of(r * GRID_W, GRID_W), GRID_W), :] = acc
            return carry

        lax.fori_loop(0, rows, body, 0, unroll=CONV_UNROLL)


def _conv_call(u, w, b, bsz, t, cast_jobs=()):
    ch = u.shape[1]
    rows = t // GRID_W
    n_cb = ch // CONV_CB
    pad_rows = max(rows * ROW_PITCH, (rows + 2 * CONV_HALF) * GRID_W)
    blk = pl.BlockSpec((t, CONV_CB), lambda bi, c: (bi, c))
    cast_specs, cast_shapes = _cast_specs(cast_jobs)
    return pl.pallas_call(
        functools.partial(_conv_kernel, n_cb // 2, len(cast_jobs)),
        grid=(bsz, n_cb),
        in_specs=[blk, pl.BlockSpec((CONV_WIDTH, CONV_CB), lambda bi, c: (0, c)),
                  pl.BlockSpec((1, CONV_CB), lambda bi, c: (0, c))] + cast_specs,
        out_specs=[blk] + cast_specs,
        out_shape=[jax.ShapeDtypeStruct(u.shape, F32)] + cast_shapes,
        scratch_shapes=[pltpu.VMEM((pad_rows, CONV_CB), F32)],
        compiler_params=_params("arbitrary", "arbitrary"),
        name="conv",
    )(u, w, b, *[job.array for job in cast_jobs])


def _outproj_kernel(og_ref, y_ref, lng_ref, lnb_ref, w_ref, h_ref, mod_ref, o_ref):
    y = y_ref[...]
    mu = jnp.mean(y, axis=-1, keepdims=True)
    yc = y - mu
    var = jnp.mean(yc * yc, axis=-1, keepdims=True)
    yn = yc * lax.rsqrt(var + LN_EPS) * lng_ref[...] + lnb_ref[...]
    oc = _silu(yn).astype(BF16)
    res = jnp.dot(og_ref[...], w_ref[0:D_GLA, :], preferred_element_type=F32)
    res = res + jnp.dot(oc, w_ref[D_GLA:D_GLA + D_CONV, :], preferred_element_type=F32)
    o_ref[...] = h_ref[...] + mod_ref[5:6, :] * res


def _outproj_call(og, y, ln_g, ln_b, w_out_bf, h, mod, group_of_tile):
    rows, d = h.shape
    tm = OUT_TM
    row = lambda n: pl.BlockSpec((tm, n), lambda i: (i, 0))
    vec = lambda n: pl.BlockSpec((1, n), lambda i: (0, 0))
    return pl.pallas_call(
        _outproj_kernel,
        grid=(rows // tm,),
        in_specs=[row(D_GLA), row(D_CONV), vec(D_CONV), vec(D_CONV),
                  pl.BlockSpec(w_out_bf.shape, lambda i: (0, 0), pipeline_mode=pl.Buffered(1)),
                  row(d), pl.BlockSpec((None, N_MOD, d), lambda i: (group_of_tile(i), 0, 0))],
        out_specs=row(d),
        out_shape=jax.ShapeDtypeStruct((rows, d), F32),
        compiler_params=_params("parallel"),
        name="outproj",
    )(og, y, ln_g, ln_b, w_out_bf, h, mod)


def kernel(x, c, ctx, c_ctx, w_mod, b_mod, norm_ffn1, w_ffn1_in, w_ffn1_out, norm_mix, w_in, w_gk2, b_gk2, gla_norm, conv_w, conv_b, conv_ln_g, conv_ln_b, w_out, norm_ffn2, w_ffn2_in, w_ffn2_out, norm_final):
    bsz, t, d = x.shape
    t_ctx = ctx.shape[1]
    assert w_mod.shape[0] == 1, "single layer only"
    assert t % GLA_TB == 0 and (t // GLA_TB) % 2 == 0 and t % FFN_TM == 0
    vec = lambda a: a.reshape(1, -1)

    w2pad = jnp.zeros((V7X_LANES, 2 * GLA_KEY), BF16)
    w2pad = w2pad.at[:GATE_RANK, :GLA_KEY].set(w_gk2[0, 0].astype(BF16))
    w2pad = w2pad.at[GATE_RANK:2 * GATE_RANK, GLA_KEY:].set(w_gk2[0, 1].astype(BF16))
    b2 = b_gk2[0].reshape(1, 2 * GLA_KEY)
    w_r = _wprep_call(jnp.swapaxes(w_in[0], 0, 1))

    n_rows = 8
    s_in = jnp.concatenate([c, c_ctx[None, :], jnp.zeros((n_rows - bsz - 1, d), F32)], axis=0)
    mod = _mod_call(s_in, w_mod[0], vec(b_mod[0])).reshape(n_rows, N_MOD, d)

    tiles_per_batch = t // FFN_TM
    lat_group = lambda i: i // tiles_per_batch
    ctx_group = lambda i: bsz
    g1, gm = vec(norm_ffn1[0]), vec(norm_mix[0])

    xc = ctx.reshape(bsz * t_ctx, d)
    hxc, wg1, wu1, wo1 = _ffn_call(
        xc, mod, ctx_group, g1, gm, w_ffn1_in[0], w_ffn1_in[0], w_ffn1_out[0], D_FF,
        _FfnCfg(mod_row=0, emit_h=False, emit_hx=True, final_norm=False, cast_w=True), "ffn1_ctx")
    s0f, s0b = _ctx_call(hxc, w_r, w2pad, b2, bsz, t_ctx)

    xl = x.reshape(bsz * t, d)
    n_i, n_j = (bsz * t) // FFN_TM, D_FF // FFN_TF
    h1, hx, wf2_in = _ffn_call(
        xl, mod, lat_group, g1, gm, wg1, wu1, wo1, 0,
        _FfnCfg(mod_row=0, emit_h=True, emit_hx=True, final_norm=False, cast_w=False), "ffn1",
        (_CastJob(w_ffn2_in[0], (d // n_i, 2 * D_FF // n_j), lambda i, j: (i, j)),))

    qkv, ld, sg, u = _proj_call(hx, w_r, w2pad, b2)
    (og,) = _gla_call(qkv, ld, sg, s0f, s0b, vec(gla_norm[0]), bsz, t)
    n_conv_steps = bsz * (D_CONV // CONV_CB)
    step = lambda bi, cb: (bi * (D_CONV // CONV_CB) + cb, 0)
    y, wf2_out, w_out_bf = _conv_call(
        u, conv_w[0], vec(conv_b[0]), bsz, t,
        (_CastJob(w_ffn2_out[0], (D_FF // n_conv_steps, d), step),
         _CastJob(w_out[0], (d // n_conv_steps, d), step)))
    h2 = _outproj_call(og, y, vec(conv_ln_g[0]), vec(conv_ln_b[0]), w_out_bf, h1, mod,
                       lambda i: i // (t // OUT_TM))

    (out,) = _ffn_call(
        h2, mod, lat_group, vec(norm_ffn2[0]), vec(norm_final), wf2_in, wf2_in, wf2_out, D_FF,
        _FfnCfg(mod_row=6, emit_h=True, emit_hx=False, final_norm=True, cast_w=False), "ffn2")
    return out.reshape(bsz, t, d)
```

```python
import functools
from typing import NamedTuple

import jax
import jax.numpy as jnp
from jax import lax
from jax.experimental import pallas as pl
from jax.experimental.pallas import tpu as pltpu

F32 = jnp.float32
BF16 = jnp.bfloat16

D_MODEL = 2048
GRID_W = 64
GLA_HEADS = 4
GLA_DK = 128
GLA_DV = 256
GLA_KEY = GLA_HEADS * GLA_DK
D_GLA = GLA_HEADS * GLA_DV
D_CONV = 1024
GATE_RANK = 16
GATE_NORMALIZER = 16.0
CHUNK = 64
CONV_WIDTH = 31
CONV_HALF = CONV_WIDTH // 2
D_FF = 5632
N_MOD = 9
RMS_EPS = 1e-6
HEAD_NORM_EPS = 1e-5
LN_EPS = 1e-5

OFF_V = GLA_KEY
OFF_GKF = OFF_V + D_GLA
CTX_COLS = OFF_GKF + 2 * GATE_RANK
OFF_Q = CTX_COLS
OFF_G = OFF_Q + GLA_KEY
OFF_GLU = OFF_G + D_GLA

V7X_LANES = 128
R_K = 0
R_V = R_K + GLA_KEY
R_Q = R_V + D_GLA
R_G = R_Q + GLA_KEY
R_A = R_G + D_GLA
R_B = R_A + D_CONV
R_GK = R_B + D_CONV
R_END = R_GK + V7X_LANES

V7X_VMEM_SCOPED_LIMIT_BYTES = 60000 * 1024
V7X_BF16_SUBLANES = 16

FFN_TM = 512
FFN_TF = 512
FFN_TF_CAST = 512
PROJ_TM = 512
OUT_TM = 512
MOD_TN = 2048
GLA_TB = 2048
GLA_SUB = 256
CONV_CB = 128
CONV_UNROLL = 8
WPREP_TC = 256

NT_DIMS = (((1,), (1,)), ((), ()))
TN_DIMS = (((0,), (0,)), ((), ()))


def _params(*semantics):
    return pltpu.CompilerParams(dimension_semantics=semantics,
                                vmem_limit_bytes=V7X_VMEM_SCOPED_LIMIT_BYTES)


def _silu(x):
    return x * jax.nn.sigmoid(x)


def _rmsnorm_rows(x, gain, eps):
    ms = jnp.mean(x * x, axis=-1, keepdims=True)
    return x * lax.rsqrt(ms + eps) * gain


def _modulate(x, gain, shift, scale):
    return _rmsnorm_rows(x, gain, RMS_EPS) * (1.0 + scale) + shift


def _log_sigmoid(z):
    return jnp.minimum(z, 0.0) - jnp.log1p(jnp.exp(-jnp.abs(z)))


class _CastJob(NamedTuple):
    array: jax.Array
    block: tuple
    index_map: object


def _cast_specs(jobs):
    for job in jobs:
        assert all(n % b == 0 for n, b in zip(job.array.shape, job.block))
        assert job.block[0] % V7X_BF16_SUBLANES == 0 and job.block[1] % V7X_LANES == 0
    specs = [pl.BlockSpec(job.block, job.index_map) for job in jobs]
    shapes = [jax.ShapeDtypeStruct(job.array.shape, BF16) for job in jobs]
    return specs, shapes


def _run_casts(src_refs, dst_refs):
    for src_ref, dst_ref in zip(src_refs, dst_refs):
        dst_ref[...] = src_ref[...].astype(dst_ref.dtype)


def _mod_kernel(s_ref, w_ref, b_ref, o_ref):
    s = _silu(s_ref[...]).astype(BF16)
    o_ref[...] = jnp.dot(s, w_ref[...].astype(BF16), preferred_element_type=F32) + b_ref[...]


def _mod_call(s_in, w_mod, b_mod):
    rows, d = s_in.shape
    n = w_mod.shape[1]
    return pl.pallas_call(
        _mod_kernel,
        grid=(n // MOD_TN,),
        in_specs=[
            pl.BlockSpec((rows, d), lambda j: (0, 0)),
            pl.BlockSpec((d, MOD_TN), lambda j: (0, j)),
            pl.BlockSpec((1, MOD_TN), lambda j: (0, j)),
        ],
        out_specs=pl.BlockSpec((rows, MOD_TN), lambda j: (0, j)),
        out_shape=jax.ShapeDtypeStruct((rows, n), F32),
        compiler_params=_params("arbitrary"),
        name="mod",
    )(s_in, w_mod, b_mod)


class _FfnCfg(NamedTuple):
    mod_row: int
    emit_h: bool
    emit_hx: bool
    final_norm: bool
    cast_w: bool


def _ffn_kernel(cfg, n_cast, x_ref, mod_ref, g_in_ref, g_next_ref, wg_ref, wu_ref, wo_ref, *rest):
    cast_in, rest = rest[:n_cast], rest[n_cast:]
    outs = []
    for flag in (cfg.emit_h, cfg.emit_hx, cfg.cast_w, cfg.cast_w, cfg.cast_w):
        outs.append(rest[0] if flag else None)
        rest = rest[1:] if flag else rest
    h_ref, hx_ref, wg_bf_ref, wu_bf_ref, wo_bf_ref = outs
    cast_out, (hm_ref, acc_ref) = rest[:n_cast], rest[n_cast:]
    j = pl.program_id(1)
    last = pl.num_programs(1) - 1
    r0 = cfg.mod_row
    _run_casts(cast_in, cast_out)

    def partial_out(hm):
        wg, wu, wo = wg_ref[...], wu_ref[...], wo_ref[...]
        if cfg.cast_w:
            wg, wu, wo = wg.astype(BF16), wu.astype(BF16), wo.astype(BF16)
            wg_bf_ref[...] = wg
            wu_bf_ref[...] = wu
            wo_bf_ref[...] = wo
        gate = jnp.dot(hm, wg, preferred_element_type=F32)
        up = jnp.dot(hm, wu, preferred_element_type=F32)
        act = (_silu(gate) * up).astype(BF16)
        return jnp.dot(act, wo, preferred_element_type=F32)

    @pl.when(j == 0)
    def _():
        hm = _modulate(x_ref[...], g_in_ref[...], mod_ref[r0:r0 + 1, :], mod_ref[r0 + 1:r0 + 2, :])
        hm = hm.astype(BF16)
        hm_ref[...] = hm
        acc_ref[...] = partial_out(hm)

    @pl.when((j > 0) & (j < last))
    def _():
        acc_ref[...] += partial_out(hm_ref[...])

    @pl.when(j == last)
    def _():
        g = mod_ref[r0 + 2:r0 + 3, :]
        h = x_ref[...] + (0.5 * g) * (acc_ref[...] + partial_out(hm_ref[...]))
        if cfg.emit_hx:
            hx = _modulate(h, g_next_ref[...], mod_ref[r0 + 3:r0 + 4, :], mod_ref[r0 + 4:r0 + 5, :])
            hx_ref[...] = hx.astype(BF16)
        if cfg.emit_h:
            h_ref[...] = _rmsnorm_rows(h, g_next_ref[...], RMS_EPS) if cfg.final_norm else h


def _ffn_call(x, mod, group_of_tile, g_in, g_next, wg, wu, wo, up_col0, cfg, name, cast_jobs=()):
    rows, d = x.shape
    n_ff = wo.shape[0]
    tf = FFN_TF_CAST if cfg.cast_w else FFN_TF
    n_j = n_ff // tf
    assert n_j >= 2, "first and last hidden block must be distinct grid steps"
    assert not cfg.cast_w or rows == FFN_TM, "bf16 weight copies are written by one row tile only"
    up_blk0 = up_col0 // tf
    row_spec = pl.BlockSpec((FFN_TM, d), lambda i, j: (i, 0))
    vec_spec = pl.BlockSpec((1, d), lambda i, j: (0, 0))
    wg_spec = pl.BlockSpec((d, tf), lambda i, j: (0, j))
    wu_spec = pl.BlockSpec((d, tf), lambda i, j: (0, up_blk0 + j))
    wo_spec = pl.BlockSpec((tf, d), lambda i, j: (j, 0))
    cast_specs, cast_shapes = _cast_specs(cast_jobs)
    in_specs = [row_spec, pl.BlockSpec((None, N_MOD, d), lambda i, j: (group_of_tile(i), 0, 0)),
                vec_spec, vec_spec, wg_spec, wu_spec, wo_spec] + cast_specs
    out_specs, out_shape = [], []
    if cfg.emit_h:
        out_specs.append(row_spec)
        out_shape.append(jax.ShapeDtypeStruct((rows, d), F32))
    if cfg.emit_hx:
        out_specs.append(row_spec)
        out_shape.append(jax.ShapeDtypeStruct((rows, d), BF16))
    if cfg.cast_w:
        out_specs += [wg_spec, pl.BlockSpec((d, tf), lambda i, j: (0, j)), wo_spec]
        out_shape += [jax.ShapeDtypeStruct((d, n_ff), BF16), jax.ShapeDtypeStruct((d, n_ff), BF16),
                      jax.ShapeDtypeStruct((n_ff, d), BF16)]
    return pl.pallas_call(
        functools.partial(_ffn_kernel, cfg, len(cast_jobs)),
        grid=(rows // FFN_TM, n_j),
        in_specs=in_specs,
        out_specs=out_specs + cast_specs,
        out_shape=out_shape + cast_shapes,
        scratch_shapes=[pltpu.VMEM((FFN_TM, d), BF16), pltpu.VMEM((FFN_TM, d), F32)],
        compiler_params=_params("parallel", "arbitrary"),
        name=name,
    )(x, mod, g_in, g_next, wg, wu, wo, *[job.array for job in cast_jobs])


def _wprep_kernel(w_ref, o_ref):
    o_ref[R_K:R_Q, :] = w_ref[0:OFF_GKF, :].astype(BF16)
    o_ref[R_Q:R_GK, :] = w_ref[OFF_Q:OFF_GLU + 2 * D_CONV, :].astype(BF16)
    o_ref[R_GK:R_GK + 2 * GATE_RANK, :] = w_ref[OFF_GKF:CTX_COLS, :].astype(BF16)
    o_ref[R_GK + 2 * GATE_RANK:R_END, :] = jnp.zeros((V7X_LANES - 2 * GATE_RANK, o_ref.shape[1]), BF16)


def _wprep_call(w_t):
    n, d = w_t.shape
    tc = WPREP_TC
    return pl.pallas_call(
        _wprep_kernel,
        grid=(d // tc,),
        in_specs=[pl.BlockSpec((n, tc), lambda i: (0, i))],
        out_specs=pl.BlockSpec((R_END, tc), lambda i: (0, i)),
        out_shape=jax.ShapeDtypeStruct((R_END, d), BF16),
        compiler_params=_params("parallel"),
        name="wprep",
    )(w_t)


def _log_decays(p_gk, w2_ref, b2_ref):
    z = jnp.dot(p_gk.astype(BF16), w2_ref[...], preferred_element_type=F32) + b2_ref[...]
    return _log_sigmoid(z) * (1.0 / GATE_NORMALIZER)


def _chunk_cumsum(x, reverse):
    n = x.shape[0]
    pos = lax.broadcasted_iota(jnp.int32, x.shape, 0) % CHUNK
    d = 1
    while d < CHUNK:
        if reverse:
            shifted = pltpu.roll(x, n - d, 0)
            x = x + jnp.where(pos < CHUNK - d, shifted, 0.0)
        else:
            shifted = pltpu.roll(x, d, 0)
            x = x + jnp.where(pos >= d, shifted, 0.0)
        d *= 2
    return x


def _chunk_rows(x, row_in_chunk):
    return jnp.concatenate([x[c0 + row_in_chunk:c0 + row_in_chunk + 1, :]
                            for c0 in range(0, x.shape[0], CHUNK)], axis=0)


def _chunk_bcast(rows_per_chunk):
    return jnp.concatenate([jnp.broadcast_to(rows_per_chunk[c:c + 1, :], (CHUNK, rows_per_chunk.shape[1]))
                            for c in range(rows_per_chunk.shape[0])], axis=0)


QKV_HEAD = 2 * GLA_DK + GLA_DV
LD_HEAD = 2 * GLA_DK


def _proj_kernel(hx_ref, w_ref, w2_ref, b2_ref, qkv_ref, ld_ref, sg_ref, u_ref):
    hx = hx_ref[...]

    def proj(lo, hi):
        return lax.dot_general(hx, w_ref[lo:hi, :], NT_DIMS, preferred_element_type=F32)

    ld = _log_decays(proj(R_GK, R_END), w2_ref, b2_ref)
    k = proj(R_K, R_V).astype(BF16)
    v = proj(R_V, R_Q).astype(BF16)
    q = (proj(R_Q, R_G) * (GLA_DK ** -0.5)).astype(BF16)
    for h in range(GLA_HEADS):
        dk = slice(h * GLA_DK, (h + 1) * GLA_DK)
        c0 = h * QKV_HEAD
        qkv_ref[:, c0:c0 + GLA_DK] = q[:, dk]
        qkv_ref[:, c0 + GLA_DK:c0 + 2 * GLA_DK] = k[:, dk]
        qkv_ref[:, c0 + 2 * GLA_DK:c0 + QKV_HEAD] = v[:, h * GLA_DV:(h + 1) * GLA_DV]
        ld_ref[:, h * LD_HEAD:h * LD_HEAD + GLA_DK] = ld[:, dk]
        ld_ref[:, h * LD_HEAD + GLA_DK:(h + 1) * LD_HEAD] = ld[:, GLA_KEY + h * GLA_DK:GLA_KEY + (h + 1) * GLA_DK]
    sg_ref[...] = _silu(proj(R_G, R_A)).astype(BF16)
    u_ref[...] = proj(R_A, R_B) * jax.nn.sigmoid(proj(R_B, R_GK))


def _proj_call(hx, w_r, w2pad, b2):
    rows, d = hx.shape
    tm = PROJ_TM
    row = lambda n: pl.BlockSpec((tm, n), lambda i: (i, 0))
    whole = lambda a: pl.BlockSpec(a.shape, lambda i: (0, 0), pipeline_mode=pl.Buffered(1))
    outs = [(GLA_HEADS * QKV_HEAD, BF16), (GLA_HEADS * LD_HEAD, F32), (D_GLA, BF16), (D_CONV, F32)]
    return pl.pallas_call(
        _proj_kernel,
        grid=(rows // tm,),
        in_specs=[row(d), whole(w_r), whole(w2pad), whole(b2)],
        out_specs=[row(n) for n, _ in outs],
        out_shape=[jax.ShapeDtypeStruct((rows, n), dt) for n, dt in outs],
        compiler_params=_params("parallel"),
        name="proj",
    )(hx, w_r, w2pad, b2)


def _ctx_kernel(hx_ref, wkv_ref, wgk_ref, w2_ref, b2_ref, sf_ref, sb_ref):
    hx = hx_ref[...]
    t = hx.shape[0]
    kv = lax.dot_general(hx, wkv_ref[...], NT_DIMS, preferred_element_type=F32)
    ld = _log_decays(lax.dot_general(hx, wgk_ref[...], NT_DIMS, preferred_element_type=F32),
                     w2_ref, b2_ref)
    r = lax.broadcasted_iota(jnp.int32, (t, t), 0)
    c = lax.broadcasted_iota(jnp.int32, (t, t), 1)
    hi = lax.Precision.HIGHEST
    e_f = jnp.dot((c > r).astype(F32), ld[:, :GLA_KEY], preferred_element_type=F32, precision=hi)
    e_b = jnp.dot((c < r).astype(F32), ld[:, GLA_KEY:], preferred_element_type=F32, precision=hi)
    for h in range(GLA_HEADS):
        ks = slice(h * GLA_DK, (h + 1) * GLA_DK)
        k = kv[:, R_K + h * GLA_DK:R_K + (h + 1) * GLA_DK]
        v = kv[:, R_V + h * GLA_DV:R_V + (h + 1) * GLA_DV].astype(BF16)
        sf_ref[h] = lax.dot_general(v, (k * jnp.exp(e_f[:, ks])).astype(BF16), TN_DIMS,
                                    preferred_element_type=F32)
        sb_ref[h] = lax.dot_general(v, (k * jnp.exp(e_b[:, ks])).astype(BF16), TN_DIMS,
                                    preferred_element_type=F32)


def _ctx_call(hxc, w_r, w2pad, b2, bsz, t):
    d = hxc.shape[1]
    const = lambda a: pl.BlockSpec(a.shape, lambda b: (0, 0))
    st = pl.BlockSpec((None, GLA_HEADS, GLA_DV, GLA_DK), lambda b: (b, 0, 0, 0))
    shape = jax.ShapeDtypeStruct((bsz, GLA_HEADS, GLA_DV, GLA_DK), F32)
    return pl.pallas_call(
        _ctx_kernel,
        grid=(bsz,),
        in_specs=[pl.BlockSpec((t, d), lambda b: (b, 0)),
                  pl.BlockSpec((R_Q, d), lambda b: (0, 0)),
                  pl.BlockSpec((V7X_LANES, d), lambda b: (R_GK // V7X_LANES, 0)),
                  const(w2pad), const(b2)],
        out_specs=[st, st],
        out_shape=[shape, shape],
        compiler_params=_params("parallel"),
        name="ctx",
    )(hxc, w_r, w_r, w2pad, b2)


def _gla_direction(qkv_ref, ld_ref, s_ref, reverse):
    n_sub = GLA_TB // GLA_SUB
    n_chunk = GLA_SUB // CHUNK
    rr = lax.broadcasted_iota(jnp.int32, (GLA_SUB, GLA_SUB), 0)
    cc = lax.broadcasted_iota(jnp.int32, (GLA_SUB, GLA_SUB), 1)
    same_chunk = (rr // CHUNK) == (cc // CHUNK)
    mask = same_chunk & ((cc >= rr) if reverse else (cc <= rr))
    mid_row = CHUNK // 2 if reverse else CHUNK // 2 - 1
    last_row = 0 if reverse else CHUNK - 1
    ld0 = GLA_DK if reverse else 0

    outs = [None] * n_sub
    subs = range(n_sub - 1, -1, -1) if reverse else range(n_sub)
    state = s_ref[...]
    for s in subs:
        rows = slice(s * GLA_SUB, (s + 1) * GLA_SUB)
        b = _chunk_cumsum(ld_ref[rows, ld0:ld0 + GLA_DK], reverse)
        b_mid = _chunk_bcast(_chunk_rows(b, mid_row))
        b_last_rows = _chunk_rows(b, last_row)
        b_last = _chunk_bcast(b_last_rows)
        q = qkv_ref[rows, 0:GLA_DK].astype(F32)
        k = qkv_ref[rows, GLA_DK:2 * GLA_DK].astype(F32)
        v = qkv_ref[rows, 2 * GLA_DK:QKV_HEAD]
        qs = (q * jnp.exp(b - b_mid)).astype(BF16)
        ks = (k * jnp.exp(b_mid - b)).astype(BF16)
        qi = (q * jnp.exp(b)).astype(BF16)
        kd = (k * jnp.exp(b_last - b)).astype(BF16)
        att = lax.dot_general(qs, ks, NT_DIMS, preferred_element_type=F32)
        att = jnp.where(mask, att, 0.0).astype(BF16)
        o_intra = jnp.dot(att, v, preferred_element_type=F32)
        decay = jnp.exp(b_last_rows)
        o_parts = [None] * n_chunk
        chunks = range(n_chunk - 1, -1, -1) if reverse else range(n_chunk)
        for c in chunks:
            cr = slice(c * CHUNK, (c + 1) * CHUNK)
            o_parts[c] = o_intra[cr, :] + lax.dot_general(
                qi[cr, :], state.astype(BF16), NT_DIMS, preferred_element_type=F32)
            kv = lax.dot_general(v[cr, :], kd[cr, :], TN_DIMS, preferred_element_type=F32)
            state = state * decay[c:c + 1, :] + kv
        outs[s] = jnp.concatenate(o_parts, axis=0)
    s_ref[...] = state
    return jnp.concatenate(outs, axis=0)


def _gla_kernel(n_cast, qkvf_ref, ldf_ref, qkvb_ref, ldb_ref, sg_ref, s0f_ref, s0b_ref, gain_ref,
                *rest):
    cast_in, o_ref, rest = rest[:n_cast], rest[n_cast], rest[n_cast + 1:]
    cast_out, (sf_ref, sb_ref, acc_ref) = rest[:n_cast], rest[n_cast:]
    nb = pl.program_id(2)
    n_blocks = pl.num_programs(2)

    @pl.when(nb == 0)
    def _():
        sf_ref[...] = s0f_ref[...]
        sb_ref[...] = s0b_ref[...]
        acc_ref[...] = jnp.zeros_like(acc_ref)

    _run_casts(cast_in, cast_out)
    o_f = _gla_direction(qkvf_ref, ldf_ref, sf_ref, reverse=False)
    o_b = _gla_direction(qkvb_ref, ldb_ref, sb_ref, reverse=True)
    row_f = pl.multiple_of(nb * GLA_TB, GLA_TB)
    row_b = pl.multiple_of((n_blocks - 1 - nb) * GLA_TB, GLA_TB)

    for row, part in ((row_f, o_f), (row_b, o_b)):
        o = acc_ref[pl.ds(row, GLA_TB), :] + part
        acc_ref[pl.ds(row, GLA_TB), :] = o
        ms = jnp.mean(o * o, axis=-1, keepdims=True)
        o = o * lax.rsqrt(ms + HEAD_NORM_EPS) * gain_ref[...]
        o_ref[pl.ds(row, GLA_TB), :] = (o * sg_ref[pl.ds(row, GLA_TB), :].astype(F32)).astype(BF16)


def _gla_call(qkv, ld, sg, s0f, s0b, gain, bsz, t, cast_jobs=()):
    nb = t // GLA_TB
    fwd = lambda n: pl.BlockSpec((GLA_TB, n), lambda b, h, i: (b * nb + i, h))
    bwd = lambda n: pl.BlockSpec((GLA_TB, n), lambda b, h, i: (b * nb + nb - 1 - i, h))
    seq = pl.BlockSpec((t, GLA_DV), lambda b, h, i: (b, h))
    st = pl.BlockSpec((None, None, GLA_DV, GLA_DK), lambda b, h, i: (b, h, 0, 0))
    cast_specs, cast_shapes = _cast_specs(cast_jobs)
    return pl.pallas_call(
        functools.partial(_gla_kernel, len(cast_jobs)),
        grid=(bsz, GLA_HEADS, nb),
        in_specs=[fwd(QKV_HEAD), fwd(LD_HEAD), bwd(QKV_HEAD), bwd(LD_HEAD),
                  seq, st, st, pl.BlockSpec((1, GLA_DV), lambda b, h, i: (0, 0))] + cast_specs,
        out_specs=[seq] + cast_specs,
        out_shape=[jax.ShapeDtypeStruct((bsz * t, D_GLA), BF16)] + cast_shapes,
        scratch_shapes=[pltpu.VMEM((GLA_DV, GLA_DK), F32), pltpu.VMEM((GLA_DV, GLA_DK), F32),
                        pltpu.VMEM((t, GLA_DV), F32)],
        compiler_params=_params("parallel", "parallel", "arbitrary"),
        name="gla",
    )(qkv, ld, qkv, ld, sg, s0f, s0b, gain, *[job.array for job in cast_jobs])


ROW_PAD = 16
ROW_PITCH = GRID_W + 2 * ROW_PAD


def _conv_kernel(n_row_blocks, n_cast, u_ref, w_ref, b_ref, *rest):
    cast_in, y_ref, rest = rest[:n_cast], rest[n_cast], rest[n_cast + 1:]
    cast_out, (pad_ref,) = rest[:n_cast], rest[n_cast:]
    cb = pl.program_id(1)
    rows = u_ref.shape[0] // GRID_W
    bias = jnp.broadcast_to(b_ref[...], (GRID_W, CONV_CB))
    _run_casts(cast_in, cast_out)

    @pl.when(cb < n_row_blocks)
    def _():
        @pl.when(cb == 0)
        def _():
            pad_ref[...] = jnp.zeros_like(pad_ref)

        def fill(r, carry):
            src = pl.multiple_of(r * GRID_W, GRID_W)
            dst = pl.multiple_of(r * ROW_PITCH + ROW_PAD, 8)
            pad_ref[pl.ds(dst, GRID_W), :] = u_ref[pl.ds(src, GRID_W), :]
            return carry

        lax.fori_loop(0, rows, fill, 0)

        def body(r, carry):
            base = r * ROW_PITCH + (ROW_PAD - CONV_HALF)
            acc = bias
            for j in range(CONV_WIDTH):
                acc = acc + w_ref[j:j + 1, :] * pad_ref[pl.ds(base + j, GRID_W), :]
            y_ref[pl.ds(pl.multiple_of(r * GRID_W, GRID_W), GRID_W), :] = acc
            return carry

        lax.fori_loop(0, rows, body, 0, unroll=CONV_UNROLL)

    @pl.when(cb >= n_row_blocks)
    def _():
        edge = CONV_HALF * GRID_W

        @pl.when(cb == n_row_blocks)
        def _():
            pad_ref[0:edge, :] = jnp.zeros((edge, CONV_CB), F32)
            pad_ref[edge + rows * GRID_W:2 * edge + rows * GRID_W, :] = jnp.zeros((edge, CONV_CB), F32)

        pad_ref[edge:edge + rows * GRID_W, :] = u_ref[...]

        def body(r, carry):
            acc = bias
            for j in range(CONV_WIDTH):
                src = pl.multiple_of((r + j) * GRID_W, GRID_W)
                acc = acc + w_ref[j:j + 1, :] * pad_ref[pl.ds(src, GRID_W), :]
            y_ref[pl.ds(pl.multiple_of(r * GRID_W, GRID_W), GRID_W), :] = acc
            return carry

        lax.fori_loop(0, rows, body, 0, unroll=CONV_UNROLL)


def _conv_call(u, w, b, bsz, t, cast_jobs=()):
    ch = u.shape[1]
    rows = t // GRID_W
    n_cb = ch // CONV_CB
    pad_rows = max(rows * ROW_PITCH, (rows + 2 * CONV_HALF) * GRID_W)
    blk = pl.BlockSpec((t, CONV_CB), lambda bi, c: (bi, c))
    cast_specs, cast_shapes = _cast_specs(cast_jobs)
    return pl.pallas_call(
        functools.partial(_conv_kernel, n_cb // 2, len(cast_jobs)),
        grid=(bsz, n_cb),
        in_specs=[blk, pl.BlockSpec((CONV_WIDTH, CONV_CB), lambda bi, c: (0, c)),
                  pl.BlockSpec((1, CONV_CB), lambda bi, c: (0, c))] + cast_specs,
        out_specs=[blk] + cast_specs,
        out_shape=[jax.ShapeDtypeStruct(u.shape, F32)] + cast_shapes,
        scratch_shapes=[pltpu.VMEM((pad_rows, CONV_CB), F32)],
        compiler_params=_params("arbitrary", "arbitrary"),
        name="conv",
    )(u, w, b, *[job.array for job in cast_jobs])


def _outproj_kernel(og_ref, y_ref, lng_ref, lnb_ref, w_ref, h_ref, mod_ref, o_ref):
    y = y_ref[...]
    mu = jnp.mean(y, axis=-1, keepdims=True)
    yc = y - mu
    var = jnp.mean(yc * yc, axis=-1, keepdims=True)
    yn = yc * lax.rsqrt(var + LN_EPS) * lng_ref[...] + lnb_ref[...]
    oc = _silu(yn).astype(BF16)
    res = jnp.dot(og_ref[...], w_ref[0:D_GLA, :], preferred_element_type=F32)
    res = res + jnp.dot(oc, w_ref[D_GLA:D_GLA + D_CONV, :], preferred_element_type=F32)
    o_ref[...] = h_ref[...] + mod_ref[5:6, :] * res


def _outproj_call(og, y, ln_g, ln_b, w_out_bf, h, mod, group_of_tile):
    rows, d = h.shape
    tm = OUT_TM
    row = lambda n: pl.BlockSpec((tm, n), lambda i: (i, 0))
    vec = lambda n: pl.BlockSpec((1, n), lambda i: (0, 0))
    return pl.pallas_call(
        _outproj_kernel,
        grid=(rows // tm,),
        in_specs=[row(D_GLA), row(D_CONV), vec(D_CONV), vec(D_CONV),
                  pl.BlockSpec(w_out_bf.shape, lambda i: (0, 0), pipeline_mode=pl.Buffered(1)),
                  row(d), pl.BlockSpec((None, N_MOD, d), lambda i: (group_of_tile(i), 0, 0))],
        out_specs=row(d),
        out_shape=jax.ShapeDtypeStruct((rows, d), F32),
        compiler_params=_params("parallel"),
        name="outproj",
    )(og, y, ln_g, ln_b, w_out_bf, h, mod)


def kernel(x, c, ctx, c_ctx, w_mod, b_mod, norm_ffn1, w_ffn1_in, w_ffn1_out, norm_mix, w_in, w_gk2, b_gk2, gla_norm, conv_w, conv_b, conv_ln_g, conv_ln_b, w_out, norm_ffn2, w_ffn2_in, w_ffn2_out, norm_final):
    bsz, t, d = x.shape
    t_ctx = ctx.shape[1]
    assert w_mod.shape[0] == 1, "single layer only"
    assert t % GLA_TB == 0 and (t // GLA_TB) % 2 == 0 and t % FFN_TM == 0
    vec = lambda a: a.reshape(1, -1)

    w2pad = jnp.zeros((V7X_LANES, 2 * GLA_KEY), BF16)
    w2pad = w2pad.at[:GATE_RANK, :GLA_KEY].set(w_gk2[0, 0].astype(BF16))
    w2pad = w2pad.at[GATE_RANK:2 * GATE_RANK, GLA_KEY:].set(w_gk2[0, 1].astype(BF16))
    b2 = b_gk2[0].reshape(1, 2 * GLA_KEY)
    w_r = _wprep_call(jnp.swapaxes(w_in[0], 0, 1))

    n_rows = 8
    s_in = jnp.concatenate([c, c_ctx[None, :], jnp.zeros((n_rows - bsz - 1, d), F32)], axis=0)
    mod = _mod_call(s_in, w_mod[0], vec(b_mod[0])).reshape(n_rows, N_MOD, d)

    tiles_per_batch = t // FFN_TM
    lat_group = lambda i: i // tiles_per_batch
    ctx_group = lambda i: bsz
    g1, gm = vec(norm_ffn1[0]), vec(norm_mix[0])

    xc = ctx.reshape(bsz * t_ctx, d)
    hxc, wg1, wu1, wo1 = _ffn_call(
        xc, mod, ctx_group, g1, gm, w_ffn1_in[0], w_ffn1_in[0], w_ffn1_out[0], D_FF,
        _FfnCfg(mod_row=0, emit_h=False, emit_hx=True, final_norm=False, cast_w=True), "ffn1_ctx")
    s0f, s0b = _ctx_call(hxc, w_r, w2pad, b2, bsz, t_ctx)

    xl = x.reshape(bsz * t, d)
    n_i, n_j = (bsz * t) // FFN_TM, D_FF // FFN_TF
    h1, hx, wf2_in = _ffn_call(
        xl, mod, lat_group, g1, gm, wg1, wu1, wo1, 0,
        _FfnCfg(mod_row=0, emit_h=True, emit_hx=True, final_norm=False, cast_w=False), "ffn1",
        (_CastJob(w_ffn2_in[0], (d // n_i, 2 * D_FF // n_j), lambda i, j: (i, j)),))

    qkv, ld, sg, u = _proj_call(hx, w_r, w2pad, b2)
    (og,) = _gla_call(qkv, ld, sg, s0f, s0b, vec(gla_norm[0]), bsz, t)
    n_conv_steps = bsz * (D_CONV // CONV_CB)
    step = lambda bi, cb: (bi * (D_CONV // CONV_CB) + cb, 0)
    y, wf2_out, w_out_bf = _conv_call(
        u, conv_w[0], vec(conv_b[0]), bsz, t,
        (_CastJob(w_ffn2_out[0], (D_FF // n_conv_steps, d), step),
         _CastJob(w_out[0], (d // n_conv_steps, d), step)))
    h2 = _outproj_call(og, y, vec(conv_ln_g[0]), vec(conv_ln_b[0]), w_out_bf, h1, mod,
                       lambda i: i // (t // OUT_TM))

    (out,) = _ffn_call(
        h2, mod, lat_group, vec(norm_ffn2[0]), vec(norm_final), wf2_in, wf2_in, wf2_out, D_FF,
        _FfnCfg(mod_row=6, emit_h=True, emit_hx=False, final_norm=True, cast_w=False), "ffn2")
    return out.reshape(bsz, t, d)
```

```python
import functools
from typing import NamedTuple

import jax
import jax.numpy as jnp
from jax import lax
from jax.experimental import pallas as pl
from jax.experimental.pallas import tpu as pltpu

F32 = jnp.float32
BF16 = jnp.bfloat16

D_MODEL = 2048
GRID_W = 64
GLA_HEADS = 4
GLA_DK = 128
GLA_DV = 256
GLA_KEY = GLA_HEADS * GLA_DK
D_GLA = GLA_HEADS * GLA_DV
D_CONV = 1024
GATE_RANK = 16
GATE_NORMALIZER = 16.0
CHUNK = 64
CONV_WIDTH = 31
CONV_HALF = CONV_WIDTH // 2
D_FF = 5632
N_MOD = 9
RMS_EPS = 1e-6
HEAD_NORM_EPS = 1e-5
LN_EPS = 1e-5

OFF_V = GLA_KEY
OFF_GKF = OFF_V + D_GLA
CTX_COLS = OFF_GKF + 2 * GATE_RANK
OFF_Q = CTX_COLS
OFF_G = OFF_Q + GLA_KEY
OFF_GLU = OFF_G + D_GLA

V7X_LANES = 128
R_K = 0
R_V = R_K + GLA_KEY
R_Q = R_V + D_GLA
R_G = R_Q + GLA_KEY
R_A = R_G + D_GLA
R_B = R_A + D_CONV
R_GK = R_B + D_CONV
R_END = R_GK + V7X_LANES

V7X_VMEM_SCOPED_LIMIT_BYTES = 60000 * 1024
V7X_BF16_SUBLANES = 16

FFN_TM = 512
FFN_TF = 512
FFN_TF_CAST = 256
PROJ_TM = 512
OUT_TM = 512
MOD_TN = 2048
GLA_TB = 2048
GLA_SUB = 256
CONV_CB = 128
CONV_UNROLL = 8
WPREP_TC = 256

NT_DIMS = (((1,), (1,)), ((), ()))
TN_DIMS = (((0,), (0,)), ((), ()))


def _params(*semantics):
    return pltpu.CompilerParams(dimension_semantics=semantics,
                                vmem_limit_bytes=V7X_VMEM_SCOPED_LIMIT_BYTES)


def _silu(x):
    return x * jax.nn.sigmoid(x)


def _rmsnorm_rows(x, gain, eps):
    ms = jnp.mean(x * x, axis=-1, keepdims=True)
    return x * lax.rsqrt(ms + eps) * gain


def _modulate(x, gain, shift, scale):
    return _rmsnorm_rows(x, gain, RMS_EPS) * (1.0 + scale) + shift


def _log_sigmoid(z):
    return jnp.minimum(z, 0.0) - jnp.log1p(jnp.exp(-jnp.abs(z)))


class _CastJob(NamedTuple):
    array: jax.Array
    block: tuple
    index_map: object


def _cast_specs(jobs):
    for job in jobs:
        assert all(n % b == 0 for n, b in zip(job.array.shape, job.block))
        assert job.block[0] % V7X_BF16_SUBLANES == 0 and job.block[1] % V7X_LANES == 0
    specs = [pl.BlockSpec(job.block, job.index_map) for job in jobs]
    shapes = [jax.ShapeDtypeStruct(job.array.shape, BF16) for job in jobs]
    return specs, shapes


def _run_casts(src_refs, dst_refs):
    for src_ref, dst_ref in zip(src_refs, dst_refs):
        dst_ref[...] = src_ref[...].astype(dst_ref.dtype)


def _mod_kernel(s_ref, w_ref, b_ref, o_ref):
    s = _silu(s_ref[...]).astype(BF16)
    o_ref[...] = jnp.dot(s, w_ref[...].astype(BF16), preferred_element_type=F32) + b_ref[...]


def _mod_call(s_in, w_mod, b_mod):
    rows, d = s_in.shape
    n = w_mod.shape[1]
    return pl.pallas_call(
        _mod_kernel,
        grid=(n // MOD_TN,),
        in_specs=[
            pl.BlockSpec((rows, d), lambda j: (0, 0)),
            pl.BlockSpec((d, MOD_TN), lambda j: (0, j)),
            pl.BlockSpec((1, MOD_TN), lambda j: (0, j)),
        ],
        out_specs=pl.BlockSpec((rows, MOD_TN), lambda j: (0, j)),
        out_shape=jax.ShapeDtypeStruct((rows, n), F32),
        compiler_params=_params("arbitrary"),
        name="mod",
    )(s_in, w_mod, b_mod)


class _FfnCfg(NamedTuple):
    mod_row: int
    emit_h: bool
    emit_hx: bool
    final_norm: bool
    cast_w: bool


def _ffn_kernel(cfg, n_cast, x_ref, mod_ref, g_in_ref, g_next_ref, wg_ref, wu_ref, wo_ref, *rest):
    cast_in, rest = rest[:n_cast], rest[n_cast:]
    outs = []
    for flag in (cfg.emit_h, cfg.emit_hx, cfg.cast_w, cfg.cast_w, cfg.cast_w):
        outs.append(rest[0] if flag else None)
        rest = rest[1:] if flag else rest
    h_ref, hx_ref, wg_bf_ref, wu_bf_ref, wo_bf_ref = outs
    cast_out, (hm_ref, acc_ref) = rest[:n_cast], rest[n_cast:]
    j = pl.program_id(1)
    last = pl.num_programs(1) - 1
    r0 = cfg.mod_row
    _run_casts(cast_in, cast_out)

    def partial_out(hm):
        wg, wu, wo = wg_ref[...], wu_ref[...], wo_ref[...]
        if cfg.cast_w:
            wg, wu, wo = wg.astype(BF16), wu.astype(BF16), wo.astype(BF16)
            wg_bf_ref[...] = wg
            wu_bf_ref[...] = wu
            wo_bf_ref[...] = wo
        gate = jnp.dot(hm, wg, preferred_element_type=F32)
        up = jnp.dot(hm, wu, preferred_element_type=F32)
        act = (_silu(gate) * up).astype(BF16)
        return jnp.dot(act, wo, preferred_element_type=F32)

    @pl.when(j == 0)
    def _():
        hm = _modulate(x_ref[...], g_in_ref[...], mod_ref[r0:r0 + 1, :], mod_ref[r0 + 1:r0 + 2, :])
        hm = hm.astype(BF16)
        hm_ref[...] = hm
        acc_ref[...] = partial_out(hm)

    @pl.when((j > 0) & (j < last))
    def _():
        acc_ref[...] += partial_out(hm_ref[...])

    @pl.when(j == last)
    def _():
        g = mod_ref[r0 + 2:r0 + 3, :]
        h = x_ref[...] + (0.5 * g) * (acc_ref[...] + partial_out(hm_ref[...]))
        if cfg.emit_hx:
            hx = _modulate(h, g_next_ref[...], mod_ref[r0 + 3:r0 + 4, :], mod_ref[r0 + 4:r0 + 5, :])
            hx_ref[...] = hx.astype(BF16)
        if cfg.emit_h:
            h_ref[...] = _rmsnorm_rows(h, g_next_ref[...], RMS_EPS) if cfg.final_norm else h


def _ffn_call(x, mod, group_of_tile, g_in, g_next, wg, wu, wo, up_col0, cfg, name, cast_jobs=()):
    rows, d = x.shape
    n_ff = wo.shape[0]
    tf = FFN_TF_CAST if cfg.cast_w else FFN_TF
    n_j = n_ff // tf
    assert n_j >= 2, "first and last hidden block must be distinct grid steps"
    assert not cfg.cast_w or rows == FFN_TM, "bf16 weight copies are written by one row tile only"
    up_blk0 = up_col0 // tf
    row_spec = pl.BlockSpec((FFN_TM, d), lambda i, j: (i, 0))
    vec_spec = pl.BlockSpec((1, d), lambda i, j: (0, 0))
    wg_spec = pl.BlockSpec((d, tf), lambda i, j: (0, j))
    wu_spec = pl.BlockSpec((d, tf), lambda i, j: (0, up_blk0 + j))
    wo_spec = pl.BlockSpec((tf, d), lambda i, j: (j, 0))
    cast_specs, cast_shapes = _cast_specs(cast_jobs)
    in_specs = [row_spec, pl.BlockSpec((None, N_MOD, d), lambda i, j: (group_of_tile(i), 0, 0)),
                vec_spec, vec_spec, wg_spec, wu_spec, wo_spec] + cast_specs
    out_specs, out_shape = [], []
    if cfg.emit_h:
        out_specs.append(row_spec)
        out_shape.append(jax.ShapeDtypeStruct((rows, d), F32))
    if cfg.emit_hx:
        out_specs.append(row_spec)
        out_shape.append(jax.ShapeDtypeStruct((rows, d), BF16))
    if cfg.cast_w:
        out_specs += [wg_spec, pl.BlockSpec((d, tf), lambda i, j: (0, j)), wo_spec]
        out_shape += [jax.ShapeDtypeStruct((d, n_ff), BF16), jax.ShapeDtypeStruct((d, n_ff), BF16),
                      jax.ShapeDtypeStruct((n_ff, d), BF16)]
    return pl.pallas_call(
        functools.partial(_ffn_kernel, cfg, len(cast_jobs)),
        grid=(rows // FFN_TM, n_j),
        in_specs=in_specs,
        out_specs=out_specs + cast_specs,
        out_shape=out_shape + cast_shapes,
        scratch_shapes=[pltpu.VMEM((FFN_TM, d), BF16), pltpu.VMEM((FFN_TM, d), F32)],
        compiler_params=_params("parallel", "arbitrary"),
        name=name,
    )(x, mod, g_in, g_next, wg, wu, wo, *[job.array for job in cast_jobs])


def _wprep_kernel(w_ref, o_ref):
    o_ref[R_K:R_Q, :] = w_ref[0:OFF_GKF, :].astype(BF16)
    o_ref[R_Q:R_GK, :] = w_ref[OFF_Q:OFF_GLU + 2 * D_CONV, :].astype(BF16)
    o_ref[R_GK:R_GK + 2 * GATE_RANK, :] = w_ref[OFF_GKF:CTX_COLS, :].astype(BF16)
    o_ref[R_GK + 2 * GATE_RANK:R_END, :] = jnp.zeros((V7X_LANES - 2 * GATE_RANK, o_ref.shape[1]), BF16)


def _wprep_call(w_t):
    n, d = w_t.shape
    tc = WPREP_TC
    return pl.pallas_call(
        _wprep_kernel,
        grid=(d // tc,),
        in_specs=[pl.BlockSpec((n, tc), lambda i: (0, i))],
        out_specs=pl.BlockSpec((R_END, tc), lambda i: (0, i)),
        out_shape=jax.ShapeDtypeStruct((R_END, d), BF16),
        compiler_params=_params("parallel"),
        name="wprep",
    )(w_t)


def _log_decays(p_gk, w2_ref, b2_ref):
    z = jnp.dot(p_gk.astype(BF16), w2_ref[...], preferred_element_type=F32) + b2_ref[...]
    return _log_sigmoid(z) * (1.0 / GATE_NORMALIZER)


def _chunk_cumsum(x, reverse):
    n = x.shape[0]
    pos = lax.broadcasted_iota(jnp.int32, x.shape, 0) % CHUNK
    d = 1
    while d < CHUNK:
        if reverse:
            shifted = pltpu.roll(x, n - d, 0)
            x = x + jnp.where(pos < CHUNK - d, shifted, 0.0)
        else:
            shifted = pltpu.roll(x, d, 0)
            x = x + jnp.where(pos >= d, shifted, 0.0)
        d *= 2
    return x


def _chunk_rows(x, row_in_chunk):
    return jnp.concatenate([x[c0 + row_in_chunk:c0 + row_in_chunk + 1, :]
                            for c0 in range(0, x.shape[0], CHUNK)], axis=0)


def _chunk_bcast(rows_per_chunk):
    return jnp.concatenate([jnp.broadcast_to(rows_per_chunk[c:c + 1, :], (CHUNK, rows_per_chunk.shape[1]))
                            for c in range(rows_per_chunk.shape[0])], axis=0)


QKV_HEAD = 2 * GLA_DK + GLA_DV
LD_HEAD = 2 * GLA_DK


def _proj_kernel(hx_ref, w_ref, w2_ref, b2_ref, qkv_ref, ld_ref, sg_ref, u_ref):
    hx = hx_ref[...]

    def proj(lo, hi):
        return lax.dot_general(hx, w_ref[lo:hi, :], NT_DIMS, preferred_element_type=F32)

    ld = _log_decays(proj(R_GK, R_END), w2_ref, b2_ref)
    k = proj(R_K, R_V).astype(BF16)
    v = proj(R_V, R_Q).astype(BF16)
    q = (proj(R_Q, R_G) * (GLA_DK ** -0.5)).astype(BF16)
    for h in range(GLA_HEADS):
        dk = slice(h * GLA_DK, (h + 1) * GLA_DK)
        c0 = h * QKV_HEAD
        qkv_ref[:, c0:c0 + GLA_DK] = q[:, dk]
        qkv_ref[:, c0 + GLA_DK:c0 + 2 * GLA_DK] = k[:, dk]
        qkv_ref[:, c0 + 2 * GLA_DK:c0 + QKV_HEAD] = v[:, h * GLA_DV:(h + 1) * GLA_DV]
        ld_ref[:, h * LD_HEAD:h * LD_HEAD + GLA_DK] = ld[:, dk]
        ld_ref[:, h * LD_HEAD + GLA_DK:(h + 1) * LD_HEAD] = ld[:, GLA_KEY + h * GLA_DK:GLA_KEY + (h + 1) * GLA_DK]
    sg_ref[...] = _silu(proj(R_G, R_A)).astype(BF16)
    u_ref[...] = proj(R_A, R_B) * jax.nn.sigmoid(proj(R_B, R_GK))


def _proj_call(hx, w_r, w2pad, b2):
    rows, d = hx.shape
    tm = PROJ_TM
    row = lambda n: pl.BlockSpec((tm, n), lambda i: (i, 0))
    whole = lambda a: pl.BlockSpec(a.shape, lambda i: (0, 0), pipeline_mode=pl.Buffered(1))
    outs = [(GLA_HEADS * QKV_HEAD, BF16), (GLA_HEADS * LD_HEAD, F32), (D_GLA, BF16), (D_CONV, F32)]
    return pl.pallas_call(
        _proj_kernel,
        grid=(rows // tm,),
        in_specs=[row(d), whole(w_r), whole(w2pad), whole(b2)],
        out_specs=[row(n) for n, _ in outs],
        out_shape=[jax.ShapeDtypeStruct((rows, n), dt) for n, dt in outs],
        compiler_params=_params("parallel"),
        name="proj",
    )(hx, w_r, w2pad, b2)


def _ctx_kernel(hx_ref, wkv_ref, wgk_ref, w2_ref, b2_ref, sf_ref, sb_ref):
    hx = hx_ref[...]
    t = hx.shape[0]
    kv = lax.dot_general(hx, wkv_ref[...], NT_DIMS, preferred_element_type=F32)
    ld = _log_decays(lax.dot_general(hx, wgk_ref[...], NT_DIMS, preferred_element_type=F32),
                     w2_ref, b2_ref)
    r = lax.broadcasted_iota(jnp.int32, (t, t), 0)
    c = lax.broadcasted_iota(jnp.int32, (t, t), 1)
    hi = lax.Precision.HIGHEST
    e_f = jnp.dot((c > r).astype(F32), ld[:, :GLA_KEY], preferred_element_type=F32, precision=hi)
    e_b = jnp.dot((c < r).astype(F32), ld[:, GLA_KEY:], preferred_element_type=F32, precision=hi)
    for h in range(GLA_HEADS):
        ks = slice(h * GLA_DK, (h + 1) * GLA_DK)
        k = kv[:, R_K + h * GLA_DK:R_K + (h + 1) * GLA_DK]
        v = kv[:, R_V + h * GLA_DV:R_V + (h + 1) * GLA_DV].astype(BF16)
        sf_ref[h] = lax.dot_general(v, (k * jnp.exp(e_f[:, ks])).astype(BF16), TN_DIMS,
                                    preferred_element_type=F32)
        sb_ref[h] = lax.dot_general(v, (k * jnp.exp(e_b[:, ks])).astype(BF16), TN_DIMS,
                                    preferred_element_type=F32)


def _ctx_call(hxc, w_r, w2pad, b2, bsz, t):
    d = hxc.shape[1]
    const = lambda a: pl.BlockSpec(a.shape, lambda b: (0, 0))
    st = pl.BlockSpec((None, GLA_HEADS, GLA_DV, GLA_DK), lambda b: (b, 0, 0, 0))
    shape = jax.ShapeDtypeStruct((bsz, GLA_HEADS, GLA_DV, GLA_DK), F32)
    return pl.pallas_call(
        _ctx_kernel,
        grid=(bsz,),
        in_specs=[pl.BlockSpec((t, d), lambda b: (b, 0)),
                  pl.BlockSpec((R_Q, d), lambda b: (0, 0)),
                  pl.BlockSpec((V7X_LANES, d), lambda b: (R_GK // V7X_LANES, 0)),
                  const(w2pad), const(b2)],
        out_specs=[st, st],
        out_shape=[shape, shape],
        compiler_params=_params("parallel"),
        name="ctx",
    )(hxc, w_r, w_r, w2pad, b2)


def _gla_direction(qkv_ref, ld_ref, s_ref, reverse):
    n_sub = GLA_TB // GLA_SUB
    n_chunk = GLA_SUB // CHUNK
    rr = lax.broadcasted_iota(jnp.int32, (GLA_SUB, GLA_SUB), 0)
    cc = lax.broadcasted_iota(jnp.int32, (GLA_SUB, GLA_SUB), 1)
    same_chunk = (rr // CHUNK) == (cc // CHUNK)
    mask = same_chunk & ((cc >= rr) if reverse else (cc <= rr))
    mid_row = CHUNK // 2 if reverse else CHUNK // 2 - 1
    last_row = 0 if reverse else CHUNK - 1
    ld0 = GLA_DK if reverse else 0

    outs = [None] * n_sub
    subs = range(n_sub - 1, -1, -1) if reverse else range(n_sub)
    state = s_ref[...]
    for s in subs:
        rows = slice(s * GLA_SUB, (s + 1) * GLA_SUB)
        b = _chunk_cumsum(ld_ref[rows, ld0:ld0 + GLA_DK], reverse)
        b_mid = _chunk_bcast(_chunk_rows(b, mid_row))
        b_last_rows = _chunk_rows(b, last_row)
        b_last = _chunk_bcast(b_last_rows)
        q = qkv_ref[rows, 0:GLA_DK].astype(F32)
        k = qkv_ref[rows, GLA_DK:2 * GLA_DK].astype(F32)
        v = qkv_ref[rows, 2 * GLA_DK:QKV_HEAD]
        qs = (q * jnp.exp(b - b_mid)).astype(BF16)
        ks = (k * jnp.exp(b_mid - b)).astype(BF16)
        qi = (q * jnp.exp(b)).astype(BF16)
        kd = (k * jnp.exp(b_last - b)).astype(BF16)
        att = lax.dot_general(qs, ks, NT_DIMS, preferred_element_type=F32)
        att = jnp.where(mask, att, 0.0).astype(BF16)
        o_intra = jnp.dot(att, v, preferred_element_type=F32)
        decay = jnp.exp(b_last_rows)
        o_parts = [None] * n_chunk
        chunks = range(n_chunk - 1, -1, -1) if reverse else range(n_chunk)
        for c in chunks:
            cr = slice(c * CHUNK, (c + 1) * CHUNK)
            o_parts[c] = o_intra[cr, :] + lax.dot_general(
                qi[cr, :], state.astype(BF16), NT_DIMS, preferred_element_type=F32)
            kv = lax.dot_general(v[cr, :], kd[cr, :], TN_DIMS, preferred_element_type=F32)
            state = state * decay[c:c + 1, :] + kv
        outs[s] = jnp.concatenate(o_parts, axis=0)
    s_ref[...] = state
    return jnp.concatenate(outs, axis=0)


def _gla_kernel(n_cast, qkvf_ref, ldf_ref, qkvb_ref, ldb_ref, sg_ref, s0f_ref, s0b_ref, gain_ref,
                *rest):
    cast_in, o_ref, rest = rest[:n_cast], rest[n_cast], rest[n_cast + 1:]
    cast_out, (sf_ref, sb_ref, acc_ref) = rest[:n_cast], rest[n_cast:]
    nb = pl.program_id(2)
    n_blocks = pl.num_programs(2)

    @pl.when(nb == 0)
    def _():
        sf_ref[...] = s0f_ref[...]
        sb_ref[...] = s0b_ref[...]
        acc_ref[...] = jnp.zeros_like(acc_ref)

    _run_casts(cast_in, cast_out)
    o_f = _gla_direction(qkvf_ref, ldf_ref, sf_ref, reverse=False)
    o_b = _gla_direction(qkvb_ref, ldb_ref, sb_ref, reverse=True)
    row_f = pl.multiple_of(nb * GLA_TB, GLA_TB)
    row_b = pl.multiple_of((n_blocks - 1 - nb) * GLA_TB, GLA_TB)

    for row, part in ((row_f, o_f), (row_b, o_b)):
        o = acc_ref[pl.ds(row, GLA_TB), :] + part
        acc_ref[pl.ds(row, GLA_TB), :] = o
        ms = jnp.mean(o * o, axis=-1, keepdims=True)
        o = o * lax.rsqrt(ms + HEAD_NORM_EPS) * gain_ref[...]
        o_ref[pl.ds(row, GLA_TB), :] = (o * sg_ref[pl.ds(row, GLA_TB), :].astype(F32)).astype(BF16)


def _gla_call(qkv, ld, sg, s0f, s0b, gain, bsz, t, cast_jobs=()):
    nb = t // GLA_TB
    fwd = lambda n: pl.BlockSpec((GLA_TB, n), lambda b, h, i: (b * nb + i, h))
    bwd = lambda n: pl.BlockSpec((GLA_TB, n), lambda b, h, i: (b * nb + nb - 1 - i, h))
    seq = pl.BlockSpec((t, GLA_DV), lambda b, h, i: (b, h))
    st = pl.BlockSpec((None, None, GLA_DV, GLA_DK), lambda b, h, i: (b, h, 0, 0))
    cast_specs, cast_shapes = _cast_specs(cast_jobs)
    return pl.pallas_call(
        functools.partial(_gla_kernel, len(cast_jobs)),
        grid=(bsz, GLA_HEADS, nb),
        in_specs=[fwd(QKV_HEAD), fwd(LD_HEAD), bwd(QKV_HEAD), bwd(LD_HEAD),
                  seq, st, st, pl.BlockSpec((1, GLA_DV), lambda b, h, i: (0, 0))] + cast_specs,
        out_specs=[seq] + cast_specs,
        out_shape=[jax.ShapeDtypeStruct((bsz * t, D_GLA), BF16)] + cast_shapes,
        scratch_shapes=[pltpu.VMEM((GLA_DV, GLA_DK), F32), pltpu.VMEM((GLA_DV, GLA_DK), F32),
                        pltpu.VMEM((t, GLA_DV), F32)],
        compiler_params=_params("parallel", "parallel", "arbitrary"),
        name="gla",
    )(qkv, ld, qkv, ld, sg, s0f, s0b, gain, *[job.array for job in cast_jobs])


ROW_PAD = 16
ROW_PITCH = GRID_W + 2 * ROW_PAD


def _conv_kernel(n_row_blocks, n_cast, u_ref, w_ref, b_ref, *rest):
    cast_in, y_ref, rest = rest[:n_cast], rest[n_cast], rest[n_cast + 1:]
    cast_out, (pad_ref,) = rest[:n_cast], rest[n_cast:]
    cb = pl.program_id(1)
    rows = u_ref.shape[0] // GRID_W
    bias = jnp.broadcast_to(b_ref[...], (GRID_W, CONV_CB))
    _run_casts(cast_in, cast_out)

    @pl.when(cb < n_row_blocks)
    def _():
        @pl.when(cb == 0)
        def _():
            pad_ref[...] = jnp.zeros_like(pad_ref)

        def fill(r, carry):
            src = pl.multiple_of(r * GRID_W, GRID_W)
            dst = pl.multiple_of(r * ROW_PITCH + ROW_PAD, 8)
            pad_ref[pl.ds(dst, GRID_W), :] = u_ref[pl.ds(src, GRID_W), :]
            return carry

        lax.fori_loop(0, rows, fill, 0)

        def body(r, carry):
            base = r * ROW_PITCH + (ROW_PAD - CONV_HALF)
            acc = bias
            for j in range(CONV_WIDTH):
                acc = acc + w_ref[j:j + 1, :] * pad_ref[pl.ds(base + j, GRID_W), :]
            y_ref[pl.ds(pl.multiple_of(r * GRID_W, GRID_W), GRID_W), :] = acc
            return carry

        lax.fori_loop(0, rows, body, 0, unroll=CONV_UNROLL)

    @pl.when(cb >= n_row_blocks)
    def _():
        edge = CONV_HALF * GRID_W

        @pl.when(cb == n_row_blocks)
        def _():
            pad_ref[0:edge, :] = jnp.zeros((edge, CONV_CB), F32)
            pad_ref[edge + rows * GRID_W:2 * edge + rows * GRID_W, :] = jnp.zeros((edge, CONV_CB), F32)

        pad_ref[edge:edge + rows * GRID_W, :] = u_ref[...]

        def body(r, carry):
            acc = bias
            for j in range(CONV_WIDTH):
                src = pl.multiple_of((r + j) * GRID_W, GRID_W)
                acc = acc + w_ref[j:j + 1, :] * pad_ref[pl.ds(src, GRID_W), :]
            y_ref[pl.ds(pl.multiple_of(r * GRID_W, GRID_W), GRID_W), :] = acc
            return carry

        lax.fori_loop(0, rows, body, 0, unroll=CONV_UNROLL)


def _conv_call(u, w, b, bsz, t, cast_jobs=()):
    ch = u.shape[1]
    rows = t // GRID_W
    n_cb = ch // CONV_CB
    pad_rows = max(rows * ROW_PITCH, (rows + 2 * CONV_HALF) * GRID_W)
    blk = pl.BlockSpec((t, CONV_CB), lambda bi, c: (bi, c))
    cast_specs, cast_shapes = _cast_specs(cast_jobs)
    return pl.pallas_call(
        functools.partial(_conv_kernel, n_cb // 2, len(cast_jobs)),
        grid=(bsz, n_cb),
        in_specs=[blk, pl.BlockSpec((CONV_WIDTH, CONV_CB), lambda bi, c: (0, c)),
                  pl.BlockSpec((1, CONV_CB), lambda bi, c: (0, c))] + cast_specs,
        out_specs=[blk] + cast_specs,
        out_shape=[jax.ShapeDtypeStruct(u.shape, F32)] + cast_shapes,
        scratch_shapes=[pltpu.VMEM((pad_rows, CONV_CB), F32)],
        compiler_params=_params("arbitrary", "arbitrary"),
        name="conv",
    )(u, w, b, *[job.array for job in cast_jobs])


def _outproj_kernel(og_ref, y_ref, lng_ref, lnb_ref, w_ref, h_ref, mod_ref, o_ref):
    y = y_ref[...]
    mu = jnp.mean(y, axis=-1, keepdims=True)
    yc = y - mu
    var = jnp.mean(yc * yc, axis=-1, keepdims=True)
    yn = yc * lax.rsqrt(var + LN_EPS) * lng_ref[...] + lnb_ref[...]
    oc = _silu(yn).astype(BF16)
    res = jnp.dot(og_ref[...], w_ref[0:D_GLA, :], preferred_element_type=F32)
    res = res + jnp.dot(oc, w_ref[D_GLA:D_GLA + D_CONV, :], preferred_element_type=F32)
    o_ref[...] = h_ref[...] + mod_ref[5:6, :] * res


def _outproj_call(og, y, ln_g, ln_b, w_out_bf, h, mod, group_of_tile):
    rows, d = h.shape
    tm = OUT_TM
    row = lambda n: pl.BlockSpec((tm, n), lambda i: (i, 0))
    vec = lambda n: pl.BlockSpec((1, n), lambda i: (0, 0))
    return pl.pallas_call(
        _outproj_kernel,
        grid=(rows // tm,),
        in_specs=[row(D_GLA), row(D_CONV), vec(D_CONV), vec(D_CONV),
                  pl.BlockSpec(w_out_bf.shape, lambda i: (0, 0), pipeline_mode=pl.Buffered(1)),
                  row(d), pl.BlockSpec((None, N_MOD, d), lambda i: (group_of_tile(i), 0, 0))],
        out_specs=row(d),
        out_shape=jax.ShapeDtypeStruct((rows, d), F32),
        compiler_params=_params("parallel"),
        name="outproj",
    )(og, y, ln_g, ln_b, w_out_bf, h, mod)


def kernel(x, c, ctx, c_ctx, w_mod, b_mod, norm_ffn1, w_ffn1_in, w_ffn1_out, norm_mix, w_in, w_gk2, b_gk2, gla_norm, conv_w, conv_b, conv_ln_g, conv_ln_b, w_out, norm_ffn2, w_ffn2_in, w_ffn2_out, norm_final):
    bsz, t, d = x.shape
    t_ctx = ctx.shape[1]
    assert w_mod.shape[0] == 1, "single layer only"
    assert t % GLA_TB == 0 and (t // GLA_TB) % 2 == 0 and t % FFN_TM == 0
    vec = lambda a: a.reshape(1, -1)

    w2pad = jnp.zeros((V7X_LANES, 2 * GLA_KEY), BF16)
    w2pad = w2pad.at[:GATE_RANK, :GLA_KEY].set(w_gk2[0, 0].astype(BF16))
    w2pad = w2pad.at[GATE_RANK:2 * GATE_RANK, GLA_KEY:].set(w_gk2[0, 1].astype(BF16))
    b2 = b_gk2[0].reshape(1, 2 * GLA_KEY)
    w_r = _wprep_call(jnp.swapaxes(w_in[0], 0, 1))

    n_rows = 8
    s_in = jnp.concatenate([c, c_ctx[None, :], jnp.zeros((n_rows - bsz - 1, d), F32)], axis=0)
    mod = _mod_call(s_in, w_mod[0], vec(b_mod[0])).reshape(n_rows, N_MOD, d)

    tiles_per_batch = t // FFN_TM
    lat_group = lambda i: i // tiles_per_batch
    ctx_group = lambda i: bsz
    g1, gm = vec(norm_ffn1[0]), vec(norm_mix[0])

    xc = ctx.reshape(bsz * t_ctx, d)
    hxc, wg1, wu1, wo1 = _ffn_call(
        xc, mod, ctx_group, g1, gm, w_ffn1_in[0], w_ffn1_in[0], w_ffn1_out[0], D_FF,
        _FfnCfg(mod_row=0, emit_h=False, emit_hx=True, final_norm=False, cast_w=True), "ffn1_ctx")
    s0f, s0b = _ctx_call(hxc, w_r, w2pad, b2, bsz, t_ctx)

    xl = x.reshape(bsz * t, d)
    n_i, n_j = (bsz * t) // FFN_TM, D_FF // FFN_TF
    h1, hx, wf2_in = _ffn_call(
        xl, mod, lat_group, g1, gm, wg1, wu1, wo1, 0,
        _FfnCfg(mod_row=0, emit_h=True, emit_hx=True, final_norm=False, cast_w=False), "ffn1",
        (_CastJob(w_ffn2_in[0], (d // n_i, 2 * D_FF // n_j), lambda i, j: (i, j)),))

    qkv, ld, sg, u = _proj_call(hx, w_r, w2pad, b2)
    (og,) = _gla_call(qkv, ld, sg, s0f, s0b, vec(gla_norm[0]), bsz, t)
    n_conv_steps = bsz * (D_CONV // CONV_CB)
    step = lambda bi, cb: (bi * (D_CONV // CONV_CB) + cb, 0)
    y, wf2_out, w_out_bf = _conv_call(
        u, conv_w[0], vec(conv_b[0]), bsz, t,
        (_CastJob(w_ffn2_out[0], (D_FF // n_conv_steps, d), step),
         _CastJob(w_out[0], (d // n_conv_steps, d), step)))
    h2 = _outproj_call(og, y, vec(conv_ln_g[0]), vec(conv_ln_b[0]), w_out_bf, h1, mod,
                       lambda i: i // (t // OUT_TM))

    (out,) = _ffn_call(
        h2, mod, lat_group, vec(norm_ffn2[0]), vec(norm_final), wf2_in, wf2_in, wf2_out, D_FF,
        _FfnCfg(mod_row=6, emit_h=True, emit_hx=False, final_norm=True, cast_w=False), "ffn2")
    return out.reshape(bsz, t, d)
```

```python
import functools
from typing import NamedTuple

import jax
import jax.numpy as jnp
from jax import lax
from jax.experimental import pallas as pl
from jax.experimental.pallas import tpu as pltpu

F32 = jnp.float32
BF16 = jnp.bfloat16

D_MODEL = 2048
GRID_W = 64
GLA_HEADS = 4
GLA_DK = 128
GLA_DV = 256
GLA_KEY = GLA_HEADS * GLA_DK
D_GLA = GLA_HEADS * GLA_DV
D_CONV = 1024
GATE_RANK = 16
GATE_NORMALIZER = 16.0
CHUNK = 64
CONV_WIDTH = 31
CONV_HALF = CONV_WIDTH // 2
D_FF = 5632
N_MOD = 9
RMS_EPS = 1e-6
HEAD_NORM_EPS = 1e-5
LN_EPS = 1e-5

OFF_V = GLA_KEY
OFF_GKF = OFF_V + D_GLA
CTX_COLS = OFF_GKF + 2 * GATE_RANK
OFF_Q = CTX_COLS
OFF_G = OFF_Q + GLA_KEY
OFF_GLU = OFF_G + D_GLA

V7X_LANES = 128
R_K = 0
R_V = R_K + GLA_KEY
R_Q = R_V + D_GLA
R_G = R_Q + GLA_KEY
R_A = R_G + D_GLA
R_B = R_A + D_CONV
R_GK = R_B + D_CONV
R_END = R_GK + V7X_LANES

V7X_VMEM_SCOPED_LIMIT_BYTES = 60000 * 1024
V7X_BF16_SUBLANES = 16

FFN_TM = 512
FFN_TF = 512
FFN_TF_CAST = 256
PROJ_TM = 512
OUT_TM = 512
MOD_TN = 2048
GLA_TB = 2048
GLA_SUB = 256
CONV_CB = 128
CONV_UNROLL = 8
WPREP_TC = 256

NT_DIMS = (((1,), (1,)), ((), ()))
TN_DIMS = (((0,), (0,)), ((), ()))


def _params(*semantics):
    return pltpu.CompilerParams(dimension_semantics=semantics,
                                vmem_limit_bytes=V7X_VMEM_SCOPED_LIMIT_BYTES)


def _silu(x):
    return x * jax.nn.sigmoid(x)


def _rmsnorm_rows(x, gain, eps):
    ms = jnp.mean(x * x, axis=-1, keepdims=True)
    return x * lax.rsqrt(ms + eps) * gain


def _modulate(x, gain, shift, scale):
    return _rmsnorm_rows(x, gain, RMS_EPS) * (1.0 + scale) + shift


def _log_sigmoid(z):
    return jnp.minimum(z, 0.0) - jnp.log1p(jnp.exp(-jnp.abs(z)))


class _CastJob(NamedTuple):
    array: jax.Array
    block: tuple
    index_map: object
    out_shape: tuple = None
    out_index_map: object = None


def _cast_in_specs(jobs):
    for job in jobs:
        assert all(n % b == 0 for n, b in zip(job.array.shape, job.block))
        assert job.block[0] % V7X_BF16_SUBLANES == 0 and job.block[1] % V7X_LANES == 0
    return [pl.BlockSpec(job.block, job.index_map) for job in jobs]


def _cast_out_specs(jobs):
    specs = [pl.BlockSpec(job.block, job.out_index_map or job.index_map) for job in jobs]
    shapes = [jax.ShapeDtypeStruct(job.out_shape or job.array.shape, BF16) for job in jobs]
    return specs, shapes


def _cast_specs(jobs):
    return (_cast_in_specs(jobs),) + _cast_out_specs(jobs)


def _run_casts(src_refs, dst_refs):
    for src_ref, dst_ref in zip(src_refs, dst_refs):
        dst_ref[...] = src_ref[...].astype(dst_ref.dtype)


def _mod_kernel(s_ref, w_ref, b_ref, o_ref):
    s = _silu(s_ref[...]).astype(BF16)
    o_ref[...] = jnp.dot(s, w_ref[...].astype(BF16), preferred_element_type=F32) + b_ref[...]


def _mod_call(s_in, w_mod, b_mod):
    rows, d = s_in.shape
    n = w_mod.shape[1]
    return pl.pallas_call(
        _mod_kernel,
        grid=(n // MOD_TN,),
        in_specs=[
            pl.BlockSpec((rows, d), lambda j: (0, 0)),
            pl.BlockSpec((d, MOD_TN), lambda j: (0, j)),
            pl.BlockSpec((1, MOD_TN), lambda j: (0, j)),
        ],
        out_specs=pl.BlockSpec((rows, MOD_TN), lambda j: (0, j)),
        out_shape=jax.ShapeDtypeStruct((rows, n), F32),
        compiler_params=_params("arbitrary"),
        name="mod",
    )(s_in, w_mod, b_mod)


class _FfnCfg(NamedTuple):
    mod_row: int
    emit_h: bool
    emit_hx: bool
    final_norm: bool
    cast_w: bool


def _ffn_kernel(cfg, n_cast, x_ref, mod_ref, g_in_ref, g_next_ref, wg_ref, wu_ref, wo_ref, *rest):
    cast_in, rest = rest[:n_cast], rest[n_cast:]
    outs = []
    for flag in (cfg.emit_h, cfg.emit_hx, cfg.cast_w, cfg.cast_w, cfg.cast_w):
        outs.append(rest[0] if flag else None)
        rest = rest[1:] if flag else rest
    h_ref, hx_ref, wg_bf_ref, wu_bf_ref, wo_bf_ref = outs
    cast_out, (hm_ref, acc_ref) = rest[:n_cast], rest[n_cast:]
    j = pl.program_id(1)
    last = pl.num_programs(1) - 1
    r0 = cfg.mod_row
    _run_casts(cast_in, cast_out)

    def partial_out(hm):
        wg, wu, wo = wg_ref[...], wu_ref[...], wo_ref[...]
        if cfg.cast_w:
            wg, wu, wo = wg.astype(BF16), wu.astype(BF16), wo.astype(BF16)
            wg_bf_ref[...] = wg
            wu_bf_ref[...] = wu
            wo_bf_ref[...] = wo
        gate = jnp.dot(hm, wg, preferred_element_type=F32)
        up = jnp.dot(hm, wu, preferred_element_type=F32)
        act = (_silu(gate) * up).astype(BF16)
        return jnp.dot(act, wo, preferred_element_type=F32)

    @pl.when(j == 0)
    def _():
        hm = _modulate(x_ref[...], g_in_ref[...], mod_ref[r0:r0 + 1, :], mod_ref[r0 + 1:r0 + 2, :])
        hm = hm.astype(BF16)
        hm_ref[...] = hm
        acc_ref[...] = partial_out(hm)

    @pl.when((j > 0) & (j < last))
    def _():
        acc_ref[...] += partial_out(hm_ref[...])

    @pl.when(j == last)
    def _():
        g = mod_ref[r0 + 2:r0 + 3, :]
        h = x_ref[...] + (0.5 * g) * (acc_ref[...] + partial_out(hm_ref[...]))
        if cfg.emit_hx:
            hx = _modulate(h, g_next_ref[...], mod_ref[r0 + 3:r0 + 4, :], mod_ref[r0 + 4:r0 + 5, :])
            hx_ref[...] = hx.astype(BF16)
        if cfg.emit_h:
            h_ref[...] = _rmsnorm_rows(h, g_next_ref[...], RMS_EPS) if cfg.final_norm else h


def _ffn_call(x, mod, group_of_tile, g_in, g_next, wg, wu, wo, up_col0, cfg, name, cast_jobs=()):
    rows, d = x.shape
    n_ff = wo.shape[0]
    tf = FFN_TF_CAST if cfg.cast_w else FFN_TF
    n_j = n_ff // tf
    assert n_j >= 2, "first and last hidden block must be distinct grid steps"
    assert not cfg.cast_w or rows == FFN_TM, "bf16 weight copies are written by one row tile only"
    up_blk0 = up_col0 // tf
    row_spec = pl.BlockSpec((FFN_TM, d), lambda i, j: (i, 0))
    vec_spec = pl.BlockSpec((1, d), lambda i, j: (0, 0))
    wg_spec = pl.BlockSpec((d, tf), lambda i, j: (0, j))
    wu_spec = pl.BlockSpec((d, tf), lambda i, j: (0, up_blk0 + j))
    wo_spec = pl.BlockSpec((tf, d), lambda i, j: (j, 0))
    cast_in_specs, cast_specs, cast_shapes = _cast_specs(cast_jobs)
    in_specs = [row_spec, pl.BlockSpec((None, N_MOD, d), lambda i, j: (group_of_tile(i), 0, 0)),
                vec_spec, vec_spec, wg_spec, wu_spec, wo_spec] + cast_in_specs
    out_specs, out_shape = [], []
    if cfg.emit_h:
        out_specs.append(row_spec)
        out_shape.append(jax.ShapeDtypeStruct((rows, d), F32))
    if cfg.emit_hx:
        out_specs.append(row_spec)
        out_shape.append(jax.ShapeDtypeStruct((rows, d), BF16))
    if cfg.cast_w:
        out_specs += [wg_spec, pl.BlockSpec((d, tf), lambda i, j: (0, j)), wo_spec]
        out_shape += [jax.ShapeDtypeStruct((d, n_ff), BF16), jax.ShapeDtypeStruct((d, n_ff), BF16),
                      jax.ShapeDtypeStruct((n_ff, d), BF16)]
    return pl.pallas_call(
        functools.partial(_ffn_kernel, cfg, len(cast_jobs)),
        grid=(rows // FFN_TM, n_j),
        in_specs=in_specs,
        out_specs=out_specs + cast_specs,
        out_shape=out_shape + cast_shapes,
        scratch_shapes=[pltpu.VMEM((FFN_TM, d), BF16), pltpu.VMEM((FFN_TM, d), F32)],
        compiler_params=_params("parallel", "arbitrary"),
        name=name,
    )(x, mod, g_in, g_next, wg, wu, wo, *[job.array for job in cast_jobs])


def _wprep_kernel(w_ref, o_ref):
    o_ref[R_K:R_Q, :] = w_ref[0:OFF_GKF, :].astype(BF16)
    o_ref[R_Q:R_GK, :] = w_ref[OFF_Q:OFF_GLU + 2 * D_CONV, :].astype(BF16)
    o_ref[R_GK:R_GK + 2 * GATE_RANK, :] = w_ref[OFF_GKF:CTX_COLS, :].astype(BF16)
    o_ref[R_GK + 2 * GATE_RANK:R_END, :] = jnp.zeros((V7X_LANES - 2 * GATE_RANK, o_ref.shape[1]), BF16)


def _wprep_call(w_t):
    n, d = w_t.shape
    tc = WPREP_TC
    return pl.pallas_call(
        _wprep_kernel,
        grid=(d // tc,),
        in_specs=[pl.BlockSpec((n, tc), lambda i: (0, i))],
        out_specs=pl.BlockSpec((R_END, tc), lambda i: (0, i)),
        out_shape=jax.ShapeDtypeStruct((R_END, d), BF16),
        compiler_params=_params("parallel"),
        name="wprep",
    )(w_t)


def _log_decays(p_gk, w2_ref, b2_ref):
    z = jnp.dot(p_gk.astype(BF16), w2_ref[...], preferred_element_type=F32) + b2_ref[...]
    return _log_sigmoid(z) * (1.0 / GATE_NORMALIZER)


def _chunk_cumsum(x, reverse):
    n = x.shape[0]
    pos = lax.broadcasted_iota(jnp.int32, x.shape, 0) % CHUNK
    d = 1
    while d < CHUNK:
        if reverse:
            shifted = pltpu.roll(x, n - d, 0)
            x = x + jnp.where(pos < CHUNK - d, shifted, 0.0)
        else:
            shifted = pltpu.roll(x, d, 0)
            x = x + jnp.where(pos >= d, shifted, 0.0)
        d *= 2
    return x


def _chunk_rows(x, row_in_chunk):
    return jnp.concatenate([x[c0 + row_in_chunk:c0 + row_in_chunk + 1, :]
                            for c0 in range(0, x.shape[0], CHUNK)], axis=0)


def _chunk_bcast(rows_per_chunk):
    return jnp.concatenate([jnp.broadcast_to(rows_per_chunk[c:c + 1, :], (CHUNK, rows_per_chunk.shape[1]))
                            for c in range(rows_per_chunk.shape[0])], axis=0)


QKV_HEAD = 2 * GLA_DK + GLA_DV
LD_HEAD = 2 * GLA_DK


def _proj_kernel(n_cast, hx_ref, w_ref, w2_ref, b2_ref, *rest):
    cast_in, (qkv_ref, ld_ref, sg_ref, u_ref), cast_out = rest[:n_cast], rest[n_cast:n_cast + 4], rest[n_cast + 4:]
    _run_casts(cast_in, cast_out)
    hx = hx_ref[...]

    def proj(lo, hi):
        return lax.dot_general(hx, w_ref[lo:hi, :], NT_DIMS, preferred_element_type=F32)

    ld = _log_decays(proj(R_GK, R_END), w2_ref, b2_ref)
    k = proj(R_K, R_V).astype(BF16)
    v = proj(R_V, R_Q).astype(BF16)
    q = (proj(R_Q, R_G) * (GLA_DK ** -0.5)).astype(BF16)
    for h in range(GLA_HEADS):
        dk = slice(h * GLA_DK, (h + 1) * GLA_DK)
        c0 = h * QKV_HEAD
        qkv_ref[:, c0:c0 + GLA_DK] = q[:, dk]
        qkv_ref[:, c0 + GLA_DK:c0 + 2 * GLA_DK] = k[:, dk]
        qkv_ref[:, c0 + 2 * GLA_DK:c0 + QKV_HEAD] = v[:, h * GLA_DV:(h + 1) * GLA_DV]
        ld_ref[:, h * LD_HEAD:h * LD_HEAD + GLA_DK] = ld[:, dk]
        ld_ref[:, h * LD_HEAD + GLA_DK:(h + 1) * LD_HEAD] = ld[:, GLA_KEY + h * GLA_DK:GLA_KEY + (h + 1) * GLA_DK]
    sg_ref[...] = _silu(proj(R_G, R_A)).astype(BF16)
    u_ref[...] = proj(R_A, R_B) * jax.nn.sigmoid(proj(R_B, R_GK))


def _proj_call(hx, w_r, w2pad, b2, cast_jobs=()):
    rows, d = hx.shape
    tm = PROJ_TM
    row = lambda n: pl.BlockSpec((tm, n), lambda i: (i, 0))
    whole = lambda a: pl.BlockSpec(a.shape, lambda i: (0, 0), pipeline_mode=pl.Buffered(1))
    outs = [(GLA_HEADS * QKV_HEAD, BF16), (GLA_HEADS * LD_HEAD, F32), (D_GLA, BF16), (D_CONV, F32)]
    cast_in_specs, cast_specs, cast_shapes = _cast_specs(cast_jobs)
    return pl.pallas_call(
        functools.partial(_proj_kernel, len(cast_jobs)),
        grid=(rows // tm,),
        in_specs=[row(d), whole(w_r), whole(w2pad), whole(b2)] + cast_in_specs,
        out_specs=[row(n) for n, _ in outs] + cast_specs,
        out_shape=[jax.ShapeDtypeStruct((rows, n), dt) for n, dt in outs] + cast_shapes,
        compiler_params=_params("parallel"),
        name="proj",
    )(hx, w_r, w2pad, b2, *[job.array for job in cast_jobs])


def _ctx_kernel(hx_ref, wkv_ref, wgk_ref, w2_ref, b2_ref, sf_ref, sb_ref):
    hx = hx_ref[...]
    t = hx.shape[0]
    kv = lax.dot_general(hx, wkv_ref[...], NT_DIMS, preferred_element_type=F32)
    ld = _log_decays(lax.dot_general(hx, wgk_ref[...], NT_DIMS, preferred_element_type=F32),
                     w2_ref, b2_ref)
    r = lax.broadcasted_iota(jnp.int32, (t, t), 0)
    c = lax.broadcasted_iota(jnp.int32, (t, t), 1)
    hi = lax.Precision.HIGHEST
    e_f = jnp.dot((c > r).astype(F32), ld[:, :GLA_KEY], preferred_element_type=F32, precision=hi)
    e_b = jnp.dot((c < r).astype(F32), ld[:, GLA_KEY:], preferred_element_type=F32, precision=hi)
    for h in range(GLA_HEADS):
        ks = slice(h * GLA_DK, (h + 1) * GLA_DK)
        k = kv[:, R_K + h * GLA_DK:R_K + (h + 1) * GLA_DK]
        v = kv[:, R_V + h * GLA_DV:R_V + (h + 1) * GLA_DV].astype(BF16)
        sf_ref[h] = lax.dot_general(v, (k * jnp.exp(e_f[:, ks])).astype(BF16), TN_DIMS,
                                    preferred_element_type=F32)
        sb_ref[h] = lax.dot_general(v, (k * jnp.exp(e_b[:, ks])).astype(BF16), TN_DIMS,
                                    preferred_element_type=F32)


def _ctx_call(hxc, w_r, w2pad, b2, bsz, t):
    d = hxc.shape[1]
    const = lambda a: pl.BlockSpec(a.shape, lambda b: (0, 0))
    st = pl.BlockSpec((None, GLA_HEADS, GLA_DV, GLA_DK), lambda b: (b, 0, 0, 0))
    shape = jax.ShapeDtypeStruct((bsz, GLA_HEADS, GLA_DV, GLA_DK), F32)
    return pl.pallas_call(
        _ctx_kernel,
        grid=(bsz,),
        in_specs=[pl.BlockSpec((t, d), lambda b: (b, 0)),
                  pl.BlockSpec((R_Q, d), lambda b: (0, 0)),
                  pl.BlockSpec((V7X_LANES, d), lambda b: (R_GK // V7X_LANES, 0)),
                  const(w2pad), const(b2)],
        out_specs=[st, st],
        out_shape=[shape, shape],
        compiler_params=_params("parallel"),
        name="ctx",
    )(hxc, w_r, w_r, w2pad, b2)


def _gla_direction(qkv_ref, ld_ref, s_ref, reverse):
    n_sub = GLA_TB // GLA_SUB
    n_chunk = GLA_SUB // CHUNK
    rr = lax.broadcasted_iota(jnp.int32, (GLA_SUB, GLA_SUB), 0)
    cc = lax.broadcasted_iota(jnp.int32, (GLA_SUB, GLA_SUB), 1)
    same_chunk = (rr // CHUNK) == (cc // CHUNK)
    mask = same_chunk & ((cc >= rr) if reverse else (cc <= rr))
    mid_row = CHUNK // 2 if reverse else CHUNK // 2 - 1
    last_row = 0 if reverse else CHUNK - 1
    ld0 = GLA_DK if reverse else 0

    outs = [None] * n_sub
    subs = range(n_sub - 1, -1, -1) if reverse else range(n_sub)
    state = s_ref[...]
    for s in subs:
        rows = slice(s * GLA_SUB, (s + 1) * GLA_SUB)
        b = _chunk_cumsum(ld_ref[rows, ld0:ld0 + GLA_DK], reverse)
        b_mid = _chunk_bcast(_chunk_rows(b, mid_row))
        b_last_rows = _chunk_rows(b, last_row)
        b_last = _chunk_bcast(b_last_rows)
        q = qkv_ref[rows, 0:GLA_DK].astype(F32)
        k = qkv_ref[rows, GLA_DK:2 * GLA_DK].astype(F32)
        v = qkv_ref[rows, 2 * GLA_DK:QKV_HEAD]
        qs = (q * jnp.exp(b - b_mid)).astype(BF16)
        ks = (k * jnp.exp(b_mid - b)).astype(BF16)
        qi = (q * jnp.exp(b)).astype(BF16)
        kd = (k * jnp.exp(b_last - b)).astype(BF16)
        att = lax.dot_general(qs, ks, NT_DIMS, preferred_element_type=F32)
        att = jnp.where(mask, att, 0.0).astype(BF16)
        o_intra = jnp.dot(att, v, preferred_element_type=F32)
        decay = jnp.exp(b_last_rows)
        o_parts = [None] * n_chunk
        chunks = range(n_chunk - 1, -1, -1) if reverse else range(n_chunk)
        for c in chunks:
            cr = slice(c * CHUNK, (c + 1) * CHUNK)
            o_parts[c] = o_intra[cr, :] + lax.dot_general(
                qi[cr, :], state.astype(BF16), NT_DIMS, preferred_element_type=F32)
            kv = lax.dot_general(v[cr, :], kd[cr, :], TN_DIMS, preferred_element_type=F32)
            state = state * decay[c:c + 1, :] + kv
        outs[s] = jnp.concatenate(o_parts, axis=0)
    s_ref[...] = state
    return jnp.concatenate(outs, axis=0)


def _gla_kernel(n_cast, qkvf_ref, ldf_ref, qkvb_ref, ldb_ref, sg_ref, s0f_ref, s0b_ref, gain_ref,
                *rest):
    cast_in, o_ref, rest = rest[:n_cast], rest[n_cast], rest[n_cast + 1:]
    cast_out, (sf_ref, sb_ref, acc_ref) = rest[:n_cast], rest[n_cast:]
    nb = pl.program_id(2)
    n_blocks = pl.num_programs(2)

    @pl.when(nb == 0)
    def _():
        sf_ref[...] = s0f_ref[...]
        sb_ref[...] = s0b_ref[...]

    _run_casts(cast_in, cast_out)
    o_f = _gla_direction(qkvf_ref, ldf_ref, sf_ref, reverse=False)
    o_b = _gla_direction(qkvb_ref, ldb_ref, sb_ref, reverse=True)
    row_f = pl.multiple_of(nb * GLA_TB, GLA_TB)
    row_b = pl.multiple_of((n_blocks - 1 - nb) * GLA_TB, GLA_TB)

    @pl.when(nb < n_blocks // 2)
    def _():
        acc_ref[pl.ds(row_f, GLA_TB), :] = o_f
        acc_ref[pl.ds(row_b, GLA_TB), :] = o_b

    @pl.when(nb >= n_blocks // 2)
    def _():
        for row, part in ((row_f, o_f), (row_b, o_b)):
            o = acc_ref[pl.ds(row, GLA_TB), :] + part
            ms = jnp.mean(o * o, axis=-1, keepdims=True)
            o = o * lax.rsqrt(ms + HEAD_NORM_EPS) * gain_ref[...]
            o_ref[pl.ds(row, GLA_TB), :] = (o * sg_ref[pl.ds(row, GLA_TB), :].astype(F32)).astype(BF16)


def _gla_call(qkv, ld, sg, s0f, s0b, gain, bsz, t, cast_jobs=()):
    nb = t // GLA_TB
    fwd = lambda n: pl.BlockSpec((GLA_TB, n), lambda b, h, i: (b * nb + i, h))
    bwd = lambda n: pl.BlockSpec((GLA_TB, n), lambda b, h, i: (b * nb + nb - 1 - i, h))
    seq = pl.BlockSpec((t, GLA_DV), lambda b, h, i: (b, h))
    st = pl.BlockSpec((None, None, GLA_DV, GLA_DK), lambda b, h, i: (b, h, 0, 0))
    cast_in_specs, cast_specs, cast_shapes = _cast_specs(cast_jobs)
    return pl.pallas_call(
        functools.partial(_gla_kernel, len(cast_jobs)),
        grid=(bsz, GLA_HEADS, nb),
        in_specs=[fwd(QKV_HEAD), fwd(LD_HEAD), bwd(QKV_HEAD), bwd(LD_HEAD),
                  seq, st, st, pl.BlockSpec((1, GLA_DV), lambda b, h, i: (0, 0))] + cast_in_specs,
        out_specs=[seq] + cast_specs,
        out_shape=[jax.ShapeDtypeStruct((bsz * t, D_GLA), BF16)] + cast_shapes,
        scratch_shapes=[pltpu.VMEM((GLA_DV, GLA_DK), F32), pltpu.VMEM((GLA_DV, GLA_DK), F32),
                        pltpu.VMEM((t, GLA_DV), F32)],
        compiler_params=_params("parallel", "parallel", "arbitrary"),
        name="gla",
    )(qkv, ld, qkv, ld, sg, s0f, s0b, gain, *[job.array for job in cast_jobs])


ROW_PAD = 16
ROW_PITCH = GRID_W + 2 * ROW_PAD


def _conv_kernel(n_row_blocks, n_cast, u_ref, w_ref, b_ref, *rest):
    cast_in, y_ref, rest = rest[:n_cast], rest[n_cast], rest[n_cast + 1:]
    cast_out, (pad_ref,) = rest[:n_cast], rest[n_cast:]
    cb = pl.program_id(1)
    rows = u_ref.shape[0] // GRID_W
    bias = jnp.broadcast_to(b_ref[...], (GRID_W, CONV_CB))
    _run_casts(cast_in, cast_out)

    @pl.when(cb < n_row_blocks)
    def _():
        @pl.when(cb == 0)
        def _():
            pad_ref[...] = jnp.zeros_like(pad_ref)

        def fill(r, carry):
            src = pl.multiple_of(r * GRID_W, GRID_W)
            dst = pl.multiple_of(r * ROW_PITCH + ROW_PAD, 8)
            pad_ref[pl.ds(dst, GRID_W), :] = u_ref[pl.ds(src, GRID_W), :]
            return carry

        lax.fori_loop(0, rows, fill, 0)

        def body(r, carry):
            base = r * ROW_PITCH + (ROW_PAD - CONV_HALF)
            acc = bias
            for j in range(CONV_WIDTH):
                acc = acc + w_ref[j:j + 1, :] * pad_ref[pl.ds(base + j, GRID_W), :]
            y_ref[pl.ds(pl.multiple_of(r * GRID_W, GRID_W), GRID_W), :] = acc
            return carry

        lax.fori_loop(0, rows, body, 0, unroll=CONV_UNROLL)

    @pl.when(cb >= n_row_blocks)
    def _():
        edge = CONV_HALF * GRID_W

        @pl.when(cb == n_row_blocks)
        def _():
            pad_ref[0:edge, :] = jnp.zeros((edge, CONV_CB), F32)
            pad_ref[edge + rows * GRID_W:2 * edge + rows * GRID_W, :] = jnp.zeros((edge, CONV_CB), F32)

        pad_ref[edge:edge + rows * GRID_W, :] = u_ref[...]

        def body(r, carry):
            acc = bias
            for j in range(CONV_WIDTH):
                src = pl.multiple_of((r + j) * GRID_W, GRID_W)
                acc = acc + w_ref[j:j + 1, :] * pad_ref[pl.ds(src, GRID_W), :]
            y_ref[pl.ds(pl.multiple_of(r * GRID_W, GRID_W), GRID_W), :] = acc
            return carry

        lax.fori_loop(0, rows, body, 0, unroll=CONV_UNROLL)


def _conv_call(u, w, b, bsz, t, cast_jobs=()):
    ch = u.shape[1]
    rows = t // GRID_W
    n_cb = ch // CONV_CB
    pad_rows = max(rows * ROW_PITCH, (rows + 2 * CONV_HALF) * GRID_W)
    blk = pl.BlockSpec((t, CONV_CB), lambda bi, c: (bi, c))
    cast_in_specs, cast_specs, cast_shapes = _cast_specs(cast_jobs)
    return pl.pallas_call(
        functools.partial(_conv_kernel, n_cb // 2, len(cast_jobs)),
        grid=(bsz, n_cb),
        in_specs=[blk, pl.BlockSpec((CONV_WIDTH, CONV_CB), lambda bi, c: (0, c)),
                  pl.BlockSpec((1, CONV_CB), lambda bi, c: (0, c))] + cast_in_specs,
        out_specs=[blk] + cast_specs,
        out_shape=[jax.ShapeDtypeStruct(u.shape, F32)] + cast_shapes,
        scratch_shapes=[pltpu.VMEM((pad_rows, CONV_CB), F32)],
        compiler_params=_params("arbitrary", "arbitrary"),
        name="conv",
    )(u, w, b, *[job.array for job in cast_jobs])


def _outproj_kernel(n_cast, og_ref, y_ref, lng_ref, lnb_ref, w_ref, h_ref, mod_ref, *rest):
    cast_in, o_ref, cast_out = rest[:n_cast], rest[n_cast], rest[n_cast + 1:]
    _run_casts(cast_in, cast_out)
    y = y_ref[...]
    mu = jnp.mean(y, axis=-1, keepdims=True)
    yc = y - mu
    var = jnp.mean(yc * yc, axis=-1, keepdims=True)
    yn = yc * lax.rsqrt(var + LN_EPS) * lng_ref[...] + lnb_ref[...]
    oc = _silu(yn).astype(BF16)
    res = jnp.dot(og_ref[...], w_ref[0:D_GLA, :], preferred_element_type=F32)
    res = res + jnp.dot(oc, w_ref[D_GLA:D_GLA + D_CONV, :], preferred_element_type=F32)
    o_ref[...] = h_ref[...] + mod_ref[5:6, :] * res


def _outproj_call(og, y, ln_g, ln_b, w_out_bf, h, mod, group_of_tile, cast_jobs=()):
    rows, d = h.shape
    tm = OUT_TM
    row = lambda n: pl.BlockSpec((tm, n), lambda i: (i, 0))
    vec = lambda n: pl.BlockSpec((1, n), lambda i: (0, 0))
    cast_in_specs, cast_specs, cast_shapes = _cast_specs(cast_jobs)
    return pl.pallas_call(
        functools.partial(_outproj_kernel, len(cast_jobs)),
        grid=(rows // tm,),
        in_specs=[row(D_GLA), row(D_CONV), vec(D_CONV), vec(D_CONV),
                  pl.BlockSpec(w_out_bf.shape, lambda i: (0, 0), pipeline_mode=pl.Buffered(1)),
                  row(d), pl.BlockSpec((None, N_MOD, d), lambda i: (group_of_tile(i), 0, 0))] + cast_in_specs,
        out_specs=[row(d)] + cast_specs,
        out_shape=[jax.ShapeDtypeStruct((rows, d), F32)] + cast_shapes,
        compiler_params=_params("parallel"),
        name="outproj",
    )(og, y, ln_g, ln_b, w_out_bf, h, mod, *[job.array for job in cast_jobs])


def kernel(x, c, ctx, c_ctx, w_mod, b_mod, norm_ffn1, w_ffn1_in, w_ffn1_out, norm_mix, w_in, w_gk2, b_gk2, gla_norm, conv_w, conv_b, conv_ln_g, conv_ln_b, w_out, norm_ffn2, w_ffn2_in, w_ffn2_out, norm_final):
    bsz, t, d = x.shape
    t_ctx = ctx.shape[1]
    assert w_mod.shape[0] == 1, "single layer only"
    assert t % GLA_TB == 0 and (t // GLA_TB) % 2 == 0 and t % FFN_TM == 0
    vec = lambda a: a.reshape(1, -1)

    w2pad = jnp.zeros((V7X_LANES, 2 * GLA_KEY), BF16)
    w2pad = w2pad.at[:GATE_RANK, :GLA_KEY].set(w_gk2[0, 0].astype(BF16))
    w2pad = w2pad.at[GATE_RANK:2 * GATE_RANK, GLA_KEY:].set(w_gk2[0, 1].astype(BF16))
    b2 = b_gk2[0].reshape(1, 2 * GLA_KEY)
    w_r = _wprep_call(jnp.swapaxes(w_in[0], 0, 1))

    n_rows = 8
    s_in = jnp.concatenate([c, c_ctx[None, :], jnp.zeros((n_rows - bsz - 1, d), F32)], axis=0)
    mod = _mod_call(s_in, w_mod[0], vec(b_mod[0])).reshape(n_rows, N_MOD, d)

    tiles_per_batch = t // FFN_TM
    lat_group = lambda i: i // tiles_per_batch
    ctx_group = lambda i: bsz
    g1, gm = vec(norm_ffn1[0]), vec(norm_mix[0])

    xc = ctx.reshape(bsz * t_ctx, d)
    hxc, wg1, wu1, wo1 = _ffn_call(
        xc, mod, ctx_group, g1, gm, w_ffn1_in[0], w_ffn1_in[0], w_ffn1_out[0], D_FF,
        _FfnCfg(mod_row=0, emit_h=False, emit_hx=True, final_norm=False, cast_w=True), "ffn1_ctx")
    s0f, s0b = _ctx_call(hxc, w_r, w2pad, b2, bsz, t_ctx)

    xl = x.reshape(bsz * t, d)
    h1, hx = _ffn_call(
        xl, mod, lat_group, g1, gm, wg1, wu1, wo1, 0,
        _FfnCfg(mod_row=0, emit_h=True, emit_hx=True, final_norm=False, cast_w=False), "ffn1")

    n_proj, n_outp = (bsz * t) // PROJ_TM, (bsz * t) // OUT_TM
    half_job = lambda n, col: _CastJob(w_ffn2_in[0], (d // n, D_FF), lambda i: (i, col),
                                       (d, D_FF), lambda i: (i, 0))
    qkv, ld, sg, u, wg2 = _proj_call(hx, w_r, w2pad, b2, (half_job(n_proj, 0),))
    (og,) = _gla_call(qkv, ld, sg, s0f, s0b, vec(gla_norm[0]), bsz, t)
    n_conv_steps = bsz * (D_CONV // CONV_CB)
    step = lambda bi, cb: (bi * (D_CONV // CONV_CB) + cb, 0)
    y, wf2_out, w_out_bf = _conv_call(
        u, conv_w[0], vec(conv_b[0]), bsz, t,
        (_CastJob(w_ffn2_out[0], (D_FF // n_conv_steps, d), step),
         _CastJob(w_out[0], (d // n_conv_steps, d), step)))
    h2, wu2 = _outproj_call(
        og, y, vec(conv_ln_g[0]), vec(conv_ln_b[0]), w_out_bf, h1, mod, lambda i: i // (t // OUT_TM),
        (half_job(n_outp, 1),))

    (out,) = _ffn_call(
        h2, mod, lat_group, vec(norm_ffn2[0]), vec(norm_final), wg2, wu2, wf2_out, 0,
        _FfnCfg(mod_row=6, emit_h=True, emit_hx=False, final_norm=True, cast_w=False), "ffn2")
    return out.reshape(bsz, t, d)
```

```python
import functools
from typing import NamedTuple

import jax
import jax.numpy as jnp
from jax import lax
from jax.experimental import pallas as pl
from jax.experimental.pallas import tpu as pltpu

F32 = jnp.float32
BF16 = jnp.bfloat16

D_MODEL = 2048
GRID_W = 64
GLA_HEADS = 4
GLA_DK = 128
GLA_DV = 256
GLA_KEY = GLA_HEADS * GLA_DK
D_GLA = GLA_HEADS * GLA_DV
D_CONV = 1024
GATE_RANK = 16
GATE_NORMALIZER = 16.0
CHUNK = 64
CONV_WIDTH = 31
CONV_HALF = CONV_WIDTH // 2
D_FF = 5632
N_MOD = 9
RMS_EPS = 1e-6
HEAD_NORM_EPS = 1e-5
LN_EPS = 1e-5

OFF_V = GLA_KEY
OFF_GKF = OFF_V + D_GLA
CTX_COLS = OFF_GKF + 2 * GATE_RANK
OFF_Q = CTX_COLS
OFF_G = OFF_Q + GLA_KEY
OFF_GLU = OFF_G + D_GLA

V7X_LANES = 128
R_K = 0
R_V = R_K + GLA_KEY
R_Q = R_V + D_GLA
R_G = R_Q + GLA_KEY
R_A = R_G + D_GLA
R_B = R_A + D_CONV
R_GK = R_B + D_CONV
R_END = R_GK + V7X_LANES

V7X_VMEM_SCOPED_LIMIT_BYTES = 60000 * 1024
V7X_BF16_SUBLANES = 16

FFN_TM = 512
FFN2_TM = 1024
FFN_TF = 512
FFN_TF_CAST = 256
PROJ_TM = 512
OUT_TM = 512
MOD_TN = 2048
GLA_TB = 2048
GLA_SUB = 256
CONV_CB = 128
CONV_UNROLL = 8
WPREP_TC = 256

NT_DIMS = (((1,), (1,)), ((), ()))
TN_DIMS = (((0,), (0,)), ((), ()))


def _params(*semantics):
    return pltpu.CompilerParams(dimension_semantics=semantics,
                                vmem_limit_bytes=V7X_VMEM_SCOPED_LIMIT_BYTES)


def _silu(x):
    return x * jax.nn.sigmoid(x)


def _rmsnorm_rows(x, gain, eps):
    ms = jnp.mean(x * x, axis=-1, keepdims=True)
    return x * lax.rsqrt(ms + eps) * gain


def _modulate(x, gain, shift, scale):
    return _rmsnorm_rows(x, gain, RMS_EPS) * (1.0 + scale) + shift


def _log_sigmoid(z):
    return jnp.minimum(z, 0.0) - jnp.log1p(jnp.exp(-jnp.abs(z)))


class _CastJob(NamedTuple):
    array: jax.Array
    block: tuple
    index_map: object
    out_shape: tuple = None
    out_index_map: object = None


def _cast_in_specs(jobs):
    for job in jobs:
        assert all(n % b == 0 for n, b in zip(job.array.shape, job.block))
        assert job.block[0] % V7X_BF16_SUBLANES == 0 and job.block[1] % V7X_LANES == 0
    return [pl.BlockSpec(job.block, job.index_map) for job in jobs]


def _cast_out_specs(jobs):
    specs = [pl.BlockSpec(job.block, job.out_index_map or job.index_map) for job in jobs]
    shapes = [jax.ShapeDtypeStruct(job.out_shape or job.array.shape, BF16) for job in jobs]
    return specs, shapes


def _cast_specs(jobs):
    return (_cast_in_specs(jobs),) + _cast_out_specs(jobs)


def _run_casts(src_refs, dst_refs):
    for src_ref, dst_ref in zip(src_refs, dst_refs):
        dst_ref[...] = src_ref[...].astype(dst_ref.dtype)


def _mod_kernel(s_ref, w_ref, b_ref, o_ref):
    s = _silu(s_ref[...]).astype(BF16)
    o_ref[...] = jnp.dot(s, w_ref[...].astype(BF16), preferred_element_type=F32) + b_ref[...]


def _mod_call(s_in, w_mod, b_mod):
    rows, d = s_in.shape
    n = w_mod.shape[1]
    return pl.pallas_call(
        _mod_kernel,
        grid=(n // MOD_TN,),
        in_specs=[
            pl.BlockSpec((rows, d), lambda j: (0, 0)),
            pl.BlockSpec((d, MOD_TN), lambda j: (0, j)),
            pl.BlockSpec((1, MOD_TN), lambda j: (0, j)),
        ],
        out_specs=pl.BlockSpec((rows, MOD_TN), lambda j: (0, j)),
        out_shape=jax.ShapeDtypeStruct((rows, n), F32),
        compiler_params=_params("arbitrary"),
        name="mod",
    )(s_in, w_mod, b_mod)


class _FfnCfg(NamedTuple):
    mod_row: int
    emit_h: bool
    emit_hx: bool
    final_norm: bool
    cast_w: bool
    tm: int = FFN_TM
    acc_in_h: bool = False


def _ffn_kernel(cfg, n_cast, x_ref, mod_ref, g_in_ref, g_next_ref, wg_ref, wu_ref, wo_ref, *rest):
    cast_in, rest = rest[:n_cast], rest[n_cast:]
    outs = []
    for flag in (cfg.emit_h, cfg.emit_hx, cfg.cast_w, cfg.cast_w, cfg.cast_w):
        outs.append(rest[0] if flag else None)
        rest = rest[1:] if flag else rest
    h_ref, hx_ref, wg_bf_ref, wu_bf_ref, wo_bf_ref = outs
    cast_out, scratch = rest[:n_cast], rest[n_cast:]
    hm_ref = scratch[0]
    acc_ref = h_ref if cfg.acc_in_h else scratch[1]
    j = pl.program_id(1)
    last = pl.num_programs(1) - 1
    r0 = cfg.mod_row
    _run_casts(cast_in, cast_out)

    def partial_out(hm):
        wg, wu, wo = wg_ref[...], wu_ref[...], wo_ref[...]
        if cfg.cast_w:
            wg, wu, wo = wg.astype(BF16), wu.astype(BF16), wo.astype(BF16)
            wg_bf_ref[...] = wg
            wu_bf_ref[...] = wu
            wo_bf_ref[...] = wo
        gate = jnp.dot(hm, wg, preferred_element_type=F32)
        up = jnp.dot(hm, wu, preferred_element_type=F32)
        act = (_silu(gate) * up).astype(BF16)
        return jnp.dot(act, wo, preferred_element_type=F32)

    @pl.when(j == 0)
    def _():
        hm = _modulate(x_ref[...], g_in_ref[...], mod_ref[r0:r0 + 1, :], mod_ref[r0 + 1:r0 + 2, :])
        hm = hm.astype(BF16)
        hm_ref[...] = hm
        acc_ref[...] = partial_out(hm)

    @pl.when((j > 0) & (j < last))
    def _():
        acc_ref[...] += partial_out(hm_ref[...])

    @pl.when(j == last)
    def _():
        g = mod_ref[r0 + 2:r0 + 3, :]
        h = x_ref[...] + (0.5 * g) * (acc_ref[...] + partial_out(hm_ref[...]))
        if cfg.emit_hx:
            hx = _modulate(h, g_next_ref[...], mod_ref[r0 + 3:r0 + 4, :], mod_ref[r0 + 4:r0 + 5, :])
            hx_ref[...] = hx.astype(BF16)
        if cfg.emit_h:
            h_ref[...] = _rmsnorm_rows(h, g_next_ref[...], RMS_EPS) if cfg.final_norm else h


def _ffn_call(x, mod, group_of_tile, g_in, g_next, wg, wu, wo, up_col0, cfg, name, cast_jobs=()):
    rows, d = x.shape
    n_ff = wo.shape[0]
    tf = FFN_TF_CAST if cfg.cast_w else FFN_TF
    n_j = n_ff // tf
    assert n_j >= 2, "first and last hidden block must be distinct grid steps"
    tm = cfg.tm
    assert not cfg.cast_w or rows == tm, "bf16 weight copies are written by one row tile only"
    assert cfg.emit_h or not cfg.acc_in_h
    up_blk0 = up_col0 // tf
    row_spec = pl.BlockSpec((tm, d), lambda i, j: (i, 0))
    vec_spec = pl.BlockSpec((1, d), lambda i, j: (0, 0))
    wg_spec = pl.BlockSpec((d, tf), lambda i, j: (0, j))
    wu_spec = pl.BlockSpec((d, tf), lambda i, j: (0, up_blk0 + j))
    wo_spec = pl.BlockSpec((tf, d), lambda i, j: (j, 0))
    cast_in_specs, cast_specs, cast_shapes = _cast_specs(cast_jobs)
    in_specs = [row_spec, pl.BlockSpec((None, N_MOD, d), lambda i, j: (group_of_tile(i), 0, 0)),
                vec_spec, vec_spec, wg_spec, wu_spec, wo_spec] + cast_in_specs
    out_specs, out_shape = [], []
    if cfg.emit_h:
        out_specs.append(row_spec)
        out_shape.append(jax.ShapeDtypeStruct((rows, d), F32))
    if cfg.emit_hx:
        out_specs.append(row_spec)
        out_shape.append(jax.ShapeDtypeStruct((rows, d), BF16))
    if cfg.cast_w:
        out_specs += [wg_spec, pl.BlockSpec((d, tf), lambda i, j: (0, j)), wo_spec]
        out_shape += [jax.ShapeDtypeStruct((d, n_ff), BF16), jax.ShapeDtypeStruct((d, n_ff), BF16),
                      jax.ShapeDtypeStruct((n_ff, d), BF16)]
    return pl.pallas_call(
        functools.partial(_ffn_kernel, cfg, len(cast_jobs)),
        grid=(rows // tm, n_j),
        in_specs=in_specs,
        out_specs=out_specs + cast_specs,
        out_shape=out_shape + cast_shapes,
        scratch_shapes=[pltpu.VMEM((tm, d), BF16)] + ([] if cfg.acc_in_h else [pltpu.VMEM((tm, d), F32)]),
        compiler_params=_params("parallel", "arbitrary"),
        name=name,
    )(x, mod, g_in, g_next, wg, wu, wo, *[job.array for job in cast_jobs])


def _wprep_kernel(w_ref, o_ref):
    o_ref[R_K:R_Q, :] = w_ref[0:OFF_GKF, :].astype(BF16)
    o_ref[R_Q:R_GK, :] = w_ref[OFF_Q:OFF_GLU + 2 * D_CONV, :].astype(BF16)
    o_ref[R_GK:R_GK + 2 * GATE_RANK, :] = w_ref[OFF_GKF:CTX_COLS, :].astype(BF16)
    o_ref[R_GK + 2 * GATE_RANK:R_END, :] = jnp.zeros((V7X_LANES - 2 * GATE_RANK, o_ref.shape[1]), BF16)


def _wprep_call(w_t):
    n, d = w_t.shape
    tc = WPREP_TC
    return pl.pallas_call(
        _wprep_kernel,
        grid=(d // tc,),
        in_specs=[pl.BlockSpec((n, tc), lambda i: (0, i))],
        out_specs=pl.BlockSpec((R_END, tc), lambda i: (0, i)),
        out_shape=jax.ShapeDtypeStruct((R_END, d), BF16),
        compiler_params=_params("parallel"),
        name="wprep",
    )(w_t)


def _log_decays(p_gk, w2_ref, b2_ref):
    z = jnp.dot(p_gk.astype(BF16), w2_ref[...], preferred_element_type=F32) + b2_ref[...]
    return _log_sigmoid(z) * (1.0 / GATE_NORMALIZER)


def _chunk_cumsum(x, reverse):
    n = x.shape[0]
    pos = lax.broadcasted_iota(jnp.int32, x.shape, 0) % CHUNK
    d = 1
    while d < CHUNK:
        if reverse:
            shifted = pltpu.roll(x, n - d, 0)
            x = x + jnp.where(pos < CHUNK - d, shifted, 0.0)
        else:
            shifted = pltpu.roll(x, d, 0)
            x = x + jnp.where(pos >= d, shifted, 0.0)
        d *= 2
    return x


def _chunk_rows(x, row_in_chunk):
    return jnp.concatenate([x[c0 + row_in_chunk:c0 + row_in_chunk + 1, :]
                            for c0 in range(0, x.shape[0], CHUNK)], axis=0)


def _chunk_bcast(rows_per_chunk):
    return jnp.concatenate([jnp.broadcast_to(rows_per_chunk[c:c + 1, :], (CHUNK, rows_per_chunk.shape[1]))
                            for c in range(rows_per_chunk.shape[0])], axis=0)


QKV_HEAD = 2 * GLA_DK + GLA_DV
LD_HEAD = 2 * GLA_DK


def _proj_kernel(n_cast, hx_ref, w_ref, w2_ref, b2_ref, *rest):
    cast_in, (qkv_ref, ld_ref, sg_ref, u_ref), cast_out = rest[:n_cast], rest[n_cast:n_cast + 4], rest[n_cast + 4:]
    _run_casts(cast_in, cast_out)
    hx = hx_ref[...]

    def proj(lo, hi):
        return lax.dot_general(hx, w_ref[lo:hi, :], NT_DIMS, preferred_element_type=F32)

    ld = _log_decays(proj(R_GK, R_END), w2_ref, b2_ref)
    k = proj(R_K, R_V).astype(BF16)
    v = proj(R_V, R_Q).astype(BF16)
    q = (proj(R_Q, R_G) * (GLA_DK ** -0.5)).astype(BF16)
    for h in range(GLA_HEADS):
        dk = slice(h * GLA_DK, (h + 1) * GLA_DK)
        c0 = h * QKV_HEAD
        qkv_ref[:, c0:c0 + GLA_DK] = q[:, dk]
        qkv_ref[:, c0 + GLA_DK:c0 + 2 * GLA_DK] = k[:, dk]
        qkv_ref[:, c0 + 2 * GLA_DK:c0 + QKV_HEAD] = v[:, h * GLA_DV:(h + 1) * GLA_DV]
        ld_ref[:, h * LD_HEAD:h * LD_HEAD + GLA_DK] = ld[:, dk]
        ld_ref[:, h * LD_HEAD + GLA_DK:(h + 1) * LD_HEAD] = ld[:, GLA_KEY + h * GLA_DK:GLA_KEY + (h + 1) * GLA_DK]
    sg_ref[...] = _silu(proj(R_G, R_A)).astype(BF16)
    u_ref[...] = proj(R_A, R_B) * jax.nn.sigmoid(proj(R_B, R_GK))


def _proj_call(hx, w_r, w2pad, b2, cast_jobs=()):
    rows, d = hx.shape
    tm = PROJ_TM
    row = lambda n: pl.BlockSpec((tm, n), lambda i: (i, 0))
    whole = lambda a: pl.BlockSpec(a.shape, lambda i: (0, 0), pipeline_mode=pl.Buffered(1))
    outs = [(GLA_HEADS * QKV_HEAD, BF16), (GLA_HEADS * LD_HEAD, F32), (D_GLA, BF16), (D_CONV, F32)]
    cast_in_specs, cast_specs, cast_shapes = _cast_specs(cast_jobs)
    return pl.pallas_call(
        functools.partial(_proj_kernel, len(cast_jobs)),
        grid=(rows // tm,),
        in_specs=[row(d), whole(w_r), whole(w2pad), whole(b2)] + cast_in_specs,
        out_specs=[row(n) for n, _ in outs] + cast_specs,
        out_shape=[jax.ShapeDtypeStruct((rows, n), dt) for n, dt in outs] + cast_shapes,
        compiler_params=_params("parallel"),
        name="proj",
    )(hx, w_r, w2pad, b2, *[job.array for job in cast_jobs])


def _ctx_kernel(hx_ref, wkv_ref, wgk_ref, w2_ref, b2_ref, sf_ref, sb_ref):
    hx = hx_ref[...]
    t = hx.shape[0]
    kv = lax.dot_general(hx, wkv_ref[...], NT_DIMS, preferred_element_type=F32)
    ld = _log_decays(lax.dot_general(hx, wgk_ref[...], NT_DIMS, preferred_element_type=F32),
                     w2_ref, b2_ref)
    r = lax.broadcasted_iota(jnp.int32, (t, t), 0)
    c = lax.broadcasted_iota(jnp.int32, (t, t), 1)
    hi = lax.Precision.HIGHEST
    e_f = jnp.dot((c > r).astype(F32), ld[:, :GLA_KEY], preferred_element_type=F32, precision=hi)
    e_b = jnp.dot((c < r).astype(F32), ld[:, GLA_KEY:], preferred_element_type=F32, precision=hi)
    for h in range(GLA_HEADS):
        ks = slice(h * GLA_DK, (h + 1) * GLA_DK)
        k = kv[:, R_K + h * GLA_DK:R_K + (h + 1) * GLA_DK]
        v = kv[:, R_V + h * GLA_DV:R_V + (h + 1) * GLA_DV].astype(BF16)
        sf_ref[h] = lax.dot_general(v, (k * jnp.exp(e_f[:, ks])).astype(BF16), TN_DIMS,
                                    preferred_element_type=F32)
        sb_ref[h] = lax.dot_general(v, (k * jnp.exp(e_b[:, ks])).astype(BF16), TN_DIMS,
                                    preferred_element_type=F32)


def _ctx_call(hxc, w_r, w2pad, b2, bsz, t):
    d = hxc.shape[1]
    const = lambda a: pl.BlockSpec(a.shape, lambda b: (0, 0))
    st = pl.BlockSpec((None, GLA_HEADS, GLA_DV, GLA_DK), lambda b: (b, 0, 0, 0))
    shape = jax.ShapeDtypeStruct((bsz, GLA_HEADS, GLA_DV, GLA_DK), F32)
    return pl.pallas_call(
        _ctx_kernel,
        grid=(bsz,),
        in_specs=[pl.BlockSpec((t, d), lambda b: (b, 0)),
                  pl.BlockSpec((R_Q, d), lambda b: (0, 0)),
                  pl.BlockSpec((V7X_LANES, d), lambda b: (R_GK // V7X_LANES, 0)),
                  const(w2pad), const(b2)],
        out_specs=[st, st],
        out_shape=[shape, shape],
        compiler_params=_params("parallel"),
        name="ctx",
    )(hxc, w_r, w_r, w2pad, b2)


def _gla_direction(qkv_ref, ld_ref, s_ref, reverse):
    n_sub = GLA_TB // GLA_SUB
    n_chunk = GLA_SUB // CHUNK
    rr = lax.broadcasted_iota(jnp.int32, (GLA_SUB, GLA_SUB), 0)
    cc = lax.broadcasted_iota(jnp.int32, (GLA_SUB, GLA_SUB), 1)
    same_chunk = (rr // CHUNK) == (cc // CHUNK)
    mask = same_chunk & ((cc >= rr) if reverse else (cc <= rr))
    mid_row = CHUNK // 2 if reverse else CHUNK // 2 - 1
    last_row = 0 if reverse else CHUNK - 1
    ld0 = GLA_DK if reverse else 0

    outs = [None] * n_sub
    subs = range(n_sub - 1, -1, -1) if reverse else range(n_sub)
    state = s_ref[...]
    for s in subs:
        rows = slice(s * GLA_SUB, (s + 1) * GLA_SUB)
        b = _chunk_cumsum(ld_ref[rows, ld0:ld0 + GLA_DK], reverse)
        b_mid = _chunk_bcast(_chunk_rows(b, mid_row))
        b_last_rows = _chunk_rows(b, last_row)
        b_last = _chunk_bcast(b_last_rows)
        q = qkv_ref[rows, 0:GLA_DK].astype(F32)
        k = qkv_ref[rows, GLA_DK:2 * GLA_DK].astype(F32)
        v = qkv_ref[rows, 2 * GLA_DK:QKV_HEAD]
        qs = (q * jnp.exp(b - b_mid)).astype(BF16)
        ks = (k * jnp.exp(b_mid - b)).astype(BF16)
        qi = (q * jnp.exp(b)).astype(BF16)
        kd = (k * jnp.exp(b_last - b)).astype(BF16)
        att = lax.dot_general(qs, ks, NT_DIMS, preferred_element_type=F32)
        att = jnp.where(mask, att, 0.0).astype(BF16)
        o_intra = jnp.dot(att, v, preferred_element_type=F32)
        decay = jnp.exp(b_last_rows)
        o_parts = [None] * n_chunk
        chunks = range(n_chunk - 1, -1, -1) if reverse else range(n_chunk)
        for c in chunks:
            cr = slice(c * CHUNK, (c + 1) * CHUNK)
            o_parts[c] = o_intra[cr, :] + lax.dot_general(
                qi[cr, :], state.astype(BF16), NT_DIMS, preferred_element_type=F32)
            kv = lax.dot_general(v[cr, :], kd[cr, :], TN_DIMS, preferred_element_type=F32)
            state = state * decay[c:c + 1, :] + kv
        outs[s] = jnp.concatenate(o_parts, axis=0)
    s_ref[...] = state
    return jnp.concatenate(outs, axis=0)


def _gla_kernel(n_cast, qkvf_ref, ldf_ref, qkvb_ref, ldb_ref, sg_ref, s0f_ref, s0b_ref, gain_ref,
                *rest):
    cast_in, o_ref, rest = rest[:n_cast], rest[n_cast], rest[n_cast + 1:]
    cast_out, (sf_ref, sb_ref, acc_ref) = rest[:n_cast], rest[n_cast:]
    nb = pl.program_id(2)
    n_blocks = pl.num_programs(2)

    @pl.when(nb == 0)
    def _():
        sf_ref[...] = s0f_ref[...]
        sb_ref[...] = s0b_ref[...]

    _run_casts(cast_in, cast_out)
    o_f = _gla_direction(qkvf_ref, ldf_ref, sf_ref, reverse=False)
    o_b = _gla_direction(qkvb_ref, ldb_ref, sb_ref, reverse=True)
    row_f = pl.multiple_of(nb * GLA_TB, GLA_TB)
    row_b = pl.multiple_of((n_blocks - 1 - nb) * GLA_TB, GLA_TB)

    @pl.when(nb < n_blocks // 2)
    def _():
        acc_ref[pl.ds(row_f, GLA_TB), :] = o_f
        acc_ref[pl.ds(row_b, GLA_TB), :] = o_b

    @pl.when(nb >= n_blocks // 2)
    def _():
        for row, part in ((row_f, o_f), (row_b, o_b)):
            o = acc_ref[pl.ds(row, GLA_TB), :] + part
            ms = jnp.mean(o * o, axis=-1, keepdims=True)
            o = o * lax.rsqrt(ms + HEAD_NORM_EPS) * gain_ref[...]
            o_ref[pl.ds(row, GLA_TB), :] = (o * sg_ref[pl.ds(row, GLA_TB), :].astype(F32)).astype(BF16)


def _gla_call(qkv, ld, sg, s0f, s0b, gain, bsz, t, cast_jobs=()):
    nb = t // GLA_TB
    fwd = lambda n: pl.BlockSpec((GLA_TB, n), lambda b, h, i: (b * nb + i, h))
    bwd = lambda n: pl.BlockSpec((GLA_TB, n), lambda b, h, i: (b * nb + nb - 1 - i, h))
    seq = pl.BlockSpec((t, GLA_DV), lambda b, h, i: (b, h))
    st = pl.BlockSpec((None, None, GLA_DV, GLA_DK), lambda b, h, i: (b, h, 0, 0))
    cast_in_specs, cast_specs, cast_shapes = _cast_specs(cast_jobs)
    return pl.pallas_call(
        functools.partial(_gla_kernel, len(cast_jobs)),
        grid=(bsz, GLA_HEADS, nb),
        in_specs=[fwd(QKV_HEAD), fwd(LD_HEAD), bwd(QKV_HEAD), bwd(LD_HEAD),
                  seq, st, st, pl.BlockSpec((1, GLA_DV), lambda b, h, i: (0, 0))] + cast_in_specs,
        out_specs=[seq] + cast_specs,
        out_shape=[jax.ShapeDtypeStruct((bsz * t, D_GLA), BF16)] + cast_shapes,
        scratch_shapes=[pltpu.VMEM((GLA_DV, GLA_DK), F32), pltpu.VMEM((GLA_DV, GLA_DK), F32),
                        pltpu.VMEM((t, GLA_DV), F32)],
        compiler_params=_params("parallel", "parallel", "arbitrary"),
        name="gla",
    )(qkv, ld, qkv, ld, sg, s0f, s0b, gain, *[job.array for job in cast_jobs])


ROW_PAD = 16
ROW_PITCH = GRID_W + 2 * ROW_PAD


def _conv_kernel(n_row_blocks, n_cast, u_ref, w_ref, b_ref, *rest):
    cast_in, y_ref, rest = rest[:n_cast], rest[n_cast], rest[n_cast + 1:]
    cast_out, (pad_ref,) = rest[:n_cast], rest[n_cast:]
    cb = pl.program_id(1)
    rows = u_ref.shape[0] // GRID_W
    bias = jnp.broadcast_to(b_ref[...], (GRID_W, CONV_CB))
    _run_casts(cast_in, cast_out)

    @pl.when(cb < n_row_blocks)
    def _():
        @pl.when(cb == 0)
        def _():
            pad_ref[...] = jnp.zeros_like(pad_ref)

        def fill(r, carry):
            src = pl.multiple_of(r * GRID_W, GRID_W)
            dst = pl.multiple_of(r * ROW_PITCH + ROW_PAD, 8)
            pad_ref[pl.ds(dst, GRID_W), :] = u_ref[pl.ds(src, GRID_W), :]
            return carry

        lax.fori_loop(0, rows, fill, 0)

        def body(r, carry):
            base = r * ROW_PITCH + (ROW_PAD - CONV_HALF)
            acc = bias
            for j in range(CONV_WIDTH):
                acc = acc + w_ref[j:j + 1, :] * pad_ref[pl.ds(base + j, GRID_W), :]
            y_ref[pl.ds(pl.multiple_of(r * GRID_W, GRID_W), GRID_W), :] = acc
            return carry

        lax.fori_loop(0, rows, body, 0, unroll=CONV_UNROLL)

    @pl.when(cb >= n_row_blocks)
    def _():
        edge = CONV_HALF * GRID_W

        @pl.when(cb == n_row_blocks)
        def _():
            pad_ref[0:edge, :] = jnp.zeros((edge, CONV_CB), F32)
            pad_ref[edge + rows * GRID_W:2 * edge + rows * GRID_W, :] = jnp.zeros((edge, CONV_CB), F32)

        pad_ref[edge:edge + rows * GRID_W, :] = u_ref[...]

        def body(r, carry):
            acc = bias
            for j in range(CONV_WIDTH):
                src = pl.multiple_of((r + j) * GRID_W, GRID_W)
                acc = acc + w_ref[j:j + 1, :] * pad_ref[pl.ds(src, GRID_W), :]
            y_ref[pl.ds(pl.multiple_of(r * GRID_W, GRID_W), GRID_W), :] = acc
            return carry

        lax.fori_loop(0, rows, body, 0, unroll=CONV_UNROLL)


def _conv_call(u, w, b, bsz, t, cast_jobs=()):
    ch = u.shape[1]
    rows = t // GRID_W
    n_cb = ch // CONV_CB
    pad_rows = max(rows * ROW_PITCH, (rows + 2 * CONV_HALF) * GRID_W)
    blk = pl.BlockSpec((t, CONV_CB), lambda bi, c: (bi, c))
    cast_in_specs, cast_specs, cast_shapes = _cast_specs(cast_jobs)
    return pl.pallas_call(
        functools.partial(_conv_kernel, n_cb // 2, len(cast_jobs)),
        grid=(bsz, n_cb),
        in_specs=[blk, pl.BlockSpec((CONV_WIDTH, CONV_CB), lambda bi, c: (0, c)),
                  pl.BlockSpec((1, CONV_CB), lambda bi, c: (0, c))] + cast_in_specs,
        out_specs=[blk] + cast_specs,
        out_shape=[jax.ShapeDtypeStruct(u.shape, F32)] + cast_shapes,
        scratch_shapes=[pltpu.VMEM((pad_rows, CONV_CB), F32)],
        compiler_params=_params("arbitrary", "arbitrary"),
        name="conv",
    )(u, w, b, *[job.array for job in cast_jobs])


def _outproj_kernel(n_cast, og_ref, y_ref, lng_ref, lnb_ref, w_ref, h_ref, mod_ref, *rest):
    cast_in, o_ref, cast_out = rest[:n_cast], rest[n_cast], rest[n_cast + 1:]
    _run_casts(cast_in, cast_out)
    y = y_ref[...]
    mu = jnp.mean(y, axis=-1, keepdims=True)
    yc = y - mu
    var = jnp.mean(yc * yc, axis=-1, keepdims=True)
    yn = yc * lax.rsqrt(var + LN_EPS) * lng_ref[...] + lnb_ref[...]
    oc = _silu(yn).astype(BF16)
    res = jnp.dot(og_ref[...], w_ref[0:D_GLA, :], preferred_element_type=F32)
    res = res + jnp.dot(oc, w_ref[D_GLA:D_GLA + D_CONV, :], preferred_element_type=F32)
    o_ref[...] = h_ref[...] + mod_ref[5:6, :] * res


def _outproj_call(og, y, ln_g, ln_b, w_out_bf, h, mod, group_of_tile, cast_jobs=()):
    rows, d = h.shape
    tm = OUT_TM
    row = lambda n: pl.BlockSpec((tm, n), lambda i: (i, 0))
    vec = lambda n: pl.BlockSpec((1, n), lambda i: (0, 0))
    cast_in_specs, cast_specs, cast_shapes = _cast_specs(cast_jobs)
    return pl.pallas_call(
        functools.partial(_outproj_kernel, len(cast_jobs)),
        grid=(rows // tm,),
        in_specs=[row(D_GLA), row(D_CONV), vec(D_CONV), vec(D_CONV),
                  pl.BlockSpec(w_out_bf.shape, lambda i: (0, 0), pipeline_mode=pl.Buffered(1)),
                  row(d), pl.BlockSpec((None, N_MOD, d), lambda i: (group_of_tile(i), 0, 0))] + cast_in_specs,
        out_specs=[row(d)] + cast_specs,
        out_shape=[jax.ShapeDtypeStruct((rows, d), F32)] + cast_shapes,
        compiler_params=_params("parallel"),
        name="outproj",
    )(og, y, ln_g, ln_b, w_out_bf, h, mod, *[job.array for job in cast_jobs])


def kernel(x, c, ctx, c_ctx, w_mod, b_mod, norm_ffn1, w_ffn1_in, w_ffn1_out, norm_mix, w_in, w_gk2, b_gk2, gla_norm, conv_w, conv_b, conv_ln_g, conv_ln_b, w_out, norm_ffn2, w_ffn2_in, w_ffn2_out, norm_final):
    bsz, t, d = x.shape
    t_ctx = ctx.shape[1]
    assert w_mod.shape[0] == 1, "single layer only"
    assert t % GLA_TB == 0 and (t // GLA_TB) % 2 == 0 and t % FFN_TM == 0
    vec = lambda a: a.reshape(1, -1)

    w2pad = jnp.zeros((V7X_LANES, 2 * GLA_KEY), BF16)
    w2pad = w2pad.at[:GATE_RANK, :GLA_KEY].set(w_gk2[0, 0].astype(BF16))
    w2pad = w2pad.at[GATE_RANK:2 * GATE_RANK, GLA_KEY:].set(w_gk2[0, 1].astype(BF16))
    b2 = b_gk2[0].reshape(1, 2 * GLA_KEY)
    w_r = _wprep_call(jnp.swapaxes(w_in[0], 0, 1))

    n_rows = 8
    s_in = jnp.concatenate([c, c_ctx[None, :], jnp.zeros((n_rows - bsz - 1, d), F32)], axis=0)
    mod = _mod_call(s_in, w_mod[0], vec(b_mod[0])).reshape(n_rows, N_MOD, d)

    tiles_per_batch = t // FFN_TM
    lat_group = lambda i: i // tiles_per_batch
    ctx_group = lambda i: bsz
    g1, gm = vec(norm_ffn1[0]), vec(norm_mix[0])

    xc = ctx.reshape(bsz * t_ctx, d)
    hxc, wg1, wu1, wo1 = _ffn_call(
        xc, mod, ctx_group, g1, gm, w_ffn1_in[0], w_ffn1_in[0], w_ffn1_out[0], D_FF,
        _FfnCfg(mod_row=0, emit_h=False, emit_hx=True, final_norm=False, cast_w=True), "ffn1_ctx")
    s0f, s0b = _ctx_call(hxc, w_r, w2pad, b2, bsz, t_ctx)

    xl = x.reshape(bsz * t, d)
    h1, hx = _ffn_call(
        xl, mod, lat_group, g1, gm, wg1, wu1, wo1, 0,
        _FfnCfg(mod_row=0, emit_h=True, emit_hx=True, final_norm=False, cast_w=False), "ffn1")

    n_proj, n_outp = (bsz * t) // PROJ_TM, (bsz * t) // OUT_TM
    half_job = lambda n, col: _CastJob(w_ffn2_in[0], (d // n, D_FF), lambda i: (i, col),
                                       (d, D_FF), lambda i: (i, 0))
    qkv, ld, sg, u, wg2 = _proj_call(hx, w_r, w2pad, b2, (half_job(n_proj, 0),))
    (og,) = _gla_call(qkv, ld, sg, s0f, s0b, vec(gla_norm[0]), bsz, t)
    n_conv_steps = bsz * (D_CONV // CONV_CB)
    step = lambda bi, cb: (bi * (D_CONV // CONV_CB) + cb, 0)
    y, wf2_out, w_out_bf = _conv_call(
        u, conv_w[0], vec(conv_b[0]), bsz, t,
        (_CastJob(w_ffn2_out[0], (D_FF // n_conv_steps, d), step),
         _CastJob(w_out[0], (d // n_conv_steps, d), step)))
    h2, wu2 = _outproj_call(
        og, y, vec(conv_ln_g[0]), vec(conv_ln_b[0]), w_out_bf, h1, mod, lambda i: i // (t // OUT_TM),
        (half_job(n_outp, 1),))

    (out,) = _ffn_call(
        h2, mod, lambda i: i // (t // FFN2_TM), vec(norm_ffn2[0]), vec(norm_final), wg2, wu2, wf2_out, 0,
        _FfnCfg(mod_row=6, emit_h=True, emit_hx=False, final_norm=True, cast_w=False, tm=FFN2_TM,
                acc_in_h=True), "ffn2")
    return out.reshape(bsz, t, d)
```

```python
import functools
from typing import NamedTuple

import jax
import jax.numpy as jnp
from jax import lax
from jax.experimental import pallas as pl
from jax.experimental.pallas import tpu as pltpu

F32 = jnp.float32
BF16 = jnp.bfloat16

D_MODEL = 2048
GRID_W = 64
GLA_HEADS = 4
GLA_DK = 128
GLA_DV = 256
GLA_KEY = GLA_HEADS * GLA_DK
D_GLA = GLA_HEADS * GLA_DV
D_CONV = 1024
GATE_RANK = 16
GATE_NORMALIZER = 16.0
CHUNK = 64
CONV_WIDTH = 31
CONV_HALF = CONV_WIDTH // 2
D_FF = 5632
N_MOD = 9
RMS_EPS = 1e-6
HEAD_NORM_EPS = 1e-5
LN_EPS = 1e-5

OFF_V = GLA_KEY
OFF_GKF = OFF_V + D_GLA
CTX_COLS = OFF_GKF + 2 * GATE_RANK
OFF_Q = CTX_COLS
OFF_G = OFF_Q + GLA_KEY
OFF_GLU = OFF_G + D_GLA

V7X_LANES = 128
R_K = 0
R_V = R_K + GLA_KEY
R_Q = R_V + D_GLA
R_G = R_Q + GLA_KEY
R_A = R_G + D_GLA
R_B = R_A + D_CONV
R_GK = R_B + D_CONV
R_END = R_GK + V7X_LANES

V7X_VMEM_SCOPED_LIMIT_BYTES = 60000 * 1024
V7X_BF16_SUBLANES = 16

FFN_TM = 512
FFN_TALL_TM = 1024
FFN_TF = 512
FFN_TF_CAST = 256
PROJ_TM = 512
OUT_TM = 512
MOD_TN = 2048
GLA_TB = 2048
GLA_SUB = 256
CONV_CB = 128
CONV_UNROLL = 8
WPREP_TC = 256

NT_DIMS = (((1,), (1,)), ((), ()))
TN_DIMS = (((0,), (0,)), ((), ()))


def _params(*semantics):
    return pltpu.CompilerParams(dimension_semantics=semantics,
                                vmem_limit_bytes=V7X_VMEM_SCOPED_LIMIT_BYTES)


def _silu(x):
    return x * jax.nn.sigmoid(x)


def _rmsnorm_rows(x, gain, eps):
    ms = jnp.mean(x * x, axis=-1, keepdims=True)
    return x * lax.rsqrt(ms + eps) * gain


def _modulate(x, gain, shift, scale):
    return _rmsnorm_rows(x, gain, RMS_EPS) * (1.0 + scale) + shift


def _log_sigmoid(z):
    return jnp.minimum(z, 0.0) - jnp.log1p(jnp.exp(-jnp.abs(z)))


class _CastJob(NamedTuple):
    array: jax.Array
    block: tuple
    index_map: object
    out_shape: tuple = None
    out_index_map: object = None


def _cast_in_specs(jobs):
    for job in jobs:
        assert all(n % b == 0 for n, b in zip(job.array.shape, job.block))
        assert job.block[0] % V7X_BF16_SUBLANES == 0 and job.block[1] % V7X_LANES == 0
    return [pl.BlockSpec(job.block, job.index_map) for job in jobs]


def _cast_out_specs(jobs):
    specs = [pl.BlockSpec(job.block, job.out_index_map or job.index_map) for job in jobs]
    shapes = [jax.ShapeDtypeStruct(job.out_shape or job.array.shape, BF16) for job in jobs]
    return specs, shapes


def _cast_specs(jobs):
    return (_cast_in_specs(jobs),) + _cast_out_specs(jobs)


def _run_casts(src_refs, dst_refs):
    for src_ref, dst_ref in zip(src_refs, dst_refs):
        dst_ref[...] = src_ref[...].astype(dst_ref.dtype)


def _mod_kernel(s_ref, w_ref, b_ref, o_ref):
    s = _silu(s_ref[...]).astype(BF16)
    o_ref[...] = jnp.dot(s, w_ref[...].astype(BF16), preferred_element_type=F32) + b_ref[...]


def _mod_call(s_in, w_mod, b_mod):
    rows, d = s_in.shape
    n = w_mod.shape[1]
    return pl.pallas_call(
        _mod_kernel,
        grid=(n // MOD_TN,),
        in_specs=[
            pl.BlockSpec((rows, d), lambda j: (0, 0)),
            pl.BlockSpec((d, MOD_TN), lambda j: (0, j)),
            pl.BlockSpec((1, MOD_TN), lambda j: (0, j)),
        ],
        out_specs=pl.BlockSpec((rows, MOD_TN), lambda j: (0, j)),
        out_shape=jax.ShapeDtypeStruct((rows, n), F32),
        compiler_params=_params("arbitrary"),
        name="mod",
    )(s_in, w_mod, b_mod)


class _FfnCfg(NamedTuple):
    mod_row: int
    emit_h: bool
    emit_hx: bool
    final_norm: bool
    cast_w: bool
    tm: int = FFN_TM
    acc_in_h: bool = False


def _ffn_kernel(cfg, n_cast, x_ref, mod_ref, g_in_ref, g_next_ref, wg_ref, wu_ref, wo_ref, *rest):
    cast_in, rest = rest[:n_cast], rest[n_cast:]
    outs = []
    for flag in (cfg.emit_h, cfg.emit_hx, cfg.cast_w, cfg.cast_w, cfg.cast_w):
        outs.append(rest[0] if flag else None)
        rest = rest[1:] if flag else rest
    h_ref, hx_ref, wg_bf_ref, wu_bf_ref, wo_bf_ref = outs
    cast_out, scratch = rest[:n_cast], rest[n_cast:]
    hm_ref = scratch[0]
    acc_ref = h_ref if cfg.acc_in_h else scratch[1]
    j = pl.program_id(1)
    last = pl.num_programs(1) - 1
    r0 = cfg.mod_row
    _run_casts(cast_in, cast_out)

    def partial_out(hm):
        wg, wu, wo = wg_ref[...], wu_ref[...], wo_ref[...]
        if cfg.cast_w:
            wg, wu, wo = wg.astype(BF16), wu.astype(BF16), wo.astype(BF16)
            wg_bf_ref[...] = wg
            wu_bf_ref[...] = wu
            wo_bf_ref[...] = wo
        gate = jnp.dot(hm, wg, preferred_element_type=F32)
        up = jnp.dot(hm, wu, preferred_element_type=F32)
        act = (_silu(gate) * up).astype(BF16)
        return jnp.dot(act, wo, preferred_element_type=F32)

    @pl.when(j == 0)
    def _():
        hm = _modulate(x_ref[...], g_in_ref[...], mod_ref[r0:r0 + 1, :], mod_ref[r0 + 1:r0 + 2, :])
        hm = hm.astype(BF16)
        hm_ref[...] = hm
        acc_ref[...] = partial_out(hm)

    @pl.when((j > 0) & (j < last))
    def _():
        acc_ref[...] += partial_out(hm_ref[...])

    @pl.when(j == last)
    def _():
        g = mod_ref[r0 + 2:r0 + 3, :]
        h = x_ref[...] + (0.5 * g) * (acc_ref[...] + partial_out(hm_ref[...]))
        if cfg.emit_hx:
            hx = _modulate(h, g_next_ref[...], mod_ref[r0 + 3:r0 + 4, :], mod_ref[r0 + 4:r0 + 5, :])
            hx_ref[...] = hx.astype(BF16)
        if cfg.emit_h:
            h_ref[...] = _rmsnorm_rows(h, g_next_ref[...], RMS_EPS) if cfg.final_norm else h


def _ffn_call(x, mod, group_of_tile, g_in, g_next, wg, wu, wo, up_col0, cfg, name, cast_jobs=()):
    rows, d = x.shape
    n_ff = wo.shape[0]
    tf = FFN_TF_CAST if cfg.cast_w else FFN_TF
    n_j = n_ff // tf
    assert n_j >= 2, "first and last hidden block must be distinct grid steps"
    tm = cfg.tm
    assert not cfg.cast_w or rows == tm, "bf16 weight copies are written by one row tile only"
    assert cfg.emit_h or not cfg.acc_in_h
    up_blk0 = up_col0 // tf
    row_spec = pl.BlockSpec((tm, d), lambda i, j: (i, 0))
    vec_spec = pl.BlockSpec((1, d), lambda i, j: (0, 0))
    wg_spec = pl.BlockSpec((d, tf), lambda i, j: (0, j))
    wu_spec = pl.BlockSpec((d, tf), lambda i, j: (0, up_blk0 + j))
    wo_spec = pl.BlockSpec((tf, d), lambda i, j: (j, 0))
    cast_in_specs, cast_specs, cast_shapes = _cast_specs(cast_jobs)
    in_specs = [row_spec, pl.BlockSpec((None, N_MOD, d), lambda i, j: (group_of_tile(i), 0, 0)),
                vec_spec, vec_spec, wg_spec, wu_spec, wo_spec] + cast_in_specs
    out_specs, out_shape = [], []
    if cfg.emit_h:
        out_specs.append(row_spec)
        out_shape.append(jax.ShapeDtypeStruct((rows, d), F32))
    if cfg.emit_hx:
        out_specs.append(row_spec)
        out_shape.append(jax.ShapeDtypeStruct((rows, d), BF16))
    if cfg.cast_w:
        out_specs += [wg_spec, pl.BlockSpec((d, tf), lambda i, j: (0, j)), wo_spec]
        out_shape += [jax.ShapeDtypeStruct((d, n_ff), BF16), jax.ShapeDtypeStruct((d, n_ff), BF16),
                      jax.ShapeDtypeStruct((n_ff, d), BF16)]
    return pl.pallas_call(
        functools.partial(_ffn_kernel, cfg, len(cast_jobs)),
        grid=(rows // tm, n_j),
        in_specs=in_specs,
        out_specs=out_specs + cast_specs,
        out_shape=out_shape + cast_shapes,
        scratch_shapes=[pltpu.VMEM((tm, d), BF16)] + ([] if cfg.acc_in_h else [pltpu.VMEM((tm, d), F32)]),
        compiler_params=_params("parallel", "arbitrary"),
        name=name,
    )(x, mod, g_in, g_next, wg, wu, wo, *[job.array for job in cast_jobs])


def _wprep_kernel(w_ref, o_ref):
    o_ref[R_K:R_Q, :] = w_ref[0:OFF_GKF, :].astype(BF16)
    o_ref[R_Q:R_GK, :] = w_ref[OFF_Q:OFF_GLU + 2 * D_CONV, :].astype(BF16)
    o_ref[R_GK:R_GK + 2 * GATE_RANK, :] = w_ref[OFF_GKF:CTX_COLS, :].astype(BF16)
    o_ref[R_GK + 2 * GATE_RANK:R_END, :] = jnp.zeros((V7X_LANES - 2 * GATE_RANK, o_ref.shape[1]), BF16)


def _wprep_call(w_t):
    n, d = w_t.shape
    tc = WPREP_TC
    return pl.pallas_call(
        _wprep_kernel,
        grid=(d // tc,),
        in_specs=[pl.BlockSpec((n, tc), lambda i: (0, i))],
        out_specs=pl.BlockSpec((R_END, tc), lambda i: (0, i)),
        out_shape=jax.ShapeDtypeStruct((R_END, d), BF16),
        compiler_params=_params("parallel"),
        name="wprep",
    )(w_t)


def _log_decays(p_gk, w2_ref, b2_ref):
    z = jnp.dot(p_gk.astype(BF16), w2_ref[...], preferred_element_type=F32) + b2_ref[...]
    return _log_sigmoid(z) * (1.0 / GATE_NORMALIZER)


def _chunk_cumsum(x, reverse):
    n = x.shape[0]
    pos = lax.broadcasted_iota(jnp.int32, x.shape, 0) % CHUNK
    d = 1
    while d < CHUNK:
        if reverse:
            shifted = pltpu.roll(x, n - d, 0)
            x = x + jnp.where(pos < CHUNK - d, shifted, 0.0)
        else:
            shifted = pltpu.roll(x, d, 0)
            x = x + jnp.where(pos >= d, shifted, 0.0)
        d *= 2
    return x


def _chunk_rows(x, row_in_chunk):
    return jnp.concatenate([x[c0 + row_in_chunk:c0 + row_in_chunk + 1, :]
                            for c0 in range(0, x.shape[0], CHUNK)], axis=0)


def _chunk_bcast(rows_per_chunk):
    return jnp.concatenate([jnp.broadcast_to(rows_per_chunk[c:c + 1, :], (CHUNK, rows_per_chunk.shape[1]))
                            for c in range(rows_per_chunk.shape[0])], axis=0)


QKV_HEAD = 2 * GLA_DK + GLA_DV
LD_HEAD = 2 * GLA_DK


def _proj_kernel(n_cast, h_ref, mod_ref, g_ref, w_ref, w2_ref, b2_ref, *rest):
    cast_in, (qkv_ref, ld_ref, sg_ref, u_ref), cast_out = rest[:n_cast], rest[n_cast:n_cast + 4], rest[n_cast + 4:]
    _run_casts(cast_in, cast_out)
    hx = _modulate(h_ref[...], g_ref[...], mod_ref[3:4, :], mod_ref[4:5, :]).astype(BF16)

    def proj(lo, hi):
        return lax.dot_general(hx, w_ref[lo:hi, :], NT_DIMS, preferred_element_type=F32)

    ld = _log_decays(proj(R_GK, R_END), w2_ref, b2_ref)
    k = proj(R_K, R_V).astype(BF16)
    v = proj(R_V, R_Q).astype(BF16)
    q = (proj(R_Q, R_G) * (GLA_DK ** -0.5)).astype(BF16)
    for h in range(GLA_HEADS):
        dk = slice(h * GLA_DK, (h + 1) * GLA_DK)
        c0 = h * QKV_HEAD
        qkv_ref[:, c0:c0 + GLA_DK] = q[:, dk]
        qkv_ref[:, c0 + GLA_DK:c0 + 2 * GLA_DK] = k[:, dk]
        qkv_ref[:, c0 + 2 * GLA_DK:c0 + QKV_HEAD] = v[:, h * GLA_DV:(h + 1) * GLA_DV]
        ld_ref[:, h * LD_HEAD:h * LD_HEAD + GLA_DK] = ld[:, dk]
        ld_ref[:, h * LD_HEAD + GLA_DK:(h + 1) * LD_HEAD] = ld[:, GLA_KEY + h * GLA_DK:GLA_KEY + (h + 1) * GLA_DK]
    sg_ref[...] = _silu(proj(R_G, R_A)).astype(BF16)
    u_ref[...] = proj(R_A, R_B) * jax.nn.sigmoid(proj(R_B, R_GK))


def _proj_call(h, mod, group_of_tile, g_mix, w_r, w2pad, b2, cast_jobs=()):
    rows, d = h.shape
    tm = PROJ_TM
    row = lambda n: pl.BlockSpec((tm, n), lambda i: (i, 0))
    whole = lambda a: pl.BlockSpec(a.shape, lambda i: (0, 0), pipeline_mode=pl.Buffered(1))
    outs = [(GLA_HEADS * QKV_HEAD, BF16), (GLA_HEADS * LD_HEAD, F32), (D_GLA, BF16), (D_CONV, F32)]
    cast_in_specs, cast_specs, cast_shapes = _cast_specs(cast_jobs)
    return pl.pallas_call(
        functools.partial(_proj_kernel, len(cast_jobs)),
        grid=(rows // tm,),
        in_specs=[row(d), pl.BlockSpec((None, N_MOD, d), lambda i: (group_of_tile(i), 0, 0)),
                  pl.BlockSpec((1, d), lambda i: (0, 0)), whole(w_r), whole(w2pad), whole(b2)] + cast_in_specs,
        out_specs=[row(n) for n, _ in outs] + cast_specs,
        out_shape=[jax.ShapeDtypeStruct((rows, n), dt) for n, dt in outs] + cast_shapes,
        compiler_params=_params("parallel"),
        name="proj",
    )(h, mod, g_mix, w_r, w2pad, b2, *[job.array for job in cast_jobs])


def _ctx_kernel(hx_ref, wkv_ref, wgk_ref, w2_ref, b2_ref, sf_ref, sb_ref):
    hx = hx_ref[...]
    t = hx.shape[0]
    kv = lax.dot_general(hx, wkv_ref[...], NT_DIMS, preferred_element_type=F32)
    ld = _log_decays(lax.dot_general(hx, wgk_ref[...], NT_DIMS, preferred_element_type=F32),
                     w2_ref, b2_ref)
    r = lax.broadcasted_iota(jnp.int32, (t, t), 0)
    c = lax.broadcasted_iota(jnp.int32, (t, t), 1)
    hi = lax.Precision.HIGHEST
    e_f = jnp.dot((c > r).astype(F32), ld[:, :GLA_KEY], preferred_element_type=F32, precision=hi)
    e_b = jnp.dot((c < r).astype(F32), ld[:, GLA_KEY:], preferred_element_type=F32, precision=hi)
    for h in range(GLA_HEADS):
        ks = slice(h * GLA_DK, (h + 1) * GLA_DK)
        k = kv[:, R_K + h * GLA_DK:R_K + (h + 1) * GLA_DK]
        v = kv[:, R_V + h * GLA_DV:R_V + (h + 1) * GLA_DV].astype(BF16)
        sf_ref[h] = lax.dot_general(v, (k * jnp.exp(e_f[:, ks])).astype(BF16), TN_DIMS,
                                    preferred_element_type=F32)
        sb_ref[h] = lax.dot_general(v, (k * jnp.exp(e_b[:, ks])).astype(BF16), TN_DIMS,
                                    preferred_element_type=F32)


def _ctx_call(hxc, w_r, w2pad, b2, bsz, t):
    d = hxc.shape[1]
    const = lambda a: pl.BlockSpec(a.shape, lambda b: (0, 0))
    st = pl.BlockSpec((None, GLA_HEADS, GLA_DV, GLA_DK), lambda b: (b, 0, 0, 0))
    shape = jax.ShapeDtypeStruct((bsz, GLA_HEADS, GLA_DV, GLA_DK), F32)
    return pl.pallas_call(
        _ctx_kernel,
        grid=(bsz,),
        in_specs=[pl.BlockSpec((t, d), lambda b: (b, 0)),
                  pl.BlockSpec((R_Q, d), lambda b: (0, 0)),
                  pl.BlockSpec((V7X_LANES, d), lambda b: (R_GK // V7X_LANES, 0)),
                  const(w2pad), const(b2)],
        out_specs=[st, st],
        out_shape=[shape, shape],
        compiler_params=_params("parallel"),
        name="ctx",
    )(hxc, w_r, w_r, w2pad, b2)


def _gla_direction(qkv_ref, ld_ref, s_ref, reverse):
    n_sub = GLA_TB // GLA_SUB
    n_chunk = GLA_SUB // CHUNK
    rr = lax.broadcasted_iota(jnp.int32, (GLA_SUB, GLA_SUB), 0)
    cc = lax.broadcasted_iota(jnp.int32, (GLA_SUB, GLA_SUB), 1)
    same_chunk = (rr // CHUNK) == (cc // CHUNK)
    mask = same_chunk & ((cc >= rr) if reverse else (cc <= rr))
    mid_row = CHUNK // 2 if reverse else CHUNK // 2 - 1
    last_row = 0 if reverse else CHUNK - 1
    ld0 = GLA_DK if reverse else 0

    outs = [None] * n_sub
    subs = range(n_sub - 1, -1, -1) if reverse else range(n_sub)
    state = s_ref[...]
    for s in subs:
        rows = slice(s * GLA_SUB, (s + 1) * GLA_SUB)
        b = _chunk_cumsum(ld_ref[rows, ld0:ld0 + GLA_DK], reverse)
        b_mid = _chunk_bcast(_chunk_rows(b, mid_row))
        b_last_rows = _chunk_rows(b, last_row)
        b_last = _chunk_bcast(b_last_rows)
        q = qkv_ref[rows, 0:GLA_DK].astype(F32)
        k = qkv_ref[rows, GLA_DK:2 * GLA_DK].astype(F32)
        v = qkv_ref[rows, 2 * GLA_DK:QKV_HEAD]
        qs = (q * jnp.exp(b - b_mid)).astype(BF16)
        ks = (k * jnp.exp(b_mid - b)).astype(BF16)
        qi = (q * jnp.exp(b)).astype(BF16)
        kd = (k * jnp.exp(b_last - b)).astype(BF16)
        att = lax.dot_general(qs, ks, NT_DIMS, preferred_element_type=F32)
        att = jnp.where(mask, att, 0.0).astype(BF16)
        o_intra = jnp.dot(att, v, preferred_element_type=F32)
        decay = jnp.exp(b_last_rows)
        o_parts = [None] * n_chunk
        chunks = range(n_chunk - 1, -1, -1) if reverse else range(n_chunk)
        for c in chunks:
            cr = slice(c * CHUNK, (c + 1) * CHUNK)
            o_parts[c] = o_intra[cr, :] + lax.dot_general(
                qi[cr, :], state.astype(BF16), NT_DIMS, preferred_element_type=F32)
            kv = lax.dot_general(v[cr, :], kd[cr, :], TN_DIMS, preferred_element_type=F32)
            state = state * decay[c:c + 1, :] + kv
        outs[s] = jnp.concatenate(o_parts, axis=0)
    s_ref[...] = state
    return jnp.concatenate(outs, axis=0)


def _gla_kernel(n_cast, qkvf_ref, ldf_ref, qkvb_ref, ldb_ref, sg_ref, s0f_ref, s0b_ref, gain_ref,
                *rest):
    cast_in, o_ref, rest = rest[:n_cast], rest[n_cast], rest[n_cast + 1:]
    cast_out, (sf_ref, sb_ref, acc_ref) = rest[:n_cast], rest[n_cast:]
    nb = pl.program_id(2)
    n_blocks = pl.num_programs(2)

    @pl.when(nb == 0)
    def _():
        sf_ref[...] = s0f_ref[...]
        sb_ref[...] = s0b_ref[...]

    _run_casts(cast_in, cast_out)
    o_f = _gla_direction(qkvf_ref, ldf_ref, sf_ref, reverse=False)
    o_b = _gla_direction(qkvb_ref, ldb_ref, sb_ref, reverse=True)
    row_f = pl.multiple_of(nb * GLA_TB, GLA_TB)
    row_b = pl.multiple_of((n_blocks - 1 - nb) * GLA_TB, GLA_TB)

    @pl.when(nb < n_blocks // 2)
    def _():
        acc_ref[pl.ds(row_f, GLA_TB), :] = o_f
        acc_ref[pl.ds(row_b, GLA_TB), :] = o_b

    @pl.when(nb >= n_blocks // 2)
    def _():
        for row, part in ((row_f, o_f), (row_b, o_b)):
            o = acc_ref[pl.ds(row, GLA_TB), :] + part
            ms = jnp.mean(o * o, axis=-1, keepdims=True)
            o = o * lax.rsqrt(ms + HEAD_NORM_EPS) * gain_ref[...]
            o_ref[pl.ds(row, GLA_TB), :] = (o * sg_ref[pl.ds(row, GLA_TB), :].astype(F32)).astype(BF16)


def _gla_call(qkv, ld, sg, s0f, s0b, gain, bsz, t, cast_jobs=()):
    nb = t // GLA_TB
    fwd = lambda n: pl.BlockSpec((GLA_TB, n), lambda b, h, i: (b * nb + i, h))
    bwd = lambda n: pl.BlockSpec((GLA_TB, n), lambda b, h, i: (b * nb + nb - 1 - i, h))
    seq = pl.BlockSpec((t, GLA_DV), lambda b, h, i: (b, h))
    st = pl.BlockSpec((None, None, GLA_DV, GLA_DK), lambda b, h, i: (b, h, 0, 0))
    cast_in_specs, cast_specs, cast_shapes = _cast_specs(cast_jobs)
    return pl.pallas_call(
        functools.partial(_gla_kernel, len(cast_jobs)),
        grid=(bsz, GLA_HEADS, nb),
        in_specs=[fwd(QKV_HEAD), fwd(LD_HEAD), bwd(QKV_HEAD), bwd(LD_HEAD),
                  seq, st, st, pl.BlockSpec((1, GLA_DV), lambda b, h, i: (0, 0))] + cast_in_specs,
        out_specs=[seq] + cast_specs,
        out_shape=[jax.ShapeDtypeStruct((bsz * t, D_GLA), BF16)] + cast_shapes,
        scratch_shapes=[pltpu.VMEM((GLA_DV, GLA_DK), F32), pltpu.VMEM((GLA_DV, GLA_DK), F32),
                        pltpu.VMEM((t, GLA_DV), F32)],
        compiler_params=_params("parallel", "parallel", "arbitrary"),
        name="gla",
    )(qkv, ld, qkv, ld, sg, s0f, s0b, gain, *[job.array for job in cast_jobs])


ROW_PAD = 16
ROW_PITCH = GRID_W + 2 * ROW_PAD


def _conv_kernel(n_row_blocks, n_cast, u_ref, w_ref, b_ref, *rest):
    cast_in, y_ref, rest = rest[:n_cast], rest[n_cast], rest[n_cast + 1:]
    cast_out, (pad_ref,) = rest[:n_cast], rest[n_cast:]
    cb = pl.program_id(1)
    rows = u_ref.shape[0] // GRID_W
    bias = jnp.broadcast_to(b_ref[...], (GRID_W, CONV_CB))
    _run_casts(cast_in, cast_out)

    @pl.when(cb < n_row_blocks)
    def _():
        @pl.when(cb == 0)
        def _():
            pad_ref[...] = jnp.zeros_like(pad_ref)

        def fill(r, carry):
            src = pl.multiple_of(r * GRID_W, GRID_W)
            dst = pl.multiple_of(r * ROW_PITCH + ROW_PAD, 8)
            pad_ref[pl.ds(dst, GRID_W), :] = u_ref[pl.ds(src, GRID_W), :]
            return carry

        lax.fori_loop(0, rows, fill, 0)

        def body(r, carry):
            base = r * ROW_PITCH + (ROW_PAD - CONV_HALF)
            acc = bias
            for j in range(CONV_WIDTH):
                acc = acc + w_ref[j:j + 1, :] * pad_ref[pl.ds(base + j, GRID_W), :]
            y_ref[pl.ds(pl.multiple_of(r * GRID_W, GRID_W), GRID_W), :] = acc
            return carry

        lax.fori_loop(0, rows, body, 0, unroll=CONV_UNROLL)

    @pl.when(cb >= n_row_blocks)
    def _():
        edge = CONV_HALF * GRID_W

        @pl.when(cb == n_row_blocks)
        def _():
            pad_ref[0:edge, :] = jnp.zeros((edge, CONV_CB), F32)
            pad_ref[edge + rows * GRID_W:2 * edge + rows * GRID_W, :] = jnp.zeros((edge, CONV_CB), F32)

        pad_ref[edge:edge + rows * GRID_W, :] = u_ref[...]

        def body(r, carry):
            acc = bias
            for j in range(CONV_WIDTH):
                src = pl.multiple_of((r + j) * GRID_W, GRID_W)
                acc = acc + w_ref[j:j + 1, :] * pad_ref[pl.ds(src, GRID_W), :]
            y_ref[pl.ds(pl.multiple_of(r * GRID_W, GRID_W), GRID_W), :] = acc
            return carry

        lax.fori_loop(0, rows, body, 0, unroll=CONV_UNROLL)


def _conv_call(u, w, b, bsz, t, cast_jobs=()):
    ch = u.shape[1]
    rows = t // GRID_W
    n_cb = ch // CONV_CB
    pad_rows = max(rows * ROW_PITCH, (rows + 2 * CONV_HALF) * GRID_W)
    blk = pl.BlockSpec((t, CONV_CB), lambda bi, c: (bi, c))
    cast_in_specs, cast_specs, cast_shapes = _cast_specs(cast_jobs)
    return pl.pallas_call(
        functools.partial(_conv_kernel, n_cb // 2, len(cast_jobs)),
        grid=(bsz, n_cb),
        in_specs=[blk, pl.BlockSpec((CONV_WIDTH, CONV_CB), lambda bi, c: (0, c)),
                  pl.BlockSpec((1, CONV_CB), lambda bi, c: (0, c))] + cast_in_specs,
        out_specs=[blk] + cast_specs,
        out_shape=[jax.ShapeDtypeStruct(u.shape, F32)] + cast_shapes,
        scratch_shapes=[pltpu.VMEM((pad_rows, CONV_CB), F32)],
        compiler_params=_params("arbitrary", "arbitrary"),
        name="conv",
    )(u, w, b, *[job.array for job in cast_jobs])


def _outproj_kernel(n_cast, og_ref, y_ref, lng_ref, lnb_ref, w_ref, h_ref, mod_ref, *rest):
    cast_in, o_ref, cast_out = rest[:n_cast], rest[n_cast], rest[n_cast + 1:]
    _run_casts(cast_in, cast_out)
    y = y_ref[...]
    mu = jnp.mean(y, axis=-1, keepdims=True)
    yc = y - mu
    var = jnp.mean(yc * yc, axis=-1, keepdims=True)
    yn = yc * lax.rsqrt(var + LN_EPS) * lng_ref[...] + lnb_ref[...]
    oc = _silu(yn).astype(BF16)
    res = jnp.dot(og_ref[...], w_ref[0:D_GLA, :], preferred_element_type=F32)
    res = res + jnp.dot(oc, w_ref[D_GLA:D_GLA + D_CONV, :], preferred_element_type=F32)
    o_ref[...] = h_ref[...] + mod_ref[5:6, :] * res


def _outproj_call(og, y, ln_g, ln_b, w_out_bf, h, mod, group_of_tile, cast_jobs=()):
    rows, d = h.shape
    tm = OUT_TM
    row = lambda n: pl.BlockSpec((tm, n), lambda i: (i, 0))
    vec = lambda n: pl.BlockSpec((1, n), lambda i: (0, 0))
    cast_in_specs, cast_specs, cast_shapes = _cast_specs(cast_jobs)
    return pl.pallas_call(
        functools.partial(_outproj_kernel, len(cast_jobs)),
        grid=(rows // tm,),
        in_specs=[row(D_GLA), row(D_CONV), vec(D_CONV), vec(D_CONV),
                  pl.BlockSpec(w_out_bf.shape, lambda i: (0, 0), pipeline_mode=pl.Buffered(1)),
                  row(d), pl.BlockSpec((None, N_MOD, d), lambda i: (group_of_tile(i), 0, 0))] + cast_in_specs,
        out_specs=[row(d)] + cast_specs,
        out_shape=[jax.ShapeDtypeStruct((rows, d), F32)] + cast_shapes,
        compiler_params=_params("parallel"),
        name="outproj",
    )(og, y, ln_g, ln_b, w_out_bf, h, mod, *[job.array for job in cast_jobs])


def kernel(x, c, ctx, c_ctx, w_mod, b_mod, norm_ffn1, w_ffn1_in, w_ffn1_out, norm_mix, w_in, w_gk2, b_gk2, gla_norm, conv_w, conv_b, conv_ln_g, conv_ln_b, w_out, norm_ffn2, w_ffn2_in, w_ffn2_out, norm_final):
    bsz, t, d = x.shape
    t_ctx = ctx.shape[1]
    assert w_mod.shape[0] == 1, "single layer only"
    assert t % GLA_TB == 0 and (t // GLA_TB) % 2 == 0 and t % FFN_TALL_TM == 0 and t % PROJ_TM == 0
    vec = lambda a: a.reshape(1, -1)

    w2pad = jnp.zeros((V7X_LANES, 2 * GLA_KEY), BF16)
    w2pad = w2pad.at[:GATE_RANK, :GLA_KEY].set(w_gk2[0, 0].astype(BF16))
    w2pad = w2pad.at[GATE_RANK:2 * GATE_RANK, GLA_KEY:].set(w_gk2[0, 1].astype(BF16))
    b2 = b_gk2[0].reshape(1, 2 * GLA_KEY)
    w_r = _wprep_call(jnp.swapaxes(w_in[0], 0, 1))

    n_rows = 8
    s_in = jnp.concatenate([c, c_ctx[None, :], jnp.zeros((n_rows - bsz - 1, d), F32)], axis=0)
    mod = _mod_call(s_in, w_mod[0], vec(b_mod[0])).reshape(n_rows, N_MOD, d)

    batch_of = lambda tile_rows: (lambda i: i // (t // tile_rows))
    ctx_group = lambda i: bsz
    g1, gm = vec(norm_ffn1[0]), vec(norm_mix[0])

    xc = ctx.reshape(bsz * t_ctx, d)
    hxc, wg1, wu1, wo1 = _ffn_call(
        xc, mod, ctx_group, g1, gm, w_ffn1_in[0], w_ffn1_in[0], w_ffn1_out[0], D_FF,
        _FfnCfg(mod_row=0, emit_h=False, emit_hx=True, final_norm=False, cast_w=True), "ffn1_ctx")
    s0f, s0b = _ctx_call(hxc, w_r, w2pad, b2, bsz, t_ctx)

    xl = x.reshape(bsz * t, d)
    (h1,) = _ffn_call(
        xl, mod, batch_of(FFN_TALL_TM), g1, gm, wg1, wu1, wo1, 0,
        _FfnCfg(mod_row=0, emit_h=True, emit_hx=False, final_norm=False, cast_w=False, tm=FFN_TALL_TM,
                acc_in_h=True), "ffn1")

    n_proj, n_outp = (bsz * t) // PROJ_TM, (bsz * t) // OUT_TM
    half_job = lambda n, col: _CastJob(w_ffn2_in[0], (d // n, D_FF), lambda i: (i, col),
                                       (d, D_FF), lambda i: (i, 0))
    qkv, ld, sg, u, wg2 = _proj_call(h1, mod, batch_of(PROJ_TM), gm, w_r, w2pad, b2,
                                     (half_job(n_proj, 0),))
    (og,) = _gla_call(qkv, ld, sg, s0f, s0b, vec(gla_norm[0]), bsz, t)
    n_conv_steps = bsz * (D_CONV // CONV_CB)
    step = lambda bi, cb: (bi * (D_CONV // CONV_CB) + cb, 0)
    y, wf2_out, w_out_bf = _conv_call(
        u, conv_w[0], vec(conv_b[0]), bsz, t,
        (_CastJob(w_ffn2_out[0], (D_FF // n_conv_steps, d), step),
         _CastJob(w_out[0], (d // n_conv_steps, d), step)))
    h2, wu2 = _outproj_call(
        og, y, vec(conv_ln_g[0]), vec(conv_ln_b[0]), w_out_bf, h1, mod, batch_of(OUT_TM),
        (half_job(n_outp, 1),))

    (out,) = _ffn_call(
        h2, mod, batch_of(FFN_TALL_TM), vec(norm_ffn2[0]), vec(norm_final), wg2, wu2, wf2_out, 0,
        _FfnCfg(mod_row=6, emit_h=True, emit_hx=False, final_norm=True, cast_w=False, tm=FFN_TALL_TM,
                acc_in_h=True), "ffn2")
    return out.reshape(bsz, t, d)
```

```python
import functools
from typing import NamedTuple

import jax
import jax.numpy as jnp
from jax import lax
from jax.experimental import pallas as pl
from jax.experimental.pallas import tpu as pltpu

F32 = jnp.float32
BF16 = jnp.bfloat16

D_MODEL = 2048
GRID_W = 64
GLA_HEADS = 4
GLA_DK = 128
GLA_DV = 256
GLA_KEY = GLA_HEADS * GLA_DK
D_GLA = GLA_HEADS * GLA_DV
D_CONV = 1024
GATE_RANK = 16
GATE_NORMALIZER = 16.0
CHUNK = 64
CONV_WIDTH = 31
CONV_HALF = CONV_WIDTH // 2
D_FF = 5632
N_MOD = 9
RMS_EPS = 1e-6
HEAD_NORM_EPS = 1e-5
LN_EPS = 1e-5

OFF_V = GLA_KEY
OFF_GKF = OFF_V + D_GLA
CTX_COLS = OFF_GKF + 2 * GATE_RANK
OFF_Q = CTX_COLS
OFF_G = OFF_Q + GLA_KEY
OFF_GLU = OFF_G + D_GLA

V7X_LANES = 128
R_K = 0
R_V = R_K + GLA_KEY
R_Q = R_V + D_GLA
R_G = R_Q + GLA_KEY
R_A = R_G + D_GLA
R_B = R_A + D_CONV
R_GK = R_B + D_CONV
R_END = R_GK + V7X_LANES

V7X_VMEM_SCOPED_LIMIT_BYTES = 60000 * 1024
V7X_BF16_SUBLANES = 16

FFN_TM = 512
FFN_TALL_TM = 1024
FFN_TF = 512
FFN_TF_CAST = 256
PROJ_TM = 512
OUT_TM = 512
MOD_TN = 2048
GLA_TB = 2048
GLA_SUB = 256
CONV_CB = 128
CONV_UNROLL = 8
WPREP_TC = 256

NT_DIMS = (((1,), (1,)), ((), ()))
TN_DIMS = (((0,), (0,)), ((), ()))


def _params(*semantics):
    return pltpu.CompilerParams(dimension_semantics=semantics,
                                vmem_limit_bytes=V7X_VMEM_SCOPED_LIMIT_BYTES)


def _silu(x):
    return x * jax.nn.sigmoid(x)


def _rmsnorm_rows(x, gain, eps):
    ms = jnp.mean(x * x, axis=-1, keepdims=True)
    return x * lax.rsqrt(ms + eps) * gain


def _modulate(x, gain, shift, scale):
    return _rmsnorm_rows(x, gain, RMS_EPS) * (1.0 + scale) + shift


def _log_sigmoid(z):
    return jnp.minimum(z, 0.0) - jnp.log1p(jnp.exp(-jnp.abs(z)))


class _CastJob(NamedTuple):
    array: jax.Array
    block: tuple
    index_map: object
    out_shape: tuple = None
    out_index_map: object = None


def _cast_in_specs(jobs):
    for job in jobs:
        assert all(n % b == 0 for n, b in zip(job.array.shape, job.block))
        assert job.block[0] % V7X_BF16_SUBLANES == 0 and job.block[1] % V7X_LANES == 0
    return [pl.BlockSpec(job.block, job.index_map) for job in jobs]


def _cast_out_specs(jobs):
    specs = [pl.BlockSpec(job.block, job.out_index_map or job.index_map) for job in jobs]
    shapes = [jax.ShapeDtypeStruct(job.out_shape or job.array.shape, BF16) for job in jobs]
    return specs, shapes


def _cast_specs(jobs):
    return (_cast_in_specs(jobs),) + _cast_out_specs(jobs)


def _run_casts(src_refs, dst_refs):
    for src_ref, dst_ref in zip(src_refs, dst_refs):
        dst_ref[...] = src_ref[...].astype(dst_ref.dtype)


MOD_STREAMS = 2


def _mod_kernel(s_ref, *rest):
    w_refs, (b_ref, o_ref) = rest[:MOD_STREAMS], rest[MOD_STREAMS:]
    s = _silu(s_ref[...]).astype(BF16)
    kr = s.shape[1] // MOD_STREAMS
    acc = b_ref[...]
    for m, w_ref in enumerate(w_refs):
        acc = acc + jnp.dot(s[:, m * kr:(m + 1) * kr], w_ref[...].astype(BF16), preferred_element_type=F32)
    o_ref[...] = acc


def _mod_call(s_in, w_mod, b_mod):
    rows, d = s_in.shape
    n = w_mod.shape[1]
    kr = d // MOD_STREAMS
    w_specs = [pl.BlockSpec((kr, MOD_TN), lambda j, m=m: (m, j)) for m in range(MOD_STREAMS)]
    return pl.pallas_call(
        _mod_kernel,
        grid=(n // MOD_TN,),
        in_specs=[pl.BlockSpec((rows, d), lambda j: (0, 0))] + w_specs + [
            pl.BlockSpec((1, MOD_TN), lambda j: (0, j))],
        out_specs=pl.BlockSpec((rows, MOD_TN), lambda j: (0, j)),
        out_shape=jax.ShapeDtypeStruct((rows, n), F32),
        compiler_params=_params("arbitrary"),
        name="mod",
    )(s_in, *([w_mod] * MOD_STREAMS), b_mod)


class _FfnCfg(NamedTuple):
    mod_row: int
    emit_h: bool
    emit_hx: bool
    final_norm: bool
    cast_w: bool
    tm: int = FFN_TM
    acc_in_h: bool = False


def _ffn_kernel(cfg, n_cast, x_ref, mod_ref, g_in_ref, g_next_ref, wg_ref, wu_ref, wo_ref, *rest):
    cast_in, rest = rest[:n_cast], rest[n_cast:]
    outs = []
    for flag in (cfg.emit_h, cfg.emit_hx, cfg.cast_w, cfg.cast_w, cfg.cast_w):
        outs.append(rest[0] if flag else None)
        rest = rest[1:] if flag else rest
    h_ref, hx_ref, wg_bf_ref, wu_bf_ref, wo_bf_ref = outs
    cast_out, scratch = rest[:n_cast], rest[n_cast:]
    hm_ref = scratch[0]
    acc_ref = h_ref if cfg.acc_in_h else scratch[1]
    j = pl.program_id(1)
    last = pl.num_programs(1) - 1
    r0 = cfg.mod_row
    _run_casts(cast_in, cast_out)

    def partial_out(hm):
        wg, wu, wo = wg_ref[...], wu_ref[...], wo_ref[...]
        if cfg.cast_w:
            wg, wu, wo = wg.astype(BF16), wu.astype(BF16), wo.astype(BF16)
            wg_bf_ref[...] = wg
            wu_bf_ref[...] = wu
            wo_bf_ref[...] = wo
        gate = jnp.dot(hm, wg, preferred_element_type=F32)
        up = jnp.dot(hm, wu, preferred_element_type=F32)
        act = (_silu(gate) * up).astype(BF16)
        return jnp.dot(act, wo, preferred_element_type=F32)

    @pl.when(j == 0)
    def _():
        hm = _modulate(x_ref[...], g_in_ref[...], mod_ref[r0:r0 + 1, :], mod_ref[r0 + 1:r0 + 2, :])
        hm = hm.astype(BF16)
        hm_ref[...] = hm
        acc_ref[...] = partial_out(hm)

    @pl.when((j > 0) & (j < last))
    def _():
        acc_ref[...] += partial_out(hm_ref[...])

    @pl.when(j == last)
    def _():
        g = mod_ref[r0 + 2:r0 + 3, :]
        h = x_ref[...] + (0.5 * g) * (acc_ref[...] + partial_out(hm_ref[...]))
        if cfg.emit_hx:
            hx = _modulate(h, g_next_ref[...], mod_ref[r0 + 3:r0 + 4, :], mod_ref[r0 + 4:r0 + 5, :])
            hx_ref[...] = hx.astype(BF16)
        if cfg.emit_h:
            h_ref[...] = _rmsnorm_rows(h, g_next_ref[...], RMS_EPS) if cfg.final_norm else h


def _ffn_call(x, mod, group_of_tile, g_in, g_next, wg, wu, wo, up_col0, cfg, name, cast_jobs=()):
    rows, d = x.shape
    n_ff = wo.shape[0]
    tf = FFN_TF_CAST if cfg.cast_w else FFN_TF
    n_j = n_ff // tf
    assert n_j >= 2, "first and last hidden block must be distinct grid steps"
    tm = cfg.tm
    assert not cfg.cast_w or rows == tm, "bf16 weight copies are written by one row tile only"
    assert cfg.emit_h or not cfg.acc_in_h
    up_blk0 = up_col0 // tf
    row_spec = pl.BlockSpec((tm, d), lambda i, j: (i, 0))
    vec_spec = pl.BlockSpec((1, d), lambda i, j: (0, 0))
    wg_spec = pl.BlockSpec((d, tf), lambda i, j: (0, j))
    wu_spec = pl.BlockSpec((d, tf), lambda i, j: (0, up_blk0 + j))
    wo_spec = pl.BlockSpec((tf, d), lambda i, j: (j, 0))
    cast_in_specs, cast_specs, cast_shapes = _cast_specs(cast_jobs)
    in_specs = [row_spec, pl.BlockSpec((None, N_MOD, d), lambda i, j: (group_of_tile(i), 0, 0)),
                vec_spec, vec_spec, wg_spec, wu_spec, wo_spec] + cast_in_specs
    out_specs, out_shape = [], []
    if cfg.emit_h:
        out_specs.append(row_spec)
        out_shape.append(jax.ShapeDtypeStruct((rows, d), F32))
    if cfg.emit_hx:
        out_specs.append(row_spec)
        out_shape.append(jax.ShapeDtypeStruct((rows, d), BF16))
    if cfg.cast_w:
        out_specs += [wg_spec, pl.BlockSpec((d, tf), lambda i, j: (0, j)), wo_spec]
        out_shape += [jax.ShapeDtypeStruct((d, n_ff), BF16), jax.ShapeDtypeStruct((d, n_ff), BF16),
                      jax.ShapeDtypeStruct((n_ff, d), BF16)]
    return pl.pallas_call(
        functools.partial(_ffn_kernel, cfg, len(cast_jobs)),
        grid=(rows // tm, n_j),
        in_specs=in_specs,
        out_specs=out_specs + cast_specs,
        out_shape=out_shape + cast_shapes,
        scratch_shapes=[pltpu.VMEM((tm, d), BF16)] + ([] if cfg.acc_in_h else [pltpu.VMEM((tm, d), F32)]),
        compiler_params=_params("parallel", "arbitrary"),
        name=name,
    )(x, mod, g_in, g_next, wg, wu, wo, *[job.array for job in cast_jobs])


def _wprep_kernel(w_ref, o_ref):
    o_ref[R_K:R_Q, :] = w_ref[0:OFF_GKF, :].astype(BF16)
    o_ref[R_Q:R_GK, :] = w_ref[OFF_Q:OFF_GLU + 2 * D_CONV, :].astype(BF16)
    o_ref[R_GK:R_GK + 2 * GATE_RANK, :] = w_ref[OFF_GKF:CTX_COLS, :].astype(BF16)
    o_ref[R_GK + 2 * GATE_RANK:R_END, :] = jnp.zeros((V7X_LANES - 2 * GATE_RANK, o_ref.shape[1]), BF16)


def _wprep_call(w_t):
    n, d = w_t.shape
    tc = WPREP_TC
    return pl.pallas_call(
        _wprep_kernel,
        grid=(d // tc,),
        in_specs=[pl.BlockSpec((n, tc), lambda i: (0, i))],
        out_specs=pl.BlockSpec((R_END, tc), lambda i: (0, i)),
        out_shape=jax.ShapeDtypeStruct((R_END, d), BF16),
        compiler_params=_params("parallel"),
        name="wprep",
    )(w_t)


def _log_decays(p_gk, w2_ref, b2_ref):
    z = jnp.dot(p_gk.astype(BF16), w2_ref[...], preferred_element_type=F32) + b2_ref[...]
    return _log_sigmoid(z) * (1.0 / GATE_NORMALIZER)


def _chunk_cumsum(x, reverse):
    n = x.shape[0]
    pos = lax.broadcasted_iota(jnp.int32, x.shape, 0) % CHUNK
    d = 1
    while d < CHUNK:
        if reverse:
            shifted = pltpu.roll(x, n - d, 0)
            x = x + jnp.where(pos < CHUNK - d, shifted, 0.0)
        else:
            shifted = pltpu.roll(x, d, 0)
            x = x + jnp.where(pos >= d, shifted, 0.0)
        d *= 2
    return x


def _chunk_rows(x, row_in_chunk):
    return jnp.concatenate([x[c0 + row_in_chunk:c0 + row_in_chunk + 1, :]
                            for c0 in range(0, x.shape[0], CHUNK)], axis=0)


def _chunk_bcast(rows_per_chunk):
    return jnp.concatenate([jnp.broadcast_to(rows_per_chunk[c:c + 1, :], (CHUNK, rows_per_chunk.shape[1]))
                            for c in range(rows_per_chunk.shape[0])], axis=0)


QKV_HEAD = 2 * GLA_DK + GLA_DV
LD_HEAD = 2 * GLA_DK


def _proj_kernel(n_cast, h_ref, mod_ref, g_ref, w_ref, w2_ref, b2_ref, *rest):
    cast_in, (qkv_ref, ld_ref, sg_ref, u_ref), cast_out = rest[:n_cast], rest[n_cast:n_cast + 4], rest[n_cast + 4:]
    _run_casts(cast_in, cast_out)
    hx = _modulate(h_ref[...], g_ref[...], mod_ref[3:4, :], mod_ref[4:5, :]).astype(BF16)

    def proj(lo, hi):
        return lax.dot_general(hx, w_ref[lo:hi, :], NT_DIMS, preferred_element_type=F32)

    ld = _log_decays(proj(R_GK, R_END), w2_ref, b2_ref)
    k = proj(R_K, R_V).astype(BF16)
    v = proj(R_V, R_Q).astype(BF16)
    q = (proj(R_Q, R_G) * (GLA_DK ** -0.5)).astype(BF16)
    for h in range(GLA_HEADS):
        dk = slice(h * GLA_DK, (h + 1) * GLA_DK)
        c0 = h * QKV_HEAD
        qkv_ref[:, c0:c0 + GLA_DK] = q[:, dk]
        qkv_ref[:, c0 + GLA_DK:c0 + 2 * GLA_DK] = k[:, dk]
        qkv_ref[:, c0 + 2 * GLA_DK:c0 + QKV_HEAD] = v[:, h * GLA_DV:(h + 1) * GLA_DV]
        ld_ref[:, h * LD_HEAD:h * LD_HEAD + GLA_DK] = ld[:, dk]
        ld_ref[:, h * LD_HEAD + GLA_DK:(h + 1) * LD_HEAD] = ld[:, GLA_KEY + h * GLA_DK:GLA_KEY + (h + 1) * GLA_DK]
    sg_ref[...] = _silu(proj(R_G, R_A)).astype(BF16)
    u_ref[...] = proj(R_A, R_B) * jax.nn.sigmoid(proj(R_B, R_GK))


def _proj_call(h, mod, group_of_tile, g_mix, w_r, w2pad, b2, cast_jobs=()):
    rows, d = h.shape
    tm = PROJ_TM
    row = lambda n: pl.BlockSpec((tm, n), lambda i: (i, 0))
    whole = lambda a: pl.BlockSpec(a.shape, lambda i: (0, 0), pipeline_mode=pl.Buffered(1))
    outs = [(GLA_HEADS * QKV_HEAD, BF16), (GLA_HEADS * LD_HEAD, F32), (D_GLA, BF16), (D_CONV, F32)]
    cast_in_specs, cast_specs, cast_shapes = _cast_specs(cast_jobs)
    return pl.pallas_call(
        functools.partial(_proj_kernel, len(cast_jobs)),
        grid=(rows // tm,),
        in_specs=[row(d), pl.BlockSpec((None, N_MOD, d), lambda i: (group_of_tile(i), 0, 0)),
                  pl.BlockSpec((1, d), lambda i: (0, 0)), whole(w_r), whole(w2pad), whole(b2)] + cast_in_specs,
        out_specs=[row(n) for n, _ in outs] + cast_specs,
        out_shape=[jax.ShapeDtypeStruct((rows, n), dt) for n, dt in outs] + cast_shapes,
        compiler_params=_params("parallel"),
        name="proj",
    )(h, mod, g_mix, w_r, w2pad, b2, *[job.array for job in cast_jobs])


def _ctx_kernel(hx_ref, wkv_ref, wgk_ref, w2_ref, b2_ref, sf_ref, sb_ref):
    hx = hx_ref[...]
    t = hx.shape[0]
    kv = lax.dot_general(hx, wkv_ref[...], NT_DIMS, preferred_element_type=F32)
    ld = _log_decays(lax.dot_general(hx, wgk_ref[...], NT_DIMS, preferred_element_type=F32),
                     w2_ref, b2_ref)
    r = lax.broadcasted_iota(jnp.int32, (t, t), 0)
    c = lax.broadcasted_iota(jnp.int32, (t, t), 1)
    hi = lax.Precision.HIGHEST
    e_f = jnp.dot((c > r).astype(F32), ld[:, :GLA_KEY], preferred_element_type=F32, precision=hi)
    e_b = jnp.dot((c < r).astype(F32), ld[:, GLA_KEY:], preferred_element_type=F32, precision=hi)
    for h in range(GLA_HEADS):
        ks = slice(h * GLA_DK, (h + 1) * GLA_DK)
        k = kv[:, R_K + h * GLA_DK:R_K + (h + 1) * GLA_DK]
        v = kv[:, R_V + h * GLA_DV:R_V + (h + 1) * GLA_DV].astype(BF16)
        sf_ref[h] = lax.dot_general(v, (k * jnp.exp(e_f[:, ks])).astype(BF16), TN_DIMS,
                                    preferred_element_type=F32)
        sb_ref[h] = lax.dot_general(v, (k * jnp.exp(e_b[:, ks])).astype(BF16), TN_DIMS,
                                    preferred_element_type=F32)


def _ctx_call(hxc, w_r, w2pad, b2, bsz, t):
    d = hxc.shape[1]
    const = lambda a: pl.BlockSpec(a.shape, lambda b: (0, 0))
    st = pl.BlockSpec((None, GLA_HEADS, GLA_DV, GLA_DK), lambda b: (b, 0, 0, 0))
    shape = jax.ShapeDtypeStruct((bsz, GLA_HEADS, GLA_DV, GLA_DK), F32)
    return pl.pallas_call(
        _ctx_kernel,
        grid=(bsz,),
        in_specs=[pl.BlockSpec((t, d), lambda b: (b, 0)),
                  pl.BlockSpec((R_Q, d), lambda b: (0, 0)),
                  pl.BlockSpec((V7X_LANES, d), lambda b: (R_GK // V7X_LANES, 0)),
                  const(w2pad), const(b2)],
        out_specs=[st, st],
        out_shape=[shape, shape],
        compiler_params=_params("parallel"),
        name="ctx",
    )(hxc, w_r, w_r, w2pad, b2)


def _gla_direction(qkv_ref, ld_ref, s_ref, reverse):
    n_sub = GLA_TB // GLA_SUB
    n_chunk = GLA_SUB // CHUNK
    rr = lax.broadcasted_iota(jnp.int32, (GLA_SUB, GLA_SUB), 0)
    cc = lax.broadcasted_iota(jnp.int32, (GLA_SUB, GLA_SUB), 1)
    same_chunk = (rr // CHUNK) == (cc // CHUNK)
    mask = same_chunk & ((cc >= rr) if reverse else (cc <= rr))
    mid_row = CHUNK // 2 if reverse else CHUNK // 2 - 1
    last_row = 0 if reverse else CHUNK - 1
    ld0 = GLA_DK if reverse else 0

    outs = [None] * n_sub
    subs = range(n_sub - 1, -1, -1) if reverse else range(n_sub)
    state = s_ref[...]
    for s in subs:
        rows = slice(s * GLA_SUB, (s + 1) * GLA_SUB)
        b = _chunk_cumsum(ld_ref[rows, ld0:ld0 + GLA_DK], reverse)
        b_mid = _chunk_bcast(_chunk_rows(b, mid_row))
        b_last_rows = _chunk_rows(b, last_row)
        b_last = _chunk_bcast(b_last_rows)
        q = qkv_ref[rows, 0:GLA_DK].astype(F32)
        k = qkv_ref[rows, GLA_DK:2 * GLA_DK].astype(F32)
        v = qkv_ref[rows, 2 * GLA_DK:QKV_HEAD]
        qs = (q * jnp.exp(b - b_mid)).astype(BF16)
        ks = (k * jnp.exp(b_mid - b)).astype(BF16)
        qi = (q * jnp.exp(b)).astype(BF16)
        kd = (k * jnp.exp(b_last - b)).astype(BF16)
        att = lax.dot_general(qs, ks, NT_DIMS, preferred_element_type=F32)
        att = jnp.where(mask, att, 0.0).astype(BF16)
        o_intra = jnp.dot(att, v, preferred_element_type=F32)
        decay = jnp.exp(b_last_rows)
        o_parts = [None] * n_chunk
        chunks = range(n_chunk - 1, -1, -1) if reverse else range(n_chunk)
        for c in chunks:
            cr = slice(c * CHUNK, (c + 1) * CHUNK)
            o_parts[c] = o_intra[cr, :] + lax.dot_general(
                qi[cr, :], state.astype(BF16), NT_DIMS, preferred_element_type=F32)
            kv = lax.dot_general(v[cr, :], kd[cr, :], TN_DIMS, preferred_element_type=F32)
            state = state * decay[c:c + 1, :] + kv
        outs[s] = jnp.concatenate(o_parts, axis=0)
    s_ref[...] = state
    return jnp.concatenate(outs, axis=0)


def _gla_kernel(n_cast, qkvf_ref, ldf_ref, qkvb_ref, ldb_ref, sg_ref, s0f_ref, s0b_ref, gain_ref,
                *rest):
    cast_in, o_ref, rest = rest[:n_cast], rest[n_cast], rest[n_cast + 1:]
    cast_out, (sf_ref, sb_ref, acc_ref) = rest[:n_cast], rest[n_cast:]
    nb = pl.program_id(2)
    n_blocks = pl.num_programs(2)

    @pl.when(nb == 0)
    def _():
        sf_ref[...] = s0f_ref[...]
        sb_ref[...] = s0b_ref[...]

    _run_casts(cast_in, cast_out)
    o_f = _gla_direction(qkvf_ref, ldf_ref, sf_ref, reverse=False)
    o_b = _gla_direction(qkvb_ref, ldb_ref, sb_ref, reverse=True)
    row_f = pl.multiple_of(nb * GLA_TB, GLA_TB)
    row_b = pl.multiple_of((n_blocks - 1 - nb) * GLA_TB, GLA_TB)

    @pl.when(nb < n_blocks // 2)
    def _():
        acc_ref[pl.ds(row_f, GLA_TB), :] = o_f
        acc_ref[pl.ds(row_b, GLA_TB), :] = o_b

    @pl.when(nb >= n_blocks // 2)
    def _():
        for row, part in ((row_f, o_f), (row_b, o_b)):
            o = acc_ref[pl.ds(row, GLA_TB), :] + part
            ms = jnp.mean(o * o, axis=-1, keepdims=True)
            o = o * lax.rsqrt(ms + HEAD_NORM_EPS) * gain_ref[...]
            o_ref[pl.ds(row, GLA_TB), :] = (o * sg_ref[pl.ds(row, GLA_TB), :].astype(F32)).astype(BF16)


def _gla_call(qkv, ld, sg, s0f, s0b, gain, bsz, t, cast_jobs=()):
    nb = t // GLA_TB
    fwd = lambda n: pl.BlockSpec((GLA_TB, n), lambda b, h, i: (b * nb + i, h))
    bwd = lambda n: pl.BlockSpec((GLA_TB, n), lambda b, h, i: (b * nb + nb - 1 - i, h))
    seq = pl.BlockSpec((t, GLA_DV), lambda b, h, i: (b, h))
    st = pl.BlockSpec((None, None, GLA_DV, GLA_DK), lambda b, h, i: (b, h, 0, 0))
    cast_in_specs, cast_specs, cast_shapes = _cast_specs(cast_jobs)
    return pl.pallas_call(
        functools.partial(_gla_kernel, len(cast_jobs)),
        grid=(bsz, GLA_HEADS, nb),
        in_specs=[fwd(QKV_HEAD), fwd(LD_HEAD), bwd(QKV_HEAD), bwd(LD_HEAD),
                  seq, st, st, pl.BlockSpec((1, GLA_DV), lambda b, h, i: (0, 0))] + cast_in_specs,
        out_specs=[seq] + cast_specs,
        out_shape=[jax.ShapeDtypeStruct((bsz * t, D_GLA), BF16)] + cast_shapes,
        scratch_shapes=[pltpu.VMEM((GLA_DV, GLA_DK), F32), pltpu.VMEM((GLA_DV, GLA_DK), F32),
                        pltpu.VMEM((t, GLA_DV), F32)],
        compiler_params=_params("parallel", "parallel", "arbitrary"),
        name="gla",
    )(qkv, ld, qkv, ld, sg, s0f, s0b, gain, *[job.array for job in cast_jobs])


ROW_PAD = 16
ROW_PITCH = GRID_W + 2 * ROW_PAD


def _conv_kernel(n_row_blocks, n_cast, u_ref, w_ref, b_ref, *rest):
    cast_in, y_ref, rest = rest[:n_cast], rest[n_cast], rest[n_cast + 1:]
    cast_out, (pad_ref,) = rest[:n_cast], rest[n_cast:]
    cb = pl.program_id(1)
    rows = u_ref.shape[0] // GRID_W
    bias = jnp.broadcast_to(b_ref[...], (GRID_W, CONV_CB))
    _run_casts(cast_in, cast_out)

    @pl.when(cb < n_row_blocks)
    def _():
        @pl.when(cb == 0)
        def _():
            pad_ref[...] = jnp.zeros_like(pad_ref)

        def fill(r, carry):
            src = pl.multiple_of(r * GRID_W, GRID_W)
            dst = pl.multiple_of(r * ROW_PITCH + ROW_PAD, 8)
            pad_ref[pl.ds(dst, GRID_W), :] = u_ref[pl.ds(src, GRID_W), :]
            return carry

        lax.fori_loop(0, rows, fill, 0)

        def body(r, carry):
            base = r * ROW_PITCH + (ROW_PAD - CONV_HALF)
            acc = bias
            for j in range(CONV_WIDTH):
                acc = acc + w_ref[j:j + 1, :] * pad_ref[pl.ds(base + j, GRID_W), :]
            y_ref[pl.ds(pl.multiple_of(r * GRID_W, GRID_W), GRID_W), :] = acc
            return carry

        lax.fori_loop(0, rows, body, 0, unroll=CONV_UNROLL)

    @pl.when(cb >= n_row_blocks)
    def _():
        edge = CONV_HALF * GRID_W

        @pl.when(cb == n_row_blocks)
        def _():
            pad_ref[0:edge, :] = jnp.zeros((edge, CONV_CB), F32)
            pad_ref[edge + rows * GRID_W:2 * edge + rows * GRID_W, :] = jnp.zeros((edge, CONV_CB), F32)

        pad_ref[edge:edge + rows * GRID_W, :] = u_ref[...]

        def body(r, carry):
            acc = bias
            for j in range(CONV_WIDTH):
                src = pl.multiple_of((r + j) * GRID_W, GRID_W)
                acc = acc + w_ref[j:j + 1, :] * pad_ref[pl.ds(src, GRID_W), :]
            y_ref[pl.ds(pl.multiple_of(r * GRID_W, GRID_W), GRID_W), :] = acc
            return carry

        lax.fori_loop(0, rows, body, 0, unroll=CONV_UNROLL)


def _conv_call(u, w, b, bsz, t, cast_jobs=()):
    ch = u.shape[1]
    rows = t // GRID_W
    n_cb = ch // CONV_CB
    pad_rows = max(rows * ROW_PITCH, (rows + 2 * CONV_HALF) * GRID_W)
    blk = pl.BlockSpec((t, CONV_CB), lambda bi, c: (bi, c))
    cast_in_specs, cast_specs, cast_shapes = _cast_specs(cast_jobs)
    return pl.pallas_call(
        functools.partial(_conv_kernel, n_cb // 2, len(cast_jobs)),
        grid=(bsz, n_cb),
        in_specs=[blk, pl.BlockSpec((CONV_WIDTH, CONV_CB), lambda bi, c: (0, c)),
                  pl.BlockSpec((1, CONV_CB), lambda bi, c: (0, c))] + cast_in_specs,
        out_specs=[blk] + cast_specs,
        out_shape=[jax.ShapeDtypeStruct(u.shape, F32)] + cast_shapes,
        scratch_shapes=[pltpu.VMEM((pad_rows, CONV_CB), F32)],
        compiler_params=_params("arbitrary", "arbitrary"),
        name="conv",
    )(u, w, b, *[job.array for job in cast_jobs])


def _outproj_kernel(n_cast, og_ref, y_ref, lng_ref, lnb_ref, w_ref, h_ref, mod_ref, *rest):
    cast_in, o_ref, cast_out = rest[:n_cast], rest[n_cast], rest[n_cast + 1:]
    _run_casts(cast_in, cast_out)
    y = y_ref[...]
    mu = jnp.mean(y, axis=-1, keepdims=True)
    yc = y - mu
    var = jnp.mean(yc * yc, axis=-1, keepdims=True)
    yn = yc * lax.rsqrt(var + LN_EPS) * lng_ref[...] + lnb_ref[...]
    oc = _silu(yn).astype(BF16)
    res = jnp.dot(og_ref[...], w_ref[0:D_GLA, :], preferred_element_type=F32)
    res = res + jnp.dot(oc, w_ref[D_GLA:D_GLA + D_CONV, :], preferred_element_type=F32)
    o_ref[...] = h_ref[...] + mod_ref[5:6, :] * res


def _outproj_call(og, y, ln_g, ln_b, w_out_bf, h, mod, group_of_tile, cast_jobs=()):
    rows, d = h.shape
    tm = OUT_TM
    row = lambda n: pl.BlockSpec((tm, n), lambda i: (i, 0))
    vec = lambda n: pl.BlockSpec((1, n), lambda i: (0, 0))
    cast_in_specs, cast_specs, cast_shapes = _cast_specs(cast_jobs)
    return pl.pallas_call(
        functools.partial(_outproj_kernel, len(cast_jobs)),
        grid=(rows // tm,),
        in_specs=[row(D_GLA), row(D_CONV), vec(D_CONV), vec(D_CONV),
                  pl.BlockSpec(w_out_bf.shape, lambda i: (0, 0), pipeline_mode=pl.Buffered(1)),
                  row(d), pl.BlockSpec((None, N_MOD, d), lambda i: (group_of_tile(i), 0, 0))] + cast_in_specs,
        out_specs=[row(d)] + cast_specs,
        out_shape=[jax.ShapeDtypeStruct((rows, d), F32)] + cast_shapes,
        compiler_params=_params("parallel"),
        name="outproj",
    )(og, y, ln_g, ln_b, w_out_bf, h, mod, *[job.array for job in cast_jobs])


def kernel(x, c, ctx, c_ctx, w_mod, b_mod, norm_ffn1, w_ffn1_in, w_ffn1_out, norm_mix, w_in, w_gk2, b_gk2, gla_norm, conv_w, conv_b, conv_ln_g, conv_ln_b, w_out, norm_ffn2, w_ffn2_in, w_ffn2_out, norm_final):
    bsz, t, d = x.shape
    t_ctx = ctx.shape[1]
    assert w_mod.shape[0] == 1, "single layer only"
    assert t % GLA_TB == 0 and (t // GLA_TB) % 2 == 0 and t % FFN_TALL_TM == 0 and t % PROJ_TM == 0
    vec = lambda a: a.reshape(1, -1)

    w2pad = jnp.zeros((V7X_LANES, 2 * GLA_KEY), BF16)
    w2pad = w2pad.at[:GATE_RANK, :GLA_KEY].set(w_gk2[0, 0].astype(BF16))
    w2pad = w2pad.at[GATE_RANK:2 * GATE_RANK, GLA_KEY:].set(w_gk2[0, 1].astype(BF16))
    b2 = b_gk2[0].reshape(1, 2 * GLA_KEY)
    w_r = _wprep_call(jnp.swapaxes(w_in[0], 0, 1))

    n_rows = 8
    s_in = jnp.concatenate([c, c_ctx[None, :], jnp.zeros((n_rows - bsz - 1, d), F32)], axis=0)
    mod = _mod_call(s_in, w_mod[0], vec(b_mod[0])).reshape(n_rows, N_MOD, d)

    batch_of = lambda tile_rows: (lambda i: i // (t // tile_rows))
    ctx_group = lambda i: bsz
    g1, gm = vec(norm_ffn1[0]), vec(norm_mix[0])

    xc = ctx.reshape(bsz * t_ctx, d)
    hxc, wg1, wu1, wo1 = _ffn_call(
        xc, mod, ctx_group, g1, gm, w_ffn1_in[0], w_ffn1_in[0], w_ffn1_out[0], D_FF,
        _FfnCfg(mod_row=0, emit_h=False, emit_hx=True, final_norm=False, cast_w=True), "ffn1_ctx")
    s0f, s0b = _ctx_call(hxc, w_r, w2pad, b2, bsz, t_ctx)

    xl = x.reshape(bsz * t, d)
    (h1,) = _ffn_call(
        xl, mod, batch_of(FFN_TALL_TM), g1, gm, wg1, wu1, wo1, 0,
        _FfnCfg(mod_row=0, emit_h=True, emit_hx=False, final_norm=False, cast_w=False, tm=FFN_TALL_TM,
                acc_in_h=True), "ffn1")

    n_proj, n_outp = (bsz * t) // PROJ_TM, (bsz * t) // OUT_TM
    half_job = lambda n, col: _CastJob(w_ffn2_in[0], (d // n, D_FF), lambda i: (i, col),
                                       (d, D_FF), lambda i: (i, 0))
    qkv, ld, sg, u, wg2 = _proj_call(h1, mod, batch_of(PROJ_TM), gm, w_r, w2pad, b2,
                                     (half_job(n_proj, 0),))
    (og,) = _gla_call(qkv, ld, sg, s0f, s0b, vec(gla_norm[0]), bsz, t)
    n_conv_steps = bsz * (D_CONV // CONV_CB)
    step = lambda bi, cb: (bi * (D_CONV // CONV_CB) + cb, 0)
    y, wf2_out, w_out_bf = _conv_call(
        u, conv_w[0], vec(conv_b[0]), bsz, t,
        (_CastJob(w_ffn2_out[0], (D_FF // n_conv_steps, d), step),
         _CastJob(w_out[0], (d // n_conv_steps, d), step)))
    h2, wu2 = _outproj_call(
        og, y, vec(conv_ln_g[0]), vec(conv_ln_b[0]), w_out_bf, h1, mod, batch_of(OUT_TM),
        (half_job(n_outp, 1),))

    (out,) = _ffn_call(
        h2, mod, batch_of(FFN_TALL_TM), vec(norm_ffn2[0]), vec(norm_final), wg2, wu2, wf2_out, 0,
        _FfnCfg(mod_row=6, emit_h=True, emit_hx=False, final_norm=True, cast_w=False, tm=FFN_TALL_TM,
                acc_in_h=True), "ffn2")
    return out.reshape(bsz, t, d)
```

```python
import functools
from typing import NamedTuple

import jax
import jax.numpy as jnp
from jax import lax
from jax.experimental import pallas as pl
from jax.experimental.pallas import tpu as pltpu

F32 = jnp.float32
BF16 = jnp.bfloat16

D_MODEL = 2048
GRID_W = 64
GLA_HEADS = 4
GLA_DK = 128
GLA_DV = 256
GLA_KEY = GLA_HEADS * GLA_DK
D_GLA = GLA_HEADS * GLA_DV
D_CONV = 1024
GATE_RANK = 16
GATE_NORMALIZER = 16.0
CHUNK = 64
CONV_WIDTH = 31
CONV_HALF = CONV_WIDTH // 2
D_FF = 5632
N_MOD = 9
RMS_EPS = 1e-6
HEAD_NORM_EPS = 1e-5
LN_EPS = 1e-5

OFF_V = GLA_KEY
OFF_GKF = OFF_V + D_GLA
CTX_COLS = OFF_GKF + 2 * GATE_RANK
OFF_Q = CTX_COLS
OFF_G = OFF_Q + GLA_KEY
OFF_GLU = OFF_G + D_GLA

V7X_LANES = 128
R_K = 0
R_V = R_K + GLA_KEY
R_Q = R_V + D_GLA
R_G = R_Q + GLA_KEY
R_A = R_G + D_GLA
R_B = R_A + D_CONV
R_GK = R_B + D_CONV
R_END = R_GK + V7X_LANES

V7X_VMEM_SCOPED_LIMIT_BYTES = 60000 * 1024
V7X_BF16_SUBLANES = 16

FFN_TM = 512
FFN_TALL_TM = 1024
FFN_TF = 512
FFN_TF_CAST = 256
PROJ_TM = 512
OUT_TM = 512
MOD_TN = 2048
GLA_TB = 2048
GLA_SUB = 256
CONV_CB = 128
CONV_UNROLL = 8
WPREP_TC = 256

NT_DIMS = (((1,), (1,)), ((), ()))
TN_DIMS = (((0,), (0,)), ((), ()))


def _params(*semantics):
    return pltpu.CompilerParams(dimension_semantics=semantics,
                                vmem_limit_bytes=V7X_VMEM_SCOPED_LIMIT_BYTES)


def _silu(x):
    return x * jax.nn.sigmoid(x)


def _rmsnorm_rows(x, gain, eps):
    ms = jnp.mean(x * x, axis=-1, keepdims=True)
    return x * lax.rsqrt(ms + eps) * gain


def _modulate(x, gain, shift, scale):
    return _rmsnorm_rows(x, gain, RMS_EPS) * (1.0 + scale) + shift


def _log_sigmoid(z):
    return jnp.minimum(z, 0.0) - jnp.log1p(jnp.exp(-jnp.abs(z)))


class _CastJob(NamedTuple):
    array: jax.Array
    block: tuple
    index_map: object
    out_shape: tuple = None
    out_index_map: object = None


def _cast_in_specs(jobs):
    for job in jobs:
        assert all(n % b == 0 for n, b in zip(job.array.shape, job.block))
        assert job.block[0] % V7X_BF16_SUBLANES == 0 and job.block[1] % V7X_LANES == 0
    return [pl.BlockSpec(job.block, job.index_map) for job in jobs]


def _cast_out_specs(jobs):
    specs = [pl.BlockSpec(job.block, job.out_index_map or job.index_map) for job in jobs]
    shapes = [jax.ShapeDtypeStruct(job.out_shape or job.array.shape, BF16) for job in jobs]
    return specs, shapes


def _cast_specs(jobs):
    return (_cast_in_specs(jobs),) + _cast_out_specs(jobs)


def _run_casts(src_refs, dst_refs):
    for src_ref, dst_ref in zip(src_refs, dst_refs):
        dst_ref[...] = src_ref[...].astype(dst_ref.dtype)


def _mod_kernel(s_ref, w_ref, b_ref, o_ref):
    s = _silu(s_ref[...]).astype(BF16)
    o_ref[...] = jnp.dot(s, w_ref[...].astype(BF16), preferred_element_type=F32) + b_ref[...]


def _mod_call(s_in, w_mod, b_mod):
    rows, d = s_in.shape
    n = w_mod.shape[1]
    return pl.pallas_call(
        _mod_kernel,
        grid=(n // MOD_TN,),
        in_specs=[
            pl.BlockSpec((rows, d), lambda j: (0, 0)),
            pl.BlockSpec((d, MOD_TN), lambda j: (0, j)),
            pl.BlockSpec((1, MOD_TN), lambda j: (0, j)),
        ],
        out_specs=pl.BlockSpec((rows, MOD_TN), lambda j: (0, j)),
        out_shape=jax.ShapeDtypeStruct((rows, n), F32),
        compiler_params=_params("arbitrary"),
        name="mod",
    )(s_in, w_mod, b_mod)


class _FfnCfg(NamedTuple):
    mod_row: int
    emit_h: bool
    emit_hx: bool
    final_norm: bool
    cast_w: bool
    tm: int = FFN_TM
    acc_in_h: bool = False


def _ffn_kernel(cfg, n_cast, x_ref, mod_ref, g_in_ref, g_next_ref, wg_ref, wu_ref, wo_ref, *rest):
    cast_in, rest = rest[:n_cast], rest[n_cast:]
    outs = []
    for flag in (cfg.emit_h, cfg.emit_hx, cfg.cast_w, cfg.cast_w, cfg.cast_w):
        outs.append(rest[0] if flag else None)
        rest = rest[1:] if flag else rest
    h_ref, hx_ref, wg_bf_ref, wu_bf_ref, wo_bf_ref = outs
    cast_out, scratch = rest[:n_cast], rest[n_cast:]
    hm_ref = scratch[0]
    acc_ref = h_ref if cfg.acc_in_h else scratch[1]
    j = pl.program_id(1)
    last = pl.num_programs(1) - 1
    r0 = cfg.mod_row
    _run_casts(cast_in, cast_out)

    def partial_out(hm):
        wg, wu, wo = wg_ref[...], wu_ref[...], wo_ref[...]
        if cfg.cast_w:
            wg, wu, wo = wg.astype(BF16), wu.astype(BF16), wo.astype(BF16)
            wg_bf_ref[...] = wg
            wu_bf_ref[...] = wu
            wo_bf_ref[...] = wo
        gate = jnp.dot(hm, wg, preferred_element_type=F32)
        up = jnp.dot(hm, wu, preferred_element_type=F32)
        act = (_silu(gate) * up).astype(BF16)
        return jnp.dot(act, wo, preferred_element_type=F32)

    @pl.when(j == 0)
    def _():
        hm = _modulate(x_ref[...], g_in_ref[...], mod_ref[r0:r0 + 1, :], mod_ref[r0 + 1:r0 + 2, :])
        hm = hm.astype(BF16)
        hm_ref[...] = hm
        acc_ref[...] = partial_out(hm)

    @pl.when((j > 0) & (j < last))
    def _():
        acc_ref[...] += partial_out(hm_ref[...])

    @pl.when(j == last)
    def _():
        g = mod_ref[r0 + 2:r0 + 3, :]
        h = x_ref[...] + (0.5 * g) * (acc_ref[...] + partial_out(hm_ref[...]))
        if cfg.emit_hx:
            hx = _modulate(h, g_next_ref[...], mod_ref[r0 + 3:r0 + 4, :], mod_ref[r0 + 4:r0 + 5, :])
            hx_ref[...] = hx.astype(BF16)
        if cfg.emit_h:
            h_ref[...] = _rmsnorm_rows(h, g_next_ref[...], RMS_EPS) if cfg.final_norm else h


def _ffn_call(x, mod, group_of_tile, g_in, g_next, wg, wu, wo, up_col0, cfg, name, cast_jobs=()):
    rows, d = x.shape
    n_ff = wo.shape[0]
    tf = FFN_TF_CAST if cfg.cast_w else FFN_TF
    n_j = n_ff // tf
    assert n_j >= 2, "first and last hidden block must be distinct grid steps"
    tm = cfg.tm
    assert not cfg.cast_w or rows == tm, "bf16 weight copies are written by one row tile only"
    assert cfg.emit_h or not cfg.acc_in_h
    up_blk0 = up_col0 // tf
    row_spec = pl.BlockSpec((tm, d), lambda i, j: (i, 0))
    vec_spec = pl.BlockSpec((1, d), lambda i, j: (0, 0))
    wg_spec = pl.BlockSpec((d, tf), lambda i, j: (0, j))
    wu_spec = pl.BlockSpec((d, tf), lambda i, j: (0, up_blk0 + j))
    wo_spec = pl.BlockSpec((tf, d), lambda i, j: (j, 0))
    cast_in_specs, cast_specs, cast_shapes = _cast_specs(cast_jobs)
    in_specs = [row_spec, pl.BlockSpec((None, N_MOD, d), lambda i, j: (group_of_tile(i), 0, 0)),
                vec_spec, vec_spec, wg_spec, wu_spec, wo_spec] + cast_in_specs
    out_specs, out_shape = [], []
    if cfg.emit_h:
        out_specs.append(row_spec)
        out_shape.append(jax.ShapeDtypeStruct((rows, d), F32))
    if cfg.emit_hx:
        out_specs.append(row_spec)
        out_shape.append(jax.ShapeDtypeStruct((rows, d), BF16))
    if cfg.cast_w:
        out_specs += [wg_spec, pl.BlockSpec((d, tf), lambda i, j: (0, j)), wo_spec]
        out_shape += [jax.ShapeDtypeStruct((d, n_ff), BF16), jax.ShapeDtypeStruct((d, n_ff), BF16),
                      jax.ShapeDtypeStruct((n_ff, d), BF16)]
    return pl.pallas_call(
        functools.partial(_ffn_kernel, cfg, len(cast_jobs)),
        grid=(rows // tm, n_j),
        in_specs=in_specs,
        out_specs=out_specs + cast_specs,
        out_shape=out_shape + cast_shapes,
        scratch_shapes=[pltpu.VMEM((tm, d), BF16)] + ([] if cfg.acc_in_h else [pltpu.VMEM((tm, d), F32)]),
        compiler_params=_params("parallel", "arbitrary"),
        name=name,
    )(x, mod, g_in, g_next, wg, wu, wo, *[job.array for job in cast_jobs])


def _wprep_kernel(w_ref, o_ref):
    o_ref[R_K:R_Q, :] = w_ref[0:OFF_GKF, :].astype(BF16)
    o_ref[R_Q:R_GK, :] = w_ref[OFF_Q:OFF_GLU + 2 * D_CONV, :].astype(BF16)
    o_ref[R_GK:R_GK + 2 * GATE_RANK, :] = w_ref[OFF_GKF:CTX_COLS, :].astype(BF16)
    o_ref[R_GK + 2 * GATE_RANK:R_END, :] = jnp.zeros((V7X_LANES - 2 * GATE_RANK, o_ref.shape[1]), BF16)


def _wprep_call(w_t):
    n, d = w_t.shape
    tc = WPREP_TC
    return pl.pallas_call(
        _wprep_kernel,
        grid=(d // tc,),
        in_specs=[pl.BlockSpec((n, tc), lambda i: (0, i))],
        out_specs=pl.BlockSpec((R_END, tc), lambda i: (0, i)),
        out_shape=jax.ShapeDtypeStruct((R_END, d), BF16),
        compiler_params=_params("parallel"),
        name="wprep",
    )(w_t)


def _log_decays(p_gk, w2_ref, b2_ref):
    z = jnp.dot(p_gk.astype(BF16), w2_ref[...], preferred_element_type=F32) + b2_ref[...]
    return _log_sigmoid(z) * (1.0 / GATE_NORMALIZER)


def _chunk_cumsum(x, reverse):
    n = x.shape[0]
    pos = lax.broadcasted_iota(jnp.int32, x.shape, 0) % CHUNK
    d = 1
    while d < CHUNK:
        if reverse:
            shifted = pltpu.roll(x, n - d, 0)
            x = x + jnp.where(pos < CHUNK - d, shifted, 0.0)
        else:
            shifted = pltpu.roll(x, d, 0)
            x = x + jnp.where(pos >= d, shifted, 0.0)
        d *= 2
    return x


def _chunk_rows(x, row_in_chunk):
    return jnp.concatenate([x[c0 + row_in_chunk:c0 + row_in_chunk + 1, :]
                            for c0 in range(0, x.shape[0], CHUNK)], axis=0)


def _chunk_bcast(rows_per_chunk):
    return jnp.concatenate([jnp.broadcast_to(rows_per_chunk[c:c + 1, :], (CHUNK, rows_per_chunk.shape[1]))
                            for c in range(rows_per_chunk.shape[0])], axis=0)


QKV_HEAD = 2 * GLA_DK + GLA_DV
LD_HEAD = 2 * GLA_DK


def _proj_kernel(n_cast, h_ref, mod_ref, g_ref, w_ref, w2_ref, b2_ref, *rest):
    cast_in, (qkv_ref, ld_ref, sg_ref, u_ref), cast_out = rest[:n_cast], rest[n_cast:n_cast + 4], rest[n_cast + 4:]
    _run_casts(cast_in, cast_out)
    hx = _modulate(h_ref[...], g_ref[...], mod_ref[3:4, :], mod_ref[4:5, :]).astype(BF16)

    def proj(lo, hi):
        return lax.dot_general(hx, w_ref[lo:hi, :], NT_DIMS, preferred_element_type=F32)

    ld = _log_decays(proj(R_GK, R_END), w2_ref, b2_ref)
    k = proj(R_K, R_V).astype(BF16)
    v = proj(R_V, R_Q).astype(BF16)
    q = (proj(R_Q, R_G) * (GLA_DK ** -0.5)).astype(BF16)
    for h in range(GLA_HEADS):
        dk = slice(h * GLA_DK, (h + 1) * GLA_DK)
        c0 = h * QKV_HEAD
        qkv_ref[:, c0:c0 + GLA_DK] = q[:, dk]
        qkv_ref[:, c0 + GLA_DK:c0 + 2 * GLA_DK] = k[:, dk]
        qkv_ref[:, c0 + 2 * GLA_DK:c0 + QKV_HEAD] = v[:, h * GLA_DV:(h + 1) * GLA_DV]
        ld_ref[:, h * LD_HEAD:h * LD_HEAD + GLA_DK] = ld[:, dk]
        ld_ref[:, h * LD_HEAD + GLA_DK:(h + 1) * LD_HEAD] = ld[:, GLA_KEY + h * GLA_DK:GLA_KEY + (h + 1) * GLA_DK]
    sg_ref[...] = _silu(proj(R_G, R_A)).astype(BF16)
    u_ref[...] = proj(R_A, R_B) * jax.nn.sigmoid(proj(R_B, R_GK))


def _proj_call(h, mod, group_of_tile, g_mix, w_r, w2pad, b2, cast_jobs=()):
    rows, d = h.shape
    tm = PROJ_TM
    row = lambda n: pl.BlockSpec((tm, n), lambda i: (i, 0))
    whole = lambda a: pl.BlockSpec(a.shape, lambda i: (0, 0), pipeline_mode=pl.Buffered(1))
    outs = [(GLA_HEADS * QKV_HEAD, BF16), (GLA_HEADS * LD_HEAD, F32), (D_GLA, BF16), (D_CONV, F32)]
    cast_in_specs, cast_specs, cast_shapes = _cast_specs(cast_jobs)
    return pl.pallas_call(
        functools.partial(_proj_kernel, len(cast_jobs)),
        grid=(rows // tm,),
        in_specs=[row(d), pl.BlockSpec((None, N_MOD, d), lambda i: (group_of_tile(i), 0, 0)),
                  pl.BlockSpec((1, d), lambda i: (0, 0)), whole(w_r), whole(w2pad), whole(b2)] + cast_in_specs,
        out_specs=[row(n) for n, _ in outs] + cast_specs,
        out_shape=[jax.ShapeDtypeStruct((rows, n), dt) for n, dt in outs] + cast_shapes,
        compiler_params=_params("parallel"),
        name="proj",
    )(h, mod, g_mix, w_r, w2pad, b2, *[job.array for job in cast_jobs])


def _ctx_kernel(hx_ref, wkv_ref, wgk_ref, w2_ref, b2_ref, sf_ref, sb_ref):
    hx = hx_ref[...]
    t = hx.shape[0]
    kv = lax.dot_general(hx, wkv_ref[...], NT_DIMS, preferred_element_type=F32)
    ld = _log_decays(lax.dot_general(hx, wgk_ref[...], NT_DIMS, preferred_element_type=F32),
                     w2_ref, b2_ref)
    r = lax.broadcasted_iota(jnp.int32, (t, t), 0)
    c = lax.broadcasted_iota(jnp.int32, (t, t), 1)
    hi = lax.Precision.HIGHEST
    e_f = jnp.dot((c > r).astype(F32), ld[:, :GLA_KEY], preferred_element_type=F32, precision=hi)
    e_b = jnp.dot((c < r).astype(F32), ld[:, GLA_KEY:], preferred_element_type=F32, precision=hi)
    for h in range(GLA_HEADS):
        ks = slice(h * GLA_DK, (h + 1) * GLA_DK)
        k = kv[:, R_K + h * GLA_DK:R_K + (h + 1) * GLA_DK]
        v = kv[:, R_V + h * GLA_DV:R_V + (h + 1) * GLA_DV].astype(BF16)
        sf_ref[h] = lax.dot_general(v, (k * jnp.exp(e_f[:, ks])).astype(BF16), TN_DIMS,
                                    preferred_element_type=F32)
        sb_ref[h] = lax.dot_general(v, (k * jnp.exp(e_b[:, ks])).astype(BF16), TN_DIMS,
                                    preferred_element_type=F32)


def _ctx_call(hxc, w_r, w2pad, b2, bsz, t):
    d = hxc.shape[1]
    const = lambda a: pl.BlockSpec(a.shape, lambda b: (0, 0))
    st = pl.BlockSpec((None, GLA_HEADS, GLA_DV, GLA_DK), lambda b: (b, 0, 0, 0))
    shape = jax.ShapeDtypeStruct((bsz, GLA_HEADS, GLA_DV, GLA_DK), F32)
    return pl.pallas_call(
        _ctx_kernel,
        grid=(bsz,),
        in_specs=[pl.BlockSpec((t, d), lambda b: (b, 0)),
                  pl.BlockSpec((R_Q, d), lambda b: (0, 0)),
                  pl.BlockSpec((V7X_LANES, d), lambda b: (R_GK // V7X_LANES, 0)),
                  const(w2pad), const(b2)],
        out_specs=[st, st],
        out_shape=[shape, shape],
        compiler_params=_params("parallel"),
        name="ctx",
    )(hxc, w_r, w_r, w2pad, b2)


def _gla_direction(qkv_ref, ld_ref, s_ref, reverse):
    n_sub = GLA_TB // GLA_SUB
    n_chunk = GLA_SUB // CHUNK
    rr = lax.broadcasted_iota(jnp.int32, (GLA_SUB, GLA_SUB), 0)
    cc = lax.broadcasted_iota(jnp.int32, (GLA_SUB, GLA_SUB), 1)
    same_chunk = (rr // CHUNK) == (cc // CHUNK)
    mask = same_chunk & ((cc >= rr) if reverse else (cc <= rr))
    mid_row = CHUNK // 2 if reverse else CHUNK // 2 - 1
    last_row = 0 if reverse else CHUNK - 1
    ld0 = GLA_DK if reverse else 0

    outs = [None] * n_sub
    subs = range(n_sub - 1, -1, -1) if reverse else range(n_sub)
    state = s_ref[...]
    for s in subs:
        rows = slice(s * GLA_SUB, (s + 1) * GLA_SUB)
        b = _chunk_cumsum(ld_ref[rows, ld0:ld0 + GLA_DK], reverse)
        b_mid = _chunk_bcast(_chunk_rows(b, mid_row))
        b_last_rows = _chunk_rows(b, last_row)
        b_last = _chunk_bcast(b_last_rows)
        q = qkv_ref[rows, 0:GLA_DK].astype(F32)
        k = qkv_ref[rows, GLA_DK:2 * GLA_DK].astype(F32)
        v = qkv_ref[rows, 2 * GLA_DK:QKV_HEAD]
        qs = (q * jnp.exp(b - b_mid)).astype(BF16)
        ks = (k * jnp.exp(b_mid - b)).astype(BF16)
        qi = (q * jnp.exp(b)).astype(BF16)
        kd = (k * jnp.exp(b_last - b)).astype(BF16)
        att = lax.dot_general(qs, ks, NT_DIMS, preferred_element_type=F32)
        att = jnp.where(mask, att, 0.0).astype(BF16)
        o_intra = jnp.dot(att, v, preferred_element_type=F32)
        decay = jnp.exp(b_last_rows)
        o_parts = [None] * n_chunk
        chunks = range(n_chunk - 1, -1, -1) if reverse else range(n_chunk)
        for c in chunks:
            cr = slice(c * CHUNK, (c + 1) * CHUNK)
            o_parts[c] = o_intra[cr, :] + lax.dot_general(
                qi[cr, :], state.astype(BF16), NT_DIMS, preferred_element_type=F32)
            kv = lax.dot_general(v[cr, :], kd[cr, :], TN_DIMS, preferred_element_type=F32)
            state = state * decay[c:c + 1, :] + kv
        outs[s] = jnp.concatenate(o_parts, axis=0)
    s_ref[...] = state
    return jnp.concatenate(outs, axis=0)


def _gla_kernel(n_cast, qkvf_ref, ldf_ref, qkvb_ref, ldb_ref, sg_ref, s0f_ref, s0b_ref, gain_ref,
                *rest):
    cast_in, o_ref, rest = rest[:n_cast], rest[n_cast], rest[n_cast + 1:]
    cast_out, (sf_ref, sb_ref, acc_ref) = rest[:n_cast], rest[n_cast:]
    nb = pl.program_id(2)
    n_blocks = pl.num_programs(2)

    @pl.when(nb == 0)
    def _():
        sf_ref[...] = s0f_ref[...]
        sb_ref[...] = s0b_ref[...]

    _run_casts(cast_in, cast_out)
    o_f = _gla_direction(qkvf_ref, ldf_ref, sf_ref, reverse=False)
    o_b = _gla_direction(qkvb_ref, ldb_ref, sb_ref, reverse=True)
    row_f = pl.multiple_of(nb * GLA_TB, GLA_TB)
    row_b = pl.multiple_of((n_blocks - 1 - nb) * GLA_TB, GLA_TB)

    @pl.when(nb < n_blocks // 2)
    def _():
        acc_ref[pl.ds(row_f, GLA_TB), :] = o_f
        acc_ref[pl.ds(row_b, GLA_TB), :] = o_b

    @pl.when(nb >= n_blocks // 2)
    def _():
        for row, part in ((row_f, o_f), (row_b, o_b)):
            o = acc_ref[pl.ds(row, GLA_TB), :] + part
            ms = jnp.mean(o * o, axis=-1, keepdims=True)
            o = o * lax.rsqrt(ms + HEAD_NORM_EPS) * gain_ref[...]
            o_ref[pl.ds(row, GLA_TB), :] = (o * sg_ref[pl.ds(row, GLA_TB), :].astype(F32)).astype(BF16)


def _gla_call(qkv, ld, sg, s0f, s0b, gain, bsz, t, cast_jobs=()):
    nb = t // GLA_TB
    fwd = lambda n: pl.BlockSpec((GLA_TB, n), lambda b, h, i: (b * nb + i, h))
    bwd = lambda n: pl.BlockSpec((GLA_TB, n), lambda b, h, i: (b * nb + nb - 1 - i, h))
    seq = pl.BlockSpec((t, GLA_DV), lambda b, h, i: (b, h))
    st = pl.BlockSpec((None, None, GLA_DV, GLA_DK), lambda b, h, i: (b, h, 0, 0))
    cast_in_specs, cast_specs, cast_shapes = _cast_specs(cast_jobs)
    return pl.pallas_call(
        functools.partial(_gla_kernel, len(cast_jobs)),
        grid=(bsz, GLA_HEADS, nb),
        in_specs=[fwd(QKV_HEAD), fwd(LD_HEAD), bwd(QKV_HEAD), bwd(LD_HEAD),
                  seq, st, st, pl.BlockSpec((1, GLA_DV), lambda b, h, i: (0, 0))] + cast_in_specs,
        out_specs=[seq] + cast_specs,
        out_shape=[jax.ShapeDtypeStruct((bsz * t, D_GLA), BF16)] + cast_shapes,
        scratch_shapes=[pltpu.VMEM((GLA_DV, GLA_DK), F32), pltpu.VMEM((GLA_DV, GLA_DK), F32),
                        pltpu.VMEM((t, GLA_DV), F32)],
        compiler_params=_params("parallel", "parallel", "arbitrary"),
        name="gla",
    )(qkv, ld, qkv, ld, sg, s0f, s0b, gain, *[job.array for job in cast_jobs])


ROW_PAD = 16
ROW_PITCH = GRID_W + 2 * ROW_PAD


def _conv_kernel(n_row_blocks, n_cast, u_ref, w_ref, b_ref, *rest):
    cast_in, y_ref, rest = rest[:n_cast], rest[n_cast], rest[n_cast + 1:]
    cast_out, (pad_ref,) = rest[:n_cast], rest[n_cast:]
    cb = pl.program_id(1)
    rows = u_ref.shape[0] // GRID_W
    bias = jnp.broadcast_to(b_ref[...], (GRID_W, CONV_CB))
    _run_casts(cast_in, cast_out)

    @pl.when(cb < n_row_blocks)
    def _():
        @pl.when(cb == 0)
        def _():
            pad_ref[...] = jnp.zeros_like(pad_ref)

        def fill(r, carry):
            src = pl.multiple_of(r * GRID_W, GRID_W)
            dst = pl.multiple_of(r * ROW_PITCH + ROW_PAD, 8)
            pad_ref[pl.ds(dst, GRID_W), :] = u_ref[pl.ds(src, GRID_W), :]
            return carry

        lax.fori_loop(0, rows, fill, 0)

        def body(r, carry):
            base = r * ROW_PITCH + (ROW_PAD - CONV_HALF)
            acc = bias
            for j in range(CONV_WIDTH):
                acc = acc + w_ref[j:j + 1, :] * pad_ref[pl.ds(base + j, GRID_W), :]
            y_ref[pl.ds(pl.multiple_of(r * GRID_W, GRID_W), GRID_W), :] = acc
            return carry

        lax.fori_loop(0, rows, body, 0, unroll=CONV_UNROLL)

    @pl.when(cb >= n_row_blocks)
    def _():
        edge = CONV_HALF * GRID_W

        @pl.when(cb == n_row_blocks)
        def _():
            pad_ref[0:edge, :] = jnp.zeros((edge, CONV_CB), F32)
            pad_ref[edge + rows * GRID_W:2 * edge + rows * GRID_W, :] = jnp.zeros((edge, CONV_CB), F32)

        pad_ref[edge:edge + rows * GRID_W, :] = u_ref[...]

        def body(r, carry):
            acc = bias
            for j in range(CONV_WIDTH):
                src = pl.multiple_of((r + j) * GRID_W, GRID_W)
                acc = acc + w_ref[j:j + 1, :] * pad_ref[pl.ds(src, GRID_W), :]
            y_ref[pl.ds(pl.multiple_of(r * GRID_W, GRID_W), GRID_W), :] = acc
            return carry

        lax.fori_loop(0, rows, body, 0, unroll=CONV_UNROLL)


def _conv_call(u, w, b, bsz, t, cast_jobs=()):
    ch = u.shape[1]
    rows = t // GRID_W
    n_cb = ch // CONV_CB
    pad_rows = max(rows * ROW_PITCH, (rows + 2 * CONV_HALF) * GRID_W)
    blk = pl.BlockSpec((t, CONV_CB), lambda bi, c: (bi, c))
    cast_in_specs, cast_specs, cast_shapes = _cast_specs(cast_jobs)
    return pl.pallas_call(
        functools.partial(_conv_kernel, n_cb // 2, len(cast_jobs)),
        grid=(bsz, n_cb),
        in_specs=[blk, pl.BlockSpec((CONV_WIDTH, CONV_CB), lambda bi, c: (0, c)),
                  pl.BlockSpec((1, CONV_CB), lambda bi, c: (0, c))] + cast_in_specs,
        out_specs=[blk] + cast_specs,
        out_shape=[jax.ShapeDtypeStruct(u.shape, F32)] + cast_shapes,
        scratch_shapes=[pltpu.VMEM((pad_rows, CONV_CB), F32)],
        compiler_params=_params("arbitrary", "arbitrary"),
        name="conv",
    )(u, w, b, *[job.array for job in cast_jobs])


def _outproj_kernel(n_cast, og_ref, y_ref, lng_ref, lnb_ref, w_ref, h_ref, mod_ref, *rest):
    cast_in, o_ref, cast_out = rest[:n_cast], rest[n_cast], rest[n_cast + 1:]
    _run_casts(cast_in, cast_out)
    y = y_ref[...]
    mu = jnp.mean(y, axis=-1, keepdims=True)
    yc = y - mu
    var = jnp.mean(yc * yc, axis=-1, keepdims=True)
    yn = yc * lax.rsqrt(var + LN_EPS) * lng_ref[...] + lnb_ref[...]
    oc = _silu(yn).astype(BF16)
    res = jnp.dot(og_ref[...], w_ref[0:D_GLA, :], preferred_element_type=F32)
    res = res + jnp.dot(oc, w_ref[D_GLA:D_GLA + D_CONV, :], preferred_element_type=F32)
    o_ref[...] = h_ref[...] + mod_ref[5:6, :] * res


def _outproj_call(og, y, ln_g, ln_b, w_out_bf, h, mod, group_of_tile, cast_jobs=()):
    rows, d = h.shape
    tm = OUT_TM
    row = lambda n: pl.BlockSpec((tm, n), lambda i: (i, 0))
    vec = lambda n: pl.BlockSpec((1, n), lambda i: (0, 0))
    cast_in_specs, cast_specs, cast_shapes = _cast_specs(cast_jobs)
    return pl.pallas_call(
        functools.partial(_outproj_kernel, len(cast_jobs)),
        grid=(rows // tm,),
        in_specs=[row(D_GLA), row(D_CONV), vec(D_CONV), vec(D_CONV),
                  pl.BlockSpec(w_out_bf.shape, lambda i: (0, 0), pipeline_mode=pl.Buffered(1)),
                  row(d), pl.BlockSpec((None, N_MOD, d), lambda i: (group_of_tile(i), 0, 0))] + cast_in_specs,
        out_specs=[row(d)] + cast_specs,
        out_shape=[jax.ShapeDtypeStruct((rows, d), F32)] + cast_shapes,
        compiler_params=_params("parallel"),
        name="outproj",
    )(og, y, ln_g, ln_b, w_out_bf, h, mod, *[job.array for job in cast_jobs])


def kernel(x, c, ctx, c_ctx, w_mod, b_mod, norm_ffn1, w_ffn1_in, w_ffn1_out, norm_mix, w_in, w_gk2, b_gk2, gla_norm, conv_w, conv_b, conv_ln_g, conv_ln_b, w_out, norm_ffn2, w_ffn2_in, w_ffn2_out, norm_final):
    bsz, t, d = x.shape
    t_ctx = ctx.shape[1]
    assert w_mod.shape[0] == 1, "single layer only"
    assert t % GLA_TB == 0 and (t // GLA_TB) % 2 == 0 and t % FFN_TALL_TM == 0 and t % PROJ_TM == 0
    vec = lambda a: a.reshape(1, -1)

    w2pad = jnp.zeros((V7X_LANES, 2 * GLA_KEY), BF16)
    w2pad = w2pad.at[:GATE_RANK, :GLA_KEY].set(w_gk2[0, 0].astype(BF16))
    w2pad = w2pad.at[GATE_RANK:2 * GATE_RANK, GLA_KEY:].set(w_gk2[0, 1].astype(BF16))
    b2 = b_gk2[0].reshape(1, 2 * GLA_KEY)
    w_r = _wprep_call(jnp.swapaxes(w_in[0], 0, 1))

    n_rows = 8
    s_in = jnp.concatenate([c, c_ctx[None, :], jnp.zeros((n_rows - bsz - 1, d), F32)], axis=0)
    mod = _mod_call(s_in, w_mod[0], vec(b_mod[0])).reshape(n_rows, N_MOD, d)

    batch_of = lambda tile_rows: (lambda i: i // (t // tile_rows))
    ctx_group = lambda i: bsz
    g1, gm = vec(norm_ffn1[0]), vec(norm_mix[0])

    xc = ctx.reshape(bsz * t_ctx, d)
    hxc, wg1, wu1, wo1 = _ffn_call(
        xc, mod, ctx_group, g1, gm, w_ffn1_in[0], w_ffn1_in[0], w_ffn1_out[0], D_FF,
        _FfnCfg(mod_row=0, emit_h=False, emit_hx=True, final_norm=False, cast_w=True), "ffn1_ctx")
    s0f, s0b = _ctx_call(hxc, w_r, w2pad, b2, bsz, t_ctx)

    xl = x.reshape(bsz * t, d)
    (h1,) = _ffn_call(
        xl, mod, batch_of(FFN_TALL_TM), g1, gm, wg1, wu1, wo1, 0,
        _FfnCfg(mod_row=0, emit_h=True, emit_hx=False, final_norm=False, cast_w=False, tm=FFN_TALL_TM,
                acc_in_h=True), "ffn1")

    n_proj = (bsz * t) // PROJ_TM
    gate_job = _CastJob(w_ffn2_in[0], (d // n_proj, D_FF), lambda i: (i, 0), (d, D_FF), lambda i: (i, 0))
    qkv, ld, sg, u, wg2 = _proj_call(h1, mod, batch_of(PROJ_TM), gm, w_r, w2pad, b2, (gate_job,))
    n_bh, n_gla_blocks = bsz * GLA_HEADS, t // GLA_TB
    up_job = _CastJob(w_ffn2_in[0], (d // n_bh, D_FF // n_gla_blocks),
                      lambda b, h, i: (b * GLA_HEADS + h, n_gla_blocks + i),
                      (d, D_FF), lambda b, h, i: (b * GLA_HEADS + h, i))
    og, wu2 = _gla_call(qkv, ld, sg, s0f, s0b, vec(gla_norm[0]), bsz, t, (up_job,))
    n_conv_steps = bsz * (D_CONV // CONV_CB)
    step = lambda bi, cb: (bi * (D_CONV // CONV_CB) + cb, 0)
    y, wf2_out, w_out_bf = _conv_call(
        u, conv_w[0], vec(conv_b[0]), bsz, t,
        (_CastJob(w_ffn2_out[0], (D_FF // n_conv_steps, d), step),
         _CastJob(w_out[0], (d // n_conv_steps, d), step)))
    (h2,) = _outproj_call(
        og, y, vec(conv_ln_g[0]), vec(conv_ln_b[0]), w_out_bf, h1, mod, batch_of(OUT_TM))

    (out,) = _ffn_call(
        h2, mod, batch_of(FFN_TALL_TM), vec(norm_ffn2[0]), vec(norm_final), wg2, wu2, wf2_out, 0,
        _FfnCfg(mod_row=6, emit_h=True, emit_hx=False, final_norm=True, cast_w=False, tm=FFN_TALL_TM,
                acc_in_h=True), "ffn2")
    return out.reshape(bsz, t, d)
```

```python
import functools
from typing import NamedTuple

import jax
import jax.numpy as jnp
from jax import lax
from jax.experimental import pallas as pl
from jax.experimental.pallas import tpu as pltpu

F32 = jnp.float32
BF16 = jnp.bfloat16

D_MODEL = 2048
GRID_W = 64
GLA_HEADS = 4
GLA_DK = 128
GLA_DV = 256
GLA_KEY = GLA_HEADS * GLA_DK
D_GLA = GLA_HEADS * GLA_DV
D_CONV = 1024
GATE_RANK = 16
GATE_NORMALIZER = 16.0
CHUNK = 64
CONV_WIDTH = 31
CONV_HALF = CONV_WIDTH // 2
D_FF = 5632
N_MOD = 9
RMS_EPS = 1e-6
HEAD_NORM_EPS = 1e-5
LN_EPS = 1e-5

OFF_V = GLA_KEY
OFF_GKF = OFF_V + D_GLA
CTX_COLS = OFF_GKF + 2 * GATE_RANK
OFF_Q = CTX_COLS
OFF_G = OFF_Q + GLA_KEY
OFF_GLU = OFF_G + D_GLA

V7X_LANES = 128
R_K = 0
R_V = R_K + GLA_KEY
R_Q = R_V + D_GLA
R_G = R_Q + GLA_KEY
R_A = R_G + D_GLA
R_B = R_A + D_CONV
R_GK = R_B + D_CONV
R_END = R_GK + V7X_LANES

V7X_VMEM_SCOPED_LIMIT_BYTES = 60000 * 1024
V7X_BF16_SUBLANES = 16

FFN_TM = 512
FFN_TALL_TM = 1024
FFN_TF = 512
FFN_TF_CAST = 256
PROJ_TM = 512
OUT_TM = 512
MOD_TN = 2048
GLA_TB = 2048
GLA_SUB = 256
CONV_CB = 128
CONV_UNROLL = 8
WPREP_TC = 256

NT_DIMS = (((1,), (1,)), ((), ()))
TN_DIMS = (((0,), (0,)), ((), ()))


def _params(*semantics):
    return pltpu.CompilerParams(dimension_semantics=semantics,
                                vmem_limit_bytes=V7X_VMEM_SCOPED_LIMIT_BYTES)


def _silu(x):
    return x * jax.nn.sigmoid(x)


def _rmsnorm_rows(x, gain, eps):
    ms = jnp.mean(x * x, axis=-1, keepdims=True)
    return x * lax.rsqrt(ms + eps) * gain


def _modulate(x, gain, shift, scale):
    return _rmsnorm_rows(x, gain, RMS_EPS) * (1.0 + scale) + shift


def _log_sigmoid(z):
    return jnp.minimum(z, 0.0) - jnp.log1p(jnp.exp(-jnp.abs(z)))


class _CastJob(NamedTuple):
    array: jax.Array
    block: tuple
    index_map: object
    out_shape: tuple = None
    out_index_map: object = None


def _cast_in_specs(jobs):
    for job in jobs:
        assert all(n % b == 0 for n, b in zip(job.array.shape, job.block))
        assert job.block[0] % V7X_BF16_SUBLANES == 0 and job.block[1] % V7X_LANES == 0
    return [pl.BlockSpec(job.block, job.index_map) for job in jobs]


def _cast_out_specs(jobs):
    specs = [pl.BlockSpec(job.block, job.out_index_map or job.index_map) for job in jobs]
    shapes = [jax.ShapeDtypeStruct(job.out_shape or job.array.shape, BF16) for job in jobs]
    return specs, shapes


def _cast_specs(jobs):
    return (_cast_in_specs(jobs),) + _cast_out_specs(jobs)


def _run_casts(src_refs, dst_refs):
    for src_ref, dst_ref in zip(src_refs, dst_refs):
        dst_ref[...] = src_ref[...].astype(dst_ref.dtype)


def _mod_kernel(s_ref, w_ref, b_ref, o_ref):
    s = _silu(s_ref[...]).astype(BF16)
    o_ref[...] = jnp.dot(s, w_ref[...].astype(BF16), preferred_element_type=F32) + b_ref[...]


def _mod_call(s_in, w_mod, b_mod):
    rows, d = s_in.shape
    n = w_mod.shape[1]
    return pl.pallas_call(
        _mod_kernel,
        grid=(n // MOD_TN,),
        in_specs=[
            pl.BlockSpec((rows, d), lambda j: (0, 0)),
            pl.BlockSpec((d, MOD_TN), lambda j: (0, j)),
            pl.BlockSpec((1, MOD_TN), lambda j: (0, j)),
        ],
        out_specs=pl.BlockSpec((rows, MOD_TN), lambda j: (0, j)),
        out_shape=jax.ShapeDtypeStruct((rows, n), F32),
        compiler_params=_params("arbitrary"),
        name="mod",
    )(s_in, w_mod, b_mod)


class _FfnCfg(NamedTuple):
    mod_row: int
    emit_h: bool
    emit_hx: bool
    final_norm: bool
    cast_w: bool
    tm: int = FFN_TM
    acc_in_h: bool = False


def _ffn_kernel(cfg, n_cast, n_alias, x_ref, mod_ref, g_in_ref, g_next_ref, wg_ref, wu_ref, wo_ref, *rest):
    cast_in, rest = rest[:n_cast], rest[n_cast + n_alias:]
    outs = []
    for flag in (cfg.emit_h, cfg.emit_hx, cfg.cast_w, cfg.cast_w, cfg.cast_w):
        outs.append(rest[0] if flag else None)
        rest = rest[1:] if flag else rest
    h_ref, hx_ref, wg_bf_ref, wu_bf_ref, wo_bf_ref = outs
    cast_out, scratch = rest[:n_cast], rest[n_cast:]
    hm_ref = scratch[0]
    acc_ref = h_ref if cfg.acc_in_h else scratch[1]
    j = pl.program_id(1)
    last = pl.num_programs(1) - 1
    r0 = cfg.mod_row
    _run_casts(cast_in, cast_out)

    def partial_out(hm):
        wg, wu, wo = wg_ref[...], wu_ref[...], wo_ref[...]
        if cfg.cast_w:
            wg, wu, wo = wg.astype(BF16), wu.astype(BF16), wo.astype(BF16)
            wg_bf_ref[...] = wg
            wu_bf_ref[...] = wu
            wo_bf_ref[...] = wo
        gate = jnp.dot(hm, wg, preferred_element_type=F32)
        up = jnp.dot(hm, wu, preferred_element_type=F32)
        act = (_silu(gate) * up).astype(BF16)
        return jnp.dot(act, wo, preferred_element_type=F32)

    @pl.when(j == 0)
    def _():
        hm = _modulate(x_ref[...], g_in_ref[...], mod_ref[r0:r0 + 1, :], mod_ref[r0 + 1:r0 + 2, :])
        hm = hm.astype(BF16)
        hm_ref[...] = hm
        acc_ref[...] = partial_out(hm)

    @pl.when((j > 0) & (j < last))
    def _():
        acc_ref[...] += partial_out(hm_ref[...])

    @pl.when(j == last)
    def _():
        g = mod_ref[r0 + 2:r0 + 3, :]
        h = x_ref[...] + (0.5 * g) * (acc_ref[...] + partial_out(hm_ref[...]))
        if cfg.emit_hx:
            hx = _modulate(h, g_next_ref[...], mod_ref[r0 + 3:r0 + 4, :], mod_ref[r0 + 4:r0 + 5, :])
            hx_ref[...] = hx.astype(BF16)
        if cfg.emit_h:
            h_ref[...] = _rmsnorm_rows(h, g_next_ref[...], RMS_EPS) if cfg.final_norm else h


def _ffn_call(x, mod, group_of_tile, g_in, g_next, wg, wu, wo, up_col0, cfg, name, cast_jobs=(),
              tiles=None, h_into=None):
    rows, d = x.shape
    n_ff = wo.shape[0]
    tf = FFN_TF_CAST if cfg.cast_w else FFN_TF
    n_j = n_ff // tf
    assert n_j >= 2, "first and last hidden block must be distinct grid steps"
    tm = cfg.tm
    tile0, n_tiles = tiles or (0, rows // tm)
    assert not cfg.cast_w or n_tiles == 1, "bf16 weight copies are written by one row tile only"
    assert cfg.emit_h or not cfg.acc_in_h
    assert h_into is None or (cfg.emit_h and not cast_jobs)
    up_blk0 = up_col0 // tf
    inner_group = group_of_tile
    group_of_tile = lambda i: inner_group(tile0 + i)
    row_spec = pl.BlockSpec((tm, d), lambda i, j: (tile0 + i, 0))
    vec_spec = pl.BlockSpec((1, d), lambda i, j: (0, 0))
    wg_spec = pl.BlockSpec((d, tf), lambda i, j: (0, j))
    wu_spec = pl.BlockSpec((d, tf), lambda i, j: (0, up_blk0 + j))
    wo_spec = pl.BlockSpec((tf, d), lambda i, j: (j, 0))
    cast_in_specs, cast_specs, cast_shapes = _cast_specs(cast_jobs)
    in_specs = [row_spec, pl.BlockSpec((None, N_MOD, d), lambda i, j: (group_of_tile(i), 0, 0)),
                vec_spec, vec_spec, wg_spec, wu_spec, wo_spec] + cast_in_specs
    out_specs, out_shape = [], []
    if cfg.emit_h:
        out_specs.append(row_spec)
        out_shape.append(jax.ShapeDtypeStruct((rows, d), F32))
    if cfg.emit_hx:
        out_specs.append(row_spec)
        out_shape.append(jax.ShapeDtypeStruct((rows, d), BF16))
    if cfg.cast_w:
        out_specs += [wg_spec, pl.BlockSpec((d, tf), lambda i, j: (0, j)), wo_spec]
        out_shape += [jax.ShapeDtypeStruct((d, n_ff), BF16), jax.ShapeDtypeStruct((d, n_ff), BF16),
                      jax.ShapeDtypeStruct((n_ff, d), BF16)]
    operands = [x, mod, g_in, g_next, wg, wu, wo] + [job.array for job in cast_jobs]
    aliases = {}
    if h_into is not None:
        aliases = {len(operands): 0}
        in_specs = in_specs + [pl.BlockSpec(memory_space=pl.ANY)]
        operands.append(h_into)
    return pl.pallas_call(
        functools.partial(_ffn_kernel, cfg, len(cast_jobs), len(aliases)),
        grid=(n_tiles, n_j),
        in_specs=in_specs,
        out_specs=out_specs + cast_specs,
        out_shape=out_shape + cast_shapes,
        input_output_aliases=aliases,
        scratch_shapes=[pltpu.VMEM((tm, d), BF16)] + ([] if cfg.acc_in_h else [pltpu.VMEM((tm, d), F32)]),
        compiler_params=_params("parallel", "arbitrary"),
        name=name,
    )(*operands)


def _wprep_kernel(w_ref, o_ref):
    o_ref[R_K:R_Q, :] = w_ref[0:OFF_GKF, :].astype(BF16)
    o_ref[R_Q:R_GK, :] = w_ref[OFF_Q:OFF_GLU + 2 * D_CONV, :].astype(BF16)
    o_ref[R_GK:R_GK + 2 * GATE_RANK, :] = w_ref[OFF_GKF:CTX_COLS, :].astype(BF16)
    o_ref[R_GK + 2 * GATE_RANK:R_END, :] = jnp.zeros((V7X_LANES - 2 * GATE_RANK, o_ref.shape[1]), BF16)


def _wprep_call(w_t):
    n, d = w_t.shape
    tc = WPREP_TC
    return pl.pallas_call(
        _wprep_kernel,
        grid=(d // tc,),
        in_specs=[pl.BlockSpec((n, tc), lambda i: (0, i))],
        out_specs=pl.BlockSpec((R_END, tc), lambda i: (0, i)),
        out_shape=jax.ShapeDtypeStruct((R_END, d), BF16),
        compiler_params=_params("parallel"),
        name="wprep",
    )(w_t)


def _log_decays(p_gk, w2_ref, b2_ref):
    z = jnp.dot(p_gk.astype(BF16), w2_ref[...], preferred_element_type=F32) + b2_ref[...]
    return _log_sigmoid(z) * (1.0 / GATE_NORMALIZER)


def _chunk_cumsum(x, reverse):
    n = x.shape[0]
    pos = lax.broadcasted_iota(jnp.int32, x.shape, 0) % CHUNK
    d = 1
    while d < CHUNK:
        if reverse:
            shifted = pltpu.roll(x, n - d, 0)
            x = x + jnp.where(pos < CHUNK - d, shifted, 0.0)
        else:
            shifted = pltpu.roll(x, d, 0)
            x = x + jnp.where(pos >= d, shifted, 0.0)
        d *= 2
    return x


def _chunk_rows(x, row_in_chunk):
    return jnp.concatenate([x[c0 + row_in_chunk:c0 + row_in_chunk + 1, :]
                            for c0 in range(0, x.shape[0], CHUNK)], axis=0)


def _chunk_bcast(rows_per_chunk):
    return jnp.concatenate([jnp.broadcast_to(rows_per_chunk[c:c + 1, :], (CHUNK, rows_per_chunk.shape[1]))
                            for c in range(rows_per_chunk.shape[0])], axis=0)


QKV_HEAD = 2 * GLA_DK + GLA_DV
LD_HEAD = 2 * GLA_DK


def _proj_kernel(n_cast, h_ref, mod_ref, g_ref, w_ref, w2_ref, b2_ref, *rest):
    cast_in, (qkv_ref, ld_ref, sg_ref, u_ref), cast_out = rest[:n_cast], rest[n_cast:n_cast + 4], rest[n_cast + 4:]
    _run_casts(cast_in, cast_out)
    hx = _modulate(h_ref[...], g_ref[...], mod_ref[3:4, :], mod_ref[4:5, :]).astype(BF16)

    def proj(lo, hi):
        return lax.dot_general(hx, w_ref[lo:hi, :], NT_DIMS, preferred_element_type=F32)

    ld = _log_decays(proj(R_GK, R_END), w2_ref, b2_ref)
    k = proj(R_K, R_V).astype(BF16)
    v = proj(R_V, R_Q).astype(BF16)
    q = (proj(R_Q, R_G) * (GLA_DK ** -0.5)).astype(BF16)
    for h in range(GLA_HEADS):
        dk = slice(h * GLA_DK, (h + 1) * GLA_DK)
        c0 = h * QKV_HEAD
        qkv_ref[:, c0:c0 + GLA_DK] = q[:, dk]
        qkv_ref[:, c0 + GLA_DK:c0 + 2 * GLA_DK] = k[:, dk]
        qkv_ref[:, c0 + 2 * GLA_DK:c0 + QKV_HEAD] = v[:, h * GLA_DV:(h + 1) * GLA_DV]
        ld_ref[:, h * LD_HEAD:h * LD_HEAD + GLA_DK] = ld[:, dk]
        ld_ref[:, h * LD_HEAD + GLA_DK:(h + 1) * LD_HEAD] = ld[:, GLA_KEY + h * GLA_DK:GLA_KEY + (h + 1) * GLA_DK]
    sg_ref[...] = _silu(proj(R_G, R_A)).astype(BF16)
    u_ref[...] = proj(R_A, R_B) * jax.nn.sigmoid(proj(R_B, R_GK))


def _proj_call(h, mod, group_of_tile, g_mix, w_r, w2pad, b2, cast_jobs=()):
    rows, d = h.shape
    tm = PROJ_TM
    row = lambda n: pl.BlockSpec((tm, n), lambda i: (i, 0))
    whole = lambda a: pl.BlockSpec(a.shape, lambda i: (0, 0), pipeline_mode=pl.Buffered(1))
    outs = [(GLA_HEADS * QKV_HEAD, BF16), (GLA_HEADS * LD_HEAD, F32), (D_GLA, BF16), (D_CONV, F32)]
    cast_in_specs, cast_specs, cast_shapes = _cast_specs(cast_jobs)
    return pl.pallas_call(
        functools.partial(_proj_kernel, len(cast_jobs)),
        grid=(rows // tm,),
        in_specs=[row(d), pl.BlockSpec((None, N_MOD, d), lambda i: (group_of_tile(i), 0, 0)),
                  pl.BlockSpec((1, d), lambda i: (0, 0)), whole(w_r), whole(w2pad), whole(b2)] + cast_in_specs,
        out_specs=[row(n) for n, _ in outs] + cast_specs,
        out_shape=[jax.ShapeDtypeStruct((rows, n), dt) for n, dt in outs] + cast_shapes,
        compiler_params=_params("parallel"),
        name="proj",
    )(h, mod, g_mix, w_r, w2pad, b2, *[job.array for job in cast_jobs])


def _ctx_kernel(hx_ref, wkv_ref, wgk_ref, w2_ref, b2_ref, sf_ref, sb_ref):
    hx = hx_ref[...]
    t = hx.shape[0]
    kv = lax.dot_general(hx, wkv_ref[...], NT_DIMS, preferred_element_type=F32)
    ld = _log_decays(lax.dot_general(hx, wgk_ref[...], NT_DIMS, preferred_element_type=F32),
                     w2_ref, b2_ref)
    r = lax.broadcasted_iota(jnp.int32, (t, t), 0)
    c = lax.broadcasted_iota(jnp.int32, (t, t), 1)
    hi = lax.Precision.HIGHEST
    e_f = jnp.dot((c > r).astype(F32), ld[:, :GLA_KEY], preferred_element_type=F32, precision=hi)
    e_b = jnp.dot((c < r).astype(F32), ld[:, GLA_KEY:], preferred_element_type=F32, precision=hi)
    for h in range(GLA_HEADS):
        ks = slice(h * GLA_DK, (h + 1) * GLA_DK)
        k = kv[:, R_K + h * GLA_DK:R_K + (h + 1) * GLA_DK]
        v = kv[:, R_V + h * GLA_DV:R_V + (h + 1) * GLA_DV].astype(BF16)
        sf_ref[h] = lax.dot_general(v, (k * jnp.exp(e_f[:, ks])).astype(BF16), TN_DIMS,
                                    preferred_element_type=F32)
        sb_ref[h] = lax.dot_general(v, (k * jnp.exp(e_b[:, ks])).astype(BF16), TN_DIMS,
                                    preferred_element_type=F32)


def _ctx_call(hxc, w_r, w2pad, b2, bsz, t):
    d = hxc.shape[1]
    const = lambda a: pl.BlockSpec(a.shape, lambda b: (0, 0))
    st = pl.BlockSpec((None, GLA_HEADS, GLA_DV, GLA_DK), lambda b: (b, 0, 0, 0))
    shape = jax.ShapeDtypeStruct((bsz, GLA_HEADS, GLA_DV, GLA_DK), F32)
    return pl.pallas_call(
        _ctx_kernel,
        grid=(bsz,),
        in_specs=[pl.BlockSpec((t, d), lambda b: (b, 0)),
                  pl.BlockSpec((R_Q, d), lambda b: (0, 0)),
                  pl.BlockSpec((V7X_LANES, d), lambda b: (R_GK // V7X_LANES, 0)),
                  const(w2pad), const(b2)],
        out_specs=[st, st],
        out_shape=[shape, shape],
        compiler_params=_params("parallel"),
        name="ctx",
    )(hxc, w_r, w_r, w2pad, b2)


def _gla_direction(qkv_ref, ld_ref, s_ref, reverse):
    n_sub = GLA_TB // GLA_SUB
    n_chunk = GLA_SUB // CHUNK
    rr = lax.broadcasted_iota(jnp.int32, (GLA_SUB, GLA_SUB), 0)
    cc = lax.broadcasted_iota(jnp.int32, (GLA_SUB, GLA_SUB), 1)
    same_chunk = (rr // CHUNK) == (cc // CHUNK)
    mask = same_chunk & ((cc >= rr) if reverse else (cc <= rr))
    mid_row = CHUNK // 2 if reverse else CHUNK // 2 - 1
    last_row = 0 if reverse else CHUNK - 1
    ld0 = GLA_DK if reverse else 0

    outs = [None] * n_sub
    subs = range(n_sub - 1, -1, -1) if reverse else range(n_sub)
    state = s_ref[...]
    for s in subs:
        rows = slice(s * GLA_SUB, (s + 1) * GLA_SUB)
        b = _chunk_cumsum(ld_ref[rows, ld0:ld0 + GLA_DK], reverse)
        b_mid = _chunk_bcast(_chunk_rows(b, mid_row))
        b_last_rows = _chunk_rows(b, last_row)
        b_last = _chunk_bcast(b_last_rows)
        q = qkv_ref[rows, 0:GLA_DK].astype(F32)
        k = qkv_ref[rows, GLA_DK:2 * GLA_DK].astype(F32)
        v = qkv_ref[rows, 2 * GLA_DK:QKV_HEAD]
        qs = (q * jnp.exp(b - b_mid)).astype(BF16)
        ks = (k * jnp.exp(b_mid - b)).astype(BF16)
        qi = (q * jnp.exp(b)).astype(BF16)
        kd = (k * jnp.exp(b_last - b)).astype(BF16)
        att = lax.dot_general(qs, ks, NT_DIMS, preferred_element_type=F32)
        att = jnp.where(mask, att, 0.0).astype(BF16)
        o_intra = jnp.dot(att, v, preferred_element_type=F32)
        decay = jnp.exp(b_last_rows)
        o_parts = [None] * n_chunk
        chunks = range(n_chunk - 1, -1, -1) if reverse else range(n_chunk)
        for c in chunks:
            cr = slice(c * CHUNK, (c + 1) * CHUNK)
            o_parts[c] = o_intra[cr, :] + lax.dot_general(
                qi[cr, :], state.astype(BF16), NT_DIMS, preferred_element_type=F32)
            kv = lax.dot_general(v[cr, :], kd[cr, :], TN_DIMS, preferred_element_type=F32)
            state = state * decay[c:c + 1, :] + kv
        outs[s] = jnp.concatenate(o_parts, axis=0)
    s_ref[...] = state
    return jnp.concatenate(outs, axis=0)


def _gla_kernel(n_cast, qkvf_ref, ldf_ref, qkvb_ref, ldb_ref, sg_ref, s0f_ref, s0b_ref, gain_ref,
                *rest):
    cast_in, o_ref, rest = rest[:n_cast], rest[n_cast], rest[n_cast + 1:]
    cast_out, (sf_ref, sb_ref, acc_ref) = rest[:n_cast], rest[n_cast:]
    nb = pl.program_id(2)
    n_blocks = pl.num_programs(2)

    @pl.when(nb == 0)
    def _():
        sf_ref[...] = s0f_ref[...]
        sb_ref[...] = s0b_ref[...]

    _run_casts(cast_in, cast_out)
    o_f = _gla_direction(qkvf_ref, ldf_ref, sf_ref, reverse=False)
    o_b = _gla_direction(qkvb_ref, ldb_ref, sb_ref, reverse=True)
    row_f = pl.multiple_of(nb * GLA_TB, GLA_TB)
    row_b = pl.multiple_of((n_blocks - 1 - nb) * GLA_TB, GLA_TB)

    @pl.when(nb < n_blocks // 2)
    def _():
        acc_ref[pl.ds(row_f, GLA_TB), :] = o_f
        acc_ref[pl.ds(row_b, GLA_TB), :] = o_b

    @pl.when(nb >= n_blocks // 2)
    def _():
        for row, part in ((row_f, o_f), (row_b, o_b)):
            o = acc_ref[pl.ds(row, GLA_TB), :] + part
            ms = jnp.mean(o * o, axis=-1, keepdims=True)
            o = o * lax.rsqrt(ms + HEAD_NORM_EPS) * gain_ref[...]
            o_ref[pl.ds(row, GLA_TB), :] = (o * sg_ref[pl.ds(row, GLA_TB), :].astype(F32)).astype(BF16)


def _gla_call(qkv, ld, sg, s0f, s0b, gain, bsz, t, cast_jobs=()):
    nb = t // GLA_TB
    fwd = lambda n: pl.BlockSpec((GLA_TB, n), lambda b, h, i: (b * nb + i, h))
    bwd = lambda n: pl.BlockSpec((GLA_TB, n), lambda b, h, i: (b * nb + nb - 1 - i, h))
    seq = pl.BlockSpec((t, GLA_DV), lambda b, h, i: (b, h))
    st = pl.BlockSpec((None, None, GLA_DV, GLA_DK), lambda b, h, i: (b, h, 0, 0))
    cast_in_specs, cast_specs, cast_shapes = _cast_specs(cast_jobs)
    return pl.pallas_call(
        functools.partial(_gla_kernel, len(cast_jobs)),
        grid=(bsz, GLA_HEADS, nb),
        in_specs=[fwd(QKV_HEAD), fwd(LD_HEAD), bwd(QKV_HEAD), bwd(LD_HEAD),
                  seq, st, st, pl.BlockSpec((1, GLA_DV), lambda b, h, i: (0, 0))] + cast_in_specs,
        out_specs=[seq] + cast_specs,
        out_shape=[jax.ShapeDtypeStruct((bsz * t, D_GLA), BF16)] + cast_shapes,
        scratch_shapes=[pltpu.VMEM((GLA_DV, GLA_DK), F32), pltpu.VMEM((GLA_DV, GLA_DK), F32),
                        pltpu.VMEM((t, GLA_DV), F32)],
        compiler_params=_params("parallel", "parallel", "arbitrary"),
        name="gla",
    )(qkv, ld, qkv, ld, sg, s0f, s0b, gain, *[job.array for job in cast_jobs])


ROW_PAD = 16
ROW_PITCH = GRID_W + 2 * ROW_PAD


def _conv_kernel(n_row_blocks, n_cast, u_ref, w_ref, b_ref, *rest):
    cast_in, y_ref, rest = rest[:n_cast], rest[n_cast], rest[n_cast + 1:]
    cast_out, (pad_ref,) = rest[:n_cast], rest[n_cast:]
    cb = pl.program_id(1)
    rows = u_ref.shape[0] // GRID_W
    bias = jnp.broadcast_to(b_ref[...], (GRID_W, CONV_CB))
    _run_casts(cast_in, cast_out)

    @pl.when(cb < n_row_blocks)
    def _():
        @pl.when(cb == 0)
        def _():
            pad_ref[...] = jnp.zeros_like(pad_ref)

        def fill(r, carry):
            src = pl.multiple_of(r * GRID_W, GRID_W)
            dst = pl.multiple_of(r * ROW_PITCH + ROW_PAD, 8)
            pad_ref[pl.ds(dst, GRID_W), :] = u_ref[pl.ds(src, GRID_W), :]
            return carry

        lax.fori_loop(0, rows, fill, 0)

        def body(r, carry):
            base = r * ROW_PITCH + (ROW_PAD - CONV_HALF)
            acc = bias
            for j in range(CONV_WIDTH):
                acc = acc + w_ref[j:j + 1, :] * pad_ref[pl.ds(base + j, GRID_W), :]
            y_ref[pl.ds(pl.multiple_of(r * GRID_W, GRID_W), GRID_W), :] = acc
            return carry

        lax.fori_loop(0, rows, body, 0, unroll=CONV_UNROLL)

    @pl.when(cb >= n_row_blocks)
    def _():
        edge = CONV_HALF * GRID_W

        @pl.when(cb == n_row_blocks)
        def _():
            pad_ref[0:edge, :] = jnp.zeros((edge, CONV_CB), F32)
            pad_ref[edge + rows * GRID_W:2 * edge + rows * GRID_W, :] = jnp.zeros((edge, CONV_CB), F32)

        pad_ref[edge:edge + rows * GRID_W, :] = u_ref[...]

        def body(r, carry):
            acc = bias
            for j in range(CONV_WIDTH):
                src = pl.multiple_of((r + j) * GRID_W, GRID_W)
                acc = acc + w_ref[j:j + 1, :] * pad_ref[pl.ds(src, GRID_W), :]
            y_ref[pl.ds(pl.multiple_of(r * GRID_W, GRID_W), GRID_W), :] = acc
            return carry

        lax.fori_loop(0, rows, body, 0, unroll=CONV_UNROLL)


def _conv_call(u, w, b, bsz, t, cast_jobs=()):
    ch = u.shape[1]
    rows = t // GRID_W
    n_cb = ch // CONV_CB
    pad_rows = max(rows * ROW_PITCH, (rows + 2 * CONV_HALF) * GRID_W)
    blk = pl.BlockSpec((t, CONV_CB), lambda bi, c: (bi, c))
    cast_in_specs, cast_specs, cast_shapes = _cast_specs(cast_jobs)
    return pl.pallas_call(
        functools.partial(_conv_kernel, n_cb // 2, len(cast_jobs)),
        grid=(bsz, n_cb),
        in_specs=[blk, pl.BlockSpec((CONV_WIDTH, CONV_CB), lambda bi, c: (0, c)),
                  pl.BlockSpec((1, CONV_CB), lambda bi, c: (0, c))] + cast_in_specs,
        out_specs=[blk] + cast_specs,
        out_shape=[jax.ShapeDtypeStruct(u.shape, F32)] + cast_shapes,
        scratch_shapes=[pltpu.VMEM((pad_rows, CONV_CB), F32)],
        compiler_params=_params("arbitrary", "arbitrary"),
        name="conv",
    )(u, w, b, *[job.array for job in cast_jobs])


def _outproj_kernel(n_cast, og_ref, y_ref, lng_ref, lnb_ref, w_ref, h_ref, mod_ref, *rest):
    cast_in, o_ref, cast_out = rest[:n_cast], rest[n_cast], rest[n_cast + 1:]
    _run_casts(cast_in, cast_out)
    y = y_ref[...]
    mu = jnp.mean(y, axis=-1, keepdims=True)
    yc = y - mu
    var = jnp.mean(yc * yc, axis=-1, keepdims=True)
    yn = yc * lax.rsqrt(var + LN_EPS) * lng_ref[...] + lnb_ref[...]
    oc = _silu(yn).astype(BF16)
    res = jnp.dot(og_ref[...], w_ref[0:D_GLA, :], preferred_element_type=F32)
    res = res + jnp.dot(oc, w_ref[D_GLA:D_GLA + D_CONV, :], preferred_element_type=F32)
    o_ref[...] = h_ref[...] + mod_ref[5:6, :] * res


def _outproj_call(og, y, ln_g, ln_b, w_out_bf, h, mod, group_of_tile, cast_jobs=()):
    rows, d = h.shape
    tm = OUT_TM
    row = lambda n: pl.BlockSpec((tm, n), lambda i: (i, 0))
    vec = lambda n: pl.BlockSpec((1, n), lambda i: (0, 0))
    cast_in_specs, cast_specs, cast_shapes = _cast_specs(cast_jobs)
    return pl.pallas_call(
        functools.partial(_outproj_kernel, len(cast_jobs)),
        grid=(rows // tm,),
        in_specs=[row(D_GLA), row(D_CONV), vec(D_CONV), vec(D_CONV),
                  pl.BlockSpec(w_out_bf.shape, lambda i: (0, 0), pipeline_mode=pl.Buffered(1)),
                  row(d), pl.BlockSpec((None, N_MOD, d), lambda i: (group_of_tile(i), 0, 0))] + cast_in_specs,
        out_specs=[row(d)] + cast_specs,
        out_shape=[jax.ShapeDtypeStruct((rows, d), F32)] + cast_shapes,
        compiler_params=_params("parallel"),
        name="outproj",
    )(og, y, ln_g, ln_b, w_out_bf, h, mod, *[job.array for job in cast_jobs])


def kernel(x, c, ctx, c_ctx, w_mod, b_mod, norm_ffn1, w_ffn1_in, w_ffn1_out, norm_mix, w_in, w_gk2, b_gk2, gla_norm, conv_w, conv_b, conv_ln_g, conv_ln_b, w_out, norm_ffn2, w_ffn2_in, w_ffn2_out, norm_final):
    bsz, t, d = x.shape
    t_ctx = ctx.shape[1]
    assert w_mod.shape[0] == 1, "single layer only"
    assert t % GLA_TB == 0 and (t // GLA_TB) % 2 == 0 and t % FFN_TALL_TM == 0 and t % PROJ_TM == 0
    vec = lambda a: a.reshape(1, -1)

    w2pad = jnp.zeros((V7X_LANES, 2 * GLA_KEY), BF16)
    w2pad = w2pad.at[:GATE_RANK, :GLA_KEY].set(w_gk2[0, 0].astype(BF16))
    w2pad = w2pad.at[GATE_RANK:2 * GATE_RANK, GLA_KEY:].set(w_gk2[0, 1].astype(BF16))
    b2 = b_gk2[0].reshape(1, 2 * GLA_KEY)
    w_r = _wprep_call(jnp.swapaxes(w_in[0], 0, 1))

    n_rows = 8
    s_in = jnp.concatenate([c, c_ctx[None, :], jnp.zeros((n_rows - bsz - 1, d), F32)], axis=0)
    mod = _mod_call(s_in, w_mod[0], vec(b_mod[0])).reshape(n_rows, N_MOD, d)

    batch_of = lambda tile_rows: (lambda i: i // (t // tile_rows))
    ctx_group = lambda i: bsz
    g1, gm = vec(norm_ffn1[0]), vec(norm_mix[0])

    xl = x.reshape(bsz * t, d)
    xc = ctx.reshape(bsz * t_ctx, d)
    n_lat_tiles = (bsz * t) // FFN_TALL_TM
    tall = lambda cast_w: _FfnCfg(mod_row=0, emit_h=True, emit_hx=False, final_norm=False, cast_w=cast_w,
                                  tm=FFN_TALL_TM, acc_in_h=True)
    h1_head, wg1, wu1, wo1 = _ffn_call(
        xl, mod, batch_of(FFN_TALL_TM), g1, gm, w_ffn1_in[0], w_ffn1_in[0], w_ffn1_out[0], D_FF,
        tall(True), "ffn1_head", tiles=(0, 1))
    (hxc,) = _ffn_call(
        xc, mod, ctx_group, g1, gm, wg1, wu1, wo1, 0,
        _FfnCfg(mod_row=0, emit_h=False, emit_hx=True, final_norm=False, cast_w=False), "ffn1_ctx")
    s0f, s0b = _ctx_call(hxc, w_r, w2pad, b2, bsz, t_ctx)
    (h1,) = _ffn_call(
        xl, mod, batch_of(FFN_TALL_TM), g1, gm, wg1, wu1, wo1, 0,
        tall(False), "ffn1", tiles=(1, n_lat_tiles - 1), h_into=h1_head)

    n_proj, n_outp = (bsz * t) // PROJ_TM, (bsz * t) // OUT_TM
    half_job = lambda n, col: _CastJob(w_ffn2_in[0], (d // n, D_FF), lambda i: (i, col),
                                       (d, D_FF), lambda i: (i, 0))
    qkv, ld, sg, u, wg2 = _proj_call(h1, mod, batch_of(PROJ_TM), gm, w_r, w2pad, b2,
                                     (half_job(n_proj, 0),))
    (og,) = _gla_call(qkv, ld, sg, s0f, s0b, vec(gla_norm[0]), bsz, t)
    n_conv_steps = bsz * (D_CONV // CONV_CB)
    step = lambda bi, cb: (bi * (D_CONV // CONV_CB) + cb, 0)
    y, wf2_out, w_out_bf = _conv_call(
        u, conv_w[0], vec(conv_b[0]), bsz, t,
        (_CastJob(w_ffn2_out[0], (D_FF // n_conv_steps, d), step),
         _CastJob(w_out[0], (d // n_conv_steps, d), step)))
    h2, wu2 = _outproj_call(
        og, y, vec(conv_ln_g[0]), vec(conv_ln_b[0]), w_out_bf, h1, mod, batch_of(OUT_TM),
        (half_job(n_outp, 1),))

    (out,) = _ffn_call(
        h2, mod, batch_of(FFN_TALL_TM), vec(norm_ffn2[0]), vec(norm_final), wg2, wu2, wf2_out, 0,
        _FfnCfg(mod_row=6, emit_h=True, emit_hx=False, final_norm=True, cast_w=False, tm=FFN_TALL_TM,
                acc_in_h=True), "ffn2")
    return out.reshape(bsz, t, d)
```

```python
import functools
from typing import NamedTuple

import jax
import jax.numpy as jnp
from jax import lax
from jax.experimental import pallas as pl
from jax.experimental.pallas import tpu as pltpu

F32 = jnp.float32
BF16 = jnp.bfloat16

D_MODEL = 2048
GRID_W = 64
GLA_HEADS = 4
GLA_DK = 128
GLA_DV = 256
GLA_KEY = GLA_HEADS * GLA_DK
D_GLA = GLA_HEADS * GLA_DV
D_CONV = 1024
GATE_RANK = 16
GATE_NORMALIZER = 16.0
CHUNK = 64
CONV_WIDTH = 31
CONV_HALF = CONV_WIDTH // 2
D_FF = 5632
N_MOD = 9
RMS_EPS = 1e-6
HEAD_NORM_EPS = 1e-5
LN_EPS = 1e-5

OFF_V = GLA_KEY
OFF_GKF = OFF_V + D_GLA
CTX_COLS = OFF_GKF + 2 * GATE_RANK
OFF_Q = CTX_COLS
OFF_G = OFF_Q + GLA_KEY
OFF_GLU = OFF_G + D_GLA

V7X_LANES = 128
R_K = 0
R_V = R_K + GLA_KEY
R_Q = R_V + D_GLA
R_G = R_Q + GLA_KEY
R_A = R_G + D_GLA
R_B = R_A + D_CONV
R_GK = R_B + D_CONV
R_END = R_GK + V7X_LANES

V7X_VMEM_SCOPED_LIMIT_BYTES = 60000 * 1024
V7X_BF16_SUBLANES = 16

FFN_TM = 512
FFN_TALL_TM = 1024
FFN_TF = 512
FFN_TF_CAST = 256
PROJ_TM = 512
OUT_TM = 512
MOD_TN = 2048
GLA_TB = 2048
GLA_SUB = 128
CONV_CB = 128
CONV_UNROLL = 8
WPREP_TC = 256

NT_DIMS = (((1,), (1,)), ((), ()))
TN_DIMS = (((0,), (0,)), ((), ()))


def _params(*semantics):
    return pltpu.CompilerParams(dimension_semantics=semantics,
                                vmem_limit_bytes=V7X_VMEM_SCOPED_LIMIT_BYTES)


def _silu(x):
    return x * jax.nn.sigmoid(x)


def _rmsnorm_rows(x, gain, eps):
    ms = jnp.mean(x * x, axis=-1, keepdims=True)
    return x * lax.rsqrt(ms + eps) * gain


def _modulate(x, gain, shift, scale):
    return _rmsnorm_rows(x, gain, RMS_EPS) * (1.0 + scale) + shift


def _log_sigmoid(z):
    return jnp.minimum(z, 0.0) - jnp.log1p(jnp.exp(-jnp.abs(z)))


class _CastJob(NamedTuple):
    array: jax.Array
    block: tuple
    index_map: object
    out_shape: tuple = None
    out_index_map: object = None


def _cast_in_specs(jobs):
    for job in jobs:
        assert all(n % b == 0 for n, b in zip(job.array.shape, job.block))
        assert job.block[0] % V7X_BF16_SUBLANES == 0 and job.block[1] % V7X_LANES == 0
    return [pl.BlockSpec(job.block, job.index_map) for job in jobs]


def _cast_out_specs(jobs):
    specs = [pl.BlockSpec(job.block, job.out_index_map or job.index_map) for job in jobs]
    shapes = [jax.ShapeDtypeStruct(job.out_shape or job.array.shape, BF16) for job in jobs]
    return specs, shapes


def _cast_specs(jobs):
    return (_cast_in_specs(jobs),) + _cast_out_specs(jobs)


def _run_casts(src_refs, dst_refs):
    for src_ref, dst_ref in zip(src_refs, dst_refs):
        dst_ref[...] = src_ref[...].astype(dst_ref.dtype)


def _mod_kernel(s_ref, w_ref, b_ref, o_ref):
    s = _silu(s_ref[...]).astype(BF16)
    o_ref[...] = jnp.dot(s, w_ref[...].astype(BF16), preferred_element_type=F32) + b_ref[...]


def _mod_call(s_in, w_mod, b_mod):
    rows, d = s_in.shape
    n = w_mod.shape[1]
    return pl.pallas_call(
        _mod_kernel,
        grid=(n // MOD_TN,),
        in_specs=[
            pl.BlockSpec((rows, d), lambda j: (0, 0)),
            pl.BlockSpec((d, MOD_TN), lambda j: (0, j)),
            pl.BlockSpec((1, MOD_TN), lambda j: (0, j)),
        ],
        out_specs=pl.BlockSpec((rows, MOD_TN), lambda j: (0, j)),
        out_shape=jax.ShapeDtypeStruct((rows, n), F32),
        compiler_params=_params("arbitrary"),
        name="mod",
    )(s_in, w_mod, b_mod)


class _FfnCfg(NamedTuple):
    mod_row: int
    emit_h: bool
    emit_hx: bool
    final_norm: bool
    cast_w: bool
    tm: int = FFN_TM
    acc_in_h: bool = False


def _ffn_kernel(cfg, n_cast, n_alias, x_ref, mod_ref, g_in_ref, g_next_ref, wg_ref, wu_ref, wo_ref, *rest):
    cast_in, rest = rest[:n_cast], rest[n_cast + n_alias:]
    outs = []
    for flag in (cfg.emit_h, cfg.emit_hx, cfg.cast_w, cfg.cast_w, cfg.cast_w):
        outs.append(rest[0] if flag else None)
        rest = rest[1:] if flag else rest
    h_ref, hx_ref, wg_bf_ref, wu_bf_ref, wo_bf_ref = outs
    cast_out, scratch = rest[:n_cast], rest[n_cast:]
    hm_ref = scratch[0]
    acc_ref = h_ref if cfg.acc_in_h else scratch[1]
    j = pl.program_id(1)
    last = pl.num_programs(1) - 1
    r0 = cfg.mod_row
    _run_casts(cast_in, cast_out)

    def partial_out(hm):
        wg, wu, wo = wg_ref[...], wu_ref[...], wo_ref[...]
        if cfg.cast_w:
            wg, wu, wo = wg.astype(BF16), wu.astype(BF16), wo.astype(BF16)
            wg_bf_ref[...] = wg
            wu_bf_ref[...] = wu
            wo_bf_ref[...] = wo
        gate = jnp.dot(hm, wg, preferred_element_type=F32)
        up = jnp.dot(hm, wu, preferred_element_type=F32)
        act = (_silu(gate) * up).astype(BF16)
        return jnp.dot(act, wo, preferred_element_type=F32)

    @pl.when(j == 0)
    def _():
        hm = _modulate(x_ref[...], g_in_ref[...], mod_ref[r0:r0 + 1, :], mod_ref[r0 + 1:r0 + 2, :])
        hm = hm.astype(BF16)
        hm_ref[...] = hm
        acc_ref[...] = partial_out(hm)

    @pl.when((j > 0) & (j < last))
    def _():
        acc_ref[...] += partial_out(hm_ref[...])

    @pl.when(j == last)
    def _():
        g = mod_ref[r0 + 2:r0 + 3, :]
        h = x_ref[...] + (0.5 * g) * (acc_ref[...] + partial_out(hm_ref[...]))
        if cfg.emit_hx:
            hx = _modulate(h, g_next_ref[...], mod_ref[r0 + 3:r0 + 4, :], mod_ref[r0 + 4:r0 + 5, :])
            hx_ref[...] = hx.astype(BF16)
        if cfg.emit_h:
            h_ref[...] = _rmsnorm_rows(h, g_next_ref[...], RMS_EPS) if cfg.final_norm else h


def _ffn_call(x, mod, group_of_tile, g_in, g_next, wg, wu, wo, up_col0, cfg, name, cast_jobs=(),
              tiles=None, h_into=None):
    rows, d = x.shape
    n_ff = wo.shape[0]
    tf = FFN_TF_CAST if cfg.cast_w else FFN_TF
    n_j = n_ff // tf
    assert n_j >= 2, "first and last hidden block must be distinct grid steps"
    tm = cfg.tm
    tile0, n_tiles = tiles or (0, rows // tm)
    assert not cfg.cast_w or n_tiles == 1, "bf16 weight copies are written by one row tile only"
    assert cfg.emit_h or not cfg.acc_in_h
    assert h_into is None or (cfg.emit_h and not cast_jobs)
    up_blk0 = up_col0 // tf
    inner_group = group_of_tile
    group_of_tile = lambda i: inner_group(tile0 + i)
    row_spec = pl.BlockSpec((tm, d), lambda i, j: (tile0 + i, 0))
    vec_spec = pl.BlockSpec((1, d), lambda i, j: (0, 0))
    wg_spec = pl.BlockSpec((d, tf), lambda i, j: (0, j))
    wu_spec = pl.BlockSpec((d, tf), lambda i, j: (0, up_blk0 + j))
    wo_spec = pl.BlockSpec((tf, d), lambda i, j: (j, 0))
    cast_in_specs, cast_specs, cast_shapes = _cast_specs(cast_jobs)
    in_specs = [row_spec, pl.BlockSpec((None, N_MOD, d), lambda i, j: (group_of_tile(i), 0, 0)),
                vec_spec, vec_spec, wg_spec, wu_spec, wo_spec] + cast_in_specs
    out_specs, out_shape = [], []
    if cfg.emit_h:
        out_specs.append(row_spec)
        out_shape.append(jax.ShapeDtypeStruct((rows, d), F32))
    if cfg.emit_hx:
        out_specs.append(row_spec)
        out_shape.append(jax.ShapeDtypeStruct((rows, d), BF16))
    if cfg.cast_w:
        out_specs += [wg_spec, pl.BlockSpec((d, tf), lambda i, j: (0, j)), wo_spec]
        out_shape += [jax.ShapeDtypeStruct((d, n_ff), BF16), jax.ShapeDtypeStruct((d, n_ff), BF16),
                      jax.ShapeDtypeStruct((n_ff, d), BF16)]
    operands = [x, mod, g_in, g_next, wg, wu, wo] + [job.array for job in cast_jobs]
    aliases = {}
    if h_into is not None:
        aliases = {len(operands): 0}
        in_specs = in_specs + [pl.BlockSpec(memory_space=pl.ANY)]
        operands.append(h_into)
    return pl.pallas_call(
        functools.partial(_ffn_kernel, cfg, len(cast_jobs), len(aliases)),
        grid=(n_tiles, n_j),
        in_specs=in_specs,
        out_specs=out_specs + cast_specs,
        out_shape=out_shape + cast_shapes,
        input_output_aliases=aliases,
        scratch_shapes=[pltpu.VMEM((tm, d), BF16)] + ([] if cfg.acc_in_h else [pltpu.VMEM((tm, d), F32)]),
        compiler_params=_params("parallel", "arbitrary"),
        name=name,
    )(*operands)


def _wprep_kernel(w_ref, o_ref):
    o_ref[R_K:R_Q, :] = w_ref[0:OFF_GKF, :].astype(BF16)
    o_ref[R_Q:R_GK, :] = w_ref[OFF_Q:OFF_GLU + 2 * D_CONV, :].astype(BF16)
    o_ref[R_GK:R_GK + 2 * GATE_RANK, :] = w_ref[OFF_GKF:CTX_COLS, :].astype(BF16)
    o_ref[R_GK + 2 * GATE_RANK:R_END, :] = jnp.zeros((V7X_LANES - 2 * GATE_RANK, o_ref.shape[1]), BF16)


def _wprep_call(w_t):
    n, d = w_t.shape
    tc = WPREP_TC
    return pl.pallas_call(
        _wprep_kernel,
        grid=(d // tc,),
        in_specs=[pl.BlockSpec((n, tc), lambda i: (0, i))],
        out_specs=pl.BlockSpec((R_END, tc), lambda i: (0, i)),
        out_shape=jax.ShapeDtypeStruct((R_END, d), BF16),
        compiler_params=_params("parallel"),
        name="wprep",
    )(w_t)


def _log_decays(p_gk, w2_ref, b2_ref):
    z = jnp.dot(p_gk.astype(BF16), w2_ref[...], preferred_element_type=F32) + b2_ref[...]
    return _log_sigmoid(z) * (1.0 / GATE_NORMALIZER)


def _chunk_cumsum(x, reverse):
    n = x.shape[0]
    pos = lax.broadcasted_iota(jnp.int32, x.shape, 0) % CHUNK
    d = 1
    while d < CHUNK:
        if reverse:
            shifted = pltpu.roll(x, n - d, 0)
            x = x + jnp.where(pos < CHUNK - d, shifted, 0.0)
        else:
            shifted = pltpu.roll(x, d, 0)
            x = x + jnp.where(pos >= d, shifted, 0.0)
        d *= 2
    return x


def _chunk_rows(x, row_in_chunk):
    return jnp.concatenate([x[c0 + row_in_chunk:c0 + row_in_chunk + 1, :]
                            for c0 in range(0, x.shape[0], CHUNK)], axis=0)


def _chunk_bcast(rows_per_chunk):
    return jnp.concatenate([jnp.broadcast_to(rows_per_chunk[c:c + 1, :], (CHUNK, rows_per_chunk.shape[1]))
                            for c in range(rows_per_chunk.shape[0])], axis=0)


QKV_HEAD = 2 * GLA_DK + GLA_DV
LD_HEAD = 2 * GLA_DK


def _proj_kernel(n_cast, h_ref, mod_ref, g_ref, w_ref, w2_ref, b2_ref, *rest):
    cast_in, (qkv_ref, ld_ref, sg_ref, u_ref), cast_out = rest[:n_cast], rest[n_cast:n_cast + 4], rest[n_cast + 4:]
    _run_casts(cast_in, cast_out)
    hx = _modulate(h_ref[...], g_ref[...], mod_ref[3:4, :], mod_ref[4:5, :]).astype(BF16)

    def proj(lo, hi):
        return lax.dot_general(hx, w_ref[lo:hi, :], NT_DIMS, preferred_element_type=F32)

    ld = _log_decays(proj(R_GK, R_END), w2_ref, b2_ref)
    k = proj(R_K, R_V).astype(BF16)
    v = proj(R_V, R_Q).astype(BF16)
    q = (proj(R_Q, R_G) * (GLA_DK ** -0.5)).astype(BF16)
    for h in range(GLA_HEADS):
        dk = slice(h * GLA_DK, (h + 1) * GLA_DK)
        c0 = h * QKV_HEAD
        qkv_ref[:, c0:c0 + GLA_DK] = q[:, dk]
        qkv_ref[:, c0 + GLA_DK:c0 + 2 * GLA_DK] = k[:, dk]
        qkv_ref[:, c0 + 2 * GLA_DK:c0 + QKV_HEAD] = v[:, h * GLA_DV:(h + 1) * GLA_DV]
        ld_ref[:, h * LD_HEAD:h * LD_HEAD + GLA_DK] = ld[:, dk]
        ld_ref[:, h * LD_HEAD + GLA_DK:(h + 1) * LD_HEAD] = ld[:, GLA_KEY + h * GLA_DK:GLA_KEY + (h + 1) * GLA_DK]
    sg_ref[...] = _silu(proj(R_G, R_A)).astype(BF16)
    u_ref[...] = proj(R_A, R_B) * jax.nn.sigmoid(proj(R_B, R_GK))


def _proj_call(h, mod, group_of_tile, g_mix, w_r, w2pad, b2, cast_jobs=()):
    rows, d = h.shape
    tm = PROJ_TM
    row = lambda n: pl.BlockSpec((tm, n), lambda i: (i, 0))
    whole = lambda a: pl.BlockSpec(a.shape, lambda i: (0, 0), pipeline_mode=pl.Buffered(1))
    outs = [(GLA_HEADS * QKV_HEAD, BF16), (GLA_HEADS * LD_HEAD, F32), (D_GLA, BF16), (D_CONV, F32)]
    cast_in_specs, cast_specs, cast_shapes = _cast_specs(cast_jobs)
    return pl.pallas_call(
        functools.partial(_proj_kernel, len(cast_jobs)),
        grid=(rows // tm,),
        in_specs=[row(d), pl.BlockSpec((None, N_MOD, d), lambda i: (group_of_tile(i), 0, 0)),
                  pl.BlockSpec((1, d), lambda i: (0, 0)), whole(w_r), whole(w2pad), whole(b2)] + cast_in_specs,
        out_specs=[row(n) for n, _ in outs] + cast_specs,
        out_shape=[jax.ShapeDtypeStruct((rows, n), dt) for n, dt in outs] + cast_shapes,
        compiler_params=_params("parallel"),
        name="proj",
    )(h, mod, g_mix, w_r, w2pad, b2, *[job.array for job in cast_jobs])


def _ctx_kernel(hx_ref, wkv_ref, wgk_ref, w2_ref, b2_ref, sf_ref, sb_ref):
    hx = hx_ref[...]
    t = hx.shape[0]
    kv = lax.dot_general(hx, wkv_ref[...], NT_DIMS, preferred_element_type=F32)
    ld = _log_decays(lax.dot_general(hx, wgk_ref[...], NT_DIMS, preferred_element_type=F32),
                     w2_ref, b2_ref)
    r = lax.broadcasted_iota(jnp.int32, (t, t), 0)
    c = lax.broadcasted_iota(jnp.int32, (t, t), 1)
    hi = lax.Precision.HIGHEST
    e_f = jnp.dot((c > r).astype(F32), ld[:, :GLA_KEY], preferred_element_type=F32, precision=hi)
    e_b = jnp.dot((c < r).astype(F32), ld[:, GLA_KEY:], preferred_element_type=F32, precision=hi)
    for h in range(GLA_HEADS):
        ks = slice(h * GLA_DK, (h + 1) * GLA_DK)
        k = kv[:, R_K + h * GLA_DK:R_K + (h + 1) * GLA_DK]
        v = kv[:, R_V + h * GLA_DV:R_V + (h + 1) * GLA_DV].astype(BF16)
        sf_ref[h] = lax.dot_general(v, (k * jnp.exp(e_f[:, ks])).astype(BF16), TN_DIMS,
                                    preferred_element_type=F32)
        sb_ref[h] = lax.dot_general(v, (k * jnp.exp(e_b[:, ks])).astype(BF16), TN_DIMS,
                                    preferred_element_type=F32)


def _ctx_call(hxc, w_r, w2pad, b2, bsz, t):
    d = hxc.shape[1]
    const = lambda a: pl.BlockSpec(a.shape, lambda b: (0, 0))
    st = pl.BlockSpec((None, GLA_HEADS, GLA_DV, GLA_DK), lambda b: (b, 0, 0, 0))
    shape = jax.ShapeDtypeStruct((bsz, GLA_HEADS, GLA_DV, GLA_DK), F32)
    return pl.pallas_call(
        _ctx_kernel,
        grid=(bsz,),
        in_specs=[pl.BlockSpec((t, d), lambda b: (b, 0)),
                  pl.BlockSpec((R_Q, d), lambda b: (0, 0)),
                  pl.BlockSpec((V7X_LANES, d), lambda b: (R_GK // V7X_LANES, 0)),
                  const(w2pad), const(b2)],
        out_specs=[st, st],
        out_shape=[shape, shape],
        compiler_params=_params("parallel"),
        name="ctx",
    )(hxc, w_r, w_r, w2pad, b2)


def _gla_direction(qkv_ref, ld_ref, s_ref, reverse):
    n_sub = GLA_TB // GLA_SUB
    n_chunk = GLA_SUB // CHUNK
    rr = lax.broadcasted_iota(jnp.int32, (GLA_SUB, GLA_SUB), 0)
    cc = lax.broadcasted_iota(jnp.int32, (GLA_SUB, GLA_SUB), 1)
    same_chunk = (rr // CHUNK) == (cc // CHUNK)
    mask = same_chunk & ((cc >= rr) if reverse else (cc <= rr))
    mid_row = CHUNK // 2 if reverse else CHUNK // 2 - 1
    last_row = 0 if reverse else CHUNK - 1
    ld0 = GLA_DK if reverse else 0

    outs = [None] * n_sub
    subs = range(n_sub - 1, -1, -1) if reverse else range(n_sub)
    state = s_ref[...]
    for s in subs:
        rows = slice(s * GLA_SUB, (s + 1) * GLA_SUB)
        b = _chunk_cumsum(ld_ref[rows, ld0:ld0 + GLA_DK], reverse)
        b_mid = _chunk_bcast(_chunk_rows(b, mid_row))
        b_last_rows = _chunk_rows(b, last_row)
        b_last = _chunk_bcast(b_last_rows)
        q = qkv_ref[rows, 0:GLA_DK].astype(F32)
        k = qkv_ref[rows, GLA_DK:2 * GLA_DK].astype(F32)
        v = qkv_ref[rows, 2 * GLA_DK:QKV_HEAD]
        qs = (q * jnp.exp(b - b_mid)).astype(BF16)
        ks = (k * jnp.exp(b_mid - b)).astype(BF16)
        qi = (q * jnp.exp(b)).astype(BF16)
        kd = (k * jnp.exp(b_last - b)).astype(BF16)
        att = lax.dot_general(qs, ks, NT_DIMS, preferred_element_type=F32)
        att = jnp.where(mask, att, 0.0).astype(BF16)
        o_intra = jnp.dot(att, v, preferred_element_type=F32)
        decay = jnp.exp(b_last_rows)
        o_parts = [None] * n_chunk
        chunks = range(n_chunk - 1, -1, -1) if reverse else range(n_chunk)
        for c in chunks:
            cr = slice(c * CHUNK, (c + 1) * CHUNK)
            o_parts[c] = o_intra[cr, :] + lax.dot_general(
                qi[cr, :], state.astype(BF16), NT_DIMS, preferred_element_type=F32)
            kv = lax.dot_general(v[cr, :], kd[cr, :], TN_DIMS, preferred_element_type=F32)
            state = state * decay[c:c + 1, :] + kv
        outs[s] = jnp.concatenate(o_parts, axis=0)
    s_ref[...] = state
    return jnp.concatenate(outs, axis=0)


def _gla_kernel(n_cast, qkvf_ref, ldf_ref, qkvb_ref, ldb_ref, sg_ref, s0f_ref, s0b_ref, gain_ref,
                *rest):
    cast_in, o_ref, rest = rest[:n_cast], rest[n_cast], rest[n_cast + 1:]
    cast_out, (sf_ref, sb_ref, acc_ref) = rest[:n_cast], rest[n_cast:]
    nb = pl.program_id(2)
    n_blocks = pl.num_programs(2)

    @pl.when(nb == 0)
    def _():
        sf_ref[...] = s0f_ref[...]
        sb_ref[...] = s0b_ref[...]

    _run_casts(cast_in, cast_out)
    o_f = _gla_direction(qkvf_ref, ldf_ref, sf_ref, reverse=False)
    o_b = _gla_direction(qkvb_ref, ldb_ref, sb_ref, reverse=True)
    row_f = pl.multiple_of(nb * GLA_TB, GLA_TB)
    row_b = pl.multiple_of((n_blocks - 1 - nb) * GLA_TB, GLA_TB)

    @pl.when(nb < n_blocks // 2)
    def _():
        acc_ref[pl.ds(row_f, GLA_TB), :] = o_f
        acc_ref[pl.ds(row_b, GLA_TB), :] = o_b

    @pl.when(nb >= n_blocks // 2)
    def _():
        for row, part in ((row_f, o_f), (row_b, o_b)):
            o = acc_ref[pl.ds(row, GLA_TB), :] + part
            ms = jnp.mean(o * o, axis=-1, keepdims=True)
            o = o * lax.rsqrt(ms + HEAD_NORM_EPS) * gain_ref[...]
            o_ref[pl.ds(row, GLA_TB), :] = (o * sg_ref[pl.ds(row, GLA_TB), :].astype(F32)).astype(BF16)


def _gla_call(qkv, ld, sg, s0f, s0b, gain, bsz, t, cast_jobs=()):
    nb = t // GLA_TB
    fwd = lambda n: pl.BlockSpec((GLA_TB, n), lambda b, h, i: (b * nb + i, h))
    bwd = lambda n: pl.BlockSpec((GLA_TB, n), lambda b, h, i: (b * nb + nb - 1 - i, h))
    seq = pl.BlockSpec((t, GLA_DV), lambda b, h, i: (b, h))
    st = pl.BlockSpec((None, None, GLA_DV, GLA_DK), lambda b, h, i: (b, h, 0, 0))
    cast_in_specs, cast_specs, cast_shapes = _cast_specs(cast_jobs)
    return pl.pallas_call(
        functools.partial(_gla_kernel, len(cast_jobs)),
        grid=(bsz, GLA_HEADS, nb),
        in_specs=[fwd(QKV_HEAD), fwd(LD_HEAD), bwd(QKV_HEAD), bwd(LD_HEAD),
                  seq, st, st, pl.BlockSpec((1, GLA_DV), lambda b, h, i: (0, 0))] + cast_in_specs,
        out_specs=[seq] + cast_specs,
        out_shape=[jax.ShapeDtypeStruct((bsz * t, D_GLA), BF16)] + cast_shapes,
        scratch_shapes=[pltpu.VMEM((GLA_DV, GLA_DK), F32), pltpu.VMEM((GLA_DV, GLA_DK), F32),
                        pltpu.VMEM((t, GLA_DV), F32)],
        compiler_params=_params("parallel", "parallel", "arbitrary"),
        name="gla",
    )(qkv, ld, qkv, ld, sg, s0f, s0b, gain, *[job.array for job in cast_jobs])


ROW_PAD = 16
ROW_PITCH = GRID_W + 2 * ROW_PAD


def _conv_kernel(n_row_blocks, n_cast, u_ref, w_ref, b_ref, *rest):
    cast_in, y_ref, rest = rest[:n_cast], rest[n_cast], rest[n_cast + 1:]
    cast_out, (pad_ref,) = rest[:n_cast], rest[n_cast:]
    cb = pl.program_id(1)
    rows = u_ref.shape[0] // GRID_W
    bias = jnp.broadcast_to(b_ref[...], (GRID_W, CONV_CB))
    _run_casts(cast_in, cast_out)

    @pl.when(cb < n_row_blocks)
    def _():
        @pl.when(cb == 0)
        def _():
            pad_ref[...] = jnp.zeros_like(pad_ref)

        def fill(r, carry):
            src = pl.multiple_of(r * GRID_W, GRID_W)
            dst = pl.multiple_of(r * ROW_PITCH + ROW_PAD, 8)
            pad_ref[pl.ds(dst, GRID_W), :] = u_ref[pl.ds(src, GRID_W), :]
            return carry

        lax.fori_loop(0, rows, fill, 0)

        def body(r, carry):
            base = r * ROW_PITCH + (ROW_PAD - CONV_HALF)
            acc = bias
            for j in range(CONV_WIDTH):
                acc = acc + w_ref[j:j + 1, :] * pad_ref[pl.ds(base + j, GRID_W), :]
            y_ref[pl.ds(pl.multiple_of(r * GRID_W, GRID_W), GRID_W), :] = acc
            return carry

        lax.fori_loop(0, rows, body, 0, unroll=CONV_UNROLL)

    @pl.when(cb >= n_row_blocks)
    def _():
        edge = CONV_HALF * GRID_W

        @pl.when(cb == n_row_blocks)
        def _():
            pad_ref[0:edge, :] = jnp.zeros((edge, CONV_CB), F32)
            pad_ref[edge + rows * GRID_W:2 * edge + rows * GRID_W, :] = jnp.zeros((edge, CONV_CB), F32)

        pad_ref[edge:edge + rows * GRID_W, :] = u_ref[...]

        def body(r, carry):
            acc = bias
            for j in range(CONV_WIDTH):
                src = pl.multiple_of((r + j) * GRID_W, GRID_W)
                acc = acc + w_ref[j:j + 1, :] * pad_ref[pl.ds(src, GRID_W), :]
            y_ref[pl.ds(pl.multiple_of(r * GRID_W, GRID_W), GRID_W), :] = acc
            return carry

        lax.fori_loop(0, rows, body, 0, unroll=CONV_UNROLL)


def _conv_call(u, w, b, bsz, t, cast_jobs=()):
    ch = u.shape[1]
    rows = t // GRID_W
    n_cb = ch // CONV_CB
    pad_rows = max(rows * ROW_PITCH, (rows + 2 * CONV_HALF) * GRID_W)
    blk = pl.BlockSpec((t, CONV_CB), lambda bi, c: (bi, c))
    cast_in_specs, cast_specs, cast_shapes = _cast_specs(cast_jobs)
    return pl.pallas_call(
        functools.partial(_conv_kernel, n_cb // 2, len(cast_jobs)),
        grid=(bsz, n_cb),
        in_specs=[blk, pl.BlockSpec((CONV_WIDTH, CONV_CB), lambda bi, c: (0, c)),
                  pl.BlockSpec((1, CONV_CB), lambda bi, c: (0, c))] + cast_in_specs,
        out_specs=[blk] + cast_specs,
        out_shape=[jax.ShapeDtypeStruct(u.shape, F32)] + cast_shapes,
        scratch_shapes=[pltpu.VMEM((pad_rows, CONV_CB), F32)],
        compiler_params=_params("arbitrary", "arbitrary"),
        name="conv",
    )(u, w, b, *[job.array for job in cast_jobs])


def _outproj_kernel(n_cast, og_ref, y_ref, lng_ref, lnb_ref, w_ref, h_ref, mod_ref, *rest):
    cast_in, o_ref, cast_out = rest[:n_cast], rest[n_cast], rest[n_cast + 1:]
    _run_casts(cast_in, cast_out)
    y = y_ref[...]
    mu = jnp.mean(y, axis=-1, keepdims=True)
    yc = y - mu
    var = jnp.mean(yc * yc, axis=-1, keepdims=True)
    yn = yc * lax.rsqrt(var + LN_EPS) * lng_ref[...] + lnb_ref[...]
    oc = _silu(yn).astype(BF16)
    res = jnp.dot(og_ref[...], w_ref[0:D_GLA, :], preferred_element_type=F32)
    res = res + jnp.dot(oc, w_ref[D_GLA:D_GLA + D_CONV, :], preferred_element_type=F32)
    o_ref[...] = h_ref[...] + mod_ref[5:6, :] * res


def _outproj_call(og, y, ln_g, ln_b, w_out_bf, h, mod, group_of_tile, cast_jobs=()):
    rows, d = h.shape
    tm = OUT_TM
    row = lambda n: pl.BlockSpec((tm, n), lambda i: (i, 0))
    vec = lambda n: pl.BlockSpec((1, n), lambda i: (0, 0))
    cast_in_specs, cast_specs, cast_shapes = _cast_specs(cast_jobs)
    return pl.pallas_call(
        functools.partial(_outproj_kernel, len(cast_jobs)),
        grid=(rows // tm,),
        in_specs=[row(D_GLA), row(D_CONV), vec(D_CONV), vec(D_CONV),
                  pl.BlockSpec(w_out_bf.shape, lambda i: (0, 0), pipeline_mode=pl.Buffered(1)),
                  row(d), pl.BlockSpec((None, N_MOD, d), lambda i: (group_of_tile(i), 0, 0))] + cast_in_specs,
        out_specs=[row(d)] + cast_specs,
        out_shape=[jax.ShapeDtypeStruct((rows, d), F32)] + cast_shapes,
        compiler_params=_params("parallel"),
        name="outproj",
    )(og, y, ln_g, ln_b, w_out_bf, h, mod, *[job.array for job in cast_jobs])


def kernel(x, c, ctx, c_ctx, w_mod, b_mod, norm_ffn1, w_ffn1_in, w_ffn1_out, norm_mix, w_in, w_gk2, b_gk2, gla_norm, conv_w, conv_b, conv_ln_g, conv_ln_b, w_out, norm_ffn2, w_ffn2_in, w_ffn2_out, norm_final):
    bsz, t, d = x.shape
    t_ctx = ctx.shape[1]
    assert w_mod.shape[0] == 1, "single layer only"
    assert t % GLA_TB == 0 and (t // GLA_TB) % 2 == 0 and t % FFN_TALL_TM == 0 and t % PROJ_TM == 0
    vec = lambda a: a.reshape(1, -1)

    w2pad = jnp.zeros((V7X_LANES, 2 * GLA_KEY), BF16)
    w2pad = w2pad.at[:GATE_RANK, :GLA_KEY].set(w_gk2[0, 0].astype(BF16))
    w2pad = w2pad.at[GATE_RANK:2 * GATE_RANK, GLA_KEY:].set(w_gk2[0, 1].astype(BF16))
    b2 = b_gk2[0].reshape(1, 2 * GLA_KEY)
    w_r = _wprep_call(jnp.swapaxes(w_in[0], 0, 1))

    n_rows = 8
    s_in = jnp.concatenate([c, c_ctx[None, :], jnp.zeros((n_rows - bsz - 1, d), F32)], axis=0)
    mod = _mod_call(s_in, w_mod[0], vec(b_mod[0])).reshape(n_rows, N_MOD, d)

    batch_of = lambda tile_rows: (lambda i: i // (t // tile_rows))
    ctx_group = lambda i: bsz
    g1, gm = vec(norm_ffn1[0]), vec(norm_mix[0])

    xl = x.reshape(bsz * t, d)
    xc = ctx.reshape(bsz * t_ctx, d)
    n_lat_tiles = (bsz * t) // FFN_TALL_TM
    tall = lambda cast_w: _FfnCfg(mod_row=0, emit_h=True, emit_hx=False, final_norm=False, cast_w=cast_w,
                                  tm=FFN_TALL_TM, acc_in_h=True)
    h1_head, wg1, wu1, wo1 = _ffn_call(
        xl, mod, batch_of(FFN_TALL_TM), g1, gm, w_ffn1_in[0], w_ffn1_in[0], w_ffn1_out[0], D_FF,
        tall(True), "ffn1_head", tiles=(0, 1))
    (hxc,) = _ffn_call(
        xc, mod, ctx_group, g1, gm, wg1, wu1, wo1, 0,
        _FfnCfg(mod_row=0, emit_h=False, emit_hx=True, final_norm=False, cast_w=False), "ffn1_ctx")
    s0f, s0b = _ctx_call(hxc, w_r, w2pad, b2, bsz, t_ctx)
    (h1,) = _ffn_call(
        xl, mod, batch_of(FFN_TALL_TM), g1, gm, wg1, wu1, wo1, 0,
        tall(False), "ffn1", tiles=(1, n_lat_tiles - 1), h_into=h1_head)

    n_proj, n_outp = (bsz * t) // PROJ_TM, (bsz * t) // OUT_TM
    half_job = lambda n, col: _CastJob(w_ffn2_in[0], (d // n, D_FF), lambda i: (i, col),
                                       (d, D_FF), lambda i: (i, 0))
    qkv, ld, sg, u, wg2 = _proj_call(h1, mod, batch_of(PROJ_TM), gm, w_r, w2pad, b2,
                                     (half_job(n_proj, 0),))
    (og,) = _gla_call(qkv, ld, sg, s0f, s0b, vec(gla_norm[0]), bsz, t)
    n_conv_steps = bsz * (D_CONV // CONV_CB)
    step = lambda bi, cb: (bi * (D_CONV // CONV_CB) + cb, 0)
    y, wf2_out, w_out_bf = _conv_call(
        u, conv_w[0], vec(conv_b[0]), bsz, t,
        (_CastJob(w_ffn2_out[0], (D_FF // n_conv_steps, d), step),
         _CastJob(w_out[0], (d // n_conv_steps, d), step)))
    h2, wu2 = _outproj_call(
        og, y, vec(conv_ln_g[0]), vec(conv_ln_b[0]), w_out_bf, h1, mod, batch_of(OUT_TM),
        (half_job(n_outp, 1),))

    (out,) = _ffn_call(
        h2, mod, batch_of(FFN_TALL_TM), vec(norm_ffn2[0]), vec(norm_final), wg2, wu2, wf2_out, 0,
        _FfnCfg(mod_row=6, emit_h=True, emit_hx=False, final_norm=True, cast_w=False, tm=FFN_TALL_TM,
                acc_in_h=True), "ffn2")
    return out.reshape(bsz, t, d)
```
